```python
import math
import jax, jax.numpy as jnp
from jax import lax
import numpy as np

D_MODEL = 1024
BATCH = 2
SEQ = 8192
DEPTH = 2

N_MIXERS = 2
N_EVEN = (DEPTH + 1) // 2
N_ODD = DEPTH // 2
ALPHA = (2.0 * DEPTH) ** 0.25
BETA_INIT = (8.0 * DEPTH) ** -0.25
LN_EPS = 1e-5

GDN_K_HEADS = 4
GDN_V_HEADS = 8
GDN_HEAD_K = 128
GDN_HEAD_V = 128
GDN_KDIM = GDN_K_HEADS * GDN_HEAD_K
GDN_VDIM = GDN_V_HEADS * GDN_HEAD_V
GDN_CONV = 4
GDN_CHUNK = 64
GDN_QKV = 2 * GDN_KDIM + GDN_VDIM
GDN_IN = GDN_QKV + GDN_VDIM + 2 * GDN_V_HEADS
GDN_EPS = 1e-6

SWA_Q_HEADS = 16
SWA_KV_HEADS = 2
SWA_GROUP = SWA_Q_HEADS // SWA_KV_HEADS
SWA_HEAD_DIM = 64
SWA_WINDOW = 128
SWA_BLOCK = 128
SWA_QDIM = SWA_Q_HEADS * SWA_HEAD_DIM
SWA_KVDIM = SWA_KV_HEADS * SWA_HEAD_DIM
SWA_IN = SWA_QDIM + 2 * SWA_KVDIM

REL_BUCKETS = 32
REL_MAX_DIST = 128

FFN_DIM = 2816
N_EXPERTS = 8
TOP_K = 2
EXPERT_DIM = 3584

kernel_name = 'hybrid_gdn_swa_moe_deepnorm'


def layer_norm(x, g, b):
    xf = x.astype(jnp.float32)
    mu = jnp.mean(xf, axis=-1, keepdims=True)
    var = jnp.mean(jnp.square(xf - mu), axis=-1, keepdims=True)
    return ((xf - mu) * lax.rsqrt(var + LN_EPS) * g.astype(jnp.float32) + b.astype(jnp.float32)).astype(x.dtype)


def l2_normalize(x):
    xf = x.astype(jnp.float32)
    return xf * lax.rsqrt(jnp.sum(xf * xf, axis=-1, keepdims=True) + GDN_EPS)


def causal_depthwise_conv(x, w):
    width = w.shape[0]
    return lax.conv_general_dilated(
        x, w[:, None, :].astype(x.dtype), window_strides=(1,), padding=[(width - 1, 0)],
        dimension_numbers=('NWC', 'WIO', 'NWC'), feature_group_count=x.shape[-1])


def gated_delta_rule_chunked(q, k, v, g, beta):
    bsz, t, h, dk = q.shape
    dv = v.shape[-1]
    c = GDN_CHUNK
    n = t // c
    f32 = jnp.float32

    def blocks(a):
        a = a.astype(f32).reshape((bsz, n, c, h) + a.shape[3:])
        return jnp.moveaxis(a, 2, 3)

    q, k, v, g, beta = blocks(q), blocks(k), blocks(v), blocks(g), blocks(beta)
    g_cum = jnp.cumsum(g, axis=-1)
    causal = jnp.tril(jnp.ones((c, c), dtype=bool))
    strict = jnp.tril(jnp.ones((c, c), dtype=bool), -1)
    diff = g_cum[..., :, None] - g_cum[..., None, :]
    decay = jnp.exp(jnp.where(causal, diff, -jnp.inf))
    k_beta = k * beta[..., None]
    v_beta = v * beta[..., None]
    a_kk = jnp.where(strict, jnp.einsum('bnhid,bnhjd->bnhij', k_beta, k) * decay, 0.0)
    eye = jnp.eye(c, dtype=f32)
    t_inv = lax.linalg.triangular_solve(eye + a_kk, jnp.broadcast_to(eye, a_kk.shape),
                                        left_side=True, lower=True, unit_diagonal=True)
    u = jnp.einsum('bnhij,bnhje->bnhie', t_inv, v_beta)
    w = jnp.einsum('bnhij,bnhjd->bnhid', t_inv, k_beta * jnp.exp(g_cum)[..., None])
    a_qk = jnp.einsum('bnhid,bnhjd->bnhij', q, k) * decay
    g_last = g_cum[..., -1]
    q_dec = q * jnp.exp(g_cum)[..., None]
    k_dec = k * jnp.exp(g_last[..., None] - g_cum)[..., None]
    xs = (jnp.moveaxis(q_dec, 1, 0), jnp.moveaxis(k_dec, 1, 0), jnp.moveaxis(u, 1, 0),
          jnp.moveaxis(w, 1, 0), jnp.moveaxis(a_qk, 1, 0), jnp.moveaxis(jnp.exp(g_last), 1, 0))

    def step(s, inp):
        qd, kd, uc, wc, aqk, dl = inp
        v_new = uc - jnp.einsum('bhcd,bhde->bhce', wc, s)
        o = jnp.einsum('bhcd,bhde->bhce', qd, s) + jnp.einsum('bhcs,bhse->bhce', aqk, v_new)
        s = s * dl[..., None, None] + jnp.einsum('bhcd,bhce->bhde', kd, v_new)
        return s, o

    s0 = jnp.zeros((bsz, h, dk, dv), f32)
    _, o = lax.scan(step, s0, xs)
    o = jnp.moveaxis(o, 0, 1)
    return jnp.moveaxis(o, 2, 3).reshape(bsz, t, h, dv)


def gated_deltanet(x, w_in, conv_w, a_log, dt_bias, norm_w, w_out):
    bsz, t, _ = x.shape
    proj = x @ w_in
    o1 = GDN_QKV
    o2 = o1 + GDN_VDIM
    o3 = o2 + GDN_V_HEADS
    qkv = jax.nn.silu(causal_depthwise_conv(proj[..., :o1], conv_w))
    z = proj[..., o1:o2].reshape(bsz, t, GDN_V_HEADS, GDN_HEAD_V)
    b_logit = proj[..., o2:o3]
    a_logit = proj[..., o3:]
    q = qkv[..., :GDN_KDIM].reshape(bsz, t, GDN_K_HEADS, GDN_HEAD_K)
    k = qkv[..., GDN_KDIM:2 * GDN_KDIM].reshape(bsz, t, GDN_K_HEADS, GDN_HEAD_K)
    v = qkv[..., 2 * GDN_KDIM:].reshape(bsz, t, GDN_V_HEADS, GDN_HEAD_V)
    rep = GDN_V_HEADS // GDN_K_HEADS
    q = jnp.repeat(l2_normalize(q) * (GDN_HEAD_K ** -0.5), rep, axis=2)
    k = jnp.repeat(l2_normalize(k), rep, axis=2)
    beta = jax.nn.sigmoid(b_logit.astype(jnp.float32))
    g = -jnp.exp(a_log.astype(jnp.float32)) * jax.nn.softplus(
        a_logit.astype(jnp.float32) + dt_bias.astype(jnp.float32))
    o = gated_delta_rule_chunked(q, k, v, g, beta)
    o = o * lax.rsqrt(jnp.mean(o * o, axis=-1, keepdims=True) + GDN_EPS) * norm_w.astype(jnp.float32)
    o = o * jax.nn.silu(z.astype(jnp.float32))
    return o.reshape(bsz, t, GDN_VDIM).astype(x.dtype) @ w_out


def t5_bucket(dist):
    max_exact = REL_BUCKETS // 2
    df = jnp.maximum(dist, 1).astype(jnp.float32)
    large = max_exact + (jnp.log(df / max_exact) / math.log(REL_MAX_DIST / max_exact)
                         * (REL_BUCKETS - max_exact)).astype(jnp.int32)
    large = jnp.minimum(large, REL_BUCKETS - 1)
    return jnp.where(dist < max_exact, dist, large)


def swa_sink_attention(x, w_in, b_in, sinks, rel_bias, w_out):
    bsz, t, _ = x.shape
    nb = t // SWA_BLOCK
    proj = x @ w_in + b_in
    q = proj[..., :SWA_QDIM].reshape(bsz, nb, SWA_BLOCK, SWA_KV_HEADS, SWA_GROUP, SWA_HEAD_DIM)
    k = proj[..., SWA_QDIM:SWA_QDIM + SWA_KVDIM]
    v = proj[..., SWA_QDIM + SWA_KVDIM:]

    def band(a):
        ab = a.reshape(bsz, nb, SWA_BLOCK, SWA_KV_HEADS, SWA_HEAD_DIM)
        prev = jnp.pad(ab, ((0, 0), (1, 0), (0, 0), (0, 0), (0, 0)))[:, :-1]
        return jnp.concatenate([prev, ab], axis=2)

    kb, vb = band(k), band(v)
    s = jnp.einsum('bnqkgd,bnskd->bnkgqs', q, kb).astype(jnp.float32) * (SWA_HEAD_DIM ** -0.5)
    qi = jnp.arange(SWA_BLOCK)[:, None]
    kj = jnp.arange(2 * SWA_BLOCK)[None, :]
    dist = qi + SWA_BLOCK - kj
    bias = rel_bias.astype(jnp.float32)[t5_bucket(jnp.maximum(dist, 0))]
    bias = jnp.transpose(bias, (2, 0, 1)).reshape(SWA_KV_HEADS, SWA_GROUP, SWA_BLOCK, 2 * SWA_BLOCK)
    band_ok = (dist >= 0) & (dist < SWA_WINDOW)
    blk = jnp.arange(nb)[:, None, None]
    mask = band_ok[None] & ((blk > 0) | (kj >= SWA_BLOCK)[None])
    s = jnp.where(mask[None, :, None, None], s + bias, -jnp.inf)
    sink = jnp.broadcast_to(sinks.astype(jnp.float32).reshape(SWA_KV_HEADS, SWA_GROUP, 1, 1),
                            s.shape[:-1] + (1,))
    p = jax.nn.softmax(jnp.concatenate([s, sink], axis=-1), axis=-1)[..., :-1]
    o = jnp.einsum('bnkgqs,bnskd->bnqkgd', p.astype(vb.dtype), vb).reshape(bsz, t, SWA_QDIM)
    return o @ w_out


def swiglu(x, w_gate, w_up, w_down):
    return (jax.nn.silu(x @ w_gate) * (x @ w_up)) @ w_down


def moe_swiglu(x, w_router, w_gate, w_up, w_down):
    bsz, t, d = x.shape
    xf = x.reshape(bsz * t, d)
    logits = (xf @ w_router).astype(jnp.float32)
    top_val, top_idx = lax.top_k(logits, TOP_K)
    top_w = jax.nn.softmax(top_val, axis=-1)
    gates = jnp.einsum('nk,nke->ne', top_w, jax.nn.one_hot(top_idx, N_EXPERTS, dtype=jnp.float32))
    out = jnp.zeros_like(xf)
    for e in range(N_EXPERTS):
        h = jax.nn.silu(xf @ w_gate[e]) * (xf @ w_up[e])
        out = out + gates[:, e:e + 1].astype(xf.dtype) * (h @ w_down[e])
    return out.reshape(bsz, t, d)


def setup_inputs(seed: int = 0) -> dict:
    key = jax.random.key(seed)
    ks = jax.random.split(key, 24)

    def nrm(k, shape, scale):
        return jax.random.normal(k, shape, jnp.float32) * scale

    x = nrm(ks[0], (BATCH, SEQ, D_MODEL), 1.0)
    a_w_in = nrm(ks[1], (N_EVEN, D_MODEL, GDN_IN), D_MODEL ** -0.5)
    a_conv_w = nrm(ks[2], (N_EVEN, GDN_CONV, GDN_QKV), GDN_CONV ** -0.5)
    a_a_log = jnp.log(jax.random.uniform(ks[3], (N_EVEN, GDN_V_HEADS), jnp.float32, 1.0, 16.0))
    dt = jnp.exp(jax.random.uniform(ks[4], (N_EVEN, GDN_V_HEADS), jnp.float32,
                                    math.log(1e-3), math.log(1e-1)))
    a_dt_bias = dt + jnp.log(-jnp.expm1(-dt))
    a_norm_w = 1.0 + nrm(ks[5], (N_EVEN, GDN_HEAD_V), 0.02)
    a_w_out = nrm(ks[6], (N_EVEN, GDN_VDIM, D_MODEL), GDN_VDIM ** -0.5 * BETA_INIT)
    b_w_in = nrm(ks[7], (N_ODD, D_MODEL, SWA_IN), D_MODEL ** -0.5)
    b_b_in = nrm(ks[8], (N_ODD, SWA_IN), 0.02)
    b_sinks = nrm(ks[9], (N_ODD, SWA_Q_HEADS), 1.0)
    b_w_out = nrm(ks[10], (N_ODD, SWA_QDIM, D_MODEL), SWA_QDIM ** -0.5 * BETA_INIT)
    rel_bias = nrm(ks[11], (REL_BUCKETS, SWA_Q_HEADS), 0.3)
    ffn_w_gate = nrm(ks[12], (N_EVEN, D_MODEL, FFN_DIM), D_MODEL ** -0.5)
    ffn_w_up = nrm(ks[13], (N_EVEN, D_MODEL, FFN_DIM), D_MODEL ** -0.5)
    ffn_w_down = nrm(ks[14], (N_EVEN, FFN_DIM, D_MODEL), FFN_DIM ** -0.5 * BETA_INIT)
    moe_router = nrm(ks[15], (N_ODD, D_MODEL, N_EXPERTS), D_MODEL ** -0.5)
    moe_w_gate = nrm(ks[16], (N_ODD, N_EXPERTS, D_MODEL, EXPERT_DIM), D_MODEL ** -0.5)
    moe_w_up = nrm(ks[17], (N_ODD, N_EXPERTS, D_MODEL, EXPERT_DIM), D_MODEL ** -0.5)
    moe_w_down = nrm(ks[18], (N_ODD, N_EXPERTS, EXPERT_DIM, D_MODEL), EXPERT_DIM ** -0.5 * BETA_INIT)
    ln_g = 1.0 + nrm(ks[19], (DEPTH, 2, D_MODEL), 0.02)
    ln_b = nrm(ks[20], (DEPTH, 2, D_MODEL), 0.02)
    return {'x': x, 'a_w_in': a_w_in, 'a_conv_w': a_conv_w, 'a_a_log': a_a_log, 'a_dt_bias': a_dt_bias,
            'a_norm_w': a_norm_w, 'a_w_out': a_w_out, 'b_w_in': b_w_in, 'b_b_in': b_b_in,
            'b_sinks': b_sinks, 'b_w_out': b_w_out, 'rel_bias': rel_bias, 'ffn_w_gate': ffn_w_gate,
            'ffn_w_up': ffn_w_up, 'ffn_w_down': ffn_w_down, 'moe_router': moe_router,
            'moe_w_gate': moe_w_gate, 'moe_w_up': moe_w_up, 'moe_w_down': moe_w_down,
            'ln_g': ln_g, 'ln_b': ln_b}


def reference(x, a_w_in, a_conv_w, a_a_log, a_dt_bias, a_norm_w, a_w_out, b_w_in, b_b_in, b_sinks,
              b_w_out, rel_bias, ffn_w_gate, ffn_w_up, ffn_w_down, moe_router, moe_w_gate, moe_w_up,
              moe_w_down, ln_g, ln_b):
    for i in range(DEPTH):
        j = i // N_MIXERS
        if i % N_MIXERS == 0:
            h = gated_deltanet(x, a_w_in[j], a_conv_w[j], a_a_log[j], a_dt_bias[j], a_norm_w[j], a_w_out[j])
        else:
            h = swa_sink_attention(x, b_w_in[j], b_b_in[j], b_sinks[j], rel_bias, b_w_out[j])
        x = layer_norm(ALPHA * x + h, ln_g[i, 0], ln_b[i, 0])
        if i % 2 == 0:
            f = swiglu(x, ffn_w_gate[j], ffn_w_up[j], ffn_w_down[j])
        else:
            f = moe_swiglu(x, moe_router[j], moe_w_gate[j], moe_w_up[j], moe_w_down[j])
        x = layer_norm(ALPHA * x + f, ln_g[i, 1], ln_b[i, 1])
    return x
```

```python
import functools
import math

import numpy as np
import jax
import jax.numpy as jnp
from jax import lax
from jax.experimental import pallas as pl
from jax.experimental.pallas import tpu as pltpu

F32 = jnp.float32
BF16 = jnp.bfloat16

D_MODEL = 1024
DEPTH = 2
ALPHA = (2.0 * DEPTH) ** 0.25
LN_EPS = 1e-5

GDN_K_HEADS = 4
GDN_V_HEADS = 8
GDN_HEAD = 128
GDN_KDIM = GDN_K_HEADS * GDN_HEAD
GDN_VDIM = GDN_V_HEADS * GDN_HEAD
GDN_CONV = 4
GDN_CHUNK = 64
GDN_QKV = 2 * GDN_KDIM + GDN_VDIM
GDN_EPS = 1e-6

SWA_Q_HEADS = 16
SWA_KV_HEADS = 2
SWA_GROUP = SWA_Q_HEADS // SWA_KV_HEADS
SWA_HEAD_DIM = 64
SWA_WINDOW = 128
SWA_BLOCK = 128
SWA_QDIM = SWA_Q_HEADS * SWA_HEAD_DIM
SWA_KVDIM = SWA_KV_HEADS * SWA_HEAD_DIM
REL_BUCKETS = 32
REL_MAX_DIST = 128

FFN_DIM = 2816
N_EXPERTS = 8
EXPERT_DIM = 3584

LANES = 128
SUBLANES = 8
NEG_BIG = -1e30
VMEM_LIMIT = 56 * 1024 * 1024

TM_GDN_IN = 256
TM_PROJ = 512
TM_FFN = 512
TF_FFN = 1408
TM_SWA_IN = 512
TM_ROUTER = 512
TM_MOE = 512
TF_MOE = 512
TD_DISPATCH = 512
TC_COMBINE = 256


def _params(*sem):
    return pltpu.CompilerParams(dimension_semantics=sem, vmem_limit_bytes=VMEM_LIMIT)


def _dot(a, b):
    return jnp.dot(a, b, preferred_element_type=F32)


def _dot_nt(a, b):
    return lax.dot_general(a, b, (((1,), (1,)), ((), ())), preferred_element_type=F32)


def _dot_tn(a, b):
    return lax.dot_general(a, b, (((0,), (0,)), ((), ())), preferred_element_type=F32)


def _split(x):
    hi = x.astype(BF16)
    lo = (x - hi.astype(F32)).astype(BF16)
    return hi, lo


def _dot3(a, b):
    ah, al = _split(a)
    bh, bl = _split(b)
    return _dot(ah, bh) + (_dot(ah, bl) + _dot(al, bh))


def _silu(x):
    return x * jax.nn.sigmoid(x)


def _layer_norm(y, g, b):
    mu = jnp.mean(y, axis=-1, keepdims=True)
    yc = y - mu
    var = jnp.mean(yc * yc, axis=-1, keepdims=True)
    return yc * lax.rsqrt(var + LN_EPS) * g + b


def _gdn_inproj_kernel(x_ref, w_ref, wba_ref, convw_ref, gp_ref,
                       q_ref, k_ref, v_ref, z_ref, gb_ref, ext_ref, *, tm, tiles_per_seq):
    i = pl.program_id(0)
    x = x_ref[...]
    xb = x.astype(BF16)

    @pl.when(i % tiles_per_seq == 0)
    def _():
        ext_ref[0:SUBLANES, :] = jnp.zeros((SUBLANES, GDN_QKV), F32)

    @pl.when(i % tiles_per_seq != 0)
    def _():
        ext_ref[0:SUBLANES, :] = ext_ref[tm:tm + SUBLANES, :]

    ext_ref[SUBLANES:, :] = _dot(xb, w_ref[:, :GDN_QKV])
    z_ref[...] = _silu(_dot(xb, w_ref[:, GDN_QKV:]))

    n_chunks = GDN_QKV // LANES
    for c in range(n_chunks):
        cs = slice(c * LANES, (c + 1) * LANES)
        acc = convw_ref[GDN_CONV - 1:GDN_CONV, cs] * ext_ref[SUBLANES:SUBLANES + tm, cs]
        for j in range(GDN_CONV - 1):
            off = SUBLANES - (GDN_CONV - 1) + j
            acc = acc + convw_ref[j:j + 1, cs] * ext_ref[off:off + tm, cs]
        y = _silu(acc)
        if c < 2 * GDN_K_HEADS:
            y = y * lax.rsqrt(jnp.sum(y * y, axis=-1, keepdims=True) + GDN_EPS)
            if c < GDN_K_HEADS:
                q_ref[:, cs] = y * (GDN_HEAD ** -0.5)
            else:
                k_ref[:, (c - GDN_K_HEADS) * LANES:(c - GDN_K_HEADS + 1) * LANES] = y
        else:
            cv = c - 2 * GDN_K_HEADS
            v_ref[:, cv * LANES:(cv + 1) * LANES] = y

    ba = _dot3(x, wba_ref[...])
    lane = lax.broadcasted_iota(jnp.int32, ba.shape, 1)
    sp = ba + gp_ref[1:2, :]
    softplus = jnp.maximum(sp, 0.0) + jnp.log(1.0 + jnp.exp(-jnp.abs(sp)))
    g = -jnp.exp(gp_ref[0:1, :]) * softplus
    gb_ref[...] = jnp.where(lane < GDN_V_HEADS, jax.nn.sigmoid(ba), g)


def _gdn_inproj(x2d, w_main, w_ba, conv_w, gate_params, seq_len):
    n = x2d.shape[0]
    tm = TM_GDN_IN
    kern = functools.partial(_gdn_inproj_kernel, tm=tm, tiles_per_seq=seq_len // tm)
    full = lambda shape: pl.BlockSpec(shape, lambda i: (0,) * len(shape))
    rows = lambda width: pl.BlockSpec((tm, width), lambda i: (i, 0))
    return pl.pallas_call(
        kern,
        grid=(n // tm,),
        in_specs=[rows(D_MODEL), full(w_main.shape), full(w_ba.shape), full(conv_w.shape), full(gate_params.shape)],
        out_specs=[rows(GDN_KDIM), rows(GDN_KDIM), rows(GDN_VDIM), rows(GDN_VDIM), rows(LANES)],
        out_shape=[jax.ShapeDtypeStruct((n, GDN_KDIM), F32), jax.ShapeDtypeStruct((n, GDN_KDIM), F32),
                   jax.ShapeDtypeStruct((n, GDN_VDIM), F32), jax.ShapeDtypeStruct((n, GDN_VDIM), F32),
                   jax.ShapeDtypeStruct((n, LANES), F32)],
        scratch_shapes=[pltpu.VMEM((tm + SUBLANES, GDN_QKV), F32)],
        compiler_params=_params("arbitrary"),
        name="gdn_inproj",
    )(x2d, w_main, w_ba, conv_w, gate_params)


def _unit_lower_inverse(a, row, col):
    f32_eye = jnp.where(row == col, 1.0, 0.0).astype(F32)
    rb, cb = row >> 3, col >> 3
    d = jnp.where(rb == cb, a, 0.0)
    x = f32_eye - d
    p = _dot3(d, d)
    x = x + _dot3(x, p)
    p = _dot3(p, p)
    x = x + _dot3(x, p)
    size = SUBLANES
    while size < a.shape[0]:
        shift = int(math.log2(size))
        rbl, cbl = row >> shift, col >> shift
        below = ((rbl & 1) == 1) & (cbl == rbl - 1)
        l = jnp.where(below, a, 0.0)
        x = x - _dot3(_dot3(x, l), x)
        size *= 2
    return x


def _gdn_chunk_kernel(q_ref, k_ref, v_ref, z_ref, gb_ref, nw_ref, o_ref, s_ref):
    c = GDN_CHUNK

    @pl.when(pl.program_id(1) == 0)
    def _():
        s_ref[...] = jnp.zeros(s_ref.shape, F32)

    row = lax.broadcasted_iota(jnp.int32, (c, c), 0)
    col = lax.broadcasted_iota(jnp.int32, (c, c), 1)
    causal = row >= col
    strict = row > col
    tril = jnp.where(causal, 1.0, 0.0).astype(BF16)

    gb = gb_ref[...]
    g_hi, g_lo = _split(gb)
    gc = _dot(tril, g_hi) + _dot(tril, g_lo)
    gc_t = gc.T
    eg = jnp.exp(gc)
    g_last = gc[c - 1:c, :]
    e_last = jnp.exp(g_last)
    e_rest = jnp.exp(g_last - gc)

    for h in range(GDN_V_HEADS):
        kh = h // (GDN_V_HEADS // GDN_K_HEADS)
        ks = slice(kh * GDN_HEAD, (kh + 1) * GDN_HEAD)
        vs = slice(h * GDN_HEAD, (h + 1) * GDN_HEAD)
        gl = GDN_V_HEADS + h
        q = q_ref[:, ks]
        k = k_ref[:, ks]
        v = v_ref[:, vs]
        beta = gb[:, h:h + 1]
        gc_c = gc[:, gl:gl + 1]
        gc_r = gc_t[gl:gl + 1, :]
        eg_c = eg[:, gl:gl + 1]

        decay = jnp.exp(jnp.where(causal, gc_c - gc_r, NEG_BIG))
        k_beta = k * beta
        kq = jnp.concatenate([k_beta, q], axis=0).astype(BF16)
        gram = _dot_nt(kq, k.astype(BF16))
        a_kk = jnp.where(strict, gram[:c] * decay, 0.0)
        a_qk = gram[c:] * decay
        t_inv = _unit_lower_inverse(a_kk, row, col)

        rhs = jnp.concatenate([v * beta, k_beta * eg_c], axis=1).astype(BF16)
        uw = _dot(t_inv.astype(BF16), rhs)
        u = uw[:, :GDN_HEAD]
        w = uw[:, GDN_HEAD:]

        s = s_ref[h]
        s_b = s.astype(BF16)
        wq = jnp.concatenate([w, q * eg_c], axis=0).astype(BF16)
        ws_qs = _dot(wq, s_b)
        v_new = u - ws_qs[:c]
        v_new_b = v_new.astype(BF16)
        o = ws_qs[c:] + _dot(a_qk.astype(BF16), v_new_b)
        k_dec = (k * e_rest[:, gl:gl + 1]).astype(BF16)
        s_ref[h] = s * e_last[:, gl:gl + 1] + _dot_tn(k_dec, v_new_b)

        o = o * lax.rsqrt(jnp.mean(o * o, axis=-1, keepdims=True) + GDN_EPS) * nw_ref[...]
        o_ref[:, vs] = (o * z_ref[:, vs]).astype(o_ref.dtype)


def _gdn_chunk(q, k, v, zs, gb, norm_w, batch, seq_len):
    n = q.shape[0]
    c = GDN_CHUNK
    nc = seq_len // c
    rows = lambda width: pl.BlockSpec((c, width), lambda b, j: (b * nc + j, 0))
    return pl.pallas_call(
        _gdn_chunk_kernel,
        grid=(batch, nc),
        in_specs=[rows(GDN_KDIM), rows(GDN_KDIM), rows(GDN_VDIM), rows(GDN_VDIM), rows(LANES),
                  pl.BlockSpec((1, GDN_HEAD), lambda b, j: (0, 0))],
        out_specs=rows(GDN_VDIM),
        out_shape=jax.ShapeDtypeStruct((n, GDN_VDIM), BF16),
        scratch_shapes=[pltpu.VMEM((GDN_V_HEADS, GDN_HEAD, GDN_HEAD), F32)],
        compiler_params=_params("arbitrary", "arbitrary"),
        name="gdn_chunk",
    )(q, k, v, zs, gb, norm_w)


def _proj_ln_kernel(a_ref, w_ref, r_ref, g_ref, b_ref, o_ref):
    y = ALPHA * r_ref[...] + _dot(a_ref[...], w_ref[...])
    o_ref[...] = _layer_norm(y, g_ref[...], b_ref[...])


def _proj_ln(a, w, res, g, b, name):
    n, kdim = a.shape
    tm = TM_PROJ
    return pl.pallas_call(
        _proj_ln_kernel,
        grid=(n // tm,),
        in_specs=[pl.BlockSpec((tm, kdim), lambda i: (i, 0)), pl.BlockSpec(w.shape, lambda i: (0, 0)),
                  pl.BlockSpec((tm, D_MODEL), lambda i: (i, 0)),
                  pl.BlockSpec((1, D_MODEL), lambda i: (0, 0)), pl.BlockSpec((1, D_MODEL), lambda i: (0, 0))],
        out_specs=pl.BlockSpec((tm, D_MODEL), lambda i: (i, 0)),
        out_shape=jax.ShapeDtypeStruct((n, D_MODEL), F32),
        compiler_params=_params("parallel"),
        name=name,
    )(a, w, res, g, b)


def _ffn_kernel(x_ref, wg_ref, wu_ref, wd_ref, g_ref, b_ref, o_ref, acc_ref, xb_ref):
    f = pl.program_id(1)

    @pl.when(f == 0)
    def _():
        xb_ref[...] = x_ref[...].astype(BF16)

    xb = xb_ref[...]
    h = (_silu(_dot(xb, wg_ref[...])) * _dot(xb, wu_ref[...])).astype(BF16)
    part = _dot(h, wd_ref[...])

    @pl.when(f == 0)
    def _():
        acc_ref[...] = part

    @pl.when(f != 0)
    def _():
        acc_ref[...] += part

    @pl.when(f == pl.num_programs(1) - 1)
    def _():
        o_ref[...] = _layer_norm(ALPHA * x_ref[...] + acc_ref[...], g_ref[...], b_ref[...])


def _ffn_ln(x, wg, wu, wd, g, b):
    n = x.shape[0]
    tm, tf = TM_FFN, TF_FFN
    return pl.pallas_call(
        _ffn_kernel,
        grid=(n // tm, FFN_DIM // tf),
        in_specs=[pl.BlockSpec((tm, D_MODEL), lambda i, f: (i, 0)),
                  pl.BlockSpec((D_MODEL, tf), lambda i, f: (0, f)),
                  pl.BlockSpec((D_MODEL, tf), lambda i, f: (0, f)),
                  pl.BlockSpec((tf, D_MODEL), lambda i, f: (f, 0)),
                  pl.BlockSpec((1, D_MODEL), lambda i, f: (0, 0)), pl.BlockSpec((1, D_MODEL), lambda i, f: (0, 0))],
        out_specs=pl.BlockSpec((tm, D_MODEL), lambda i, f: (i, 0)),
        out_shape=jax.ShapeDtypeStruct((n, D_MODEL), F32),
        scratch_shapes=[pltpu.VMEM((tm, D_MODEL), F32), pltpu.VMEM((tm, D_MODEL), BF16)],
        compiler_params=_params("parallel", "arbitrary"),
        name="ffn_ln",
    )(x, wg, wu, wd, g, b)


def _swa_inproj_kernel(x_ref, w_ref, b_ref, q_ref, kv_ref):
    p = _dot(x_ref[...].astype(BF16), w_ref[...]) + b_ref[...]
    q_ref[...] = (p[:, :SWA_QDIM] * (SWA_HEAD_DIM ** -0.5)).astype(BF16)
    kv_ref[...] = p[:, SWA_QDIM:].astype(BF16)


def _swa_inproj(x, w, b):
    n = x.shape[0]
    tm = TM_SWA_IN
    return pl.pallas_call(
        _swa_inproj_kernel,
        grid=(n // tm,),
        in_specs=[pl.BlockSpec((tm, D_MODEL), lambda i: (i, 0)), pl.BlockSpec(w.shape, lambda i: (0, 0)),
                  pl.BlockSpec(b.shape, lambda i: (0, 0))],
        out_specs=[pl.BlockSpec((tm, SWA_QDIM), lambda i: (i, 0)), pl.BlockSpec((tm, 2 * SWA_KVDIM), lambda i: (i, 0))],
        out_shape=[jax.ShapeDtypeStruct((n, SWA_QDIM), BF16), jax.ShapeDtypeStruct((n, 2 * SWA_KVDIM), BF16)],
        compiler_params=_params("parallel"),
        name="swa_inproj",
    )(x, w, b)


def _band_tables():
    qi = np.arange(SWA_BLOCK)[:, None]
    kj = np.arange(2 * SWA_BLOCK)[None, :]
    dist = qi + SWA_BLOCK - kj
    d = np.maximum(dist, 0)
    max_exact = REL_BUCKETS // 2
    df = np.maximum(d, 1).astype(np.float32)
    large = max_exact + (np.log(df / np.float32(max_exact)) / np.float32(math.log(REL_MAX_DIST / max_exact))
                         * np.float32(REL_BUCKETS - max_exact)).astype(np.int32)
    large = np.minimum(large, REL_BUCKETS - 1)
    bucket = np.where(d < max_exact, d, large).astype(np.int32)
    band_ok = (dist >= 0) & (dist < SWA_WINDOW)
    valid = np.stack([band_ok & (kj >= SWA_BLOCK), band_ok]).astype(np.int32)
    return bucket, valid


def _bias_kernel(relb_ref, bucket_ref, valid_ref, o_ref):
    h = pl.program_id(0)
    bucket = bucket_ref[...]
    acc = jnp.zeros(bucket.shape, F32)
    for b in range(REL_BUCKETS):
        acc = jnp.where(bucket == b, relb_ref[b, h], acc)
    for t in range(2):
        o_ref[t, 0] = jnp.where(valid_ref[t] != 0, acc, NEG_BIG)


def _bias_table(rel_bias):
    bucket, valid = _band_tables()
    shape = (SWA_BLOCK, 2 * SWA_BLOCK)
    return pl.pallas_call(
        _bias_kernel,
        grid=(SWA_Q_HEADS,),
        in_specs=[pl.BlockSpec(memory_space=pltpu.SMEM), pl.BlockSpec(shape, lambda h: (0, 0)),
                  pl.BlockSpec((2,) + shape, lambda h: (0, 0, 0))],
        out_specs=pl.BlockSpec((2, 1) + shape, lambda h: (0, h, 0, 0)),
        out_shape=jax.ShapeDtypeStruct((2, SWA_Q_HEADS) + shape, F32),
        compiler_params=_params("parallel"),
        name="swa_bias_table",
    )(rel_bias, jnp.asarray(bucket), jnp.asarray(valid))


def _swa_kernel(q_ref, kvc_ref, kvp_ref, bias_ref, sink_ref, o_ref):
    kv = jnp.concatenate([kvp_ref[...], kvc_ref[...]], axis=0)
    dh = SWA_HEAD_DIM
    for kh in range(SWA_KV_HEADS):
        kb = kv[:, kh * dh:(kh + 1) * dh]
        vb = kv[:, SWA_KVDIM + kh * dh:SWA_KVDIM + (kh + 1) * dh]
        for gp in range(SWA_GROUP // 2):
            outs = []
            for h in (kh * SWA_GROUP + 2 * gp, kh * SWA_GROUP + 2 * gp + 1):
                s = _dot_nt(q_ref[:, h * dh:(h + 1) * dh], kb) + bias_ref[0, h]
                sink = sink_ref[0:1, h:h + 1]
                m = jnp.maximum(jnp.max(s, axis=-1, keepdims=True), sink)
                p = jnp.exp(s - m)
                den = jnp.sum(p, axis=-1, keepdims=True) + jnp.exp(sink - m)
                outs.append(_dot(p.astype(BF16), vb) * (1.0 / den))
            h0 = kh * SWA_GROUP + 2 * gp
            o_ref[:, h0 * dh:(h0 + 2) * dh] = jnp.concatenate(outs, axis=1).astype(o_ref.dtype)


def _swa_attention(q, kv, bias, sinks, batch, seq_len):
    n = q.shape[0]
    nb = seq_len // SWA_BLOCK
    blk = SWA_BLOCK
    return pl.pallas_call(
        _swa_kernel,
        grid=(batch, nb),
        in_specs=[pl.BlockSpec((blk, SWA_QDIM), lambda b, j: (b * nb + j, 0)),
                  pl.BlockSpec((blk, 2 * SWA_KVDIM), lambda b, j: (b * nb + j, 0)),
                  pl.BlockSpec((blk, 2 * SWA_KVDIM), lambda b, j: (b * nb + jnp.maximum(j - 1, 0), 0)),
                  pl.BlockSpec((1, SWA_Q_HEADS, blk, 2 * blk), lambda b, j: (jnp.minimum(j, 1), 0, 0, 0)),
                  pl.BlockSpec((1, LANES), lambda b, j: (0, 0))],
        out_specs=pl.BlockSpec((blk, SWA_QDIM), lambda b, j: (b * nb + j, 0)),
        out_shape=jax.ShapeDtypeStruct((n, SWA_QDIM), BF16),
        compiler_params=_params("parallel", "arbitrary"),
        name="swa_attention",
    )(q, kv, kv, bias, sinks)


def _router_kernel(x_ref, wr_ref, route_ref, cnt_ref, run_ref, *, tm):
    @pl.when(pl.program_id(0) == 0)
    def _():
        run_ref[...] = jnp.zeros(run_ref.shape, F32)

    logits = _dot3(x_ref[...], wr_ref[...])
    lane = lax.broadcasted_iota(jnp.int32, logits.shape, 1)
    lane_f = lane.astype(F32)
    lg = jnp.where(lane < N_EXPERTS, logits, NEG_BIG)
    m1 = jnp.max(lg, axis=-1, keepdims=True)
    i1 = jnp.min(jnp.where(lg == m1, lane_f, float(LANES)), axis=-1, keepdims=True)
    oh1 = lane_f == i1
    lg2 = jnp.where(oh1, NEG_BIG, lg)
    m2 = jnp.max(lg2, axis=-1, keepdims=True)
    i2 = jnp.min(jnp.where(lg2 == m2, lane_f, float(LANES)), axis=-1, keepdims=True)
    oh2 = lane_f == i2
    e = jnp.exp(m2 - m1)
    w0 = 1.0 / (1.0 + e)
    w1 = e * w0

    cnt = jnp.where(oh1, 1.0, 0.0) + jnp.where(oh2, 1.0, 0.0)
    r = lax.broadcasted_iota(jnp.int32, (tm, tm), 0)
    c = lax.broadcasted_iota(jnp.int32, (tm, tm), 1)
    before = jnp.where(r > c, 1.0, 0.0).astype(BF16)
    excl = _dot(before, cnt.astype(BF16)) + run_ref[...]
    rank0 = jnp.sum(jnp.where(oh1, excl, 0.0), axis=-1, keepdims=True)
    rank1 = jnp.sum(jnp.where(oh2, excl, 0.0), axis=-1, keepdims=True)
    run = run_ref[...] + jnp.sum(cnt, axis=0, keepdims=True)
    run_ref[...] = run
    cnt_ref[...] = run

    vals = (i1, i2, rank0, rank1, w0, w1)
    out = jnp.zeros(logits.shape, F32)
    for idx, val in enumerate(vals):
        out = jnp.where(lane == idx, val, out)
    route_ref[...] = out


def _router(x, w_router_padded):
    n = x.shape[0]
    tm = TM_ROUTER
    return pl.pallas_call(
        functools.partial(_router_kernel, tm=tm),
        grid=(n // tm,),
        in_specs=[pl.BlockSpec((tm, D_MODEL), lambda i: (i, 0)), pl.BlockSpec((D_MODEL, LANES), lambda i: (0, 0))],
        out_specs=[pl.BlockSpec((tm, LANES), lambda i: (i, 0)), pl.BlockSpec((1, LANES), lambda i: (0, 0))],
        out_shape=[jax.ShapeDtypeStruct((n, LANES), F32), jax.ShapeDtypeStruct((1, LANES), F32)],
        scratch_shapes=[pltpu.VMEM((1, LANES), F32)],
        compiler_params=_params("arbitrary"),
        name="moe_router",
    )(x, w_router_padded)


def _dispatch_kernel(pos_ref, meta_ref, x_hbm, xs_hbm, zero_ref, sem, pad_sem, *, td):
    i = pl.program_id(0)

    def row_copy(tok, dst):
        return pltpu.make_async_copy(x_hbm.at[pl.ds(tok, 1)], xs_hbm.at[pl.ds(dst, 1)], sem)

    def start(t, carry):
        tok = i * td + t
        row_copy(tok, pos_ref[0, 0, 2 * t]).start()
        row_copy(tok, pos_ref[0, 0, 2 * t + 1]).start()
        return carry

    lax.fori_loop(0, td, start, 0)

    @pl.when(i == pl.num_programs(0) - 1)
    def _():
        zero_ref[...] = jnp.zeros(zero_ref.shape, F32)

        def pad_copy(dst):
            return pltpu.make_async_copy(zero_ref.at[pl.ds(0, 1)], xs_hbm.at[pl.ds(dst, 1)], pad_sem)

        for e in range(N_EXPERTS):
            off, cnt, padded = meta_ref[0, e], meta_ref[1, e], meta_ref[2, e]

            def pad_start(r, carry, off=off):
                pad_copy(off + r).start()
                return carry

            def pad_wait(r, carry):
                pad_copy(0).wait()
                return carry

            lax.fori_loop(cnt, padded, pad_start, 0)
            lax.fori_loop(cnt, padded, pad_wait, 0)

    def wait(t, carry):
        row_copy(0, 0).wait()
        return carry

    lax.fori_loop(0, 2 * td, wait, 0)


def _dispatch(x, pos, meta, n_rows):
    n = x.shape[0]
    td = TD_DISPATCH
    pos3 = pos.reshape(n // td, 1, 2 * td)
    return pl.pallas_call(
        functools.partial(_dispatch_kernel, td=td),
        grid=(n // td,),
        in_specs=[pl.BlockSpec((1, 1, 2 * td), lambda i: (i, 0, 0), memory_space=pltpu.SMEM),
                  pl.BlockSpec(memory_space=pltpu.SMEM),
                  pl.BlockSpec(memory_space=pl.ANY)],
        out_specs=pl.BlockSpec(memory_space=pl.ANY),
        out_shape=jax.ShapeDtypeStruct((n_rows, D_MODEL), F32),
        scratch_shapes=[pltpu.VMEM((SUBLANES, D_MODEL), F32), pltpu.SemaphoreType.DMA, pltpu.SemaphoreType.DMA],
        compiler_params=_params("arbitrary"),
        name="moe_dispatch",
    )(pos3, meta, x)


def _moe_kernel(te_ref, trb_ref, nt_ref, xs_ref, wg_ref, wu_ref, wd_ref, ys_ref, acc_ref, xb_ref):
    j = pl.program_id(0)
    f = pl.program_id(1)

    @pl.when(j < nt_ref[0])
    def _():
        @pl.when(f == 0)
        def _():
            xb_ref[...] = xs_ref[...].astype(BF16)

        xb = xb_ref[...]
        h = (_silu(_dot(xb, wg_ref[0])) * _dot(xb, wu_ref[0])).astype(BF16)
        part = _dot(h, wd_ref[0])

        @pl.when(f == 0)
        def _():
            acc_ref[...] = part

        @pl.when(f != 0)
        def _():
            acc_ref[...] += part

        @pl.when(f == pl.num_programs(1) - 1)
        def _():
            ys_ref[...] = acc_ref[...]


def _moe_experts(xs, wg, wu, wd, tile_expert, tile_rowblock, n_tiles):
    n_rows = xs.shape[0]
    tm, tf = TM_MOE, TF_MOE
    max_tiles = n_rows // tm
    grid_spec = pltpu.PrefetchScalarGridSpec(
        num_scalar_prefetch=3,
        grid=(max_tiles, EXPERT_DIM // tf),
        in_specs=[pl.BlockSpec((tm, D_MODEL), lambda j, f, te, trb, nt: (trb[j], 0)),
                  pl.BlockSpec((1, D_MODEL, tf), lambda j, f, te, trb, nt: (te[j], 0, f)),
                  pl.BlockSpec((1, D_MODEL, tf), lambda j, f, te, trb, nt: (te[j], 0, f)),
                  pl.BlockSpec((1, tf, D_MODEL), lambda j, f, te, trb, nt: (te[j], f, 0))],
        out_specs=pl.BlockSpec((tm, D_MODEL), lambda j, f, te, trb, nt: (trb[j], 0)),
        scratch_shapes=[pltpu.VMEM((tm, D_MODEL), F32), pltpu.VMEM((tm, D_MODEL), BF16)],
    )
    return pl.pallas_call(
        _moe_kernel,
        grid_spec=grid_spec,
        out_shape=jax.ShapeDtypeStruct((n_rows, D_MODEL), F32),
        compiler_params=_params("arbitrary", "arbitrary"),
        name="moe_experts",
    )(tile_expert, tile_rowblock, n_tiles, xs, wg, wu, wd)


def _combine_kernel(pos_ref, x_ref, route_ref, ys_hbm, g_ref, b_ref, o_ref, y0_ref, y1_ref, sem, *, tc):
    def row_copy(src, dst_ref, t):
        return pltpu.make_async_copy(ys_hbm.at[pl.ds(src, 1)], dst_ref.at[pl.ds(t, 1)], sem)

    def start(t, carry):
        row_copy(pos_ref[0, 0, 2 * t], y0_ref, t).start()
        row_copy(pos_ref[0, 0, 2 * t + 1], y1_ref, t).start()
        return carry

    lax.fori_loop(0, tc, start, 0)

    def wait(t, carry):
        row_copy(0, y0_ref, 0).wait()
        return carry

    lax.fori_loop(0, 2 * tc, wait, 0)

    route = route_ref[...]
    f = route[:, 4:5] * y0_ref[...] + route[:, 5:6] * y1_ref[...]
    o_ref[...] = _layer_norm(ALPHA * x_ref[...] + f, g_ref[...], b_ref[...])


def _combine_ln(x, route, pos, ys, g, b):
    n = x.shape[0]
    tc = TC_COMBINE
    pos3 = pos.reshape(n // tc, 1, 2 * tc)
    return pl.pallas_call(
        functools.partial(_combine_kernel, tc=tc),
        grid=(n // tc,),
        in_specs=[pl.BlockSpec((1, 1, 2 * tc), lambda i: (i, 0, 0), memory_space=pltpu.SMEM),
                  pl.BlockSpec((tc, D_MODEL), lambda i: (i, 0)),
                  pl.BlockSpec((tc, LANES), lambda i: (i, 0)),
                  pl.BlockSpec(memory_space=pl.ANY),
                  pl.BlockSpec((1, D_MODEL), lambda i: (0, 0)), pl.BlockSpec((1, D_MODEL), lambda i: (0, 0))],
        out_specs=pl.BlockSpec((tc, D_MODEL), lambda i: (i, 0)),
        out_shape=jax.ShapeDtypeStruct((n, D_MODEL), F32),
        scratch_shapes=[pltpu.VMEM((tc, D_MODEL), F32), pltpu.VMEM((tc, D_MODEL), F32), pltpu.SemaphoreType.DMA],
        compiler_params=_params("arbitrary"),
        name="moe_combine_ln",
    )(pos3, x, route, ys, g, b)


def _moe_layer(x, w_router, wg, wu, wd, g, b):
    n = x.shape[0]
    tm = TM_MOE
    wr = jnp.pad(w_router, ((0, 0), (0, LANES - N_EXPERTS)))
    route, counts_f = _router(x, wr)

    counts = counts_f[0, :N_EXPERTS].astype(jnp.int32)
    padded = ((counts + tm - 1) // tm) * tm
    offs = jnp.cumsum(padded) - padded
    eids = jnp.arange(N_EXPERTS, dtype=jnp.int32)
    e01 = route[:, 0:2].astype(jnp.int32)
    rank01 = route[:, 2:4].astype(jnp.int32)
    base = jnp.sum(jnp.where(e01[:, :, None] == eids[None, None, :], offs[None, None, :], 0), axis=-1)
    pos = (base + rank01).reshape(-1)
    meta = jnp.stack([offs, counts, padded]).astype(jnp.int32)

    n_rows = 2 * n + N_EXPERTS * tm
    max_tiles = n_rows // tm
    tile_end = jnp.cumsum(padded // tm)
    n_tiles = tile_end[-1]
    tj = jnp.minimum(jnp.arange(max_tiles, dtype=jnp.int32), n_tiles - 1)
    tile_expert = jnp.sum((tj[:, None] >= tile_end[None, :]).astype(jnp.int32), axis=-1)
    tile_expert = jnp.minimum(tile_expert, N_EXPERTS - 1).astype(jnp.int32)

    xs = _dispatch(x, pos, meta, n_rows)
    ys = _moe_experts(xs, wg, wu, wd, tile_expert, tj.astype(jnp.int32), n_tiles.reshape(1).astype(jnp.int32))
    return _combine_ln(x, route, pos, ys, g, b)


def kernel(x, a_w_in, a_conv_w, a_a_log, a_dt_bias, a_norm_w, a_w_out, b_w_in, b_b_in, b_sinks, b_w_out, rel_bias,
           ffn_w_gate, ffn_w_up, ffn_w_down, moe_router, moe_w_gate, moe_w_up, moe_w_down, ln_g, ln_b):
    batch, seq_len, _ = x.shape
    n = batch * seq_len
    x0 = x.reshape(n, D_MODEL)
    ln_g = ln_g.reshape(DEPTH, 2, 1, D_MODEL)
    ln_b = ln_b.reshape(DEPTH, 2, 1, D_MODEL)
    main_cols = GDN_QKV + GDN_VDIM

    w_in = a_w_in[0]
    w_main = w_in[:, :main_cols].astype(BF16)
    w_ba = jnp.pad(w_in[:, main_cols:], ((0, 0), (0, LANES - 2 * GDN_V_HEADS)))
    pad_gate = lambda p: jnp.pad(p.reshape(1, GDN_V_HEADS), ((0, 0), (GDN_V_HEADS, LANES - 2 * GDN_V_HEADS)))
    gate_params = jnp.concatenate([pad_gate(a_a_log[0]), pad_gate(a_dt_bias[0])], axis=0)
    q, k, v, zs, gb = _gdn_inproj(x0, w_main, w_ba, a_conv_w[0], gate_params, seq_len)
    o = _gdn_chunk(q, k, v, zs, gb, a_norm_w[0].reshape(1, GDN_HEAD), batch, seq_len)
    x1 = _proj_ln(o, a_w_out[0].astype(BF16), x0, ln_g[0, 0], ln_b[0, 0], "gdn_outproj_ln")
    x2 = _ffn_ln(x1, ffn_w_gate[0].astype(BF16), ffn_w_up[0].astype(BF16), ffn_w_down[0].astype(BF16),
                 ln_g[0, 1], ln_b[0, 1])

    qa, kva = _swa_inproj(x2, b_w_in[0].astype(BF16), b_b_in[0].reshape(1, -1))
    bias = _bias_table(rel_bias)
    sinks = jnp.pad(b_sinks[0].reshape(1, SWA_Q_HEADS), ((0, 0), (0, LANES - SWA_Q_HEADS)))
    oa = _swa_attention(qa, kva, bias, sinks, batch, seq_len)
    x3 = _proj_ln(oa, b_w_out[0].astype(BF16), x2, ln_g[1, 0], ln_b[1, 0], "swa_outproj_ln")
    x4 = _moe_layer(x3, moe_router[0], moe_w_gate[0].astype(BF16), moe_w_up[0].astype(BF16),
                    moe_w_down[0].astype(BF16), ln_g[1, 1], ln_b[1, 1])
    return x4.reshape(batch, seq_len, D_MODEL)
```

```python
import functools
import math

import numpy as np
import jax
import jax.numpy as jnp
from jax import lax
from jax.experimental import pallas as pl
from jax.experimental.pallas import tpu as pltpu

F32 = jnp.float32
BF16 = jnp.bfloat16

D_MODEL = 1024
DEPTH = 2
ALPHA = (2.0 * DEPTH) ** 0.25
LN_EPS = 1e-5

GDN_K_HEADS = 4
GDN_V_HEADS = 8
GDN_HEAD = 128
GDN_KDIM = GDN_K_HEADS * GDN_HEAD
GDN_VDIM = GDN_V_HEADS * GDN_HEAD
GDN_CONV = 4
GDN_CHUNK = 64
GDN_QKV = 2 * GDN_KDIM + GDN_VDIM
GDN_EPS = 1e-6

SWA_Q_HEADS = 16
SWA_KV_HEADS = 2
SWA_GROUP = SWA_Q_HEADS // SWA_KV_HEADS
SWA_HEAD_DIM = 64
SWA_WINDOW = 128
SWA_BLOCK = 128
SWA_QDIM = SWA_Q_HEADS * SWA_HEAD_DIM
SWA_KVDIM = SWA_KV_HEADS * SWA_HEAD_DIM
REL_BUCKETS = 32
REL_MAX_DIST = 128

FFN_DIM = 2816
N_EXPERTS = 8
EXPERT_DIM = 3584

LANES = 128
SUBLANES = 8
NEG_BIG = -1e30
VMEM_LIMIT = 56 * 1024 * 1024

TM_GDN_IN = 256
TM_PROJ = 512
TM_FFN = 512
TF_FFN = 1408
TM_SWA_IN = 512
TM_ROUTER = 512
TM_MOE = 512
TF_MOE = 512
TD_DISPATCH = 512
TC_COMBINE = 256
DMA_UNROLL = 8


def _params(*sem):
    return pltpu.CompilerParams(dimension_semantics=sem, vmem_limit_bytes=VMEM_LIMIT)


def _dot(a, b):
    return jnp.dot(a, b, preferred_element_type=F32)


def _dot_nt(a, b):
    return lax.dot_general(a, b, (((1,), (1,)), ((), ())), preferred_element_type=F32)


def _dot_tn(a, b):
    return lax.dot_general(a, b, (((0,), (0,)), ((), ())), preferred_element_type=F32)


def _split(x):
    hi = x.astype(BF16)
    lo = (x - hi.astype(F32)).astype(BF16)
    return hi, lo


def _dot3(a, b):
    ah, al = _split(a)
    bh, bl = _split(b)
    return _dot(ah, bh) + (_dot(ah, bl) + _dot(al, bh))


def _silu(x):
    return x * jax.nn.sigmoid(x)


def _layer_norm(y, g, b):
    mu = jnp.mean(y, axis=-1, keepdims=True)
    yc = y - mu
    var = jnp.mean(yc * yc, axis=-1, keepdims=True)
    return yc * lax.rsqrt(var + LN_EPS) * g + b


def _gdn_inproj_kernel(x_ref, w_ref, wba_ref, convw_ref, gp_ref,
                       q_ref, k_ref, v_ref, z_ref, gb_ref, ext_ref, *, tm, tiles_per_seq):
    i = pl.program_id(0)
    x = x_ref[...]
    xb = x.astype(BF16)

    @pl.when(i % tiles_per_seq == 0)
    def _():
        ext_ref[0:SUBLANES, :] = jnp.zeros((SUBLANES, GDN_QKV), F32)

    @pl.when(i % tiles_per_seq != 0)
    def _():
        ext_ref[0:SUBLANES, :] = ext_ref[tm:tm + SUBLANES, :]

    ext_ref[SUBLANES:, :] = _dot(xb, w_ref[:, :GDN_QKV])
    z_ref[...] = _silu(_dot(xb, w_ref[:, GDN_QKV:]))

    n_chunks = GDN_QKV // LANES
    for c in range(n_chunks):
        cs = slice(c * LANES, (c + 1) * LANES)
        acc = convw_ref[GDN_CONV - 1:GDN_CONV, cs] * ext_ref[SUBLANES:SUBLANES + tm, cs]
        for j in range(GDN_CONV - 1):
            off = SUBLANES - (GDN_CONV - 1) + j
            acc = acc + convw_ref[j:j + 1, cs] * ext_ref[off:off + tm, cs]
        y = _silu(acc)
        if c < 2 * GDN_K_HEADS:
            y = y * lax.rsqrt(jnp.sum(y * y, axis=-1, keepdims=True) + GDN_EPS)
            if c < GDN_K_HEADS:
                q_ref[:, cs] = y * (GDN_HEAD ** -0.5)
            else:
                k_ref[:, (c - GDN_K_HEADS) * LANES:(c - GDN_K_HEADS + 1) * LANES] = y
        else:
            cv = c - 2 * GDN_K_HEADS
            v_ref[:, cv * LANES:(cv + 1) * LANES] = y

    ba = _dot3(x, wba_ref[...])
    lane = lax.broadcasted_iota(jnp.int32, ba.shape, 1)
    sp = ba + gp_ref[1:2, :]
    softplus = jnp.maximum(sp, 0.0) + jnp.log(1.0 + jnp.exp(-jnp.abs(sp)))
    g = -jnp.exp(gp_ref[0:1, :]) * softplus
    gb_ref[...] = jnp.where(lane < GDN_V_HEADS, jax.nn.sigmoid(ba), g)


def _gdn_inproj(x2d, w_main, w_ba, conv_w, gate_params, seq_len):
    n = x2d.shape[0]
    tm = TM_GDN_IN
    kern = functools.partial(_gdn_inproj_kernel, tm=tm, tiles_per_seq=seq_len // tm)
    full = lambda shape: pl.BlockSpec(shape, lambda i: (0,) * len(shape))
    rows = lambda width: pl.BlockSpec((tm, width), lambda i: (i, 0))
    return pl.pallas_call(
        kern,
        grid=(n // tm,),
        in_specs=[rows(D_MODEL), full(w_main.shape), full(w_ba.shape), full(conv_w.shape), full(gate_params.shape)],
        out_specs=[rows(GDN_KDIM), rows(GDN_KDIM), rows(GDN_VDIM), rows(GDN_VDIM), rows(LANES)],
        out_shape=[jax.ShapeDtypeStruct((n, GDN_KDIM), F32), jax.ShapeDtypeStruct((n, GDN_KDIM), F32),
                   jax.ShapeDtypeStruct((n, GDN_VDIM), F32), jax.ShapeDtypeStruct((n, GDN_VDIM), F32),
                   jax.ShapeDtypeStruct((n, LANES), F32)],
        scratch_shapes=[pltpu.VMEM((tm + SUBLANES, GDN_QKV), F32)],
        compiler_params=_params("arbitrary"),
        name="gdn_inproj",
    )(x2d, w_main, w_ba, conv_w, gate_params)


def _unit_lower_inverse(a, row, col):
    f32_eye = jnp.where(row == col, 1.0, 0.0).astype(F32)
    rb, cb = row >> 3, col >> 3
    d = jnp.where(rb == cb, a, 0.0)
    x = f32_eye - d
    p = _dot3(d, d)
    x = x + _dot3(x, p)
    p = _dot3(p, p)
    x = x + _dot3(x, p)
    size = SUBLANES
    while size < a.shape[0]:
        shift = int(math.log2(size))
        rbl, cbl = row >> shift, col >> shift
        below = ((rbl & 1) == 1) & (cbl == rbl - 1)
        l = jnp.where(below, a, 0.0)
        x = x - _dot3(_dot3(x, l), x)
        size *= 2
    return x


def _gdn_chunk_kernel(q_ref, k_ref, v_ref, z_ref, gb_ref, nw_ref, o_ref, s_ref):
    c = GDN_CHUNK

    @pl.when(pl.program_id(1) == 0)
    def _():
        s_ref[...] = jnp.zeros(s_ref.shape, F32)

    row = lax.broadcasted_iota(jnp.int32, (c, c), 0)
    col = lax.broadcasted_iota(jnp.int32, (c, c), 1)
    causal = row >= col
    strict = row > col
    tril = jnp.where(causal, 1.0, 0.0).astype(BF16)

    gb = gb_ref[...]
    g_hi, g_lo = _split(gb)
    gc = _dot(tril, g_hi) + _dot(tril, g_lo)
    gc_t = gc.T
    eg = jnp.exp(gc)
    g_last = gc[c - 1:c, :]
    e_last = jnp.exp(g_last)
    e_rest = jnp.exp(g_last - gc)

    for h in range(GDN_V_HEADS):
        kh = h // (GDN_V_HEADS // GDN_K_HEADS)
        ks = slice(kh * GDN_HEAD, (kh + 1) * GDN_HEAD)
        vs = slice(h * GDN_HEAD, (h + 1) * GDN_HEAD)
        gl = GDN_V_HEADS + h
        q = q_ref[:, ks]
        k = k_ref[:, ks]
        v = v_ref[:, vs]
        beta = gb[:, h:h + 1]
        gc_c = gc[:, gl:gl + 1]
        gc_r = gc_t[gl:gl + 1, :]
        eg_c = eg[:, gl:gl + 1]

        decay = jnp.exp(jnp.where(causal, gc_c - gc_r, NEG_BIG))
        k_beta = k * beta
        kq = jnp.concatenate([k_beta, q], axis=0).astype(BF16)
        gram = _dot_nt(kq, k.astype(BF16))
        a_kk = jnp.where(strict, gram[:c] * decay, 0.0)
        a_qk = gram[c:] * decay
        t_inv = _unit_lower_inverse(a_kk, row, col)

        rhs = jnp.concatenate([v * beta, k_beta * eg_c], axis=1).astype(BF16)
        uw = _dot(t_inv.astype(BF16), rhs)
        u = uw[:, :GDN_HEAD]
        w = uw[:, GDN_HEAD:]

        s = s_ref[h]
        s_b = s.astype(BF16)
        wq = jnp.concatenate([w, q * eg_c], axis=0).astype(BF16)
        ws_qs = _dot(wq, s_b)
        v_new = u - ws_qs[:c]
        v_new_b = v_new.astype(BF16)
        o = ws_qs[c:] + _dot(a_qk.astype(BF16), v_new_b)
        k_dec = (k * e_rest[:, gl:gl + 1]).astype(BF16)
        s_ref[h] = s * e_last[:, gl:gl + 1] + _dot_tn(k_dec, v_new_b)

        o = o * lax.rsqrt(jnp.mean(o * o, axis=-1, keepdims=True) + GDN_EPS) * nw_ref[...]
        o_ref[:, vs] = (o * z_ref[:, vs]).astype(o_ref.dtype)


def _gdn_chunk(q, k, v, zs, gb, norm_w, batch, seq_len):
    n = q.shape[0]
    c = GDN_CHUNK
    nc = seq_len // c
    rows = lambda width: pl.BlockSpec((c, width), lambda b, j: (b * nc + j, 0))
    return pl.pallas_call(
        _gdn_chunk_kernel,
        grid=(batch, nc),
        in_specs=[rows(GDN_KDIM), rows(GDN_KDIM), rows(GDN_VDIM), rows(GDN_VDIM), rows(LANES),
                  pl.BlockSpec((1, GDN_HEAD), lambda b, j: (0, 0))],
        out_specs=rows(GDN_VDIM),
        out_shape=jax.ShapeDtypeStruct((n, GDN_VDIM), BF16),
        scratch_shapes=[pltpu.VMEM((GDN_V_HEADS, GDN_HEAD, GDN_HEAD), F32)],
        compiler_params=_params("arbitrary", "arbitrary"),
        name="gdn_chunk",
    )(q, k, v, zs, gb, norm_w)


def _proj_ln_kernel(a_ref, w_ref, r_ref, g_ref, b_ref, o_ref):
    y = ALPHA * r_ref[...] + _dot(a_ref[...], w_ref[...])
    o_ref[...] = _layer_norm(y, g_ref[...], b_ref[...])


def _proj_ln(a, w, res, g, b, name):
    n, kdim = a.shape
    tm = TM_PROJ
    return pl.pallas_call(
        _proj_ln_kernel,
        grid=(n // tm,),
        in_specs=[pl.BlockSpec((tm, kdim), lambda i: (i, 0)), pl.BlockSpec(w.shape, lambda i: (0, 0)),
                  pl.BlockSpec((tm, D_MODEL), lambda i: (i, 0)),
                  pl.BlockSpec((1, D_MODEL), lambda i: (0, 0)), pl.BlockSpec((1, D_MODEL), lambda i: (0, 0))],
        out_specs=pl.BlockSpec((tm, D_MODEL), lambda i: (i, 0)),
        out_shape=jax.ShapeDtypeStruct((n, D_MODEL), F32),
        compiler_params=_params("parallel"),
        name=name,
    )(a, w, res, g, b)


def _ffn_kernel(x_ref, wg_ref, wu_ref, wd_ref, g_ref, b_ref, o_ref, acc_ref, xb_ref):
    f = pl.program_id(1)

    @pl.when(f == 0)
    def _():
        xb_ref[...] = x_ref[...].astype(BF16)

    xb = xb_ref[...]
    h = (_silu(_dot(xb, wg_ref[...])) * _dot(xb, wu_ref[...])).astype(BF16)
    part = _dot(h, wd_ref[...])

    @pl.when(f == 0)
    def _():
        acc_ref[...] = part

    @pl.when(f != 0)
    def _():
        acc_ref[...] += part

    @pl.when(f == pl.num_programs(1) - 1)
    def _():
        o_ref[...] = _layer_norm(ALPHA * x_ref[...] + acc_ref[...], g_ref[...], b_ref[...])


def _ffn_ln(x, wg, wu, wd, g, b):
    n = x.shape[0]
    tm, tf = TM_FFN, TF_FFN
    return pl.pallas_call(
        _ffn_kernel,
        grid=(n // tm, FFN_DIM // tf),
        in_specs=[pl.BlockSpec((tm, D_MODEL), lambda i, f: (i, 0)),
                  pl.BlockSpec((D_MODEL, tf), lambda i, f: (0, f)),
                  pl.BlockSpec((D_MODEL, tf), lambda i, f: (0, f)),
                  pl.BlockSpec((tf, D_MODEL), lambda i, f: (f, 0)),
                  pl.BlockSpec((1, D_MODEL), lambda i, f: (0, 0)), pl.BlockSpec((1, D_MODEL), lambda i, f: (0, 0))],
        out_specs=pl.BlockSpec((tm, D_MODEL), lambda i, f: (i, 0)),
        out_shape=jax.ShapeDtypeStruct((n, D_MODEL), F32),
        scratch_shapes=[pltpu.VMEM((tm, D_MODEL), F32), pltpu.VMEM((tm, D_MODEL), BF16)],
        compiler_params=_params("parallel", "arbitrary"),
        name="ffn_ln",
    )(x, wg, wu, wd, g, b)


def _swa_inproj_kernel(x_ref, w_ref, b_ref, q_ref, kv_ref):
    p = _dot(x_ref[...].astype(BF16), w_ref[...]) + b_ref[...]
    q_ref[...] = (p[:, :SWA_QDIM] * (SWA_HEAD_DIM ** -0.5)).astype(BF16)
    kv_ref[...] = p[:, SWA_QDIM:].astype(BF16)


def _swa_inproj(x, w, b):
    n = x.shape[0]
    tm = TM_SWA_IN
    return pl.pallas_call(
        _swa_inproj_kernel,
        grid=(n // tm,),
        in_specs=[pl.BlockSpec((tm, D_MODEL), lambda i: (i, 0)), pl.BlockSpec(w.shape, lambda i: (0, 0)),
                  pl.BlockSpec(b.shape, lambda i: (0, 0))],
        out_specs=[pl.BlockSpec((tm, SWA_QDIM), lambda i: (i, 0)), pl.BlockSpec((tm, 2 * SWA_KVDIM), lambda i: (i, 0))],
        out_shape=[jax.ShapeDtypeStruct((n, SWA_QDIM), BF16), jax.ShapeDtypeStruct((n, 2 * SWA_KVDIM), BF16)],
        compiler_params=_params("parallel"),
        name="swa_inproj",
    )(x, w, b)


def _band_tables():
    qi = np.arange(SWA_BLOCK)[:, None]
    kj = np.arange(2 * SWA_BLOCK)[None, :]
    dist = qi + SWA_BLOCK - kj
    d = np.maximum(dist, 0)
    max_exact = REL_BUCKETS // 2
    df = np.maximum(d, 1).astype(np.float32)
    large = max_exact + (np.log(df / np.float32(max_exact)) / np.float32(math.log(REL_MAX_DIST / max_exact))
                         * np.float32(REL_BUCKETS - max_exact)).astype(np.int32)
    large = np.minimum(large, REL_BUCKETS - 1)
    bucket = np.where(d < max_exact, d, large).astype(np.int32)
    band_ok = (dist >= 0) & (dist < SWA_WINDOW)
    valid = np.stack([band_ok & (kj >= SWA_BLOCK), band_ok]).astype(np.int32)
    return bucket, valid


def _bias_kernel(relb_ref, bucket_ref, valid_ref, o_ref):
    h = pl.program_id(0)
    bucket = bucket_ref[...]
    acc = jnp.zeros(bucket.shape, F32)
    for b in range(REL_BUCKETS):
        acc = jnp.where(bucket == b, relb_ref[b, h], acc)
    for t in range(2):
        o_ref[t, 0] = jnp.where(valid_ref[t] != 0, acc, NEG_BIG)


def _bias_table(rel_bias):
    bucket, valid = _band_tables()
    shape = (SWA_BLOCK, 2 * SWA_BLOCK)
    return pl.pallas_call(
        _bias_kernel,
        grid=(SWA_Q_HEADS,),
        in_specs=[pl.BlockSpec(memory_space=pltpu.SMEM), pl.BlockSpec(shape, lambda h: (0, 0)),
                  pl.BlockSpec((2,) + shape, lambda h: (0, 0, 0))],
        out_specs=pl.BlockSpec((2, 1) + shape, lambda h: (0, h, 0, 0)),
        out_shape=jax.ShapeDtypeStruct((2, SWA_Q_HEADS) + shape, F32),
        compiler_params=_params("parallel"),
        name="swa_bias_table",
    )(rel_bias, jnp.asarray(bucket), jnp.asarray(valid))


def _swa_kernel(q_ref, kvc_ref, kvp_ref, bias_ref, sink_ref, o_ref):
    kv = jnp.concatenate([kvp_ref[...], kvc_ref[...]], axis=0)
    dh = SWA_HEAD_DIM
    for kh in range(SWA_KV_HEADS):
        kb = kv[:, kh * dh:(kh + 1) * dh]
        vb = kv[:, SWA_KVDIM + kh * dh:SWA_KVDIM + (kh + 1) * dh]
        for gp in range(SWA_GROUP // 2):
            outs = []
            for h in (kh * SWA_GROUP + 2 * gp, kh * SWA_GROUP + 2 * gp + 1):
                s = _dot_nt(q_ref[:, h * dh:(h + 1) * dh], kb) + bias_ref[0, h]
                sink = sink_ref[0:1, h:h + 1]
                m = jnp.maximum(jnp.max(s, axis=-1, keepdims=True), sink)
                p = jnp.exp(s - m)
                den = jnp.sum(p, axis=-1, keepdims=True) + jnp.exp(sink - m)
                outs.append(_dot(p.astype(BF16), vb) * (1.0 / den))
            h0 = kh * SWA_GROUP + 2 * gp
            o_ref[:, h0 * dh:(h0 + 2) * dh] = jnp.concatenate(outs, axis=1).astype(o_ref.dtype)


def _swa_attention(q, kv, bias, sinks, batch, seq_len):
    n = q.shape[0]
    nb = seq_len // SWA_BLOCK
    blk = SWA_BLOCK
    return pl.pallas_call(
        _swa_kernel,
        grid=(batch, nb),
        in_specs=[pl.BlockSpec((blk, SWA_QDIM), lambda b, j: (b * nb + j, 0)),
                  pl.BlockSpec((blk, 2 * SWA_KVDIM), lambda b, j: (b * nb + j, 0)),
                  pl.BlockSpec((blk, 2 * SWA_KVDIM), lambda b, j: (b * nb + jnp.maximum(j - 1, 0), 0)),
                  pl.BlockSpec((1, SWA_Q_HEADS, blk, 2 * blk), lambda b, j: (jnp.minimum(j, 1), 0, 0, 0)),
                  pl.BlockSpec((1, LANES), lambda b, j: (0, 0))],
        out_specs=pl.BlockSpec((blk, SWA_QDIM), lambda b, j: (b * nb + j, 0)),
        out_shape=jax.ShapeDtypeStruct((n, SWA_QDIM), BF16),
        compiler_params=_params("parallel", "arbitrary"),
        name="swa_attention",
    )(q, kv, kv, bias, sinks)


def _router_kernel(x_ref, wr_ref, route_ref, cnt_ref, run_ref, *, tm):
    @pl.when(pl.program_id(0) == 0)
    def _():
        run_ref[...] = jnp.zeros(run_ref.shape, F32)

    logits = _dot3(x_ref[...], wr_ref[...])
    lane = lax.broadcasted_iota(jnp.int32, logits.shape, 1)
    lane_f = lane.astype(F32)
    lg = jnp.where(lane < N_EXPERTS, logits, NEG_BIG)
    m1 = jnp.max(lg, axis=-1, keepdims=True)
    i1 = jnp.min(jnp.where(lg == m1, lane_f, float(LANES)), axis=-1, keepdims=True)
    oh1 = lane_f == i1
    lg2 = jnp.where(oh1, NEG_BIG, lg)
    m2 = jnp.max(lg2, axis=-1, keepdims=True)
    i2 = jnp.min(jnp.where(lg2 == m2, lane_f, float(LANES)), axis=-1, keepdims=True)
    oh2 = lane_f == i2
    e = jnp.exp(m2 - m1)
    w0 = 1.0 / (1.0 + e)
    w1 = e * w0

    cnt = jnp.where(oh1, 1.0, 0.0) + jnp.where(oh2, 1.0, 0.0)
    r = lax.broadcasted_iota(jnp.int32, (tm, tm), 0)
    c = lax.broadcasted_iota(jnp.int32, (tm, tm), 1)
    before = jnp.where(r > c, 1.0, 0.0).astype(BF16)
    excl = _dot(before, cnt.astype(BF16)) + run_ref[...]
    rank0 = jnp.sum(jnp.where(oh1, excl, 0.0), axis=-1, keepdims=True)
    rank1 = jnp.sum(jnp.where(oh2, excl, 0.0), axis=-1, keepdims=True)
    run = run_ref[...] + jnp.sum(cnt, axis=0, keepdims=True)
    run_ref[...] = run
    cnt_ref[...] = run

    vals = (i1, i2, rank0, rank1, w0, w1)
    out = jnp.zeros(logits.shape, F32)
    for idx, val in enumerate(vals):
        out = jnp.where(lane == idx, val, out)
    route_ref[...] = out


def _router(x, w_router_padded):
    n = x.shape[0]
    tm = TM_ROUTER
    return pl.pallas_call(
        functools.partial(_router_kernel, tm=tm),
        grid=(n // tm,),
        in_specs=[pl.BlockSpec((tm, D_MODEL), lambda i: (i, 0)), pl.BlockSpec((D_MODEL, LANES), lambda i: (0, 0))],
        out_specs=[pl.BlockSpec((tm, LANES), lambda i: (i, 0)), pl.BlockSpec((1, LANES), lambda i: (0, 0))],
        out_shape=[jax.ShapeDtypeStruct((n, LANES), F32), jax.ShapeDtypeStruct((1, LANES), F32)],
        scratch_shapes=[pltpu.VMEM((1, LANES), F32)],
        compiler_params=_params("arbitrary"),
        name="moe_router",
    )(x, w_router_padded)


def _dispatch_kernel(pos_ref, meta_ref, x_ref, xs_hbm, zero_ref, sem, pad_sem, *, td):
    i = pl.program_id(0)

    def row_copy(t, dst):
        return pltpu.make_async_copy(x_ref.at[pl.ds(t, 1)], xs_hbm.at[pl.ds(dst, 1)], sem)

    def start(t, carry):
        row_copy(t, pos_ref[0, 0, 2 * t]).start()
        row_copy(t, pos_ref[0, 0, 2 * t + 1]).start(priority=1)
        return carry

    lax.fori_loop(0, td, start, 0, unroll=DMA_UNROLL)

    @pl.when(i == pl.num_programs(0) - 1)
    def _():
        zero_ref[...] = jnp.zeros(zero_ref.shape, F32)

        def pad_copy(dst):
            return pltpu.make_async_copy(zero_ref.at[pl.ds(0, 1)], xs_hbm.at[pl.ds(dst, 1)], pad_sem)

        for e in range(N_EXPERTS):
            off, cnt, padded = meta_ref[0, e], meta_ref[1, e], meta_ref[2, e]

            def pad_start(r, carry, off=off):
                pad_copy(off + r).start()
                return carry

            def pad_wait(r, carry):
                pad_copy(0).wait()
                return carry

            lax.fori_loop(cnt, padded, pad_start, 0)
            lax.fori_loop(cnt, padded, pad_wait, 0)

    for _ in range(2):
        pltpu.make_async_copy(x_ref, xs_hbm.at[pl.ds(0, td)], sem).wait()


def _dispatch(x, pos, meta, n_rows):
    n = x.shape[0]
    td = TD_DISPATCH
    pos3 = pos.reshape(n // td, 1, 2 * td)
    return pl.pallas_call(
        functools.partial(_dispatch_kernel, td=td),
        grid=(n // td,),
        in_specs=[pl.BlockSpec((1, 1, 2 * td), lambda i: (i, 0, 0), memory_space=pltpu.SMEM),
                  pl.BlockSpec(memory_space=pltpu.SMEM),
                  pl.BlockSpec((td, D_MODEL), lambda i: (i, 0))],
        out_specs=pl.BlockSpec(memory_space=pl.ANY),
        out_shape=jax.ShapeDtypeStruct((n_rows, D_MODEL), F32),
        scratch_shapes=[pltpu.VMEM((SUBLANES, D_MODEL), F32), pltpu.SemaphoreType.DMA, pltpu.SemaphoreType.DMA],
        compiler_params=_params("arbitrary"),
        name="moe_dispatch",
    )(pos3, meta, x)


def _moe_kernel(te_ref, trb_ref, nt_ref, xs_ref, wg_ref, wu_ref, wd_ref, ys_ref, acc_ref, xb_ref):
    j = pl.program_id(0)
    f = pl.program_id(1)

    @pl.when(j < nt_ref[0])
    def _():
        @pl.when(f == 0)
        def _():
            xb_ref[...] = xs_ref[...].astype(BF16)

        xb = xb_ref[...]
        h = (_silu(_dot(xb, wg_ref[0])) * _dot(xb, wu_ref[0])).astype(BF16)
        part = _dot(h, wd_ref[0])

        @pl.when(f == 0)
        def _():
            acc_ref[...] = part

        @pl.when(f != 0)
        def _():
            acc_ref[...] += part

        @pl.when(f == pl.num_programs(1) - 1)
        def _():
            ys_ref[...] = acc_ref[...]


def _moe_experts(xs, wg, wu, wd, tile_expert, tile_rowblock, n_tiles):
    n_rows = xs.shape[0]
    tm, tf = TM_MOE, TF_MOE
    max_tiles = n_rows // tm
    grid_spec = pltpu.PrefetchScalarGridSpec(
        num_scalar_prefetch=3,
        grid=(max_tiles, EXPERT_DIM // tf),
        in_specs=[pl.BlockSpec((tm, D_MODEL), lambda j, f, te, trb, nt: (trb[j], 0)),
                  pl.BlockSpec((1, D_MODEL, tf), lambda j, f, te, trb, nt: (te[j], 0, f)),
                  pl.BlockSpec((1, D_MODEL, tf), lambda j, f, te, trb, nt: (te[j], 0, f)),
                  pl.BlockSpec((1, tf, D_MODEL), lambda j, f, te, trb, nt: (te[j], f, 0))],
        out_specs=pl.BlockSpec((tm, D_MODEL), lambda j, f, te, trb, nt: (trb[j], 0)),
        scratch_shapes=[pltpu.VMEM((tm, D_MODEL), F32), pltpu.VMEM((tm, D_MODEL), BF16)],
    )
    return pl.pallas_call(
        _moe_kernel,
        grid_spec=grid_spec,
        out_shape=jax.ShapeDtypeStruct((n_rows, D_MODEL), F32),
        compiler_params=_params("arbitrary", "arbitrary"),
        name="moe_experts",
    )(tile_expert, tile_rowblock, n_tiles, xs, wg, wu, wd)


def _combine_kernel(pos_ref, posn_ref, x_ref, route_ref, ys_hbm, g_ref, b_ref, o_ref, y_ref, sem, *, tc):
    i = pl.program_id(0)
    slot = i % 2

    def issue(p_ref, s):
        def start(t, carry):
            for kk in range(2):
                pltpu.make_async_copy(ys_hbm.at[pl.ds(p_ref[0, 0, 2 * t + kk], 1)],
                                      y_ref.at[s, kk, pl.ds(t, 1)], sem.at[s]).start(priority=kk)
            return carry

        lax.fori_loop(0, tc, start, 0, unroll=DMA_UNROLL)

    @pl.when(i == 0)
    def _():
        issue(pos_ref, 0)

    @pl.when(i + 1 < pl.num_programs(0))
    def _():
        issue(posn_ref, 1 - slot)

    for kk in range(2):
        pltpu.make_async_copy(ys_hbm.at[pl.ds(0, tc)], y_ref.at[slot, kk], sem.at[slot]).wait()

    route = route_ref[...]
    f = route[:, 4:5] * y_ref[slot, 0] + route[:, 5:6] * y_ref[slot, 1]
    o_ref[...] = _layer_norm(ALPHA * x_ref[...] + f, g_ref[...], b_ref[...])


def _combine_ln(x, route, pos, ys, g, b):
    n = x.shape[0]
    tc = TC_COMBINE
    nt = n // tc
    pos3 = pos.reshape(nt, 1, 2 * tc)
    return pl.pallas_call(
        functools.partial(_combine_kernel, tc=tc),
        grid=(nt,),
        in_specs=[pl.BlockSpec((1, 1, 2 * tc), lambda i: (i, 0, 0), memory_space=pltpu.SMEM),
                  pl.BlockSpec((1, 1, 2 * tc), lambda i: (jnp.minimum(i + 1, nt - 1), 0, 0), memory_space=pltpu.SMEM),
                  pl.BlockSpec((tc, D_MODEL), lambda i: (i, 0)),
                  pl.BlockSpec((tc, LANES), lambda i: (i, 0)),
                  pl.BlockSpec(memory_space=pl.ANY),
                  pl.BlockSpec((1, D_MODEL), lambda i: (0, 0)), pl.BlockSpec((1, D_MODEL), lambda i: (0, 0))],
        out_specs=pl.BlockSpec((tc, D_MODEL), lambda i: (i, 0)),
        out_shape=jax.ShapeDtypeStruct((n, D_MODEL), F32),
        scratch_shapes=[pltpu.VMEM((2, 2, tc, D_MODEL), F32), pltpu.SemaphoreType.DMA((2,))],
        compiler_params=_params("arbitrary"),
        name="moe_combine_ln",
    )(pos3, pos3, x, route, ys, g, b)


def _moe_layer(x, w_router, wg, wu, wd, g, b):
    n = x.shape[0]
    tm = TM_MOE
    wr = jnp.pad(w_router, ((0, 0), (0, LANES - N_EXPERTS)))
    route, counts_f = _router(x, wr)

    counts = counts_f[0, :N_EXPERTS].astype(jnp.int32)
    padded = ((counts + tm - 1) // tm) * tm
    offs = jnp.cumsum(padded) - padded
    eids = jnp.arange(N_EXPERTS, dtype=jnp.int32)
    e01 = route[:, 0:2].astype(jnp.int32)
    rank01 = route[:, 2:4].astype(jnp.int32)
    base = jnp.sum(jnp.where(e01[:, :, None] == eids[None, None, :], offs[None, None, :], 0), axis=-1)
    pos = (base + rank01).reshape(-1)
    meta = jnp.stack([offs, counts, padded]).astype(jnp.int32)

    n_rows = 2 * n + N_EXPERTS * tm
    max_tiles = n_rows // tm
    tile_end = jnp.cumsum(padded // tm)
    n_tiles = tile_end[-1]
    tj = jnp.minimum(jnp.arange(max_tiles, dtype=jnp.int32), n_tiles - 1)
    tile_expert = jnp.sum((tj[:, None] >= tile_end[None, :]).astype(jnp.int32), axis=-1)
    tile_expert = jnp.minimum(tile_expert, N_EXPERTS - 1).astype(jnp.int32)

    xs = _dispatch(x, pos, meta, n_rows)
    ys = _moe_experts(xs, wg, wu, wd, tile_expert, tj.astype(jnp.int32), n_tiles.reshape(1).astype(jnp.int32))
    return _combine_ln(x, route, pos, ys, g, b)


def kernel(x, a_w_in, a_conv_w, a_a_log, a_dt_bias, a_norm_w, a_w_out, b_w_in, b_b_in, b_sinks, b_w_out, rel_bias,
           ffn_w_gate, ffn_w_up, ffn_w_down, moe_router, moe_w_gate, moe_w_up, moe_w_down, ln_g, ln_b):
    batch, seq_len, _ = x.shape
    n = batch * seq_len
    x0 = x.reshape(n, D_MODEL)
    ln_g = ln_g.reshape(DEPTH, 2, 1, D_MODEL)
    ln_b = ln_b.reshape(DEPTH, 2, 1, D_MODEL)
    main_cols = GDN_QKV + GDN_VDIM

    w_in = a_w_in[0]
    w_main = w_in[:, :main_cols].astype(BF16)
    w_ba = jnp.pad(w_in[:, main_cols:], ((0, 0), (0, LANES - 2 * GDN_V_HEADS)))
    pad_gate = lambda p: jnp.pad(p.reshape(1, GDN_V_HEADS), ((0, 0), (GDN_V_HEADS, LANES - 2 * GDN_V_HEADS)))
    gate_params = jnp.concatenate([pad_gate(a_a_log[0]), pad_gate(a_dt_bias[0])], axis=0)
    q, k, v, zs, gb = _gdn_inproj(x0, w_main, w_ba, a_conv_w[0], gate_params, seq_len)
    o = _gdn_chunk(q, k, v, zs, gb, a_norm_w[0].reshape(1, GDN_HEAD), batch, seq_len)
    x1 = _proj_ln(o, a_w_out[0].astype(BF16), x0, ln_g[0, 0], ln_b[0, 0], "gdn_outproj_ln")
    x2 = _ffn_ln(x1, ffn_w_gate[0].astype(BF16), ffn_w_up[0].astype(BF16), ffn_w_down[0].astype(BF16),
                 ln_g[0, 1], ln_b[0, 1])

    qa, kva = _swa_inproj(x2, b_w_in[0].astype(BF16), b_b_in[0].reshape(1, -1))
    bias = _bias_table(rel_bias)
    sinks = jnp.pad(b_sinks[0].reshape(1, SWA_Q_HEADS), ((0, 0), (0, LANES - SWA_Q_HEADS)))
    oa = _swa_attention(qa, kva, bias, sinks, batch, seq_len)
    x3 = _proj_ln(oa, b_w_out[0].astype(BF16), x2, ln_g[1, 0], ln_b[1, 0], "swa_outproj_ln")
    x4 = _moe_layer(x3, moe_router[0], moe_w_gate[0].astype(BF16), moe_w_up[0].astype(BF16),
                    moe_w_down[0].astype(BF16), ln_g[1, 1], ln_b[1, 1])
    return x4.reshape(batch, seq_len, D_MODEL)
```

```python
import functools
import math

import numpy as np
import jax
import jax.numpy as jnp
from jax import lax
from jax.experimental import pallas as pl
from jax.experimental.pallas import tpu as pltpu

F32 = jnp.float32
BF16 = jnp.bfloat16

D_MODEL = 1024
DEPTH = 2
ALPHA = (2.0 * DEPTH) ** 0.25
LN_EPS = 1e-5

GDN_K_HEADS = 4
GDN_V_HEADS = 8
GDN_HEAD = 128
GDN_KDIM = GDN_K_HEADS * GDN_HEAD
GDN_VDIM = GDN_V_HEADS * GDN_HEAD
GDN_CONV = 4
GDN_CHUNK = 64
GDN_QKV = 2 * GDN_KDIM + GDN_VDIM
GDN_EPS = 1e-6

SWA_Q_HEADS = 16
SWA_KV_HEADS = 2
SWA_GROUP = SWA_Q_HEADS // SWA_KV_HEADS
SWA_HEAD_DIM = 64
SWA_WINDOW = 128
SWA_BLOCK = 128
SWA_QDIM = SWA_Q_HEADS * SWA_HEAD_DIM
SWA_KVDIM = SWA_KV_HEADS * SWA_HEAD_DIM
REL_BUCKETS = 32
REL_MAX_DIST = 128

FFN_DIM = 2816
N_EXPERTS = 8
EXPERT_DIM = 3584

LANES = 128
SUBLANES = 8
NEG_BIG = -1e30
VMEM_LIMIT = 56 * 1024 * 1024

TM_GDN_IN = 256
TM_PROJ = 512
TM_FFN = 512
TF_FFN = 1408
TM_SWA_IN = 512
TM_ROUTER = 512
TM_MOE = 512
TF_MOE = 1792
TD_DISPATCH = 512
TC_COMBINE = 256
ZERO_ROWS = 64
DMA_UNROLL = 8


def _params(*sem):
    return pltpu.CompilerParams(dimension_semantics=sem, vmem_limit_bytes=VMEM_LIMIT)


def _dot(a, b):
    return jnp.dot(a, b, preferred_element_type=F32)


def _dot_nt(a, b):
    return lax.dot_general(a, b, (((1,), (1,)), ((), ())), preferred_element_type=F32)


def _dot_tn(a, b):
    return lax.dot_general(a, b, (((0,), (0,)), ((), ())), preferred_element_type=F32)


def _split(x):
    hi = x.astype(BF16)
    lo = (x - hi.astype(F32)).astype(BF16)
    return hi, lo


def _dot3(a, b):
    ah, al = _split(a)
    bh, bl = _split(b)
    return _dot(ah, bh) + (_dot(ah, bl) + _dot(al, bh))


def _silu(x):
    return x * jax.nn.sigmoid(x)


def _layer_norm(y, g, b):
    mu = jnp.mean(y, axis=-1, keepdims=True)
    yc = y - mu
    var = jnp.mean(yc * yc, axis=-1, keepdims=True)
    return yc * lax.rsqrt(var + LN_EPS) * g + b


def _gdn_inproj_kernel(x_ref, w_ref, wba_ref, convw_ref, gp_ref,
                       q_ref, k_ref, v_ref, z_ref, gb_ref, ext_ref, *, tm, tiles_per_seq):
    i = pl.program_id(0)
    x = x_ref[...]
    xb = x.astype(BF16)

    @pl.when(i % tiles_per_seq == 0)
    def _():
        ext_ref[0:SUBLANES, :] = jnp.zeros((SUBLANES, GDN_QKV), F32)

    @pl.when(i % tiles_per_seq != 0)
    def _():
        ext_ref[0:SUBLANES, :] = ext_ref[tm:tm + SUBLANES, :]

    ext_ref[SUBLANES:, :] = _dot(xb, w_ref[:, :GDN_QKV])
    z_ref[...] = _silu(_dot(xb, w_ref[:, GDN_QKV:]))

    n_chunks = GDN_QKV // LANES
    for c in range(n_chunks):
        cs = slice(c * LANES, (c + 1) * LANES)
        acc = convw_ref[GDN_CONV - 1:GDN_CONV, cs] * ext_ref[SUBLANES:SUBLANES + tm, cs]
        for j in range(GDN_CONV - 1):
            off = SUBLANES - (GDN_CONV - 1) + j
            acc = acc + convw_ref[j:j + 1, cs] * ext_ref[off:off + tm, cs]
        y = _silu(acc)
        if c < 2 * GDN_K_HEADS:
            y = y * lax.rsqrt(jnp.sum(y * y, axis=-1, keepdims=True) + GDN_EPS)
            if c < GDN_K_HEADS:
                q_ref[:, cs] = y * (GDN_HEAD ** -0.5)
            else:
                k_ref[:, (c - GDN_K_HEADS) * LANES:(c - GDN_K_HEADS + 1) * LANES] = y
        else:
            cv = c - 2 * GDN_K_HEADS
            v_ref[:, cv * LANES:(cv + 1) * LANES] = y

    ba = _dot3(x, wba_ref[...])
    lane = lax.broadcasted_iota(jnp.int32, ba.shape, 1)
    sp = ba + gp_ref[1:2, :]
    softplus = jnp.maximum(sp, 0.0) + jnp.log(1.0 + jnp.exp(-jnp.abs(sp)))
    g = -jnp.exp(gp_ref[0:1, :]) * softplus
    gb_ref[...] = jnp.where(lane < GDN_V_HEADS, jax.nn.sigmoid(ba), g)


def _gdn_inproj(x2d, w_main, w_ba, conv_w, gate_params, seq_len):
    n = x2d.shape[0]
    tm = TM_GDN_IN
    kern = functools.partial(_gdn_inproj_kernel, tm=tm, tiles_per_seq=seq_len // tm)
    full = lambda shape: pl.BlockSpec(shape, lambda i: (0,) * len(shape))
    rows = lambda width: pl.BlockSpec((tm, width), lambda i: (i, 0))
    return pl.pallas_call(
        kern,
        grid=(n // tm,),
        in_specs=[rows(D_MODEL), full(w_main.shape), full(w_ba.shape), full(conv_w.shape), full(gate_params.shape)],
        out_specs=[rows(GDN_KDIM), rows(GDN_KDIM), rows(GDN_VDIM), rows(GDN_VDIM), rows(LANES)],
        out_shape=[jax.ShapeDtypeStruct((n, GDN_KDIM), F32), jax.ShapeDtypeStruct((n, GDN_KDIM), F32),
                   jax.ShapeDtypeStruct((n, GDN_VDIM), F32), jax.ShapeDtypeStruct((n, GDN_VDIM), F32),
                   jax.ShapeDtypeStruct((n, LANES), F32)],
        scratch_shapes=[pltpu.VMEM((tm + SUBLANES, GDN_QKV), F32)],
        compiler_params=_params("arbitrary"),
        name="gdn_inproj",
    )(x2d, w_main, w_ba, conv_w, gate_params)


def _split_all(xs):
    return [_split(x) for x in xs]


def _dot3_all(a_splits, b_splits):
    return [_dot(ah, bh) + (_dot(ah, bl) + _dot(al, bh)) for (ah, al), (bh, bl) in zip(a_splits, b_splits)]


def _unit_lower_inverse_all(a_list, row, col):
    eye = jnp.where(row == col, 1.0, 0.0).astype(F32)
    on_diag_block = (row >> 3) == (col >> 3)
    d = [jnp.where(on_diag_block, a, 0.0) for a in a_list]
    ds = _split_all(d)
    x = [eye - di for di in d]
    p = _dot3_all(ds, ds)
    for step in range(2):
        ps = _split_all(p)
        x = [xi + t for xi, t in zip(x, _dot3_all(_split_all(x), ps))]
        if step == 0:
            p = _dot3_all(ps, ps)
    size = SUBLANES
    while size < a_list[0].shape[0]:
        shift = int(math.log2(size))
        rbl, cbl = row >> shift, col >> shift
        below = ((rbl & 1) == 1) & (cbl == rbl - 1)
        ls = _split_all([jnp.where(below, a, 0.0) for a in a_list])
        xs = _split_all(x)
        xl = _dot3_all(xs, ls)
        x = [xi - t for xi, t in zip(x, _dot3_all(_split_all(xl), xs))]
        size *= 2
    return x


def _gdn_chunk_kernel(q_ref, k_ref, v_ref, z_ref, gb_ref, nw_ref, o_ref, s_ref, *, nb):
    c = GDN_CHUNK
    nh = GDN_V_HEADS

    @pl.when(pl.program_id(0) == 0)
    def _():
        s_ref[...] = jnp.zeros(s_ref.shape, F32)

    row = lax.broadcasted_iota(jnp.int32, (c, c), 0)
    col = lax.broadcasted_iota(jnp.int32, (c, c), 1)
    causal = row >= col
    strict = row > col
    tril = jnp.where(causal, 1.0, 0.0).astype(BF16)

    per_seq = []
    for b in range(nb):
        gb = gb_ref[b]
        g_hi, g_lo = _split(gb)
        gc = _dot(tril, g_hi) + _dot(tril, g_lo)
        g_last = gc[c - 1:c, :]
        per_seq.append(dict(gb=gb, gc=gc, gc_t=gc.T, eg=jnp.exp(gc), e_last=jnp.exp(g_last),
                            e_rest=jnp.exp(g_last - gc)))

    items = [(b, h) for b in range(nb) for h in range(nh)]
    ks = lambda h: slice((h // (nh // GDN_K_HEADS)) * GDN_HEAD, (h // (nh // GDN_K_HEADS) + 1) * GDN_HEAD)
    vs = lambda h: slice(h * GDN_HEAD, (h + 1) * GDN_HEAD)

    a_kk, a_qk, k_beta = [], [], []
    for b, h in items:
        ps = per_seq[b]
        k = k_ref[b, :, ks(h)]
        kb = k * ps["gb"][:, h:h + 1]
        kq = jnp.concatenate([kb, q_ref[b, :, ks(h)]], axis=0).astype(BF16)
        gram = _dot_nt(kq, k.astype(BF16))
        gl = nh + h
        decay = jnp.exp(jnp.where(causal, ps["gc"][:, gl:gl + 1] - ps["gc_t"][gl:gl + 1, :], NEG_BIG))
        a_kk.append(jnp.where(strict, gram[:c] * decay, 0.0))
        a_qk.append((gram[c:] * decay).astype(BF16))
        k_beta.append(kb)

    t_inv = _unit_lower_inverse_all(a_kk, row, col)

    uw = []
    for (b, h), t, kb in zip(items, t_inv, k_beta):
        ps = per_seq[b]
        gl = nh + h
        rhs = jnp.concatenate([v_ref[b, :, vs(h)] * ps["gb"][:, h:h + 1], kb * ps["eg"][:, gl:gl + 1]], axis=1)
        uw.append(_dot(t.astype(BF16), rhs.astype(BF16)))

    ws_qs, states = [], []
    for idx, (b, h) in enumerate(items):
        gl = nh + h
        s = s_ref[idx]
        wq = jnp.concatenate([uw[idx][:, GDN_HEAD:], q_ref[b, :, ks(h)] * per_seq[b]["eg"][:, gl:gl + 1]], axis=0)
        ws_qs.append(_dot(wq.astype(BF16), s.astype(BF16)))
        states.append(s)

    for idx, (b, h) in enumerate(items):
        ps = per_seq[b]
        gl = nh + h
        v_new = (uw[idx][:, :GDN_HEAD] - ws_qs[idx][:c]).astype(BF16)
        o = ws_qs[idx][c:] + _dot(a_qk[idx], v_new)
        k_dec = (k_ref[b, :, ks(h)] * ps["e_rest"][:, gl:gl + 1]).astype(BF16)
        s_ref[idx] = states[idx] * ps["e_last"][:, gl:gl + 1] + _dot_tn(k_dec, v_new)
        o = o * lax.rsqrt(jnp.mean(o * o, axis=-1, keepdims=True) + GDN_EPS) * nw_ref[...]
        o_ref[b, :, vs(h)] = (o * z_ref[b, :, vs(h)]).astype(o_ref.dtype)


def _gdn_chunk(q, k, v, zs, gb, norm_w, batch, seq_len):
    c = GDN_CHUNK
    seq = lambda a: a.reshape(batch, seq_len, a.shape[-1])
    rows = lambda width: pl.BlockSpec((batch, c, width), lambda j: (0, j, 0))
    out = pl.pallas_call(
        functools.partial(_gdn_chunk_kernel, nb=batch),
        grid=(seq_len // c,),
        in_specs=[rows(GDN_KDIM), rows(GDN_KDIM), rows(GDN_VDIM), rows(GDN_VDIM), rows(LANES),
                  pl.BlockSpec((1, GDN_HEAD), lambda j: (0, 0))],
        out_specs=rows(GDN_VDIM),
        out_shape=jax.ShapeDtypeStruct((batch, seq_len, GDN_VDIM), BF16),
        scratch_shapes=[pltpu.VMEM((batch * GDN_V_HEADS, GDN_HEAD, GDN_HEAD), F32)],
        compiler_params=_params("arbitrary"),
        name="gdn_chunk",
    )(seq(q), seq(k), seq(v), seq(zs), seq(gb), norm_w)
    return out.reshape(batch * seq_len, GDN_VDIM)


def _proj_ln_kernel(a_ref, w_ref, r_ref, g_ref, b_ref, o_ref):
    y = ALPHA * r_ref[...] + _dot(a_ref[...], w_ref[...])
    o_ref[...] = _layer_norm(y, g_ref[...], b_ref[...])


def _proj_ln(a, w, res, g, b, name):
    n, kdim = a.shape
    tm = TM_PROJ
    return pl.pallas_call(
        _proj_ln_kernel,
        grid=(n // tm,),
        in_specs=[pl.BlockSpec((tm, kdim), lambda i: (i, 0)), pl.BlockSpec(w.shape, lambda i: (0, 0)),
                  pl.BlockSpec((tm, D_MODEL), lambda i: (i, 0)),
                  pl.BlockSpec((1, D_MODEL), lambda i: (0, 0)), pl.BlockSpec((1, D_MODEL), lambda i: (0, 0))],
        out_specs=pl.BlockSpec((tm, D_MODEL), lambda i: (i, 0)),
        out_shape=jax.ShapeDtypeStruct((n, D_MODEL), F32),
        compiler_params=_params("parallel"),
        name=name,
    )(a, w, res, g, b)


def _ffn_kernel(x_ref, wg_ref, wu_ref, wd_ref, g_ref, b_ref, o_ref, acc_ref, xb_ref):
    f = pl.program_id(1)

    @pl.when(f == 0)
    def _():
        xb_ref[...] = x_ref[...].astype(BF16)

    xb = xb_ref[...]
    h = (_silu(_dot(xb, wg_ref[...])) * _dot(xb, wu_ref[...])).astype(BF16)
    part = _dot(h, wd_ref[...])

    @pl.when(f == 0)
    def _():
        acc_ref[...] = part

    @pl.when(f != 0)
    def _():
        acc_ref[...] += part

    @pl.when(f == pl.num_programs(1) - 1)
    def _():
        o_ref[...] = _layer_norm(ALPHA * x_ref[...] + acc_ref[...], g_ref[...], b_ref[...])


def _ffn_ln(x, wg, wu, wd, g, b):
    n = x.shape[0]
    tm, tf = TM_FFN, TF_FFN
    return pl.pallas_call(
        _ffn_kernel,
        grid=(n // tm, FFN_DIM // tf),
        in_specs=[pl.BlockSpec((tm, D_MODEL), lambda i, f: (i, 0)),
                  pl.BlockSpec((D_MODEL, tf), lambda i, f: (0, f)),
                  pl.BlockSpec((D_MODEL, tf), lambda i, f: (0, f)),
                  pl.BlockSpec((tf, D_MODEL), lambda i, f: (f, 0)),
                  pl.BlockSpec((1, D_MODEL), lambda i, f: (0, 0)), pl.BlockSpec((1, D_MODEL), lambda i, f: (0, 0))],
        out_specs=pl.BlockSpec((tm, D_MODEL), lambda i, f: (i, 0)),
        out_shape=jax.ShapeDtypeStruct((n, D_MODEL), F32),
        scratch_shapes=[pltpu.VMEM((tm, D_MODEL), F32), pltpu.VMEM((tm, D_MODEL), BF16)],
        compiler_params=_params("parallel", "arbitrary"),
        name="ffn_ln",
    )(x, wg, wu, wd, g, b)


def _swa_inproj_kernel(x_ref, w_ref, b_ref, q_ref, kv_ref):
    p = _dot(x_ref[...].astype(BF16), w_ref[...]) + b_ref[...]
    q_ref[...] = (p[:, :SWA_QDIM] * (SWA_HEAD_DIM ** -0.5)).astype(BF16)
    kv_ref[...] = p[:, SWA_QDIM:].astype(BF16)


def _swa_inproj(x, w, b):
    n = x.shape[0]
    tm = TM_SWA_IN
    return pl.pallas_call(
        _swa_inproj_kernel,
        grid=(n // tm,),
        in_specs=[pl.BlockSpec((tm, D_MODEL), lambda i: (i, 0)), pl.BlockSpec(w.shape, lambda i: (0, 0)),
                  pl.BlockSpec(b.shape, lambda i: (0, 0))],
        out_specs=[pl.BlockSpec((tm, SWA_QDIM), lambda i: (i, 0)), pl.BlockSpec((tm, 2 * SWA_KVDIM), lambda i: (i, 0))],
        out_shape=[jax.ShapeDtypeStruct((n, SWA_QDIM), BF16), jax.ShapeDtypeStruct((n, 2 * SWA_KVDIM), BF16)],
        compiler_params=_params("parallel"),
        name="swa_inproj",
    )(x, w, b)


def _band_tables():
    qi = np.arange(SWA_BLOCK)[:, None]
    kj = np.arange(2 * SWA_BLOCK)[None, :]
    dist = qi + SWA_BLOCK - kj
    d = np.maximum(dist, 0)
    max_exact = REL_BUCKETS // 2
    df = np.maximum(d, 1).astype(np.float32)
    large = max_exact + (np.log(df / np.float32(max_exact)) / np.float32(math.log(REL_MAX_DIST / max_exact))
                         * np.float32(REL_BUCKETS - max_exact)).astype(np.int32)
    large = np.minimum(large, REL_BUCKETS - 1)
    bucket = np.where(d < max_exact, d, large).astype(np.int32)
    band_ok = (dist >= 0) & (dist < SWA_WINDOW)
    valid = np.stack([band_ok & (kj >= SWA_BLOCK), band_ok]).astype(np.int32)
    return bucket, valid


def _bias_kernel(relb_ref, bucket_ref, valid_ref, o_ref):
    h = pl.program_id(0)
    bucket = bucket_ref[...]
    acc = jnp.zeros(bucket.shape, F32)
    for b in range(REL_BUCKETS):
        acc = jnp.where(bucket == b, relb_ref[b, h], acc)
    for t in range(2):
        o_ref[t, 0] = jnp.where(valid_ref[t] != 0, acc, NEG_BIG)


def _bias_table(rel_bias):
    bucket, valid = _band_tables()
    shape = (SWA_BLOCK, 2 * SWA_BLOCK)
    return pl.pallas_call(
        _bias_kernel,
        grid=(SWA_Q_HEADS,),
        in_specs=[pl.BlockSpec(memory_space=pltpu.SMEM), pl.BlockSpec(shape, lambda h: (0, 0)),
                  pl.BlockSpec((2,) + shape, lambda h: (0, 0, 0))],
        out_specs=pl.BlockSpec((2, 1) + shape, lambda h: (0, h, 0, 0)),
        out_shape=jax.ShapeDtypeStruct((2, SWA_Q_HEADS) + shape, F32),
        compiler_params=_params("parallel"),
        name="swa_bias_table",
    )(rel_bias, jnp.asarray(bucket), jnp.asarray(valid))


def _swa_kernel(q_ref, kvc_ref, kvp_ref, bias_ref, sink_ref, o_ref):
    kv = jnp.concatenate([kvp_ref[...], kvc_ref[...]], axis=0)
    dh = SWA_HEAD_DIM
    for kh in range(SWA_KV_HEADS):
        kb = kv[:, kh * dh:(kh + 1) * dh]
        vb = kv[:, SWA_KVDIM + kh * dh:SWA_KVDIM + (kh + 1) * dh]
        heads = [kh * SWA_GROUP + g for g in range(SWA_GROUP)]
        scores = [_dot_nt(q_ref[:, h * dh:(h + 1) * dh], kb) + bias_ref[0, h] for h in heads]
        sinks = [sink_ref[0:1, h:h + 1] for h in heads]
        maxes = [jnp.maximum(jnp.max(s, axis=-1, keepdims=True), sk) for s, sk in zip(scores, sinks)]
        expo = [jnp.exp(s - m) for s, m in zip(scores, maxes)]
        sums = [jnp.sum(p, axis=-1, keepdims=True) for p in expo]
        inv_den = [1.0 / (t + jnp.exp(sk - m)) for t, sk, m in zip(sums, sinks, maxes)]
        outs = [_dot(p.astype(BF16), vb) * r for p, r in zip(expo, inv_den)]
        for g in range(0, SWA_GROUP, 2):
            h0 = heads[g]
            o_ref[:, h0 * dh:(h0 + 2) * dh] = jnp.concatenate(outs[g:g + 2], axis=1).astype(o_ref.dtype)


def _swa_attention(q, kv, bias, sinks, batch, seq_len):
    n = q.shape[0]
    nb = seq_len // SWA_BLOCK
    blk = SWA_BLOCK
    return pl.pallas_call(
        _swa_kernel,
        grid=(batch, nb),
        in_specs=[pl.BlockSpec((blk, SWA_QDIM), lambda b, j: (b * nb + j, 0)),
                  pl.BlockSpec((blk, 2 * SWA_KVDIM), lambda b, j: (b * nb + j, 0)),
                  pl.BlockSpec((blk, 2 * SWA_KVDIM), lambda b, j: (b * nb + jnp.maximum(j - 1, 0), 0)),
                  pl.BlockSpec((1, SWA_Q_HEADS, blk, 2 * blk), lambda b, j: (jnp.minimum(j, 1), 0, 0, 0)),
                  pl.BlockSpec((1, LANES), lambda b, j: (0, 0))],
        out_specs=pl.BlockSpec((blk, SWA_QDIM), lambda b, j: (b * nb + j, 0)),
        out_shape=jax.ShapeDtypeStruct((n, SWA_QDIM), BF16),
        compiler_params=_params("parallel", "arbitrary"),
        name="swa_attention",
    )(q, kv, kv, bias, sinks)


def _router_kernel(x_ref, wr_ref, route_ref, cnt_ref, run_ref, *, tm):
    @pl.when(pl.program_id(0) == 0)
    def _():
        run_ref[...] = jnp.zeros(run_ref.shape, F32)

    logits = _dot3(x_ref[...], wr_ref[...])
    lane = lax.broadcasted_iota(jnp.int32, logits.shape, 1)
    lane_f = lane.astype(F32)
    lg = jnp.where(lane < N_EXPERTS, logits, NEG_BIG)
    m1 = jnp.max(lg, axis=-1, keepdims=True)
    i1 = jnp.min(jnp.where(lg == m1, lane_f, float(LANES)), axis=-1, keepdims=True)
    oh1 = lane_f == i1
    lg2 = jnp.where(oh1, NEG_BIG, lg)
    m2 = jnp.max(lg2, axis=-1, keepdims=True)
    i2 = jnp.min(jnp.where(lg2 == m2, lane_f, float(LANES)), axis=-1, keepdims=True)
    oh2 = lane_f == i2
    e = jnp.exp(m2 - m1)
    w0 = 1.0 / (1.0 + e)
    w1 = e * w0

    cnt = jnp.where(oh1, 1.0, 0.0) + jnp.where(oh2, 1.0, 0.0)
    r = lax.broadcasted_iota(jnp.int32, (tm, tm), 0)
    c = lax.broadcasted_iota(jnp.int32, (tm, tm), 1)
    before = jnp.where(r > c, 1.0, 0.0).astype(BF16)
    excl = _dot(before, cnt.astype(BF16)) + run_ref[...]
    rank0 = jnp.sum(jnp.where(oh1, excl, 0.0), axis=-1, keepdims=True)
    rank1 = jnp.sum(jnp.where(oh2, excl, 0.0), axis=-1, keepdims=True)
    run = run_ref[...] + jnp.sum(cnt, axis=0, keepdims=True)
    run_ref[...] = run
    cnt_ref[...] = run

    vals = (i1, i2, rank0, rank1, w0, w1)
    out = jnp.zeros(logits.shape, F32)
    for idx, val in enumerate(vals):
        out = jnp.where(lane == idx, val, out)
    route_ref[...] = out


def _router(x, w_router_padded):
    n = x.shape[0]
    tm = TM_ROUTER
    return pl.pallas_call(
        functools.partial(_router_kernel, tm=tm),
        grid=(n // tm,),
        in_specs=[pl.BlockSpec((tm, D_MODEL), lambda i: (i, 0)), pl.BlockSpec((D_MODEL, LANES), lambda i: (0, 0))],
        out_specs=[pl.BlockSpec((tm, LANES), lambda i: (i, 0)), pl.BlockSpec((1, LANES), lambda i: (0, 0))],
        out_shape=[jax.ShapeDtypeStruct((n, LANES), F32), jax.ShapeDtypeStruct((1, LANES), F32)],
        scratch_shapes=[pltpu.VMEM((1, LANES), F32)],
        compiler_params=_params("arbitrary"),
        name="moe_router",
    )(x, w_router_padded)


def _dispatch_kernel(pos_ref, meta_ref, x_ref, xs_hbm, zero_ref, sem, pad_sem, *, td):
    i = pl.program_id(0)

    def row_copy(t, dst):
        return pltpu.make_async_copy(x_ref.at[pl.ds(t, 1)], xs_hbm.at[pl.ds(dst, 1)], sem)

    def start(t, carry):
        row_copy(t, pos_ref[0, 0, 2 * t]).start()
        row_copy(t, pos_ref[0, 0, 2 * t + 1]).start(priority=1)
        return carry

    lax.fori_loop(0, td, start, 0, unroll=DMA_UNROLL)

    @pl.when(i == pl.num_programs(0) - 1)
    def _():
        zero_ref[...] = jnp.zeros(zero_ref.shape, F32)

        def pad_copy(dst):
            return pltpu.make_async_copy(zero_ref.at[pl.ds(0, 1)], xs_hbm.at[pl.ds(dst, 1)], pad_sem)

        for e in range(N_EXPERTS):
            off, cnt, padded = meta_ref[0, e], meta_ref[1, e], meta_ref[2, e]

            def pad_start(r, carry, off=off):
                pad_copy(off + r).start()
                return carry

            def pad_wait(r, carry):
                pad_copy(0).wait()
                return carry

            lax.fori_loop(cnt, padded, pad_start, 0)
            lax.fori_loop(cnt, padded, pad_wait, 0)

        zr = zero_ref.shape[0]
        used = meta_ref[0, N_EXPERTS - 1] + meta_ref[2, N_EXPERTS - 1]

        def tail_copy(r):
            return pltpu.make_async_copy(zero_ref, xs_hbm.at[pl.ds(pl.multiple_of(r * zr, zr), zr)], pad_sem)

        def tail_start(r, carry):
            tail_copy(r).start()
            return carry

        def tail_wait(r, carry):
            tail_copy(0).wait()
            return carry

        lax.fori_loop(used // zr, xs_hbm.shape[0] // zr, tail_start, 0)
        lax.fori_loop(used // zr, xs_hbm.shape[0] // zr, tail_wait, 0)

    for _ in range(2):
        pltpu.make_async_copy(x_ref, xs_hbm.at[pl.ds(0, td)], sem).wait()


def _dispatch(x, pos, meta, n_rows):
    n = x.shape[0]
    td = TD_DISPATCH
    pos3 = pos.reshape(n // td, 1, 2 * td)
    return pl.pallas_call(
        functools.partial(_dispatch_kernel, td=td),
        grid=(n // td,),
        in_specs=[pl.BlockSpec((1, 1, 2 * td), lambda i: (i, 0, 0), memory_space=pltpu.SMEM),
                  pl.BlockSpec(memory_space=pltpu.SMEM),
                  pl.BlockSpec((td, D_MODEL), lambda i: (i, 0))],
        out_specs=pl.BlockSpec(memory_space=pl.ANY),
        out_shape=jax.ShapeDtypeStruct((n_rows, D_MODEL), F32),
        scratch_shapes=[pltpu.VMEM((ZERO_ROWS, D_MODEL), F32), pltpu.SemaphoreType.DMA, pltpu.SemaphoreType.DMA],
        compiler_params=_params("arbitrary"),
        name="moe_dispatch",
    )(pos3, meta, x)


def _moe_kernel(te_ref, trb_ref, nt_ref, xs_ref, wg_ref, wu_ref, wd_ref, ys_ref, acc_ref, xb_ref):
    j = pl.program_id(0)
    f = pl.program_id(1)

    @pl.when(j < nt_ref[0])
    def _():
        @pl.when(f == 0)
        def _():
            xb_ref[...] = xs_ref[...].astype(BF16)

        xb = xb_ref[...]
        h = (_silu(_dot(xb, wg_ref[0])) * _dot(xb, wu_ref[0])).astype(BF16)
        part = _dot(h, wd_ref[0])

        @pl.when(f == 0)
        def _():
            acc_ref[...] = part

        @pl.when(f != 0)
        def _():
            acc_ref[...] += part

        @pl.when(f == pl.num_programs(1) - 1)
        def _():
            ys_ref[...] = acc_ref[...]

    @pl.when((j >= nt_ref[0]) & (f == 0))
    def _():
        ys_ref[...] = jnp.zeros(ys_ref.shape, F32)


def _moe_experts(xs, wg, wu, wd, tile_expert, tile_rowblock, n_tiles):
    n_rows = xs.shape[0]
    tm, tf = TM_MOE, TF_MOE
    max_tiles = n_rows // tm
    grid_spec = pltpu.PrefetchScalarGridSpec(
        num_scalar_prefetch=3,
        grid=(max_tiles, EXPERT_DIM // tf),
        in_specs=[pl.BlockSpec((tm, D_MODEL), lambda j, f, te, trb, nt: (trb[j], 0)),
                  pl.BlockSpec((1, D_MODEL, tf), lambda j, f, te, trb, nt: (te[j], 0, f)),
                  pl.BlockSpec((1, D_MODEL, tf), lambda j, f, te, trb, nt: (te[j], 0, f)),
                  pl.BlockSpec((1, tf, D_MODEL), lambda j, f, te, trb, nt: (te[j], f, 0))],
        out_specs=pl.BlockSpec((tm, D_MODEL), lambda j, f, te, trb, nt: (j, 0)),
        scratch_shapes=[pltpu.VMEM((tm, D_MODEL), F32), pltpu.VMEM((tm, D_MODEL), BF16)],
    )
    return pl.pallas_call(
        _moe_kernel,
        grid_spec=grid_spec,
        out_shape=jax.ShapeDtypeStruct((n_rows, D_MODEL), F32),
        compiler_params=_params("arbitrary", "arbitrary"),
        name="moe_experts",
    )(tile_expert, tile_rowblock, n_tiles, xs, wg, wu, wd)


def _combine_kernel(pos_ref, posn_ref, x_ref, route_ref, ys_hbm, g_ref, b_ref, o_ref, y_ref, sem, *, tc):
    i = pl.program_id(0)
    slot = i % 2

    def issue(p_ref, s):
        def start(t, carry):
            for kk in range(2):
                pltpu.make_async_copy(ys_hbm.at[pl.ds(p_ref[0, 0, 2 * t + kk], 1)],
                                      y_ref.at[s, kk, pl.ds(t, 1)], sem.at[s]).start(priority=kk)
            return carry

        lax.fori_loop(0, tc, start, 0, unroll=DMA_UNROLL)

    @pl.when(i == 0)
    def _():
        issue(pos_ref, 0)

    @pl.when(i + 1 < pl.num_programs(0))
    def _():
        issue(posn_ref, 1 - slot)

    for kk in range(2):
        pltpu.make_async_copy(ys_hbm.at[pl.ds(0, tc)], y_ref.at[slot, kk], sem.at[slot]).wait()

    route = route_ref[...]
    f = route[:, 4:5] * y_ref[slot, 0] + route[:, 5:6] * y_ref[slot, 1]
    o_ref[...] = _layer_norm(ALPHA * x_ref[...] + f, g_ref[...], b_ref[...])


def _combine_ln(x, route, pos, ys, g, b):
    n = x.shape[0]
    tc = TC_COMBINE
    nt = n // tc
    pos3 = pos.reshape(nt, 1, 2 * tc)
    return pl.pallas_call(
        functools.partial(_combine_kernel, tc=tc),
        grid=(nt,),
        in_specs=[pl.BlockSpec((1, 1, 2 * tc), lambda i: (i, 0, 0), memory_space=pltpu.SMEM),
                  pl.BlockSpec((1, 1, 2 * tc), lambda i: (jnp.minimum(i + 1, nt - 1), 0, 0), memory_space=pltpu.SMEM),
                  pl.BlockSpec((tc, D_MODEL), lambda i: (i, 0)),
                  pl.BlockSpec((tc, LANES), lambda i: (i, 0)),
                  pl.BlockSpec(memory_space=pl.ANY),
                  pl.BlockSpec((1, D_MODEL), lambda i: (0, 0)), pl.BlockSpec((1, D_MODEL), lambda i: (0, 0))],
        out_specs=pl.BlockSpec((tc, D_MODEL), lambda i: (i, 0)),
        out_shape=jax.ShapeDtypeStruct((n, D_MODEL), F32),
        scratch_shapes=[pltpu.VMEM((2, 2, tc, D_MODEL), F32), pltpu.SemaphoreType.DMA((2,))],
        compiler_params=_params("arbitrary"),
        name="moe_combine_ln",
    )(pos3, pos3, x, route, ys, g, b)


def _moe_layer(x, w_router, wg, wu, wd, g, b):
    n = x.shape[0]
    tm = TM_MOE
    wr = jnp.pad(w_router, ((0, 0), (0, LANES - N_EXPERTS)))
    route, counts_f = _router(x, wr)

    counts = counts_f[0, :N_EXPERTS].astype(jnp.int32)
    padded = ((counts + tm - 1) // tm) * tm
    offs = jnp.cumsum(padded) - padded
    eids = jnp.arange(N_EXPERTS, dtype=jnp.int32)
    e01 = route[:, 0:2].astype(jnp.int32)
    rank01 = route[:, 2:4].astype(jnp.int32)
    base = jnp.sum(jnp.where(e01[:, :, None] == eids[None, None, :], offs[None, None, :], 0), axis=-1)
    pos = (base + rank01).reshape(-1)
    meta = jnp.stack([offs, counts, padded]).astype(jnp.int32)

    n_rows = 2 * n + N_EXPERTS * tm
    max_tiles = n_rows // tm
    tile_end = jnp.cumsum(padded // tm)
    n_tiles = tile_end[-1]
    tj = jnp.minimum(jnp.arange(max_tiles, dtype=jnp.int32), n_tiles - 1)
    tile_expert = jnp.sum((tj[:, None] >= tile_end[None, :]).astype(jnp.int32), axis=-1)
    tile_expert = jnp.minimum(tile_expert, N_EXPERTS - 1).astype(jnp.int32)

    xs = _dispatch(x, pos, meta, n_rows)
    ys = _moe_experts(xs, wg, wu, wd, tile_expert, tj.astype(jnp.int32), n_tiles.reshape(1).astype(jnp.int32))
    return _combine_ln(x, route, pos, ys, g, b)


def kernel(x, a_w_in, a_conv_w, a_a_log, a_dt_bias, a_norm_w, a_w_out, b_w_in, b_b_in, b_sinks, b_w_out, rel_bias,
           ffn_w_gate, ffn_w_up, ffn_w_down, moe_router, moe_w_gate, moe_w_up, moe_w_down, ln_g, ln_b):
    batch, seq_len, _ = x.shape
    n = batch * seq_len
    x0 = x.reshape(n, D_MODEL)
    ln_g = ln_g.reshape(DEPTH, 2, 1, D_MODEL)
    ln_b = ln_b.reshape(DEPTH, 2, 1, D_MODEL)
    main_cols = GDN_QKV + GDN_VDIM

    w_in = a_w_in[0]
    w_main = w_in[:, :main_cols].astype(BF16)
    w_ba = jnp.pad(w_in[:, main_cols:], ((0, 0), (0, LANES - 2 * GDN_V_HEADS)))
    pad_gate = lambda p: jnp.pad(p.reshape(1, GDN_V_HEADS), ((0, 0), (GDN_V_HEADS, LANES - 2 * GDN_V_HEADS)))
    gate_params = jnp.concatenate([pad_gate(a_a_log[0]), pad_gate(a_dt_bias[0])], axis=0)
    q, k, v, zs, gb = _gdn_inproj(x0, w_main, w_ba, a_conv_w[0], gate_params, seq_len)
    o = _gdn_chunk(q, k, v, zs, gb, a_norm_w[0].reshape(1, GDN_HEAD), batch, seq_len)
    x1 = _proj_ln(o, a_w_out[0].astype(BF16), x0, ln_g[0, 0], ln_b[0, 0], "gdn_outproj_ln")
    x2 = _ffn_ln(x1, ffn_w_gate[0].astype(BF16), ffn_w_up[0].astype(BF16), ffn_w_down[0].astype(BF16),
                 ln_g[0, 1], ln_b[0, 1])

    qa, kva = _swa_inproj(x2, b_w_in[0].astype(BF16), b_b_in[0].reshape(1, -1))
    bias = _bias_table(rel_bias)
    sinks = jnp.pad(b_sinks[0].reshape(1, SWA_Q_HEADS), ((0, 0), (0, LANES - SWA_Q_HEADS)))
    oa = _swa_attention(qa, kva, bias, sinks, batch, seq_len)
    x3 = _proj_ln(oa, b_w_out[0].astype(BF16), x2, ln_g[1, 0], ln_b[1, 0], "swa_outproj_ln")
    x4 = _moe_layer(x3, moe_router[0], moe_w_gate[0].astype(BF16), moe_w_up[0].astype(BF16),
                    moe_w_down[0].astype(BF16), ln_g[1, 1], ln_b[1, 1])
    return x4.reshape(batch, seq_len, D_MODEL)
```

```python
import functools
import math

import numpy as np
import jax
import jax.numpy as jnp
from jax import lax
from jax.experimental import pallas as pl
from jax.experimental.pallas import tpu as pltpu

F32 = jnp.float32
BF16 = jnp.bfloat16

D_MODEL = 1024
DEPTH = 2
ALPHA = (2.0 * DEPTH) ** 0.25
LN_EPS = 1e-5

GDN_K_HEADS = 4
GDN_V_HEADS = 8
GDN_HEAD = 128
GDN_KDIM = GDN_K_HEADS * GDN_HEAD
GDN_VDIM = GDN_V_HEADS * GDN_HEAD
GDN_CONV = 4
GDN_CHUNK = 64
GDN_QKV = 2 * GDN_KDIM + GDN_VDIM
GDN_EPS = 1e-6

SWA_Q_HEADS = 16
SWA_KV_HEADS = 2
SWA_GROUP = SWA_Q_HEADS // SWA_KV_HEADS
SWA_HEAD_DIM = 64
SWA_WINDOW = 128
SWA_BLOCK = 128
SWA_QDIM = SWA_Q_HEADS * SWA_HEAD_DIM
SWA_KVDIM = SWA_KV_HEADS * SWA_HEAD_DIM
REL_BUCKETS = 32
REL_MAX_DIST = 128

FFN_DIM = 2816
N_EXPERTS = 8
EXPERT_DIM = 3584

LANES = 128
SUBLANES = 8
NEG_BIG = -1e30
VMEM_LIMIT = 56 * 1024 * 1024

TM_GDN_IN = 512
TM_PROJ = 512
TM_FFN = 512
TF_FFN = 1408
TM_SWA_IN = 512
TM_ROUTER = 512
TM_MOE = 512
TF_MOE = 1792
TD_DISPATCH = 512
TC_COMBINE = 256
ZERO_ROWS = 64
DMA_UNROLL = 8


def _params(*sem):
    return pltpu.CompilerParams(dimension_semantics=sem, vmem_limit_bytes=VMEM_LIMIT)


def _dot(a, b):
    return jnp.dot(a, b, preferred_element_type=F32)


def _dot_nt(a, b):
    return lax.dot_general(a, b, (((1,), (1,)), ((), ())), preferred_element_type=F32)


def _dot_tn(a, b):
    return lax.dot_general(a, b, (((0,), (0,)), ((), ())), preferred_element_type=F32)


def _split(x):
    hi = x.astype(BF16)
    lo = (x - hi.astype(F32)).astype(BF16)
    return hi, lo


def _dot3(a, b):
    ah, al = _split(a)
    bh, bl = _split(b)
    return _dot(ah, bh) + (_dot(ah, bl) + _dot(al, bh))


def _silu(x):
    return x * jax.nn.sigmoid(x)


def _layer_norm(y, g, b):
    mu = jnp.mean(y, axis=-1, keepdims=True)
    yc = y - mu
    var = jnp.mean(yc * yc, axis=-1, keepdims=True)
    return yc * lax.rsqrt(var + LN_EPS) * g + b


def _gdn_inproj_kernel(x_ref, w_ref, convw_ref, gp_ref,
                       q_ref, k_ref, v_ref, z_ref, gb_ref, ext_ref, *, tm, tiles_per_seq):
    i = pl.program_id(0)
    xb = x_ref[...].astype(BF16)

    @pl.when(i % tiles_per_seq == 0)
    def _():
        ext_ref[0:SUBLANES, :] = jnp.zeros((SUBLANES, GDN_QKV), F32)

    @pl.when(i % tiles_per_seq != 0)
    def _():
        ext_ref[0:SUBLANES, :] = ext_ref[tm:tm + SUBLANES, :]

    ext_ref[SUBLANES:, :] = _dot(xb, w_ref[:, :GDN_QKV])
    z_ref[...] = _silu(_dot(xb, w_ref[:, GDN_QKV:GDN_QKV + GDN_VDIM]))

    n_chunks = GDN_QKV // LANES
    for c in range(n_chunks):
        cs = slice(c * LANES, (c + 1) * LANES)
        acc = convw_ref[GDN_CONV - 1:GDN_CONV, cs] * ext_ref[SUBLANES:SUBLANES + tm, cs]
        for j in range(GDN_CONV - 1):
            off = SUBLANES - (GDN_CONV - 1) + j
            acc = acc + convw_ref[j:j + 1, cs] * ext_ref[off:off + tm, cs]
        y = _silu(acc)
        if c < 2 * GDN_K_HEADS:
            y = y * lax.rsqrt(jnp.sum(y * y, axis=-1, keepdims=True) + GDN_EPS)
            if c < GDN_K_HEADS:
                q_ref[:, cs] = y * (GDN_HEAD ** -0.5)
            else:
                k_ref[:, (c - GDN_K_HEADS) * LANES:(c - GDN_K_HEADS + 1) * LANES] = y
        else:
            cv = c - 2 * GDN_K_HEADS
            v_ref[:, cv * LANES:(cv + 1) * LANES] = y

    ba = _dot(xb, w_ref[:, GDN_QKV + GDN_VDIM:])
    lane = lax.broadcasted_iota(jnp.int32, ba.shape, 1)
    sp = ba + gp_ref[1:2, :]
    softplus = jnp.maximum(sp, 0.0) + jnp.log(1.0 + jnp.exp(-jnp.abs(sp)))
    g = -jnp.exp(gp_ref[0:1, :]) * softplus
    gb_ref[...] = jnp.where(lane < GDN_V_HEADS, jax.nn.sigmoid(ba), g)


def _gdn_inproj(x2d, w_all, conv_w, gate_params, seq_len):
    n = x2d.shape[0]
    tm = TM_GDN_IN
    kern = functools.partial(_gdn_inproj_kernel, tm=tm, tiles_per_seq=seq_len // tm)
    full = lambda shape: pl.BlockSpec(shape, lambda i: (0,) * len(shape))
    rows = lambda width: pl.BlockSpec((tm, width), lambda i: (i, 0))
    return pl.pallas_call(
        kern,
        grid=(n // tm,),
        in_specs=[rows(D_MODEL), full(w_all.shape), full(conv_w.shape), full(gate_params.shape)],
        out_specs=[rows(GDN_KDIM), rows(GDN_KDIM), rows(GDN_VDIM), rows(GDN_VDIM), rows(LANES)],
        out_shape=[jax.ShapeDtypeStruct((n, GDN_KDIM), F32), jax.ShapeDtypeStruct((n, GDN_KDIM), F32),
                   jax.ShapeDtypeStruct((n, GDN_VDIM), F32), jax.ShapeDtypeStruct((n, GDN_VDIM), F32),
                   jax.ShapeDtypeStruct((n, LANES), F32)],
        scratch_shapes=[pltpu.VMEM((tm + SUBLANES, GDN_QKV), F32)],
        compiler_params=_params("arbitrary"),
        name="gdn_inproj",
    )(x2d, w_all, conv_w, gate_params)


GDN_CHUNKS_PER_STEP = 2
NEUMANN_BLOCK = 4


def _bf16_all(xs):
    return [x.astype(BF16) for x in xs]


def _dot_all(a_list, b_list):
    return [_dot(a, b) for a, b in zip(a_list, b_list)]


def _unit_lower_inverse_all(a_list, row, col):
    shift = int(math.log2(NEUMANN_BLOCK))
    eye = jnp.where(row == col, 1.0, 0.0).astype(F32)
    on_diag_block = (row >> shift) == (col >> shift)
    d = [jnp.where(on_diag_block, a, 0.0) for a in a_list]
    d_b = _bf16_all(d)
    x = [eye - di for di in d]
    d2_b = _bf16_all(_dot_all(d_b, d_b))
    x = [xi + t for xi, t in zip(x, _dot_all(_bf16_all(x), d2_b))]
    size = NEUMANN_BLOCK
    while size < a_list[0].shape[0]:
        shift = int(math.log2(size))
        rbl, cbl = row >> shift, col >> shift
        below = ((rbl & 1) == 1) & (cbl == rbl - 1)
        l_b = _bf16_all([jnp.where(below, a, 0.0) for a in a_list])
        x_b = _bf16_all(x)
        xl_b = _bf16_all(_dot_all(x_b, l_b))
        x = [xi - t for xi, t in zip(x, _dot_all(xl_b, x_b))]
        size *= 2
    return x


def _gdn_chunk_kernel(q_ref, k_ref, v_ref, z_ref, gb_ref, nw_ref, o_ref, s_ref, *, nb, nck):
    c = GDN_CHUNK
    nh = GDN_V_HEADS

    @pl.when(pl.program_id(0) == 0)
    def _():
        s_ref[...] = jnp.zeros(s_ref.shape, F32)

    row = lax.broadcasted_iota(jnp.int32, (c, c), 0)
    col = lax.broadcasted_iota(jnp.int32, (c, c), 1)
    causal = row >= col
    strict = row > col
    tril = jnp.where(causal, 1.0, 0.0).astype(BF16)
    rs = lambda ci: slice(ci * c, (ci + 1) * c)
    ks = lambda h: slice((h // (nh // GDN_K_HEADS)) * GDN_HEAD, (h // (nh // GDN_K_HEADS) + 1) * GDN_HEAD)
    vs = lambda h: slice(h * GDN_HEAD, (h + 1) * GDN_HEAD)

    gates = {}
    for ci in range(nck):
        for b in range(nb):
            gb = gb_ref[b, rs(ci), :]
            g_hi, g_lo = _split(gb)
            gc = _dot(tril, g_hi) + _dot(tril, g_lo)
            g_last = gc[c - 1:c, :]
            gates[ci, b] = dict(gb=gb, gc=gc, gc_t=gc.T, eg=jnp.exp(gc), e_last=jnp.exp(g_last),
                                e_rest=jnp.exp(g_last - gc))

    items = [(ci, b, h) for ci in range(nck) for b in range(nb) for h in range(nh)]

    a_kk, a_qk, k_beta = [], {}, {}
    for it in items:
        ci, b, h = it
        gt = gates[ci, b]
        k = k_ref[b, rs(ci), ks(h)]
        kb = k * gt["gb"][:, h:h + 1]
        kq = jnp.concatenate([kb, q_ref[b, rs(ci), ks(h)]], axis=0).astype(BF16)
        gram = _dot_nt(kq, k.astype(BF16))
        gl = nh + h
        decay = jnp.exp(jnp.where(causal, gt["gc"][:, gl:gl + 1] - gt["gc_t"][gl:gl + 1, :], NEG_BIG))
        a_kk.append(jnp.where(strict, gram[:c] * decay, 0.0))
        a_qk[it] = (gram[c:] * decay).astype(BF16)
        k_beta[it] = kb

    t_inv = dict(zip(items, _unit_lower_inverse_all(a_kk, row, col)))

    uw = {}
    for it in items:
        ci, b, h = it
        gt = gates[ci, b]
        gl = nh + h
        rhs = jnp.concatenate([v_ref[b, rs(ci), vs(h)] * gt["gb"][:, h:h + 1],
                               k_beta[it] * gt["eg"][:, gl:gl + 1]], axis=1)
        uw[it] = _dot(t_inv[it].astype(BF16), rhs.astype(BF16))

    for ci in range(nck):
        chunk_items = [(ci, b, h) for b in range(nb) for h in range(nh)]

        ws_qs, states = {}, {}
        for it in chunk_items:
            _, b, h = it
            gl = nh + h
            s = s_ref[b * nh + h]
            wq = jnp.concatenate([uw[it][:, GDN_HEAD:],
                                  q_ref[b, rs(ci), ks(h)] * gates[ci, b]["eg"][:, gl:gl + 1]], axis=0)
            ws_qs[it] = _dot(wq.astype(BF16), s.astype(BF16))
            states[it] = s

        for it in chunk_items:
            _, b, h = it
            gt = gates[ci, b]
            gl = nh + h
            v_new = (uw[it][:, :GDN_HEAD] - ws_qs[it][:c]).astype(BF16)
            o = ws_qs[it][c:] + _dot(a_qk[it], v_new)
            k_dec = (k_ref[b, rs(ci), ks(h)] * gt["e_rest"][:, gl:gl + 1]).astype(BF16)
            s_ref[b * nh + h] = states[it] * gt["e_last"][:, gl:gl + 1] + _dot_tn(k_dec, v_new)
            o = o * lax.rsqrt(jnp.mean(o * o, axis=-1, keepdims=True) + GDN_EPS) * nw_ref[...]
            o_ref[b, rs(ci), vs(h)] = (o * z_ref[b, rs(ci), vs(h)]).astype(o_ref.dtype)


def _gdn_chunk(q, k, v, zs, gb, norm_w, batch, seq_len):
    rows_per_step = GDN_CHUNK * GDN_CHUNKS_PER_STEP
    seq = lambda a: a.reshape(batch, seq_len, a.shape[-1])
    rows = lambda width: pl.BlockSpec((batch, rows_per_step, width), lambda j: (0, j, 0))
    out = pl.pallas_call(
        functools.partial(_gdn_chunk_kernel, nb=batch, nck=GDN_CHUNKS_PER_STEP),
        grid=(seq_len // rows_per_step,),
        in_specs=[rows(GDN_KDIM), rows(GDN_KDIM), rows(GDN_VDIM), rows(GDN_VDIM), rows(LANES),
                  pl.BlockSpec((1, GDN_HEAD), lambda j: (0, 0))],
        out_specs=rows(GDN_VDIM),
        out_shape=jax.ShapeDtypeStruct((batch, seq_len, GDN_VDIM), BF16),
        scratch_shapes=[pltpu.VMEM((batch * GDN_V_HEADS, GDN_HEAD, GDN_HEAD), F32)],
        compiler_params=_params("arbitrary"),
        name="gdn_chunk",
    )(seq(q), seq(k), seq(v), seq(zs), seq(gb), norm_w)
    return out.reshape(batch * seq_len, GDN_VDIM)


def _proj_ln_kernel(a_ref, w_ref, r_ref, g_ref, b_ref, o_ref):
    y = ALPHA * r_ref[...] + _dot(a_ref[...], w_ref[...])
    o_ref[...] = _layer_norm(y, g_ref[...], b_ref[...])


def _proj_ln(a, w, res, g, b, name):
    n, kdim = a.shape
    tm = TM_PROJ
    return pl.pallas_call(
        _proj_ln_kernel,
        grid=(n // tm,),
        in_specs=[pl.BlockSpec((tm, kdim), lambda i: (i, 0)), pl.BlockSpec(w.shape, lambda i: (0, 0)),
                  pl.BlockSpec((tm, D_MODEL), lambda i: (i, 0)),
                  pl.BlockSpec((1, D_MODEL), lambda i: (0, 0)), pl.BlockSpec((1, D_MODEL), lambda i: (0, 0))],
        out_specs=pl.BlockSpec((tm, D_MODEL), lambda i: (i, 0)),
        out_shape=jax.ShapeDtypeStruct((n, D_MODEL), F32),
        compiler_params=_params("parallel"),
        name=name,
    )(a, w, res, g, b)


def _ffn_kernel(x_ref, wg_ref, wu_ref, wd_ref, g_ref, b_ref, o_ref, acc_ref, xb_ref):
    f = pl.program_id(1)

    @pl.when(f == 0)
    def _():
        xb_ref[...] = x_ref[...].astype(BF16)

    xb = xb_ref[...]
    h = (_silu(_dot(xb, wg_ref[...])) * _dot(xb, wu_ref[...])).astype(BF16)
    part = _dot(h, wd_ref[...])

    @pl.when(f == 0)
    def _():
        acc_ref[...] = part

    @pl.when(f != 0)
    def _():
        acc_ref[...] += part

    @pl.when(f == pl.num_programs(1) - 1)
    def _():
        o_ref[...] = _layer_norm(ALPHA * x_ref[...] + acc_ref[...], g_ref[...], b_ref[...])


def _ffn_ln(x, wg, wu, wd, g, b):
    n = x.shape[0]
    tm, tf = TM_FFN, TF_FFN
    return pl.pallas_call(
        _ffn_kernel,
        grid=(n // tm, FFN_DIM // tf),
        in_specs=[pl.BlockSpec((tm, D_MODEL), lambda i, f: (i, 0)),
                  pl.BlockSpec((D_MODEL, tf), lambda i, f: (0, f)),
                  pl.BlockSpec((D_MODEL, tf), lambda i, f: (0, f)),
                  pl.BlockSpec((tf, D_MODEL), lambda i, f: (f, 0)),
                  pl.BlockSpec((1, D_MODEL), lambda i, f: (0, 0)), pl.BlockSpec((1, D_MODEL), lambda i, f: (0, 0))],
        out_specs=pl.BlockSpec((tm, D_MODEL), lambda i, f: (i, 0)),
        out_shape=jax.ShapeDtypeStruct((n, D_MODEL), F32),
        scratch_shapes=[pltpu.VMEM((tm, D_MODEL), F32), pltpu.VMEM((tm, D_MODEL), BF16)],
        compiler_params=_params("parallel", "arbitrary"),
        name="ffn_ln",
    )(x, wg, wu, wd, g, b)


def _swa_inproj_kernel(x_ref, w_ref, b_ref, q_ref, kv_ref):
    p = _dot(x_ref[...].astype(BF16), w_ref[...]) + b_ref[...]
    q_ref[...] = (p[:, :SWA_QDIM] * (SWA_HEAD_DIM ** -0.5)).astype(BF16)
    kv_ref[...] = p[:, SWA_QDIM:].astype(BF16)


def _swa_inproj(x, w, b):
    n = x.shape[0]
    tm = TM_SWA_IN
    return pl.pallas_call(
        _swa_inproj_kernel,
        grid=(n // tm,),
        in_specs=[pl.BlockSpec((tm, D_MODEL), lambda i: (i, 0)), pl.BlockSpec(w.shape, lambda i: (0, 0)),
                  pl.BlockSpec(b.shape, lambda i: (0, 0))],
        out_specs=[pl.BlockSpec((tm, SWA_QDIM), lambda i: (i, 0)), pl.BlockSpec((tm, 2 * SWA_KVDIM), lambda i: (i, 0))],
        out_shape=[jax.ShapeDtypeStruct((n, SWA_QDIM), BF16), jax.ShapeDtypeStruct((n, 2 * SWA_KVDIM), BF16)],
        compiler_params=_params("parallel"),
        name="swa_inproj",
    )(x, w, b)


def _band_tables():
    qi = np.arange(SWA_BLOCK)[:, None]
    kj = np.arange(2 * SWA_BLOCK)[None, :]
    dist = qi + SWA_BLOCK - kj
    d = np.maximum(dist, 0)
    max_exact = REL_BUCKETS // 2
    df = np.maximum(d, 1).astype(np.float32)
    large = max_exact + (np.log(df / np.float32(max_exact)) / np.float32(math.log(REL_MAX_DIST / max_exact))
                         * np.float32(REL_BUCKETS - max_exact)).astype(np.int32)
    large = np.minimum(large, REL_BUCKETS - 1)
    bucket = np.where(d < max_exact, d, large).astype(np.int32)
    band_ok = (dist >= 0) & (dist < SWA_WINDOW)
    valid = np.stack([band_ok & (kj >= SWA_BLOCK), band_ok]).astype(np.int32)
    return np.ascontiguousarray(bucket.T), np.ascontiguousarray(valid.transpose(0, 2, 1))


def _bias_kernel(relb_ref, bucket_ref, valid_ref, o_ref):
    h = pl.program_id(0)
    bucket = bucket_ref[...]
    acc = jnp.zeros(bucket.shape, F32)
    for b in range(REL_BUCKETS):
        acc = jnp.where(bucket == b, relb_ref[b, h], acc)
    for t in range(2):
        o_ref[t, 0] = jnp.where(valid_ref[t] != 0, acc, NEG_BIG)


def _bias_table(rel_bias):
    bucket, valid = _band_tables()
    shape = (2 * SWA_BLOCK, SWA_BLOCK)
    return pl.pallas_call(
        _bias_kernel,
        grid=(SWA_Q_HEADS,),
        in_specs=[pl.BlockSpec(memory_space=pltpu.SMEM), pl.BlockSpec(shape, lambda h: (0, 0)),
                  pl.BlockSpec((2,) + shape, lambda h: (0, 0, 0))],
        out_specs=pl.BlockSpec((2, 1) + shape, lambda h: (0, h, 0, 0)),
        out_shape=jax.ShapeDtypeStruct((2, SWA_Q_HEADS) + shape, F32),
        compiler_params=_params("parallel"),
        name="swa_bias_table",
    )(rel_bias, jnp.asarray(bucket), jnp.asarray(valid))


def _swa_kernel(q_ref, kvc_ref, kvp_ref, bias_ref, sink_ref, o_ref):
    kv = jnp.concatenate([kvp_ref[...], kvc_ref[...]], axis=0)
    dh = SWA_HEAD_DIM
    nk = 2 * SWA_BLOCK
    k_all = kv[:, :SWA_KVDIM]
    lane = lax.broadcasted_iota(jnp.int32, k_all.shape, 1)
    zero = jnp.zeros_like(k_all)
    for kh in range(SWA_KV_HEADS):
        k_own = jnp.where((lane >= kh * dh) & (lane < (kh + 1) * dh), k_all, zero)
        k_other = pltpu.roll(k_own.astype(F32), dh, axis=1).astype(BF16)
        k_even, k_odd = (k_own, k_other) if kh == 0 else (k_other, k_own)
        k2 = jnp.concatenate([k_even, k_odd], axis=0)
        v_t = kv[:, SWA_KVDIM + kh * dh:SWA_KVDIM + (kh + 1) * dh].T
        pairs = [kh * (SWA_GROUP // 2) + p for p in range(SWA_GROUP // 2)]
        heads = [2 * p + r for p in pairs for r in range(2)]
        st = [_dot_nt(k2, q_ref[:, p * 2 * dh:(p + 1) * 2 * dh]) for p in pairs]
        scores = [st[i // 2][(i % 2) * nk:(i % 2 + 1) * nk] + bias_ref[0, h] for i, h in enumerate(heads)]
        sinks = [sink_ref[0:1, h:h + 1] for h in heads]
        maxes = [jnp.maximum(jnp.max(s, axis=0, keepdims=True), sk) for s, sk in zip(scores, sinks)]
        expo = [jnp.exp(s - m) for s, m in zip(scores, maxes)]
        sums = [jnp.sum(p, axis=0, keepdims=True) for p in expo]
        inv_den = [1.0 / (t + jnp.exp(sk - m)) for t, sk, m in zip(sums, sinks, maxes)]
        for i, p in enumerate(pairs):
            o_t = _dot(v_t, jnp.concatenate(expo[2 * i:2 * i + 2], axis=1).astype(BF16))
            z = jnp.concatenate([o_t[:, :SWA_BLOCK] * inv_den[2 * i], o_t[:, SWA_BLOCK:] * inv_den[2 * i + 1]],
                                axis=0)
            o_ref[:, p * 2 * dh:(p + 1) * 2 * dh] = z.T.astype(o_ref.dtype)


def _swa_attention(q, kv, bias, sinks, batch, seq_len):
    n = q.shape[0]
    nb = seq_len // SWA_BLOCK
    blk = SWA_BLOCK
    return pl.pallas_call(
        _swa_kernel,
        grid=(batch, nb),
        in_specs=[pl.BlockSpec((blk, SWA_QDIM), lambda b, j: (b * nb + j, 0)),
                  pl.BlockSpec((blk, 2 * SWA_KVDIM), lambda b, j: (b * nb + j, 0)),
                  pl.BlockSpec((blk, 2 * SWA_KVDIM), lambda b, j: (b * nb + jnp.maximum(j - 1, 0), 0)),
                  pl.BlockSpec((1, SWA_Q_HEADS, 2 * blk, blk), lambda b, j: (jnp.minimum(j, 1), 0, 0, 0)),
                  pl.BlockSpec((1, LANES), lambda b, j: (0, 0))],
        out_specs=pl.BlockSpec((blk, SWA_QDIM), lambda b, j: (b * nb + j, 0)),
        out_shape=jax.ShapeDtypeStruct((n, SWA_QDIM), BF16),
        compiler_params=_params("parallel", "arbitrary"),
        name="swa_attention",
    )(q, kv, kv, bias, sinks)


def _router_kernel(x_ref, wr_ref, route_ref, cnt_ref, run_ref, *, tm):
    @pl.when(pl.program_id(0) == 0)
    def _():
        run_ref[...] = jnp.zeros(run_ref.shape, F32)

    logits = _dot3(x_ref[...], wr_ref[...])
    lane = lax.broadcasted_iota(jnp.int32, logits.shape, 1)
    lane_f = lane.astype(F32)
    lg = jnp.where(lane < N_EXPERTS, logits, NEG_BIG)
    m1 = jnp.max(lg, axis=-1, keepdims=True)
    i1 = jnp.min(jnp.where(lg == m1, lane_f, float(LANES)), axis=-1, keepdims=True)
    oh1 = lane_f == i1
    lg2 = jnp.where(oh1, NEG_BIG, lg)
    m2 = jnp.max(lg2, axis=-1, keepdims=True)
    i2 = jnp.min(jnp.where(lg2 == m2, lane_f, float(LANES)), axis=-1, keepdims=True)
    oh2 = lane_f == i2
    e = jnp.exp(m2 - m1)
    w0 = 1.0 / (1.0 + e)
    w1 = e * w0

    cnt = jnp.where(oh1, 1.0, 0.0) + jnp.where(oh2, 1.0, 0.0)
    r = lax.broadcasted_iota(jnp.int32, (tm, tm), 0)
    c = lax.broadcasted_iota(jnp.int32, (tm, tm), 1)
    before = jnp.where(r > c, 1.0, 0.0).astype(BF16)
    excl = _dot(before, cnt.astype(BF16)) + run_ref[...]
    rank0 = jnp.sum(jnp.where(oh1, excl, 0.0), axis=-1, keepdims=True)
    rank1 = jnp.sum(jnp.where(oh2, excl, 0.0), axis=-1, keepdims=True)
    run = run_ref[...] + jnp.sum(cnt, axis=0, keepdims=True)
    run_ref[...] = run
    cnt_ref[...] = run

    vals = (i1, i2, rank0, rank1, w0, w1)
    out = jnp.zeros(logits.shape, F32)
    for idx, val in enumerate(vals):
        out = jnp.where(lane == idx, val, out)
    route_ref[...] = out


def _router(x, w_router_padded):
    n = x.shape[0]
    tm = TM_ROUTER
    return pl.pallas_call(
        functools.partial(_router_kernel, tm=tm),
        grid=(n // tm,),
        in_specs=[pl.BlockSpec((tm, D_MODEL), lambda i: (i, 0)), pl.BlockSpec((D_MODEL, LANES), lambda i: (0, 0))],
        out_specs=[pl.BlockSpec((tm, LANES), lambda i: (i, 0)), pl.BlockSpec((1, LANES), lambda i: (0, 0))],
        out_shape=[jax.ShapeDtypeStruct((n, LANES), F32), jax.ShapeDtypeStruct((1, LANES), F32)],
        scratch_shapes=[pltpu.VMEM((1, LANES), F32)],
        compiler_params=_params("arbitrary"),
        name="moe_router",
    )(x, w_router_padded)


def _dispatch_kernel(pos_ref, meta_ref, x_ref, xs_hbm, zero_ref, sem, pad_sem, *, td):
    i = pl.program_id(0)

    def row_copy(t, dst):
        return pltpu.make_async_copy(x_ref.at[pl.ds(t, 1)], xs_hbm.at[pl.ds(dst, 1)], sem)

    def start(t, carry):
        row_copy(t, pos_ref[0, 0, 2 * t]).start()
        row_copy(t, pos_ref[0, 0, 2 * t + 1]).start(priority=1)
        return carry

    lax.fori_loop(0, td, start, 0, unroll=DMA_UNROLL)

    @pl.when(i == pl.num_programs(0) - 1)
    def _():
        zero_ref[...] = jnp.zeros(zero_ref.shape, F32)

        zr = zero_ref.shape[0]

        def row_zero(dst):
            return pltpu.make_async_copy(zero_ref.at[pl.ds(0, 1)], xs_hbm.at[pl.ds(dst, 1)], pad_sem)

        def block_zero(blk):
            return pltpu.make_async_copy(zero_ref, xs_hbm.at[pl.ds(pl.multiple_of(blk * zr, zr), zr)], pad_sem)

        def zero_range(copy, lo, hi):
            lax.fori_loop(lo, hi, lambda r, carry: (copy(r).start(), carry)[1], 0)
            lax.fori_loop(lo, hi, lambda r, carry: (copy(0).wait(), carry)[1], 0)

        for e in range(N_EXPERTS):
            off, cnt, cnt_up, padded = (meta_ref[r, e] for r in range(4))
            zero_range(lambda r, off=off: row_zero(off + r), cnt, cnt_up)
            zero_range(block_zero, (off + cnt_up) // zr, (off + padded) // zr)

        used = meta_ref[0, N_EXPERTS - 1] + meta_ref[3, N_EXPERTS - 1]
        zero_range(block_zero, used // zr, xs_hbm.shape[0] // zr)

    for _ in range(2):
        pltpu.make_async_copy(x_ref, xs_hbm.at[pl.ds(0, td)], sem).wait()


def _dispatch(x, pos, meta, n_rows):
    n = x.shape[0]
    td = TD_DISPATCH
    pos3 = pos.reshape(n // td, 1, 2 * td)
    return pl.pallas_call(
        functools.partial(_dispatch_kernel, td=td),
        grid=(n // td,),
        in_specs=[pl.BlockSpec((1, 1, 2 * td), lambda i: (i, 0, 0), memory_space=pltpu.SMEM),
                  pl.BlockSpec(memory_space=pltpu.SMEM),
                  pl.BlockSpec((td, D_MODEL), lambda i: (i, 0))],
        out_specs=pl.BlockSpec(memory_space=pl.ANY),
        out_shape=jax.ShapeDtypeStruct((n_rows, D_MODEL), F32),
        scratch_shapes=[pltpu.VMEM((ZERO_ROWS, D_MODEL), F32), pltpu.SemaphoreType.DMA, pltpu.SemaphoreType.DMA],
        compiler_params=_params("arbitrary"),
        name="moe_dispatch",
    )(pos3, meta, x)


def _moe_kernel(te_ref, trb_ref, nt_ref, xs_ref, wg_ref, wu_ref, wd_ref, ys_ref, acc_ref, xb_ref):
    j = pl.program_id(0)
    f = pl.program_id(1)

    @pl.when(j < nt_ref[0])
    def _():
        @pl.when(f == 0)
        def _():
            xb_ref[...] = xs_ref[...].astype(BF16)

        xb = xb_ref[...]
        h = (_silu(_dot(xb, wg_ref[0])) * _dot(xb, wu_ref[0])).astype(BF16)
        part = _dot(h, wd_ref[0])

        @pl.when(f == 0)
        def _():
            acc_ref[...] = part

        @pl.when(f != 0)
        def _():
            acc_ref[...] += part

        @pl.when(f == pl.num_programs(1) - 1)
        def _():
            ys_ref[...] = acc_ref[...]

    @pl.when((j >= nt_ref[0]) & (f == 0))
    def _():
        ys_ref[...] = jnp.zeros(ys_ref.shape, F32)


def _moe_experts(xs, wg, wu, wd, tile_expert, tile_rowblock, n_tiles):
    n_rows = xs.shape[0]
    tm, tf = TM_MOE, TF_MOE
    max_tiles = n_rows // tm
    grid_spec = pltpu.PrefetchScalarGridSpec(
        num_scalar_prefetch=3,
        grid=(max_tiles, EXPERT_DIM // tf),
        in_specs=[pl.BlockSpec((tm, D_MODEL), lambda j, f, te, trb, nt: (trb[j], 0)),
                  pl.BlockSpec((1, D_MODEL, tf), lambda j, f, te, trb, nt: (te[j], 0, f)),
                  pl.BlockSpec((1, D_MODEL, tf), lambda j, f, te, trb, nt: (te[j], 0, f)),
                  pl.BlockSpec((1, tf, D_MODEL), lambda j, f, te, trb, nt: (te[j], f, 0))],
        out_specs=pl.BlockSpec((tm, D_MODEL), lambda j, f, te, trb, nt: (j, 0)),
        scratch_shapes=[pltpu.VMEM((tm, D_MODEL), F32), pltpu.VMEM((tm, D_MODEL), BF16)],
    )
    return pl.pallas_call(
        _moe_kernel,
        grid_spec=grid_spec,
        out_shape=jax.ShapeDtypeStruct((n_rows, D_MODEL), F32),
        compiler_params=_params("arbitrary", "arbitrary"),
        name="moe_experts",
    )(tile_expert, tile_rowblock, n_tiles, xs, wg, wu, wd)


def _combine_kernel(pos_ref, posn_ref, x_ref, route_ref, ys_hbm, g_ref, b_ref, o_ref, y_ref, sem, *, tc):
    i = pl.program_id(0)
    slot = i % 2

    def issue(p_ref, s):
        def start(t, carry):
            for kk in range(2):
                pltpu.make_async_copy(ys_hbm.at[pl.ds(p_ref[0, 0, 2 * t + kk], 1)],
                                      y_ref.at[s, kk, pl.ds(t, 1)], sem.at[s]).start(priority=kk)
            return carry

        lax.fori_loop(0, tc, start, 0, unroll=DMA_UNROLL)

    @pl.when(i == 0)
    def _():
        issue(pos_ref, 0)

    @pl.when(i + 1 < pl.num_programs(0))
    def _():
        issue(posn_ref, 1 - slot)

    for kk in range(2):
        pltpu.make_async_copy(ys_hbm.at[pl.ds(0, tc)], y_ref.at[slot, kk], sem.at[slot]).wait()

    route = route_ref[...]
    f = route[:, 4:5] * y_ref[slot, 0] + route[:, 5:6] * y_ref[slot, 1]
    o_ref[...] = _layer_norm(ALPHA * x_ref[...] + f, g_ref[...], b_ref[...])


def _combine_ln(x, route, pos, ys, g, b):
    n = x.shape[0]
    tc = TC_COMBINE
    nt = n // tc
    pos3 = pos.reshape(nt, 1, 2 * tc)
    return pl.pallas_call(
        functools.partial(_combine_kernel, tc=tc),
        grid=(nt,),
        in_specs=[pl.BlockSpec((1, 1, 2 * tc), lambda i: (i, 0, 0), memory_space=pltpu.SMEM),
                  pl.BlockSpec((1, 1, 2 * tc), lambda i: (jnp.minimum(i + 1, nt - 1), 0, 0), memory_space=pltpu.SMEM),
                  pl.BlockSpec((tc, D_MODEL), lambda i: (i, 0)),
                  pl.BlockSpec((tc, LANES), lambda i: (i, 0)),
                  pl.BlockSpec(memory_space=pl.ANY),
                  pl.BlockSpec((1, D_MODEL), lambda i: (0, 0)), pl.BlockSpec((1, D_MODEL), lambda i: (0, 0))],
        out_specs=pl.BlockSpec((tc, D_MODEL), lambda i: (i, 0)),
        out_shape=jax.ShapeDtypeStruct((n, D_MODEL), F32),
        scratch_shapes=[pltpu.VMEM((2, 2, tc, D_MODEL), F32), pltpu.SemaphoreType.DMA((2,))],
        compiler_params=_params("arbitrary"),
        name="moe_combine_ln",
    )(pos3, pos3, x, route, ys, g, b)


def _moe_layer(x, w_router, wg, wu, wd, g, b):
    n = x.shape[0]
    tm = TM_MOE
    wr = jnp.pad(w_router, ((0, 0), (0, LANES - N_EXPERTS)))
    route, counts_f = _router(x, wr)

    counts = counts_f[0, :N_EXPERTS].astype(jnp.int32)
    padded = ((counts + tm - 1) // tm) * tm
    offs = jnp.cumsum(padded) - padded
    eids = jnp.arange(N_EXPERTS, dtype=jnp.int32)
    e01 = route[:, 0:2].astype(jnp.int32)
    rank01 = route[:, 2:4].astype(jnp.int32)
    base = jnp.sum(jnp.where(e01[:, :, None] == eids[None, None, :], offs[None, None, :], 0), axis=-1)
    pos = (base + rank01).reshape(-1)
    counts_up = jnp.minimum(padded, ((counts + ZERO_ROWS - 1) // ZERO_ROWS) * ZERO_ROWS)
    meta = jnp.stack([offs, counts, counts_up, padded]).astype(jnp.int32)

    n_rows = 2 * n + N_EXPERTS * tm
    max_tiles = n_rows // tm
    tile_end = jnp.cumsum(padded // tm)
    n_tiles = tile_end[-1]
    tj = jnp.minimum(jnp.arange(max_tiles, dtype=jnp.int32), n_tiles - 1)
    tile_expert = jnp.sum((tj[:, None] >= tile_end[None, :]).astype(jnp.int32), axis=-1)
    tile_expert = jnp.minimum(tile_expert, N_EXPERTS - 1).astype(jnp.int32)

    xs = _dispatch(x, pos, meta, n_rows)
    ys = _moe_experts(xs, wg, wu, wd, tile_expert, tj.astype(jnp.int32), n_tiles.reshape(1).astype(jnp.int32))
    return _combine_ln(x, route, pos, ys, g, b)


def kernel(x, a_w_in, a_conv_w, a_a_log, a_dt_bias, a_norm_w, a_w_out, b_w_in, b_b_in, b_sinks, b_w_out, rel_bias,
           ffn_w_gate, ffn_w_up, ffn_w_down, moe_router, moe_w_gate, moe_w_up, moe_w_down, ln_g, ln_b):
    batch, seq_len, _ = x.shape
    n = batch * seq_len
    x0 = x.reshape(n, D_MODEL)
    ln_g = ln_g.reshape(DEPTH, 2, 1, D_MODEL)
    ln_b = ln_b.reshape(DEPTH, 2, 1, D_MODEL)

    w_in = a_w_in[0]
    w_all = jnp.pad(w_in, ((0, 0), (0, LANES - 2 * GDN_V_HEADS))).astype(BF16)
    pad_gate = lambda p: jnp.pad(p.reshape(1, GDN_V_HEADS), ((0, 0), (GDN_V_HEADS, LANES - 2 * GDN_V_HEADS)))
    gate_params = jnp.concatenate([pad_gate(a_a_log[0]), pad_gate(a_dt_bias[0])], axis=0)
    q, k, v, zs, gb = _gdn_inproj(x0, w_all, a_conv_w[0], gate_params, seq_len)
    o = _gdn_chunk(q, k, v, zs, gb, a_norm_w[0].reshape(1, GDN_HEAD), batch, seq_len)
    x1 = _proj_ln(o, a_w_out[0].astype(BF16), x0, ln_g[0, 0], ln_b[0, 0], "gdn_outproj_ln")
    x2 = _ffn_ln(x1, ffn_w_gate[0].astype(BF16), ffn_w_up[0].astype(BF16), ffn_w_down[0].astype(BF16),
                 ln_g[0, 1], ln_b[0, 1])

    qa, kva = _swa_inproj(x2, b_w_in[0].astype(BF16), b_b_in[0].reshape(1, -1))
    bias = _bias_table(rel_bias)
    sinks = jnp.pad(b_sinks[0].reshape(1, SWA_Q_HEADS), ((0, 0), (0, LANES - SWA_Q_HEADS)))
    oa = _swa_attention(qa, kva, bias, sinks, batch, seq_len)
    x3 = _proj_ln(oa, b_w_out[0].astype(BF16), x2, ln_g[1, 0], ln_b[1, 0], "swa_outproj_ln")
    x4 = _moe_layer(x3, moe_router[0], moe_w_gate[0].astype(BF16), moe_w_up[0].astype(BF16),
                    moe_w_down[0].astype(BF16), ln_g[1, 1], ln_b[1, 1])
    return x4.reshape(batch, seq_len, D_MODEL)
```

```python
import functools
import math

import numpy as np
import jax
import jax.numpy as jnp
from jax import lax
from jax.experimental import pallas as pl
from jax.experimental.pallas import tpu as pltpu

F32 = jnp.float32
BF16 = jnp.bfloat16

D_MODEL = 1024
DEPTH = 2
ALPHA = (2.0 * DEPTH) ** 0.25
LN_EPS = 1e-5

GDN_K_HEADS = 4
GDN_V_HEADS = 8
GDN_HEAD = 128
GDN_KDIM = GDN_K_HEADS * GDN_HEAD
GDN_VDIM = GDN_V_HEADS * GDN_HEAD
GDN_CONV = 4
GDN_CHUNK = 64
GDN_QKV = 2 * GDN_KDIM + GDN_VDIM
GDN_EPS = 1e-6

SWA_Q_HEADS = 16
SWA_KV_HEADS = 2
SWA_GROUP = SWA_Q_HEADS // SWA_KV_HEADS
SWA_HEAD_DIM = 64
SWA_WINDOW = 128
SWA_BLOCK = 128
SWA_QDIM = SWA_Q_HEADS * SWA_HEAD_DIM
SWA_KVDIM = SWA_KV_HEADS * SWA_HEAD_DIM
REL_BUCKETS = 32
REL_MAX_DIST = 128

FFN_DIM = 2816
N_EXPERTS = 8
EXPERT_DIM = 3584

LANES = 128
SUBLANES = 8
NEG_BIG = -1e30
VMEM_LIMIT = 56 * 1024 * 1024

TM_GDN_IN = 512
TM_PROJ = 512
TM_FFN = 512
TF_FFN = 2816
TM_SWA_IN = 512
TM_ROUTER = 512
TM_MOE = 512
TF_MOE = 1792
MOE_TILES_PER_WINDOW = 3
TD_DISPATCH = 512
TC_COMBINE = 256
ZERO_ROWS = 64
DMA_UNROLL = 8


def _params(*sem):
    return pltpu.CompilerParams(dimension_semantics=sem, vmem_limit_bytes=VMEM_LIMIT)


def _dot(a, b):
    return jnp.dot(a, b, preferred_element_type=F32)


def _dot_nt(a, b):
    return lax.dot_general(a, b, (((1,), (1,)), ((), ())), preferred_element_type=F32)


def _dot_tn(a, b):
    return lax.dot_general(a, b, (((0,), (0,)), ((), ())), preferred_element_type=F32)


def _split(x):
    hi = x.astype(BF16)
    lo = (x - hi.astype(F32)).astype(BF16)
    return hi, lo


def _dot3(a, b):
    ah, al = _split(a)
    bh, bl = _split(b)
    return _dot(ah, bh) + (_dot(ah, bl) + _dot(al, bh))


def _silu(x):
    return x * jax.nn.sigmoid(x)


def _layer_norm(y, g, b):
    mu = jnp.mean(y, axis=-1, keepdims=True)
    yc = y - mu
    var = jnp.mean(yc * yc, axis=-1, keepdims=True)
    return yc * lax.rsqrt(var + LN_EPS) * g + b


def _gdn_inproj_kernel(x_ref, w_ref, convw_ref, gp_ref,
                       q_ref, k_ref, v_ref, z_ref, gb_ref, ext_ref, *, tm, tiles_per_seq):
    i = pl.program_id(0)
    xb = x_ref[...].astype(BF16)

    @pl.when(i % tiles_per_seq == 0)
    def _():
        ext_ref[0:SUBLANES, :] = jnp.zeros((SUBLANES, GDN_QKV), F32)

    @pl.when(i % tiles_per_seq != 0)
    def _():
        ext_ref[0:SUBLANES, :] = ext_ref[tm:tm + SUBLANES, :]

    ext_ref[SUBLANES:, :] = _dot(xb, w_ref[:, :GDN_QKV])
    z_ref[...] = _silu(_dot(xb, w_ref[:, GDN_QKV:GDN_QKV + GDN_VDIM]))

    n_chunks = GDN_QKV // LANES
    for c in range(n_chunks):
        cs = slice(c * LANES, (c + 1) * LANES)
        acc = convw_ref[GDN_CONV - 1:GDN_CONV, cs] * ext_ref[SUBLANES:SUBLANES + tm, cs]
        for j in range(GDN_CONV - 1):
            off = SUBLANES - (GDN_CONV - 1) + j
            acc = acc + convw_ref[j:j + 1, cs] * ext_ref[off:off + tm, cs]
        y = _silu(acc)
        if c < 2 * GDN_K_HEADS:
            y = y * lax.rsqrt(jnp.sum(y * y, axis=-1, keepdims=True) + GDN_EPS)
            if c < GDN_K_HEADS:
                q_ref[:, cs] = y * (GDN_HEAD ** -0.5)
            else:
                k_ref[:, (c - GDN_K_HEADS) * LANES:(c - GDN_K_HEADS + 1) * LANES] = y
        else:
            cv = c - 2 * GDN_K_HEADS
            v_ref[:, cv * LANES:(cv + 1) * LANES] = y

    ba = _dot(xb, w_ref[:, GDN_QKV + GDN_VDIM:])
    lane = lax.broadcasted_iota(jnp.int32, ba.shape, 1)
    sp = ba + gp_ref[1:2, :]
    softplus = jnp.maximum(sp, 0.0) + jnp.log(1.0 + jnp.exp(-jnp.abs(sp)))
    g = -jnp.exp(gp_ref[0:1, :]) * softplus
    gb_ref[...] = jnp.where(lane < GDN_V_HEADS, jax.nn.sigmoid(ba), g)


def _gdn_inproj(x2d, w_all, conv_w, gate_params, seq_len):
    n = x2d.shape[0]
    tm = TM_GDN_IN
    kern = functools.partial(_gdn_inproj_kernel, tm=tm, tiles_per_seq=seq_len // tm)
    full = lambda shape: pl.BlockSpec(shape, lambda i: (0,) * len(shape))
    rows = lambda width: pl.BlockSpec((tm, width), lambda i: (i, 0))
    return pl.pallas_call(
        kern,
        grid=(n // tm,),
        in_specs=[rows(D_MODEL), full(w_all.shape), full(conv_w.shape), full(gate_params.shape)],
        out_specs=[rows(GDN_KDIM), rows(GDN_KDIM), rows(GDN_VDIM), rows(GDN_VDIM), rows(LANES)],
        out_shape=[jax.ShapeDtypeStruct((n, GDN_KDIM), F32), jax.ShapeDtypeStruct((n, GDN_KDIM), F32),
                   jax.ShapeDtypeStruct((n, GDN_VDIM), F32), jax.ShapeDtypeStruct((n, GDN_VDIM), F32),
                   jax.ShapeDtypeStruct((n, LANES), F32)],
        scratch_shapes=[pltpu.VMEM((tm + SUBLANES, GDN_QKV), F32)],
        compiler_params=_params("arbitrary"),
        name="gdn_inproj",
    )(x2d, w_all, conv_w, gate_params)


GDN_CHUNKS_PER_STEP = 2
NEUMANN_BLOCK = 4


def _bf16_all(xs):
    return [x.astype(BF16) for x in xs]


def _dot_all(a_list, b_list):
    return [_dot(a, b) for a, b in zip(a_list, b_list)]


def _unit_lower_inverse_all(a_list, row, col):
    shift = int(math.log2(NEUMANN_BLOCK))
    eye = jnp.where(row == col, 1.0, 0.0).astype(F32)
    on_diag_block = (row >> shift) == (col >> shift)
    d = [jnp.where(on_diag_block, a, 0.0) for a in a_list]
    d_b = _bf16_all(d)
    x = [eye - di for di in d]
    d2_b = _bf16_all(_dot_all(d_b, d_b))
    x = [xi + t for xi, t in zip(x, _dot_all(_bf16_all(x), d2_b))]
    size = NEUMANN_BLOCK
    while size < a_list[0].shape[0]:
        shift = int(math.log2(size))
        rbl, cbl = row >> shift, col >> shift
        below = ((rbl & 1) == 1) & (cbl == rbl - 1)
        l_b = _bf16_all([jnp.where(below, a, 0.0) for a in a_list])
        x_b = _bf16_all(x)
        xl_b = _bf16_all(_dot_all(x_b, l_b))
        x = [xi - t for xi, t in zip(x, _dot_all(xl_b, x_b))]
        size *= 2
    return x


def _gdn_chunk_kernel(q_ref, k_ref, v_ref, z_ref, gb_ref, nw_ref, o_ref, s_ref, *, nb, nck):
    c = GDN_CHUNK
    nh = GDN_V_HEADS

    @pl.when(pl.program_id(0) == 0)
    def _():
        s_ref[...] = jnp.zeros(s_ref.shape, F32)

    row = lax.broadcasted_iota(jnp.int32, (c, c), 0)
    col = lax.broadcasted_iota(jnp.int32, (c, c), 1)
    causal = row >= col
    strict = row > col
    tril = jnp.where(causal, 1.0, 0.0).astype(BF16)
    rs = lambda ci: slice(ci * c, (ci + 1) * c)
    ks = lambda h: slice((h // (nh // GDN_K_HEADS)) * GDN_HEAD, (h // (nh // GDN_K_HEADS) + 1) * GDN_HEAD)
    vs = lambda h: slice(h * GDN_HEAD, (h + 1) * GDN_HEAD)

    gates = {}
    for ci in range(nck):
        for b in range(nb):
            gb = gb_ref[b, rs(ci), :]
            g_hi, g_lo = _split(gb)
            gc = _dot(tril, g_hi) + _dot(tril, g_lo)
            g_last = gc[c - 1:c, :]
            gates[ci, b] = dict(gb=gb, gc=gc, gc_t=gc.T, eg=jnp.exp(gc), e_last=jnp.exp(g_last),
                                e_rest=jnp.exp(g_last - gc))

    items = [(ci, b, h) for ci in range(nck) for b in range(nb) for h in range(nh)]

    a_kk, a_qk, k_beta = [], {}, {}
    for it in items:
        ci, b, h = it
        gt = gates[ci, b]
        k = k_ref[b, rs(ci), ks(h)]
        kb = k * gt["gb"][:, h:h + 1]
        kq = jnp.concatenate([kb, q_ref[b, rs(ci), ks(h)]], axis=0).astype(BF16)
        gram = _dot_nt(kq, k.astype(BF16))
        gl = nh + h
        decay = jnp.exp(jnp.where(causal, gt["gc"][:, gl:gl + 1] - gt["gc_t"][gl:gl + 1, :], NEG_BIG))
        a_kk.append(jnp.where(strict, gram[:c] * decay, 0.0))
        a_qk[it] = (gram[c:] * decay).astype(BF16)
        k_beta[it] = kb

    t_inv = dict(zip(items, _unit_lower_inverse_all(a_kk, row, col)))

    uw = {}
    for it in items:
        ci, b, h = it
        gt = gates[ci, b]
        gl = nh + h
        rhs = jnp.concatenate([v_ref[b, rs(ci), vs(h)] * gt["gb"][:, h:h + 1],
                               k_beta[it] * gt["eg"][:, gl:gl + 1]], axis=1)
        uw[it] = _dot(t_inv[it].astype(BF16), rhs.astype(BF16))

    for ci in range(nck):
        chunk_items = [(ci, b, h) for b in range(nb) for h in range(nh)]

        ws_qs, states = {}, {}
        for it in chunk_items:
            _, b, h = it
            gl = nh + h
            s = s_ref[b * nh + h]
            wq = jnp.concatenate([uw[it][:, GDN_HEAD:],
                                  q_ref[b, rs(ci), ks(h)] * gates[ci, b]["eg"][:, gl:gl + 1]], axis=0)
            ws_qs[it] = _dot(wq.astype(BF16), s.astype(BF16))
            states[it] = s

        for it in chunk_items:
            _, b, h = it
            gt = gates[ci, b]
            gl = nh + h
            v_new = (uw[it][:, :GDN_HEAD] - ws_qs[it][:c]).astype(BF16)
            o = ws_qs[it][c:] + _dot(a_qk[it], v_new)
            k_dec = (k_ref[b, rs(ci), ks(h)] * gt["e_rest"][:, gl:gl + 1]).astype(BF16)
            s_ref[b * nh + h] = states[it] * gt["e_last"][:, gl:gl + 1] + _dot_tn(k_dec, v_new)
            o = o * lax.rsqrt(jnp.mean(o * o, axis=-1, keepdims=True) + GDN_EPS) * nw_ref[...]
            o_ref[b, rs(ci), vs(h)] = (o * z_ref[b, rs(ci), vs(h)]).astype(o_ref.dtype)


def _gdn_chunk(q, k, v, zs, gb, norm_w, batch, seq_len):
    rows_per_step = GDN_CHUNK * GDN_CHUNKS_PER_STEP
    seq = lambda a: a.reshape(batch, seq_len, a.shape[-1])
    rows = lambda width: pl.BlockSpec((batch, rows_per_step, width), lambda j: (0, j, 0))
    out = pl.pallas_call(
        functools.partial(_gdn_chunk_kernel, nb=batch, nck=GDN_CHUNKS_PER_STEP),
        grid=(seq_len // rows_per_step,),
        in_specs=[rows(GDN_KDIM), rows(GDN_KDIM), rows(GDN_VDIM), rows(GDN_VDIM), rows(LANES),
                  pl.BlockSpec((1, GDN_HEAD), lambda j: (0, 0))],
        out_specs=rows(GDN_VDIM),
        out_shape=jax.ShapeDtypeStruct((batch, seq_len, GDN_VDIM), BF16),
        scratch_shapes=[pltpu.VMEM((batch * GDN_V_HEADS, GDN_HEAD, GDN_HEAD), F32)],
        compiler_params=_params("arbitrary"),
        name="gdn_chunk",
    )(seq(q), seq(k), seq(v), seq(zs), seq(gb), norm_w)
    return out.reshape(batch * seq_len, GDN_VDIM)


def _proj_ln_kernel(a_ref, w_ref, r_ref, g_ref, b_ref, o_ref):
    y = ALPHA * r_ref[...] + _dot(a_ref[...], w_ref[...])
    o_ref[...] = _layer_norm(y, g_ref[...], b_ref[...])


def _proj_ln(a, w, res, g, b, name):
    n, kdim = a.shape
    tm = TM_PROJ
    return pl.pallas_call(
        _proj_ln_kernel,
        grid=(n // tm,),
        in_specs=[pl.BlockSpec((tm, kdim), lambda i: (i, 0)), pl.BlockSpec(w.shape, lambda i: (0, 0)),
                  pl.BlockSpec((tm, D_MODEL), lambda i: (i, 0)),
                  pl.BlockSpec((1, D_MODEL), lambda i: (0, 0)), pl.BlockSpec((1, D_MODEL), lambda i: (0, 0))],
        out_specs=pl.BlockSpec((tm, D_MODEL), lambda i: (i, 0)),
        out_shape=jax.ShapeDtypeStruct((n, D_MODEL), F32),
        compiler_params=_params("parallel"),
        name=name,
    )(a, w, res, g, b)


def _ffn_kernel(x_ref, wg_ref, wu_ref, wd_ref, g_ref, b_ref, o_ref, acc_ref, xb_ref):
    f = pl.program_id(1)

    @pl.when(f == 0)
    def _():
        xb_ref[...] = x_ref[...].astype(BF16)

    xb = xb_ref[...]
    h = (_silu(_dot(xb, wg_ref[...])) * _dot(xb, wu_ref[...])).astype(BF16)
    part = _dot(h, wd_ref[...])

    @pl.when(f == 0)
    def _():
        acc_ref[...] = part

    @pl.when(f != 0)
    def _():
        acc_ref[...] += part

    @pl.when(f == pl.num_programs(1) - 1)
    def _():
        o_ref[...] = _layer_norm(ALPHA * x_ref[...] + acc_ref[...], g_ref[...], b_ref[...])


def _ffn_ln(x, wg, wu, wd, g, b):
    n = x.shape[0]
    tm, tf = TM_FFN, TF_FFN
    resident = pl.Buffered(1) if tf == FFN_DIM else None
    return pl.pallas_call(
        _ffn_kernel,
        grid=(n // tm, FFN_DIM // tf),
        in_specs=[pl.BlockSpec((tm, D_MODEL), lambda i, f: (i, 0)),
                  pl.BlockSpec((D_MODEL, tf), lambda i, f: (0, f), pipeline_mode=resident),
                  pl.BlockSpec((D_MODEL, tf), lambda i, f: (0, f), pipeline_mode=resident),
                  pl.BlockSpec((tf, D_MODEL), lambda i, f: (f, 0), pipeline_mode=resident),
                  pl.BlockSpec((1, D_MODEL), lambda i, f: (0, 0)), pl.BlockSpec((1, D_MODEL), lambda i, f: (0, 0))],
        out_specs=pl.BlockSpec((tm, D_MODEL), lambda i, f: (i, 0)),
        out_shape=jax.ShapeDtypeStruct((n, D_MODEL), F32),
        scratch_shapes=[pltpu.VMEM((tm, D_MODEL), F32), pltpu.VMEM((tm, D_MODEL), BF16)],
        compiler_params=_params("parallel", "arbitrary"),
        name="ffn_ln",
    )(x, wg, wu, wd, g, b)


def _swa_inproj_kernel(x_ref, w_ref, b_ref, q_ref, kv_ref):
    p = _dot(x_ref[...].astype(BF16), w_ref[...]) + b_ref[...]
    q_ref[...] = (p[:, :SWA_QDIM] * (SWA_HEAD_DIM ** -0.5)).astype(BF16)
    kv_ref[...] = p[:, SWA_QDIM:].astype(BF16)


def _swa_inproj(x, w, b):
    n = x.shape[0]
    tm = TM_SWA_IN
    return pl.pallas_call(
        _swa_inproj_kernel,
        grid=(n // tm,),
        in_specs=[pl.BlockSpec((tm, D_MODEL), lambda i: (i, 0)), pl.BlockSpec(w.shape, lambda i: (0, 0)),
                  pl.BlockSpec(b.shape, lambda i: (0, 0))],
        out_specs=[pl.BlockSpec((tm, SWA_QDIM), lambda i: (i, 0)), pl.BlockSpec((tm, 2 * SWA_KVDIM), lambda i: (i, 0))],
        out_shape=[jax.ShapeDtypeStruct((n, SWA_QDIM), BF16), jax.ShapeDtypeStruct((n, 2 * SWA_KVDIM), BF16)],
        compiler_params=_params("parallel"),
        name="swa_inproj",
    )(x, w, b)


def _band_tables():
    qi = np.arange(SWA_BLOCK)[:, None]
    kj = np.arange(2 * SWA_BLOCK)[None, :]
    dist = qi + SWA_BLOCK - kj
    d = np.maximum(dist, 0)
    max_exact = REL_BUCKETS // 2
    df = np.maximum(d, 1).astype(np.float32)
    large = max_exact + (np.log(df / np.float32(max_exact)) / np.float32(math.log(REL_MAX_DIST / max_exact))
                         * np.float32(REL_BUCKETS - max_exact)).astype(np.int32)
    large = np.minimum(large, REL_BUCKETS - 1)
    bucket = np.where(d < max_exact, d, large).astype(np.int32)
    band_ok = (dist >= 0) & (dist < SWA_WINDOW)
    valid = np.stack([band_ok & (kj >= SWA_BLOCK), band_ok]).astype(np.int32)
    return np.ascontiguousarray(bucket.T), np.ascontiguousarray(valid.transpose(0, 2, 1))


def _bias_kernel(relb_ref, bucket_ref, valid_ref, o_ref):
    h = pl.program_id(0)
    bucket = bucket_ref[...]
    acc = jnp.zeros(bucket.shape, F32)
    for b in range(REL_BUCKETS):
        acc = jnp.where(bucket == b, relb_ref[b, h], acc)
    for t in range(2):
        o_ref[t, 0] = jnp.where(valid_ref[t] != 0, acc, NEG_BIG)


def _bias_table(rel_bias):
    bucket, valid = _band_tables()
    shape = (2 * SWA_BLOCK, SWA_BLOCK)
    return pl.pallas_call(
        _bias_kernel,
        grid=(SWA_Q_HEADS,),
        in_specs=[pl.BlockSpec(memory_space=pltpu.SMEM), pl.BlockSpec(shape, lambda h: (0, 0)),
                  pl.BlockSpec((2,) + shape, lambda h: (0, 0, 0))],
        out_specs=pl.BlockSpec((2, 1) + shape, lambda h: (0, h, 0, 0)),
        out_shape=jax.ShapeDtypeStruct((2, SWA_Q_HEADS) + shape, F32),
        compiler_params=_params("parallel"),
        name="swa_bias_table",
    )(rel_bias, jnp.asarray(bucket), jnp.asarray(valid))


def _swa_kernel(q_ref, kvc_ref, kvp_ref, bias_ref, sink_ref, o_ref):
    kv = jnp.concatenate([kvp_ref[...], kvc_ref[...]], axis=0)
    dh = SWA_HEAD_DIM
    nk = 2 * SWA_BLOCK
    k_all = kv[:, :SWA_KVDIM]
    lane = lax.broadcasted_iota(jnp.int32, k_all.shape, 1)
    zero = jnp.zeros_like(k_all)
    for kh in range(SWA_KV_HEADS):
        k_own = jnp.where((lane >= kh * dh) & (lane < (kh + 1) * dh), k_all, zero)
        k_other = pltpu.roll(k_own.astype(F32), dh, axis=1).astype(BF16)
        k_even, k_odd = (k_own, k_other) if kh == 0 else (k_other, k_own)
        k2 = jnp.concatenate([k_even, k_odd], axis=0)
        v_t = kv[:, SWA_KVDIM + kh * dh:SWA_KVDIM + (kh + 1) * dh].T
        pairs = [kh * (SWA_GROUP // 2) + p for p in range(SWA_GROUP // 2)]
        heads = [2 * p + r for p in pairs for r in range(2)]
        st = [_dot_nt(k2, q_ref[:, p * 2 * dh:(p + 1) * 2 * dh]) for p in pairs]
        scores = [st[i // 2][(i % 2) * nk:(i % 2 + 1) * nk] + bias_ref[0, h] for i, h in enumerate(heads)]
        sinks = [sink_ref[0:1, h:h + 1] for h in heads]
        maxes = [jnp.maximum(jnp.max(s, axis=0, keepdims=True), sk) for s, sk in zip(scores, sinks)]
        expo = [jnp.exp(s - m) for s, m in zip(scores, maxes)]
        sums = [jnp.sum(p, axis=0, keepdims=True) for p in expo]
        inv_den = [1.0 / (t + jnp.exp(sk - m)) for t, sk, m in zip(sums, sinks, maxes)]
        for i, p in enumerate(pairs):
            o_t = _dot(v_t, jnp.concatenate(expo[2 * i:2 * i + 2], axis=1).astype(BF16))
            z = jnp.concatenate([o_t[:, :SWA_BLOCK] * inv_den[2 * i], o_t[:, SWA_BLOCK:] * inv_den[2 * i + 1]],
                                axis=0)
            o_ref[:, p * 2 * dh:(p + 1) * 2 * dh] = z.T.astype(o_ref.dtype)


def _swa_attention(q, kv, bias, sinks, batch, seq_len):
    n = q.shape[0]
    nb = seq_len // SWA_BLOCK
    blk = SWA_BLOCK
    return pl.pallas_call(
        _swa_kernel,
        grid=(batch, nb),
        in_specs=[pl.BlockSpec((blk, SWA_QDIM), lambda b, j: (b * nb + j, 0)),
                  pl.BlockSpec((blk, 2 * SWA_KVDIM), lambda b, j: (b * nb + j, 0)),
                  pl.BlockSpec((blk, 2 * SWA_KVDIM), lambda b, j: (b * nb + jnp.maximum(j - 1, 0), 0)),
                  pl.BlockSpec((1, SWA_Q_HEADS, 2 * blk, blk), lambda b, j: (jnp.minimum(j, 1), 0, 0, 0)),
                  pl.BlockSpec((1, LANES), lambda b, j: (0, 0))],
        out_specs=pl.BlockSpec((blk, SWA_QDIM), lambda b, j: (b * nb + j, 0)),
        out_shape=jax.ShapeDtypeStruct((n, SWA_QDIM), BF16),
        compiler_params=_params("parallel", "arbitrary"),
        name="swa_attention",
    )(q, kv, kv, bias, sinks)


def _router_kernel(x_ref, wr_ref, route_ref, cnt_ref, run_ref, *, tm):
    @pl.when(pl.program_id(0) == 0)
    def _():
        run_ref[...] = jnp.zeros(run_ref.shape, F32)

    logits = _dot3(x_ref[...], wr_ref[...])
    lane = lax.broadcasted_iota(jnp.int32, logits.shape, 1)
    lane_f = lane.astype(F32)
    lg = jnp.where(lane < N_EXPERTS, logits, NEG_BIG)
    m1 = jnp.max(lg, axis=-1, keepdims=True)
    i1 = jnp.min(jnp.where(lg == m1, lane_f, float(LANES)), axis=-1, keepdims=True)
    oh1 = lane_f == i1
    lg2 = jnp.where(oh1, NEG_BIG, lg)
    m2 = jnp.max(lg2, axis=-1, keepdims=True)
    i2 = jnp.min(jnp.where(lg2 == m2, lane_f, float(LANES)), axis=-1, keepdims=True)
    oh2 = lane_f == i2
    e = jnp.exp(m2 - m1)
    w0 = 1.0 / (1.0 + e)
    w1 = e * w0

    cnt = jnp.where(oh1, 1.0, 0.0) + jnp.where(oh2, 1.0, 0.0)
    r = lax.broadcasted_iota(jnp.int32, (tm, tm), 0)
    c = lax.broadcasted_iota(jnp.int32, (tm, tm), 1)
    before = jnp.where(r > c, 1.0, 0.0).astype(BF16)
    excl = _dot(before, cnt.astype(BF16)) + run_ref[...]
    rank0 = jnp.sum(jnp.where(oh1, excl, 0.0), axis=-1, keepdims=True)
    rank1 = jnp.sum(jnp.where(oh2, excl, 0.0), axis=-1, keepdims=True)
    run = run_ref[...] + jnp.sum(cnt, axis=0, keepdims=True)
    run_ref[...] = run
    cnt_ref[...] = run

    vals = (i1, i2, rank0, rank1, w0, w1)
    out = jnp.zeros(logits.shape, F32)
    for idx, val in enumerate(vals):
        out = jnp.where(lane == idx, val, out)
    route_ref[...] = out


def _router(x, w_router_padded):
    n = x.shape[0]
    tm = TM_ROUTER
    return pl.pallas_call(
        functools.partial(_router_kernel, tm=tm),
        grid=(n // tm,),
        in_specs=[pl.BlockSpec((tm, D_MODEL), lambda i: (i, 0)), pl.BlockSpec((D_MODEL, LANES), lambda i: (0, 0))],
        out_specs=[pl.BlockSpec((tm, LANES), lambda i: (i, 0)), pl.BlockSpec((1, LANES), lambda i: (0, 0))],
        out_shape=[jax.ShapeDtypeStruct((n, LANES), F32), jax.ShapeDtypeStruct((1, LANES), F32)],
        scratch_shapes=[pltpu.VMEM((1, LANES), F32)],
        compiler_params=_params("arbitrary"),
        name="moe_router",
    )(x, w_router_padded)


def _dispatch_kernel(pos_ref, meta_ref, x_ref, xs_hbm, zero_ref, sem, pad_sem, *, td):
    i = pl.program_id(0)

    def row_copy(t, dst):
        return pltpu.make_async_copy(x_ref.at[pl.ds(t, 1)], xs_hbm.at[pl.ds(dst, 1)], sem)

    def start(t, carry):
        row_copy(t, pos_ref[0, 0, 2 * t]).start()
        row_copy(t, pos_ref[0, 0, 2 * t + 1]).start(priority=1)
        return carry

    lax.fori_loop(0, td, start, 0, unroll=DMA_UNROLL)

    @pl.when(i == pl.num_programs(0) - 1)
    def _():
        zero_ref[...] = jnp.zeros(zero_ref.shape, F32)

        zr = zero_ref.shape[0]

        def row_zero(dst):
            return pltpu.make_async_copy(zero_ref.at[pl.ds(0, 1)], xs_hbm.at[pl.ds(dst, 1)], pad_sem)

        def block_zero(blk):
            return pltpu.make_async_copy(zero_ref, xs_hbm.at[pl.ds(pl.multiple_of(blk * zr, zr), zr)], pad_sem)

        def zero_range(copy, lo, hi):
            lax.fori_loop(lo, hi, lambda r, carry: (copy(r).start(), carry)[1], 0)
            lax.fori_loop(lo, hi, lambda r, carry: (copy(0).wait(), carry)[1], 0)

        for e in range(N_EXPERTS):
            off, cnt, cnt_up, padded = (meta_ref[r, e] for r in range(4))
            zero_range(lambda r, off=off: row_zero(off + r), cnt, cnt_up)
            zero_range(block_zero, (off + cnt_up) // zr, (off + padded) // zr)

        used = meta_ref[0, N_EXPERTS - 1] + meta_ref[3, N_EXPERTS - 1]
        zero_range(block_zero, used // zr, xs_hbm.shape[0] // zr)

    for _ in range(2):
        pltpu.make_async_copy(x_ref, xs_hbm.at[pl.ds(0, td)], sem).wait()


def _dispatch(x, pos, meta, n_rows):
    n = x.shape[0]
    td = TD_DISPATCH
    pos3 = pos.reshape(n // td, 1, 2 * td)
    return pl.pallas_call(
        functools.partial(_dispatch_kernel, td=td),
        grid=(n // td,),
        in_specs=[pl.BlockSpec((1, 1, 2 * td), lambda i: (i, 0, 0), memory_space=pltpu.SMEM),
                  pl.BlockSpec(memory_space=pltpu.SMEM),
                  pl.BlockSpec((td, D_MODEL), lambda i: (i, 0))],
        out_specs=pl.BlockSpec(memory_space=pl.ANY),
        out_shape=jax.ShapeDtypeStruct((n_rows, D_MODEL), F32),
        scratch_shapes=[pltpu.VMEM((ZERO_ROWS, D_MODEL), F32), pltpu.SemaphoreType.DMA, pltpu.SemaphoreType.DMA],
        compiler_params=_params("arbitrary"),
        name="moe_dispatch",
    )(pos3, meta, x)


def _moe_kernel(we_ref, wb_ref, nv_ref, cnt_ref, *refs, n_sub, tm):
    xs_refs = refs[:n_sub]
    wg_ref, wu_ref, wd_ref, ys_hbm, acc_ref, xb_ref, sem = refs[n_sub:]
    j = pl.program_id(0)
    f = pl.program_id(1)
    last_f = pl.num_programs(1) - 1
    n_windows, n_tiles = cnt_ref[0], cnt_ref[1]

    def tile_write(slot, block):
        return pltpu.make_async_copy(acc_ref.at[slot], ys_hbm.at[pl.ds(pl.multiple_of(block * tm, tm), tm)],
                                     sem.at[slot])

    @pl.when(j < n_windows)
    def _():
        prev_tiles = jnp.where(j > 0, nv_ref[jnp.maximum(j - 1, 0)], 0)

        for s in range(n_sub):
            @pl.when(s < nv_ref[j])
            def _(s=s):
                @pl.when(f == 0)
                def _():
                    xb_ref[s] = xs_refs[s][...].astype(BF16)

                xb = xb_ref[s]
                h = (_silu(_dot(xb, wg_ref[0])) * _dot(xb, wu_ref[0])).astype(BF16)
                part = _dot(h, wd_ref[0])

                @pl.when(f == 0)
                def _():
                    @pl.when(s < prev_tiles)
                    def _():
                        tile_write(s, 0).wait()

                    acc_ref[s] = part

                @pl.when(f != 0)
                def _():
                    acc_ref[s] += part

                @pl.when(f == last_f)
                def _():
                    tile_write(s, wb_ref[j] + s).start()

            @pl.when((f == 0) & (s >= nv_ref[j]) & (s < prev_tiles))
            def _(s=s):
                tile_write(s, 0).wait()

        @pl.when((f == last_f) & (j == n_windows - 1))
        def _():
            for s in range(n_sub):
                @pl.when(s < nv_ref[j])
                def _(s=s):
                    tile_write(s, 0).wait()

            acc_ref[0] = jnp.zeros(acc_ref.shape[1:], F32)
            total_tiles = ys_hbm.shape[0] // tm
            lax.fori_loop(n_tiles, total_tiles, lambda t, c: (tile_write(0, t).start(), c)[1], 0)
            lax.fori_loop(n_tiles, total_tiles, lambda t, c: (tile_write(0, 0).wait(), c)[1], 0)


def _moe_experts(xs, wg, wu, wd, win_expert, win_block, win_tiles, counts):
    n_rows = xs.shape[0]
    tm, tf, n_sub = TM_MOE, TF_MOE, MOE_TILES_PER_WINDOW
    max_windows = win_expert.shape[0]

    def xs_spec(s):
        return pl.BlockSpec((tm, D_MODEL), lambda j, f, we, wb, nv, cnt: (wb[j] + jnp.minimum(s, nv[j] - 1), 0))

    grid_spec = pltpu.PrefetchScalarGridSpec(
        num_scalar_prefetch=4,
        grid=(max_windows, EXPERT_DIM // tf),
        in_specs=[xs_spec(s) for s in range(n_sub)] + [
            pl.BlockSpec((1, D_MODEL, tf), lambda j, f, we, wb, nv, cnt: (we[j], 0, f)),
            pl.BlockSpec((1, D_MODEL, tf), lambda j, f, we, wb, nv, cnt: (we[j], 0, f)),
            pl.BlockSpec((1, tf, D_MODEL), lambda j, f, we, wb, nv, cnt: (we[j], f, 0))],
        out_specs=pl.BlockSpec(memory_space=pl.ANY),
        scratch_shapes=[pltpu.VMEM((n_sub, tm, D_MODEL), F32), pltpu.VMEM((n_sub, tm, D_MODEL), BF16),
                        pltpu.SemaphoreType.DMA((n_sub,))],
    )
    return pl.pallas_call(
        functools.partial(_moe_kernel, n_sub=n_sub, tm=tm),
        grid_spec=grid_spec,
        out_shape=jax.ShapeDtypeStruct((n_rows, D_MODEL), F32),
        compiler_params=_params("arbitrary", "arbitrary"),
        name="moe_experts",
    )(win_expert, win_block, win_tiles, counts, *([xs] * n_sub), wg, wu, wd)


def _combine_kernel(pos_ref, posn_ref, x_ref, route_ref, ys_hbm, g_ref, b_ref, o_ref, y_ref, sem, *, tc):
    i = pl.program_id(0)
    slot = i % 2

    def issue(p_ref, s):
        def start(t, carry):
            for kk in range(2):
                pltpu.make_async_copy(ys_hbm.at[pl.ds(p_ref[0, 0, 2 * t + kk], 1)],
                                      y_ref.at[s, kk, pl.ds(t, 1)], sem.at[s]).start(priority=kk)
            return carry

        lax.fori_loop(0, tc, start, 0, unroll=DMA_UNROLL)

    @pl.when(i == 0)
    def _():
        issue(pos_ref, 0)

    @pl.when(i + 1 < pl.num_programs(0))
    def _():
        issue(posn_ref, 1 - slot)

    for kk in range(2):
        pltpu.make_async_copy(ys_hbm.at[pl.ds(0, tc)], y_ref.at[slot, kk], sem.at[slot]).wait()

    route = route_ref[...]
    f = route[:, 4:5] * y_ref[slot, 0] + route[:, 5:6] * y_ref[slot, 1]
    o_ref[...] = _layer_norm(ALPHA * x_ref[...] + f, g_ref[...], b_ref[...])


def _combine_ln(x, route, pos, ys, g, b):
    n = x.shape[0]
    tc = TC_COMBINE
    nt = n // tc
    pos3 = pos.reshape(nt, 1, 2 * tc)
    return pl.pallas_call(
        functools.partial(_combine_kernel, tc=tc),
        grid=(nt,),
        in_specs=[pl.BlockSpec((1, 1, 2 * tc), lambda i: (i, 0, 0), memory_space=pltpu.SMEM),
                  pl.BlockSpec((1, 1, 2 * tc), lambda i: (jnp.minimum(i + 1, nt - 1), 0, 0), memory_space=pltpu.SMEM),
                  pl.BlockSpec((tc, D_MODEL), lambda i: (i, 0)),
                  pl.BlockSpec((tc, LANES), lambda i: (i, 0)),
                  pl.BlockSpec(memory_space=pl.ANY),
                  pl.BlockSpec((1, D_MODEL), lambda i: (0, 0)), pl.BlockSpec((1, D_MODEL), lambda i: (0, 0))],
        out_specs=pl.BlockSpec((tc, D_MODEL), lambda i: (i, 0)),
        out_shape=jax.ShapeDtypeStruct((n, D_MODEL), F32),
        scratch_shapes=[pltpu.VMEM((2, 2, tc, D_MODEL), F32), pltpu.SemaphoreType.DMA((2,))],
        compiler_params=_params("arbitrary"),
        name="moe_combine_ln",
    )(pos3, pos3, x, route, ys, g, b)


def _moe_layer(x, w_router, wg, wu, wd, g, b):
    n = x.shape[0]
    tm = TM_MOE
    wr = jnp.pad(w_router, ((0, 0), (0, LANES - N_EXPERTS)))
    route, counts_f = _router(x, wr)

    counts = counts_f[0, :N_EXPERTS].astype(jnp.int32)
    padded = ((counts + tm - 1) // tm) * tm
    offs = jnp.cumsum(padded) - padded
    eids = jnp.arange(N_EXPERTS, dtype=jnp.int32)
    e01 = route[:, 0:2].astype(jnp.int32)
    rank01 = route[:, 2:4].astype(jnp.int32)
    base = jnp.sum(jnp.where(e01[:, :, None] == eids[None, None, :], offs[None, None, :], 0), axis=-1)
    pos = (base + rank01).reshape(-1)
    counts_up = jnp.minimum(padded, ((counts + ZERO_ROWS - 1) // ZERO_ROWS) * ZERO_ROWS)
    meta = jnp.stack([offs, counts, counts_up, padded]).astype(jnp.int32)

    n_rows = 2 * n + N_EXPERTS * tm
    nsub = MOE_TILES_PER_WINDOW
    tiles = padded // tm
    wins = (tiles + nsub - 1) // nsub
    win_end = jnp.cumsum(wins)
    n_windows = win_end[-1]
    max_windows = (n_rows // tm + nsub - 1) // nsub + N_EXPERTS
    wj = jnp.minimum(jnp.arange(max_windows, dtype=jnp.int32), n_windows - 1)
    win_expert = jnp.minimum(jnp.sum((wj[:, None] >= win_end[None, :]).astype(jnp.int32), axis=-1), N_EXPERTS - 1)
    pick = lambda v: jnp.sum(jnp.where(win_expert[:, None] == eids[None, :], v[None, :], 0), axis=-1)
    local = wj - pick(win_end - wins)
    win_block = pick(offs // tm) + nsub * local
    win_tiles = jnp.minimum(nsub, pick(tiles) - nsub * local)
    counts_nt = jnp.stack([n_windows, jnp.sum(tiles)])

    xs = _dispatch(x, pos, meta, n_rows)
    i32 = lambda a: a.astype(jnp.int32)
    ys = _moe_experts(xs, wg, wu, wd, i32(win_expert), i32(win_block), i32(win_tiles), i32(counts_nt))
    return _combine_ln(x, route, pos, ys, g, b)


def kernel(x, a_w_in, a_conv_w, a_a_log, a_dt_bias, a_norm_w, a_w_out, b_w_in, b_b_in, b_sinks, b_w_out, rel_bias,
           ffn_w_gate, ffn_w_up, ffn_w_down, moe_router, moe_w_gate, moe_w_up, moe_w_down, ln_g, ln_b):
    batch, seq_len, _ = x.shape
    n = batch * seq_len
    x0 = x.reshape(n, D_MODEL)
    ln_g = ln_g.reshape(DEPTH, 2, 1, D_MODEL)
    ln_b = ln_b.reshape(DEPTH, 2, 1, D_MODEL)

    w_in = a_w_in[0]
    w_all = jnp.pad(w_in, ((0, 0), (0, LANES - 2 * GDN_V_HEADS))).astype(BF16)
    pad_gate = lambda p: jnp.pad(p.reshape(1, GDN_V_HEADS), ((0, 0), (GDN_V_HEADS, LANES - 2 * GDN_V_HEADS)))
    gate_params = jnp.concatenate([pad_gate(a_a_log[0]), pad_gate(a_dt_bias[0])], axis=0)
    q, k, v, zs, gb = _gdn_inproj(x0, w_all, a_conv_w[0], gate_params, seq_len)
    o = _gdn_chunk(q, k, v, zs, gb, a_norm_w[0].reshape(1, GDN_HEAD), batch, seq_len)
    x1 = _proj_ln(o, a_w_out[0].astype(BF16), x0, ln_g[0, 0], ln_b[0, 0], "gdn_outproj_ln")
    x2 = _ffn_ln(x1, ffn_w_gate[0].astype(BF16), ffn_w_up[0].astype(BF16), ffn_w_down[0].astype(BF16),
                 ln_g[0, 1], ln_b[0, 1])

    qa, kva = _swa_inproj(x2, b_w_in[0].astype(BF16), b_b_in[0].reshape(1, -1))
    bias = _bias_table(rel_bias)
    sinks = jnp.pad(b_sinks[0].reshape(1, SWA_Q_HEADS), ((0, 0), (0, LANES - SWA_Q_HEADS)))
    oa = _swa_attention(qa, kva, bias, sinks, batch, seq_len)
    x3 = _proj_ln(oa, b_w_out[0].astype(BF16), x2, ln_g[1, 0], ln_b[1, 0], "swa_outproj_ln")
    x4 = _moe_layer(x3, moe_router[0], moe_w_gate[0].astype(BF16), moe_w_up[0].astype(BF16),
                    moe_w_down[0].astype(BF16), ln_g[1, 1], ln_b[1, 1])
    return x4.reshape(batch, seq_len, D_MODEL)
```

```python
import functools
import math

import numpy as np
import jax
import jax.numpy as jnp
from jax import lax
from jax.experimental import pallas as pl
from jax.experimental.pallas import tpu as pltpu

F32 = jnp.float32
BF16 = jnp.bfloat16

D_MODEL = 1024
DEPTH = 2
ALPHA = (2.0 * DEPTH) ** 0.25
LN_EPS = 1e-5

GDN_K_HEADS = 4
GDN_V_HEADS = 8
GDN_HEAD = 128
GDN_KDIM = GDN_K_HEADS * GDN_HEAD
GDN_VDIM = GDN_V_HEADS * GDN_HEAD
GDN_CONV = 4
GDN_CHUNK = 64
GDN_QKV = 2 * GDN_KDIM + GDN_VDIM
GDN_EPS = 1e-6

SWA_Q_HEADS = 16
SWA_KV_HEADS = 2
SWA_GROUP = SWA_Q_HEADS // SWA_KV_HEADS
SWA_HEAD_DIM = 64
SWA_WINDOW = 128
SWA_BLOCK = 128
SWA_QDIM = SWA_Q_HEADS * SWA_HEAD_DIM
SWA_KVDIM = SWA_KV_HEADS * SWA_HEAD_DIM
REL_BUCKETS = 32
REL_MAX_DIST = 128

FFN_DIM = 2816
N_EXPERTS = 8
EXPERT_DIM = 3584

LANES = 128
SUBLANES = 8
NEG_BIG = -1e30
VMEM_LIMIT = 56 * 1024 * 1024

TM_GDN_IN = 512
TM_PROJ = 512
TM_FFN = 512
TF_FFN = 2816
TM_SWA_IN = 512
TM_ROUTER = 512
TM_MOE = 512
TF_MOE = 1792
MOE_TILES_PER_WINDOW = 3
TD_DISPATCH = 512
TC_COMBINE = 256
ZERO_ROWS = 64
DMA_UNROLL = 8


def _params(*sem):
    return pltpu.CompilerParams(dimension_semantics=sem, vmem_limit_bytes=VMEM_LIMIT)


def _dot(a, b):
    return jnp.dot(a, b, preferred_element_type=F32)


def _dot_nt(a, b):
    return lax.dot_general(a, b, (((1,), (1,)), ((), ())), preferred_element_type=F32)


def _dot_tn(a, b):
    return lax.dot_general(a, b, (((0,), (0,)), ((), ())), preferred_element_type=F32)


def _split(x):
    hi = x.astype(BF16)
    lo = (x - hi.astype(F32)).astype(BF16)
    return hi, lo


def _dot3(a, b):
    ah, al = _split(a)
    bh, bl = _split(b)
    return _dot(ah, bh) + (_dot(ah, bl) + _dot(al, bh))


def _silu(x):
    return x * jax.nn.sigmoid(x)


def _layer_norm(y, g, b):
    mu = jnp.mean(y, axis=-1, keepdims=True)
    yc = y - mu
    var = jnp.mean(yc * yc, axis=-1, keepdims=True)
    return yc * lax.rsqrt(var + LN_EPS) * g + b


def _gdn_inproj_kernel(x_ref, w_ref, convw_ref, gp_ref,
                       q_ref, k_ref, v_ref, z_ref, gb_ref, ext_ref, *, tm, tiles_per_seq):
    i = pl.program_id(0)
    xb = x_ref[...].astype(BF16)

    @pl.when(i % tiles_per_seq == 0)
    def _():
        ext_ref[0:SUBLANES, :] = jnp.zeros((SUBLANES, GDN_QKV), F32)

    @pl.when(i % tiles_per_seq != 0)
    def _():
        ext_ref[0:SUBLANES, :] = ext_ref[tm:tm + SUBLANES, :]

    ext_ref[SUBLANES:, :] = _dot(xb, w_ref[:, :GDN_QKV])
    z_ref[...] = _silu(_dot(xb, w_ref[:, GDN_QKV:GDN_QKV + GDN_VDIM]))

    n_chunks = GDN_QKV // LANES
    for c in range(n_chunks):
        cs = slice(c * LANES, (c + 1) * LANES)
        acc = convw_ref[GDN_CONV - 1:GDN_CONV, cs] * ext_ref[SUBLANES:SUBLANES + tm, cs]
        for j in range(GDN_CONV - 1):
            off = SUBLANES - (GDN_CONV - 1) + j
            acc = acc + convw_ref[j:j + 1, cs] * ext_ref[off:off + tm, cs]
        y = _silu(acc)
        if c < 2 * GDN_K_HEADS:
            y = y * lax.rsqrt(jnp.sum(y * y, axis=-1, keepdims=True) + GDN_EPS)
            if c < GDN_K_HEADS:
                q_ref[:, cs] = y * (GDN_HEAD ** -0.5)
            else:
                k_ref[:, (c - GDN_K_HEADS) * LANES:(c - GDN_K_HEADS + 1) * LANES] = y
        else:
            cv = c - 2 * GDN_K_HEADS
            v_ref[:, cv * LANES:(cv + 1) * LANES] = y

    ba = _dot(xb, w_ref[:, GDN_QKV + GDN_VDIM:])
    lane = lax.broadcasted_iota(jnp.int32, ba.shape, 1)
    sp = ba + gp_ref[1:2, :]
    softplus = jnp.maximum(sp, 0.0) + jnp.log(1.0 + jnp.exp(-jnp.abs(sp)))
    g = -jnp.exp(gp_ref[0:1, :]) * softplus
    gb_ref[...] = jnp.where(lane < GDN_V_HEADS, jax.nn.sigmoid(ba), g)


def _gdn_inproj(x2d, w_all, conv_w, gate_params, seq_len):
    n = x2d.shape[0]
    tm = TM_GDN_IN
    kern = functools.partial(_gdn_inproj_kernel, tm=tm, tiles_per_seq=seq_len // tm)
    full = lambda shape: pl.BlockSpec(shape, lambda i: (0,) * len(shape))
    rows = lambda width: pl.BlockSpec((tm, width), lambda i: (i, 0))
    return pl.pallas_call(
        kern,
        grid=(n // tm,),
        in_specs=[rows(D_MODEL), full(w_all.shape), full(conv_w.shape), full(gate_params.shape)],
        out_specs=[rows(GDN_KDIM), rows(GDN_KDIM), rows(GDN_VDIM), rows(GDN_VDIM), rows(LANES)],
        out_shape=[jax.ShapeDtypeStruct((n, GDN_KDIM), F32), jax.ShapeDtypeStruct((n, GDN_KDIM), F32),
                   jax.ShapeDtypeStruct((n, GDN_VDIM), F32), jax.ShapeDtypeStruct((n, GDN_VDIM), F32),
                   jax.ShapeDtypeStruct((n, LANES), F32)],
        scratch_shapes=[pltpu.VMEM((tm + SUBLANES, GDN_QKV), F32)],
        compiler_params=_params("arbitrary"),
        name="gdn_inproj",
    )(x2d, w_all, conv_w, gate_params)


GDN_CHUNKS_PER_STEP = 2
NEUMANN_BLOCK = 4


def _bf16_all(xs):
    return [x.astype(BF16) for x in xs]


def _dot_all(a_list, b_list):
    return [_dot(a, b) for a, b in zip(a_list, b_list)]


def _unit_lower_inverse_all(a_list, row, col):
    shift = int(math.log2(NEUMANN_BLOCK))
    eye = jnp.where(row == col, 1.0, 0.0).astype(F32)
    on_diag_block = (row >> shift) == (col >> shift)
    d = [jnp.where(on_diag_block, a, 0.0) for a in a_list]
    d_b = _bf16_all(d)
    x = [eye - di for di in d]
    d2_b = _bf16_all(_dot_all(d_b, d_b))
    x = [xi + t for xi, t in zip(x, _dot_all(_bf16_all(x), d2_b))]
    size = NEUMANN_BLOCK
    while size < a_list[0].shape[0]:
        shift = int(math.log2(size))
        rbl, cbl = row >> shift, col >> shift
        below = ((rbl & 1) == 1) & (cbl == rbl - 1)
        l_b = _bf16_all([jnp.where(below, a, 0.0) for a in a_list])
        x_b = _bf16_all(x)
        xl_b = _bf16_all(_dot_all(x_b, l_b))
        x = [xi - t for xi, t in zip(x, _dot_all(xl_b, x_b))]
        size *= 2
    return x


def _gdn_chunk_kernel(q_ref, k_ref, v_ref, z_ref, gb_ref, nw_ref, *rest, nb, nck, n_cast):
    cast_src, (o_ref, *cast_dst), s_ref = rest[:n_cast], rest[n_cast:2 * n_cast + 1], rest[-1]
    for src, dst in zip(cast_src, cast_dst):
        dst[...] = src[...].astype(BF16)
    c = GDN_CHUNK
    nh = GDN_V_HEADS

    @pl.when(pl.program_id(0) == 0)
    def _():
        s_ref[...] = jnp.zeros(s_ref.shape, F32)

    row = lax.broadcasted_iota(jnp.int32, (c, c), 0)
    col = lax.broadcasted_iota(jnp.int32, (c, c), 1)
    causal = row >= col
    strict = row > col
    tril = jnp.where(causal, 1.0, 0.0).astype(BF16)
    rs = lambda ci: slice(ci * c, (ci + 1) * c)
    ks = lambda h: slice((h // (nh // GDN_K_HEADS)) * GDN_HEAD, (h // (nh // GDN_K_HEADS) + 1) * GDN_HEAD)
    vs = lambda h: slice(h * GDN_HEAD, (h + 1) * GDN_HEAD)

    gates = {}
    for ci in range(nck):
        for b in range(nb):
            gb = gb_ref[b, rs(ci), :]
            g_hi, g_lo = _split(gb)
            gc = _dot(tril, g_hi) + _dot(tril, g_lo)
            g_last = gc[c - 1:c, :]
            gates[ci, b] = dict(gb=gb, gc=gc, gc_t=gc.T, eg=jnp.exp(gc), e_last=jnp.exp(g_last),
                                e_rest=jnp.exp(g_last - gc))

    items = [(ci, b, h) for ci in range(nck) for b in range(nb) for h in range(nh)]

    a_kk, a_qk, k_beta = [], {}, {}
    for it in items:
        ci, b, h = it
        gt = gates[ci, b]
        k = k_ref[b, rs(ci), ks(h)]
        kb = k * gt["gb"][:, h:h + 1]
        kq = jnp.concatenate([kb, q_ref[b, rs(ci), ks(h)]], axis=0).astype(BF16)
        gram = _dot_nt(kq, k.astype(BF16))
        gl = nh + h
        decay = jnp.exp(jnp.where(causal, gt["gc"][:, gl:gl + 1] - gt["gc_t"][gl:gl + 1, :], NEG_BIG))
        a_kk.append(jnp.where(strict, gram[:c] * decay, 0.0))
        a_qk[it] = (gram[c:] * decay).astype(BF16)
        k_beta[it] = kb

    t_inv = dict(zip(items, _unit_lower_inverse_all(a_kk, row, col)))

    uw = {}
    for it in items:
        ci, b, h = it
        gt = gates[ci, b]
        gl = nh + h
        rhs = jnp.concatenate([v_ref[b, rs(ci), vs(h)] * gt["gb"][:, h:h + 1],
                               k_beta[it] * gt["eg"][:, gl:gl + 1]], axis=1)
        uw[it] = _dot(t_inv[it].astype(BF16), rhs.astype(BF16))

    for ci in range(nck):
        chunk_items = [(ci, b, h) for b in range(nb) for h in range(nh)]

        ws_qs, states = {}, {}
        for it in chunk_items:
            _, b, h = it
            gl = nh + h
            s = s_ref[b * nh + h]
            wq = jnp.concatenate([uw[it][:, GDN_HEAD:],
                                  q_ref[b, rs(ci), ks(h)] * gates[ci, b]["eg"][:, gl:gl + 1]], axis=0)
            ws_qs[it] = _dot(wq.astype(BF16), s.astype(BF16))
            states[it] = s

        for it in chunk_items:
            _, b, h = it
            gt = gates[ci, b]
            gl = nh + h
            v_new = (uw[it][:, :GDN_HEAD] - ws_qs[it][:c]).astype(BF16)
            o = ws_qs[it][c:] + _dot(a_qk[it], v_new)
            k_dec = (k_ref[b, rs(ci), ks(h)] * gt["e_rest"][:, gl:gl + 1]).astype(BF16)
            s_ref[b * nh + h] = states[it] * gt["e_last"][:, gl:gl + 1] + _dot_tn(k_dec, v_new)
            o = o * lax.rsqrt(jnp.mean(o * o, axis=-1, keepdims=True) + GDN_EPS) * nw_ref[...]
            o_ref[b, rs(ci), vs(h)] = (o * z_ref[b, rs(ci), vs(h)]).astype(o_ref.dtype)


def _cast_specs(arrays, n_steps):
    specs, shapes = [], []
    for a in arrays:
        rows_per_step, rem = divmod(a.shape[0], n_steps)
        assert rem == 0 and rows_per_step % (2 * SUBLANES) == 0, a.shape
        specs.append(pl.BlockSpec((rows_per_step, a.shape[1]), lambda j: (j, 0)))
        shapes.append(jax.ShapeDtypeStruct(a.shape, BF16))
    return specs, shapes


def _gdn_chunk(q, k, v, zs, gb, norm_w, batch, seq_len, to_bf16):
    rows_per_step = GDN_CHUNK * GDN_CHUNKS_PER_STEP
    n_steps = seq_len // rows_per_step
    seq = lambda a: a.reshape(batch, seq_len, a.shape[-1])
    rows = lambda width: pl.BlockSpec((batch, rows_per_step, width), lambda j: (0, j, 0))
    cast_specs, cast_shapes = _cast_specs(to_bf16, n_steps)
    out, *casted = pl.pallas_call(
        functools.partial(_gdn_chunk_kernel, nb=batch, nck=GDN_CHUNKS_PER_STEP, n_cast=len(to_bf16)),
        grid=(n_steps,),
        in_specs=[rows(GDN_KDIM), rows(GDN_KDIM), rows(GDN_VDIM), rows(GDN_VDIM), rows(LANES),
                  pl.BlockSpec((1, GDN_HEAD), lambda j: (0, 0))] + cast_specs,
        out_specs=[rows(GDN_VDIM)] + cast_specs,
        out_shape=[jax.ShapeDtypeStruct((batch, seq_len, GDN_VDIM), BF16)] + cast_shapes,
        scratch_shapes=[pltpu.VMEM((batch * GDN_V_HEADS, GDN_HEAD, GDN_HEAD), F32)],
        compiler_params=_params("arbitrary"),
        name="gdn_chunk",
    )(seq(q), seq(k), seq(v), seq(zs), seq(gb), norm_w, *to_bf16)
    return out.reshape(batch * seq_len, GDN_VDIM), casted


def _proj_ln_kernel(a_ref, w_ref, r_ref, g_ref, b_ref, o_ref):
    y = ALPHA * r_ref[...] + _dot(a_ref[...], w_ref[...])
    o_ref[...] = _layer_norm(y, g_ref[...], b_ref[...])


def _proj_ln(a, w, res, g, b, name):
    n, kdim = a.shape
    tm = TM_PROJ
    return pl.pallas_call(
        _proj_ln_kernel,
        grid=(n // tm,),
        in_specs=[pl.BlockSpec((tm, kdim), lambda i: (i, 0)), pl.BlockSpec(w.shape, lambda i: (0, 0)),
                  pl.BlockSpec((tm, D_MODEL), lambda i: (i, 0)),
                  pl.BlockSpec((1, D_MODEL), lambda i: (0, 0)), pl.BlockSpec((1, D_MODEL), lambda i: (0, 0))],
        out_specs=pl.BlockSpec((tm, D_MODEL), lambda i: (i, 0)),
        out_shape=jax.ShapeDtypeStruct((n, D_MODEL), F32),
        compiler_params=_params("parallel"),
        name=name,
    )(a, w, res, g, b)


def _ffn_kernel(x_ref, wg_ref, wu_ref, wd_ref, g_ref, b_ref, o_ref, acc_ref, xb_ref):
    f = pl.program_id(1)

    @pl.when(f == 0)
    def _():
        xb_ref[...] = x_ref[...].astype(BF16)

    xb = xb_ref[...]
    h = (_silu(_dot(xb, wg_ref[...])) * _dot(xb, wu_ref[...])).astype(BF16)
    part = _dot(h, wd_ref[...])

    @pl.when(f == 0)
    def _():
        acc_ref[...] = part

    @pl.when(f != 0)
    def _():
        acc_ref[...] += part

    @pl.when(f == pl.num_programs(1) - 1)
    def _():
        o_ref[...] = _layer_norm(ALPHA * x_ref[...] + acc_ref[...], g_ref[...], b_ref[...])


def _ffn_ln(x, wg, wu, wd, g, b):
    n = x.shape[0]
    tm, tf = TM_FFN, TF_FFN
    resident = pl.Buffered(1) if tf == FFN_DIM else None
    return pl.pallas_call(
        _ffn_kernel,
        grid=(n // tm, FFN_DIM // tf),
        in_specs=[pl.BlockSpec((tm, D_MODEL), lambda i, f: (i, 0)),
                  pl.BlockSpec((D_MODEL, tf), lambda i, f: (0, f), pipeline_mode=resident),
                  pl.BlockSpec((D_MODEL, tf), lambda i, f: (0, f), pipeline_mode=resident),
                  pl.BlockSpec((tf, D_MODEL), lambda i, f: (f, 0), pipeline_mode=resident),
                  pl.BlockSpec((1, D_MODEL), lambda i, f: (0, 0)), pl.BlockSpec((1, D_MODEL), lambda i, f: (0, 0))],
        out_specs=pl.BlockSpec((tm, D_MODEL), lambda i, f: (i, 0)),
        out_shape=jax.ShapeDtypeStruct((n, D_MODEL), F32),
        scratch_shapes=[pltpu.VMEM((tm, D_MODEL), F32), pltpu.VMEM((tm, D_MODEL), BF16)],
        compiler_params=_params("parallel", "arbitrary"),
        name="ffn_ln",
    )(x, wg, wu, wd, g, b)


def _swa_inproj_kernel(x_ref, w_ref, b_ref, q_ref, kv_ref):
    p = _dot(x_ref[...].astype(BF16), w_ref[...]) + b_ref[...]
    q_ref[...] = (p[:, :SWA_QDIM] * (SWA_HEAD_DIM ** -0.5)).astype(BF16)
    kv_ref[...] = p[:, SWA_QDIM:].astype(BF16)


def _swa_inproj(x, w, b):
    n = x.shape[0]
    tm = TM_SWA_IN
    return pl.pallas_call(
        _swa_inproj_kernel,
        grid=(n // tm,),
        in_specs=[pl.BlockSpec((tm, D_MODEL), lambda i: (i, 0)), pl.BlockSpec(w.shape, lambda i: (0, 0)),
                  pl.BlockSpec(b.shape, lambda i: (0, 0))],
        out_specs=[pl.BlockSpec((tm, SWA_QDIM), lambda i: (i, 0)), pl.BlockSpec((tm, 2 * SWA_KVDIM), lambda i: (i, 0))],
        out_shape=[jax.ShapeDtypeStruct((n, SWA_QDIM), BF16), jax.ShapeDtypeStruct((n, 2 * SWA_KVDIM), BF16)],
        compiler_params=_params("parallel"),
        name="swa_inproj",
    )(x, w, b)


def _band_tables():
    qi = np.arange(SWA_BLOCK)[:, None]
    kj = np.arange(2 * SWA_BLOCK)[None, :]
    dist = qi + SWA_BLOCK - kj
    d = np.maximum(dist, 0)
    max_exact = REL_BUCKETS // 2
    df = np.maximum(d, 1).astype(np.float32)
    large = max_exact + (np.log(df / np.float32(max_exact)) / np.float32(math.log(REL_MAX_DIST / max_exact))
                         * np.float32(REL_BUCKETS - max_exact)).astype(np.int32)
    large = np.minimum(large, REL_BUCKETS - 1)
    bucket = np.where(d < max_exact, d, large).astype(np.int32)
    band_ok = (dist >= 0) & (dist < SWA_WINDOW)
    valid = np.stack([band_ok & (kj >= SWA_BLOCK), band_ok]).astype(np.int32)
    return np.ascontiguousarray(bucket.T), np.ascontiguousarray(valid.transpose(0, 2, 1))


def _bias_kernel(relb_ref, bucket_ref, valid_ref, o_ref):
    h = pl.program_id(0)
    bucket = bucket_ref[...]
    acc = jnp.zeros(bucket.shape, F32)
    for b in range(REL_BUCKETS):
        acc = jnp.where(bucket == b, relb_ref[b, h], acc)
    for t in range(2):
        o_ref[t, 0] = jnp.where(valid_ref[t] != 0, acc, NEG_BIG)


def _bias_table(rel_bias):
    bucket, valid = _band_tables()
    shape = (2 * SWA_BLOCK, SWA_BLOCK)
    return pl.pallas_call(
        _bias_kernel,
        grid=(SWA_Q_HEADS,),
        in_specs=[pl.BlockSpec(memory_space=pltpu.SMEM), pl.BlockSpec(shape, lambda h: (0, 0)),
                  pl.BlockSpec((2,) + shape, lambda h: (0, 0, 0))],
        out_specs=pl.BlockSpec((2, 1) + shape, lambda h: (0, h, 0, 0)),
        out_shape=jax.ShapeDtypeStruct((2, SWA_Q_HEADS) + shape, F32),
        compiler_params=_params("parallel"),
        name="swa_bias_table",
    )(rel_bias, jnp.asarray(bucket), jnp.asarray(valid))


def _swa_kernel(q_ref, kvc_ref, kvp_ref, bias_ref, sink_ref, o_ref):
    kv = jnp.concatenate([kvp_ref[...], kvc_ref[...]], axis=0)
    dh = SWA_HEAD_DIM
    nk = 2 * SWA_BLOCK
    k_all = kv[:, :SWA_KVDIM]
    lane = lax.broadcasted_iota(jnp.int32, k_all.shape, 1)
    zero = jnp.zeros_like(k_all)
    for kh in range(SWA_KV_HEADS):
        k_own = jnp.where((lane >= kh * dh) & (lane < (kh + 1) * dh), k_all, zero)
        k_other = pltpu.roll(k_own.astype(F32), dh, axis=1).astype(BF16)
        k_even, k_odd = (k_own, k_other) if kh == 0 else (k_other, k_own)
        k2 = jnp.concatenate([k_even, k_odd], axis=0)
        v_t = kv[:, SWA_KVDIM + kh * dh:SWA_KVDIM + (kh + 1) * dh].T
        pairs = [kh * (SWA_GROUP // 2) + p for p in range(SWA_GROUP // 2)]
        heads = [2 * p + r for p in pairs for r in range(2)]
        st = [_dot_nt(k2, q_ref[:, p * 2 * dh:(p + 1) * 2 * dh]) for p in pairs]
        scores = [st[i // 2][(i % 2) * nk:(i % 2 + 1) * nk] + bias_ref[0, h] for i, h in enumerate(heads)]
        sinks = [sink_ref[0:1, h:h + 1] for h in heads]
        maxes = [jnp.maximum(jnp.max(s, axis=0, keepdims=True), sk) for s, sk in zip(scores, sinks)]
        expo = [jnp.exp(s - m) for s, m in zip(scores, maxes)]
        sums = [jnp.sum(p, axis=0, keepdims=True) for p in expo]
        inv_den = [1.0 / (t + jnp.exp(sk - m)) for t, sk, m in zip(sums, sinks, maxes)]
        for i, p in enumerate(pairs):
            o_t = _dot(v_t, jnp.concatenate(expo[2 * i:2 * i + 2], axis=1).astype(BF16))
            z = jnp.concatenate([o_t[:, :SWA_BLOCK] * inv_den[2 * i], o_t[:, SWA_BLOCK:] * inv_den[2 * i + 1]],
                                axis=0)
            o_ref[:, p * 2 * dh:(p + 1) * 2 * dh] = z.T.astype(o_ref.dtype)


def _swa_attention(q, kv, bias, sinks, batch, seq_len):
    n = q.shape[0]
    nb = seq_len // SWA_BLOCK
    blk = SWA_BLOCK
    return pl.pallas_call(
        _swa_kernel,
        grid=(batch, nb),
        in_specs=[pl.BlockSpec((blk, SWA_QDIM), lambda b, j: (b * nb + j, 0)),
                  pl.BlockSpec((blk, 2 * SWA_KVDIM), lambda b, j: (b * nb + j, 0)),
                  pl.BlockSpec((blk, 2 * SWA_KVDIM), lambda b, j: (b * nb + jnp.maximum(j - 1, 0), 0)),
                  pl.BlockSpec((1, SWA_Q_HEADS, 2 * blk, blk), lambda b, j: (jnp.minimum(j, 1), 0, 0, 0)),
                  pl.BlockSpec((1, LANES), lambda b, j: (0, 0))],
        out_specs=pl.BlockSpec((blk, SWA_QDIM), lambda b, j: (b * nb + j, 0)),
        out_shape=jax.ShapeDtypeStruct((n, SWA_QDIM), BF16),
        compiler_params=_params("parallel", "arbitrary"),
        name="swa_attention",
    )(q, kv, kv, bias, sinks)


def _router_kernel(x_ref, wr_ref, route_ref, cnt_ref, run_ref, *, tm):
    @pl.when(pl.program_id(0) == 0)
    def _():
        run_ref[...] = jnp.zeros(run_ref.shape, F32)

    logits = _dot3(x_ref[...], wr_ref[...])
    lane = lax.broadcasted_iota(jnp.int32, logits.shape, 1)
    lane_f = lane.astype(F32)
    lg = jnp.where(lane < N_EXPERTS, logits, NEG_BIG)
    m1 = jnp.max(lg, axis=-1, keepdims=True)
    i1 = jnp.min(jnp.where(lg == m1, lane_f, float(LANES)), axis=-1, keepdims=True)
    oh1 = lane_f == i1
    lg2 = jnp.where(oh1, NEG_BIG, lg)
    m2 = jnp.max(lg2, axis=-1, keepdims=True)
    i2 = jnp.min(jnp.where(lg2 == m2, lane_f, float(LANES)), axis=-1, keepdims=True)
    oh2 = lane_f == i2
    e = jnp.exp(m2 - m1)
    w0 = 1.0 / (1.0 + e)
    w1 = e * w0

    cnt = jnp.where(oh1, 1.0, 0.0) + jnp.where(oh2, 1.0, 0.0)
    r = lax.broadcasted_iota(jnp.int32, (tm, tm), 0)
    c = lax.broadcasted_iota(jnp.int32, (tm, tm), 1)
    before = jnp.where(r > c, 1.0, 0.0).astype(BF16)
    excl = _dot(before, cnt.astype(BF16)) + run_ref[...]
    rank0 = jnp.sum(jnp.where(oh1, excl, 0.0), axis=-1, keepdims=True)
    rank1 = jnp.sum(jnp.where(oh2, excl, 0.0), axis=-1, keepdims=True)
    run = run_ref[...] + jnp.sum(cnt, axis=0, keepdims=True)
    run_ref[...] = run
    cnt_ref[...] = run

    vals = (i1, i2, rank0, rank1, w0, w1)
    out = jnp.zeros(logits.shape, F32)
    for idx, val in enumerate(vals):
        out = jnp.where(lane == idx, val, out)
    route_ref[...] = out


def _router(x, w_router_padded):
    n = x.shape[0]
    tm = TM_ROUTER
    return pl.pallas_call(
        functools.partial(_router_kernel, tm=tm),
        grid=(n // tm,),
        in_specs=[pl.BlockSpec((tm, D_MODEL), lambda i: (i, 0)), pl.BlockSpec((D_MODEL, LANES), lambda i: (0, 0))],
        out_specs=[pl.BlockSpec((tm, LANES), lambda i: (i, 0)), pl.BlockSpec((1, LANES), lambda i: (0, 0))],
        out_shape=[jax.ShapeDtypeStruct((n, LANES), F32), jax.ShapeDtypeStruct((1, LANES), F32)],
        scratch_shapes=[pltpu.VMEM((1, LANES), F32)],
        compiler_params=_params("arbitrary"),
        name="moe_router",
    )(x, w_router_padded)


def _dispatch_kernel(pos_ref, meta_ref, x_ref, wsrc_ref, xs_hbm, wdst_ref, zero_ref, sem, pad_sem, *, td):
    i = pl.program_id(0)
    wdst_ref[...] = wsrc_ref[...].astype(BF16)

    def row_copy(t, dst):
        return pltpu.make_async_copy(x_ref.at[pl.ds(t, 1)], xs_hbm.at[pl.ds(dst, 1)], sem)

    def start(t, carry):
        row_copy(t, pos_ref[0, 0, 2 * t]).start()
        row_copy(t, pos_ref[0, 0, 2 * t + 1]).start(priority=1)
        return carry

    lax.fori_loop(0, td, start, 0, unroll=DMA_UNROLL)

    @pl.when(i == pl.num_programs(0) - 1)
    def _():
        zero_ref[...] = jnp.zeros(zero_ref.shape, F32)

        zr = zero_ref.shape[0]

        def row_zero(dst):
            return pltpu.make_async_copy(zero_ref.at[pl.ds(0, 1)], xs_hbm.at[pl.ds(dst, 1)], pad_sem)

        def block_zero(blk):
            return pltpu.make_async_copy(zero_ref, xs_hbm.at[pl.ds(pl.multiple_of(blk * zr, zr), zr)], pad_sem)

        def zero_range(copy, lo, hi):
            lax.fori_loop(lo, hi, lambda r, carry: (copy(r).start(), carry)[1], 0)
            lax.fori_loop(lo, hi, lambda r, carry: (copy(0).wait(), carry)[1], 0)

        for e in range(N_EXPERTS):
            off, cnt, cnt_up, padded = (meta_ref[r, e] for r in range(4))
            zero_range(lambda r, off=off: row_zero(off + r), cnt, cnt_up)
            zero_range(block_zero, (off + cnt_up) // zr, (off + padded) // zr)

        used = meta_ref[0, N_EXPERTS - 1] + meta_ref[3, N_EXPERTS - 1]
        zero_range(block_zero, used // zr, xs_hbm.shape[0] // zr)

    for _ in range(2):
        pltpu.make_async_copy(x_ref, xs_hbm.at[pl.ds(0, td)], sem).wait()


def _dispatch(x, pos, meta, n_rows, to_bf16):
    n = x.shape[0]
    td = TD_DISPATCH
    pos3 = pos.reshape(n // td, 1, 2 * td)
    (cast_spec,), (cast_shape,) = _cast_specs([to_bf16], n // td)
    return pl.pallas_call(
        functools.partial(_dispatch_kernel, td=td),
        grid=(n // td,),
        in_specs=[pl.BlockSpec((1, 1, 2 * td), lambda i: (i, 0, 0), memory_space=pltpu.SMEM),
                  pl.BlockSpec(memory_space=pltpu.SMEM),
                  pl.BlockSpec((td, D_MODEL), lambda i: (i, 0)), cast_spec],
        out_specs=[pl.BlockSpec(memory_space=pl.ANY), cast_spec],
        out_shape=[jax.ShapeDtypeStruct((n_rows, D_MODEL), F32), cast_shape],
        scratch_shapes=[pltpu.VMEM((ZERO_ROWS, D_MODEL), F32), pltpu.SemaphoreType.DMA, pltpu.SemaphoreType.DMA],
        compiler_params=_params("arbitrary"),
        name="moe_dispatch",
    )(pos3, meta, x, to_bf16)


def _moe_kernel(we_ref, wb_ref, nv_ref, cnt_ref, *refs, n_sub, tm):
    xs_refs = refs[:n_sub]
    wg_ref, wu_ref, wd_ref, ys_hbm, acc_ref, xb_ref, sem = refs[n_sub:]
    j = pl.program_id(0)
    f = pl.program_id(1)
    last_f = pl.num_programs(1) - 1
    n_windows, n_tiles = cnt_ref[0], cnt_ref[1]

    def tile_write(slot, block):
        return pltpu.make_async_copy(acc_ref.at[slot], ys_hbm.at[pl.ds(pl.multiple_of(block * tm, tm), tm)],
                                     sem.at[slot])

    @pl.when(j < n_windows)
    def _():
        prev_tiles = jnp.where(j > 0, nv_ref[jnp.maximum(j - 1, 0)], 0)

        for s in range(n_sub):
            @pl.when(s < nv_ref[j])
            def _(s=s):
                @pl.when(f == 0)
                def _():
                    xb_ref[s] = xs_refs[s][...].astype(BF16)

                xb = xb_ref[s]
                h = (_silu(_dot(xb, wg_ref[0])) * _dot(xb, wu_ref[0])).astype(BF16)
                part = _dot(h, wd_ref[0])

                @pl.when(f == 0)
                def _():
                    @pl.when(s < prev_tiles)
                    def _():
                        tile_write(s, 0).wait()

                    acc_ref[s] = part

                @pl.when(f != 0)
                def _():
                    acc_ref[s] += part

                @pl.when(f == last_f)
                def _():
                    tile_write(s, wb_ref[j] + s).start()

            @pl.when((f == 0) & (s >= nv_ref[j]) & (s < prev_tiles))
            def _(s=s):
                tile_write(s, 0).wait()

        @pl.when((f == last_f) & (j == n_windows - 1))
        def _():
            for s in range(n_sub):
                @pl.when(s < nv_ref[j])
                def _(s=s):
                    tile_write(s, 0).wait()

            acc_ref[0] = jnp.zeros(acc_ref.shape[1:], F32)
            total_tiles = ys_hbm.shape[0] // tm
            lax.fori_loop(n_tiles, total_tiles, lambda t, c: (tile_write(0, t).start(), c)[1], 0)
            lax.fori_loop(n_tiles, total_tiles, lambda t, c: (tile_write(0, 0).wait(), c)[1], 0)


def _moe_experts(xs, wg, wu, wd, win_expert, win_block, win_tiles, counts):
    n_rows = xs.shape[0]
    tm, tf, n_sub = TM_MOE, TF_MOE, MOE_TILES_PER_WINDOW
    max_windows = win_expert.shape[0]

    def xs_spec(s):
        return pl.BlockSpec((tm, D_MODEL), lambda j, f, we, wb, nv, cnt: (wb[j] + jnp.minimum(s, nv[j] - 1), 0))

    grid_spec = pltpu.PrefetchScalarGridSpec(
        num_scalar_prefetch=4,
        grid=(max_windows, EXPERT_DIM // tf),
        in_specs=[xs_spec(s) for s in range(n_sub)] + [
            pl.BlockSpec((1, D_MODEL, tf), lambda j, f, we, wb, nv, cnt: (we[j], 0, f)),
            pl.BlockSpec((1, D_MODEL, tf), lambda j, f, we, wb, nv, cnt: (we[j], 0, f)),
            pl.BlockSpec((1, tf, D_MODEL), lambda j, f, we, wb, nv, cnt: (we[j], f, 0))],
        out_specs=pl.BlockSpec(memory_space=pl.ANY),
        scratch_shapes=[pltpu.VMEM((n_sub, tm, D_MODEL), F32), pltpu.VMEM((n_sub, tm, D_MODEL), BF16),
                        pltpu.SemaphoreType.DMA((n_sub,))],
    )
    return pl.pallas_call(
        functools.partial(_moe_kernel, n_sub=n_sub, tm=tm),
        grid_spec=grid_spec,
        out_shape=jax.ShapeDtypeStruct((n_rows, D_MODEL), F32),
        compiler_params=_params("arbitrary", "arbitrary"),
        name="moe_experts",
    )(win_expert, win_block, win_tiles, counts, *([xs] * n_sub), wg, wu, wd)


def _combine_kernel(pos_ref, posn_ref, x_ref, route_ref, ys_hbm, g_ref, b_ref, o_ref, y_ref, sem, *, tc):
    i = pl.program_id(0)
    slot = i % 2

    def issue(p_ref, s):
        def start(t, carry):
            for kk in range(2):
                pltpu.make_async_copy(ys_hbm.at[pl.ds(p_ref[0, 0, 2 * t + kk], 1)],
                                      y_ref.at[s, kk, pl.ds(t, 1)], sem.at[s]).start(priority=kk)
            return carry

        lax.fori_loop(0, tc, start, 0, unroll=DMA_UNROLL)

    @pl.when(i == 0)
    def _():
        issue(pos_ref, 0)

    @pl.when(i + 1 < pl.num_programs(0))
    def _():
        issue(posn_ref, 1 - slot)

    for kk in range(2):
        pltpu.make_async_copy(ys_hbm.at[pl.ds(0, tc)], y_ref.at[slot, kk], sem.at[slot]).wait()

    route = route_ref[...]
    f = route[:, 4:5] * y_ref[slot, 0] + route[:, 5:6] * y_ref[slot, 1]
    o_ref[...] = _layer_norm(ALPHA * x_ref[...] + f, g_ref[...], b_ref[...])


def _combine_ln(x, route, pos, ys, g, b):
    n = x.shape[0]
    tc = TC_COMBINE
    nt = n // tc
    pos3 = pos.reshape(nt, 1, 2 * tc)
    return pl.pallas_call(
        functools.partial(_combine_kernel, tc=tc),
        grid=(nt,),
        in_specs=[pl.BlockSpec((1, 1, 2 * tc), lambda i: (i, 0, 0), memory_space=pltpu.SMEM),
                  pl.BlockSpec((1, 1, 2 * tc), lambda i: (jnp.minimum(i + 1, nt - 1), 0, 0), memory_space=pltpu.SMEM),
                  pl.BlockSpec((tc, D_MODEL), lambda i: (i, 0)),
                  pl.BlockSpec((tc, LANES), lambda i: (i, 0)),
                  pl.BlockSpec(memory_space=pl.ANY),
                  pl.BlockSpec((1, D_MODEL), lambda i: (0, 0)), pl.BlockSpec((1, D_MODEL), lambda i: (0, 0))],
        out_specs=pl.BlockSpec((tc, D_MODEL), lambda i: (i, 0)),
        out_shape=jax.ShapeDtypeStruct((n, D_MODEL), F32),
        scratch_shapes=[pltpu.VMEM((2, 2, tc, D_MODEL), F32), pltpu.SemaphoreType.DMA((2,))],
        compiler_params=_params("arbitrary"),
        name="moe_combine_ln",
    )(pos3, pos3, x, route, ys, g, b)


def _moe_layer(x, w_router, wg, wu, wd_f32, g, b):
    n = x.shape[0]
    tm = TM_MOE
    wr = jnp.pad(w_router, ((0, 0), (0, LANES - N_EXPERTS)))
    route, counts_f = _router(x, wr)

    counts = counts_f[0, :N_EXPERTS].astype(jnp.int32)
    padded = ((counts + tm - 1) // tm) * tm
    offs = jnp.cumsum(padded) - padded
    eids = jnp.arange(N_EXPERTS, dtype=jnp.int32)
    e01 = route[:, 0:2].astype(jnp.int32)
    rank01 = route[:, 2:4].astype(jnp.int32)
    base = jnp.sum(jnp.where(e01[:, :, None] == eids[None, None, :], offs[None, None, :], 0), axis=-1)
    pos = (base + rank01).reshape(-1)
    counts_up = jnp.minimum(padded, ((counts + ZERO_ROWS - 1) // ZERO_ROWS) * ZERO_ROWS)
    meta = jnp.stack([offs, counts, counts_up, padded]).astype(jnp.int32)

    n_rows = 2 * n + N_EXPERTS * tm
    nsub = MOE_TILES_PER_WINDOW
    tiles = padded // tm
    wins = (tiles + nsub - 1) // nsub
    win_end = jnp.cumsum(wins)
    n_windows = win_end[-1]
    max_windows = (n_rows // tm + nsub - 1) // nsub + N_EXPERTS
    wj = jnp.minimum(jnp.arange(max_windows, dtype=jnp.int32), n_windows - 1)
    win_expert = jnp.minimum(jnp.sum((wj[:, None] >= win_end[None, :]).astype(jnp.int32), axis=-1), N_EXPERTS - 1)
    pick = lambda v: jnp.sum(jnp.where(win_expert[:, None] == eids[None, :], v[None, :], 0), axis=-1)
    local = wj - pick(win_end - wins)
    win_block = pick(offs // tm) + nsub * local
    win_tiles = jnp.minimum(nsub, pick(tiles) - nsub * local)
    counts_nt = jnp.stack([n_windows, jnp.sum(tiles)])

    xs, wd = _dispatch(x, pos, meta, n_rows, wd_f32.reshape(-1, D_MODEL))
    wd = wd.reshape(wd_f32.shape)
    i32 = lambda a: a.astype(jnp.int32)
    ys = _moe_experts(xs, wg, wu, wd, i32(win_expert), i32(win_block), i32(win_tiles), i32(counts_nt))
    return _combine_ln(x, route, pos, ys, g, b)


def kernel(x, a_w_in, a_conv_w, a_a_log, a_dt_bias, a_norm_w, a_w_out, b_w_in, b_b_in, b_sinks, b_w_out, rel_bias,
           ffn_w_gate, ffn_w_up, ffn_w_down, moe_router, moe_w_gate, moe_w_up, moe_w_down, ln_g, ln_b):
    batch, seq_len, _ = x.shape
    n = batch * seq_len
    x0 = x.reshape(n, D_MODEL)
    ln_g = ln_g.reshape(DEPTH, 2, 1, D_MODEL)
    ln_b = ln_b.reshape(DEPTH, 2, 1, D_MODEL)

    w_in = a_w_in[0]
    w_all = jnp.pad(w_in, ((0, 0), (0, LANES - 2 * GDN_V_HEADS))).astype(BF16)
    pad_gate = lambda p: jnp.pad(p.reshape(1, GDN_V_HEADS), ((0, 0), (GDN_V_HEADS, LANES - 2 * GDN_V_HEADS)))
    gate_params = jnp.concatenate([pad_gate(a_a_log[0]), pad_gate(a_dt_bias[0])], axis=0)
    q, k, v, zs, gb = _gdn_inproj(x0, w_all, a_conv_w[0], gate_params, seq_len)
    o, (moe_wg, moe_wu) = _gdn_chunk(q, k, v, zs, gb, a_norm_w[0].reshape(1, GDN_HEAD), batch, seq_len,
                                     [moe_w_gate[0].reshape(-1, EXPERT_DIM), moe_w_up[0].reshape(-1, EXPERT_DIM)])
    x1 = _proj_ln(o, a_w_out[0].astype(BF16), x0, ln_g[0, 0], ln_b[0, 0], "gdn_outproj_ln")
    x2 = _ffn_ln(x1, ffn_w_gate[0].astype(BF16), ffn_w_up[0].astype(BF16), ffn_w_down[0].astype(BF16),
                 ln_g[0, 1], ln_b[0, 1])

    qa, kva = _swa_inproj(x2, b_w_in[0].astype(BF16), b_b_in[0].reshape(1, -1))
    bias = _bias_table(rel_bias)
    sinks = jnp.pad(b_sinks[0].reshape(1, SWA_Q_HEADS), ((0, 0), (0, LANES - SWA_Q_HEADS)))
    oa = _swa_attention(qa, kva, bias, sinks, batch, seq_len)
    x3 = _proj_ln(oa, b_w_out[0].astype(BF16), x2, ln_g[1, 0], ln_b[1, 0], "swa_outproj_ln")
    expert_shape = (N_EXPERTS, D_MODEL, EXPERT_DIM)
    x4 = _moe_layer(x3, moe_router[0], moe_wg.reshape(expert_shape), moe_wu.reshape(expert_shape),
                    moe_w_down[0], ln_g[1, 1], ln_b[1, 1])
    return x4.reshape(batch, seq_len, D_MODEL)
```

```python
import functools
import math

import numpy as np
import jax
import jax.numpy as jnp
from jax import lax
from jax.experimental import pallas as pl
from jax.experimental.pallas import tpu as pltpu

F32 = jnp.float32
BF16 = jnp.bfloat16

D_MODEL = 1024
DEPTH = 2
ALPHA = (2.0 * DEPTH) ** 0.25
LN_EPS = 1e-5

GDN_K_HEADS = 4
GDN_V_HEADS = 8
GDN_HEAD = 128
GDN_KDIM = GDN_K_HEADS * GDN_HEAD
GDN_VDIM = GDN_V_HEADS * GDN_HEAD
GDN_CONV = 4
GDN_CHUNK = 64
GDN_QKV = 2 * GDN_KDIM + GDN_VDIM
GDN_EPS = 1e-6

SWA_Q_HEADS = 16
SWA_KV_HEADS = 2
SWA_GROUP = SWA_Q_HEADS // SWA_KV_HEADS
SWA_HEAD_DIM = 64
SWA_WINDOW = 128
SWA_BLOCK = 128
SWA_QDIM = SWA_Q_HEADS * SWA_HEAD_DIM
SWA_KVDIM = SWA_KV_HEADS * SWA_HEAD_DIM
REL_BUCKETS = 32
REL_MAX_DIST = 128

FFN_DIM = 2816
N_EXPERTS = 8
EXPERT_DIM = 3584

LANES = 128
SUBLANES = 8
NEG_BIG = -1e30
VMEM_LIMIT = 56 * 1024 * 1024

TM_GDN_IN = 512
TM_FFN = 512
TM_SWA_IN = 512
TM_ROUTER = 512
TM_MOE = 512
TF_MOE = 1792
MOE_TILES_PER_WINDOW = 3
TD_DISPATCH = 512
TC_COMBINE = 256
ZERO_ROWS = 64
DMA_UNROLL = 8


def _params(*sem):
    return pltpu.CompilerParams(dimension_semantics=sem, vmem_limit_bytes=VMEM_LIMIT)


def _dot(a, b):
    return jnp.dot(a, b, preferred_element_type=F32)


def _dot_nt(a, b):
    return lax.dot_general(a, b, (((1,), (1,)), ((), ())), preferred_element_type=F32)


def _dot_tn(a, b):
    return lax.dot_general(a, b, (((0,), (0,)), ((), ())), preferred_element_type=F32)


def _split(x):
    hi = x.astype(BF16)
    lo = (x - hi.astype(F32)).astype(BF16)
    return hi, lo


def _dot3(a, b):
    ah, al = _split(a)
    bh, bl = _split(b)
    return _dot(ah, bh) + (_dot(ah, bl) + _dot(al, bh))


def _silu(x):
    return x * jax.nn.sigmoid(x)


def _layer_norm(y, g, b):
    mu = jnp.mean(y, axis=-1, keepdims=True)
    yc = y - mu
    var = jnp.mean(yc * yc, axis=-1, keepdims=True)
    return yc * lax.rsqrt(var + LN_EPS) * g + b


def _gdn_inproj_kernel(x_ref, w_ref, convw_ref, gp_ref,
                       q_ref, k_ref, v_ref, z_ref, gb_ref, ext_ref, *, tm, tiles_per_seq):
    i = pl.program_id(0)
    xb = x_ref[...].astype(BF16)

    @pl.when(i % tiles_per_seq == 0)
    def _():
        ext_ref[0:SUBLANES, :] = jnp.zeros((SUBLANES, GDN_QKV), F32)

    @pl.when(i % tiles_per_seq != 0)
    def _():
        ext_ref[0:SUBLANES, :] = ext_ref[tm:tm + SUBLANES, :]

    ext_ref[SUBLANES:, :] = _dot(xb, w_ref[:, :GDN_QKV])
    z_ref[...] = _silu(_dot(xb, w_ref[:, GDN_QKV:GDN_QKV + GDN_VDIM]))

    n_chunks = GDN_QKV // LANES
    for c in range(n_chunks):
        cs = slice(c * LANES, (c + 1) * LANES)
        acc = convw_ref[GDN_CONV - 1:GDN_CONV, cs] * ext_ref[SUBLANES:SUBLANES + tm, cs]
        for j in range(GDN_CONV - 1):
            off = SUBLANES - (GDN_CONV - 1) + j
            acc = acc + convw_ref[j:j + 1, cs] * ext_ref[off:off + tm, cs]
        y = _silu(acc)
        if c < 2 * GDN_K_HEADS:
            y = y * lax.rsqrt(jnp.sum(y * y, axis=-1, keepdims=True) + GDN_EPS)
            if c < GDN_K_HEADS:
                q_ref[:, cs] = y * (GDN_HEAD ** -0.5)
            else:
                k_ref[:, (c - GDN_K_HEADS) * LANES:(c - GDN_K_HEADS + 1) * LANES] = y
        else:
            cv = c - 2 * GDN_K_HEADS
            v_ref[:, cv * LANES:(cv + 1) * LANES] = y

    ba = _dot(xb, w_ref[:, GDN_QKV + GDN_VDIM:])
    lane = lax.broadcasted_iota(jnp.int32, ba.shape, 1)
    sp = ba + gp_ref[1:2, :]
    softplus = jnp.maximum(sp, 0.0) + jnp.log(1.0 + jnp.exp(-jnp.abs(sp)))
    g = -jnp.exp(gp_ref[0:1, :]) * softplus
    gb_ref[...] = jnp.where(lane < GDN_V_HEADS, jax.nn.sigmoid(ba), g)


def _gdn_inproj(x2d, w_all, conv_w, gate_params, seq_len):
    n = x2d.shape[0]
    tm = TM_GDN_IN
    kern = functools.partial(_gdn_inproj_kernel, tm=tm, tiles_per_seq=seq_len // tm)
    full = lambda shape: pl.BlockSpec(shape, lambda i: (0,) * len(shape))
    rows = lambda width: pl.BlockSpec((tm, width), lambda i: (i, 0))
    return pl.pallas_call(
        kern,
        grid=(n // tm,),
        in_specs=[rows(D_MODEL), full(w_all.shape), full(conv_w.shape), full(gate_params.shape)],
        out_specs=[rows(GDN_KDIM), rows(GDN_KDIM), rows(GDN_VDIM), rows(GDN_VDIM), rows(LANES)],
        out_shape=[jax.ShapeDtypeStruct((n, GDN_KDIM), F32), jax.ShapeDtypeStruct((n, GDN_KDIM), F32),
                   jax.ShapeDtypeStruct((n, GDN_VDIM), F32), jax.ShapeDtypeStruct((n, GDN_VDIM), F32),
                   jax.ShapeDtypeStruct((n, LANES), F32)],
        scratch_shapes=[pltpu.VMEM((tm + SUBLANES, GDN_QKV), F32)],
        compiler_params=_params("arbitrary"),
        name="gdn_inproj",
    )(x2d, w_all, conv_w, gate_params)


GDN_CHUNKS_PER_STEP = 2
NEUMANN_BLOCK = 4


def _bf16_all(xs):
    return [x.astype(BF16) for x in xs]


def _dot_all(a_list, b_list):
    return [_dot(a, b) for a, b in zip(a_list, b_list)]


def _unit_lower_inverse_all(a_list, row, col):
    shift = int(math.log2(NEUMANN_BLOCK))
    eye = jnp.where(row == col, 1.0, 0.0).astype(F32)
    on_diag_block = (row >> shift) == (col >> shift)
    d = [jnp.where(on_diag_block, a, 0.0) for a in a_list]
    d_b = _bf16_all(d)
    x = [eye - di for di in d]
    d2_b = _bf16_all(_dot_all(d_b, d_b))
    x = [xi + t for xi, t in zip(x, _dot_all(_bf16_all(x), d2_b))]
    size = NEUMANN_BLOCK
    while size < a_list[0].shape[0]:
        shift = int(math.log2(size))
        rbl, cbl = row >> shift, col >> shift
        below = ((rbl & 1) == 1) & (cbl == rbl - 1)
        l_b = _bf16_all([jnp.where(below, a, 0.0) for a in a_list])
        x_b = _bf16_all(x)
        xl_b = _bf16_all(_dot_all(x_b, l_b))
        x = [xi - t for xi, t in zip(x, _dot_all(xl_b, x_b))]
        size *= 2
    return x


def _gdn_chunk_kernel(q_ref, k_ref, v_ref, z_ref, gb_ref, nw_ref, *rest, nb, nck, n_cast):
    cast_src, (o_ref, *cast_dst), s_ref = rest[:n_cast], rest[n_cast:2 * n_cast + 1], rest[-1]
    for src, dst in zip(cast_src, cast_dst):
        dst[...] = src[...].astype(BF16)
    c = GDN_CHUNK
    nh = GDN_V_HEADS

    @pl.when(pl.program_id(0) == 0)
    def _():
        s_ref[...] = jnp.zeros(s_ref.shape, F32)

    row = lax.broadcasted_iota(jnp.int32, (c, c), 0)
    col = lax.broadcasted_iota(jnp.int32, (c, c), 1)
    causal = row >= col
    strict = row > col
    tril = jnp.where(causal, 1.0, 0.0).astype(BF16)
    rs = lambda ci: slice(ci * c, (ci + 1) * c)
    ks = lambda h: slice((h // (nh // GDN_K_HEADS)) * GDN_HEAD, (h // (nh // GDN_K_HEADS) + 1) * GDN_HEAD)
    vs = lambda h: slice(h * GDN_HEAD, (h + 1) * GDN_HEAD)

    gates = {}
    for ci in range(nck):
        for b in range(nb):
            gb = gb_ref[b, rs(ci), :]
            g_hi, g_lo = _split(gb)
            gc = _dot(tril, g_hi) + _dot(tril, g_lo)
            g_last = gc[c - 1:c, :]
            gates[ci, b] = dict(gb=gb, gc=gc, gc_t=gc.T, eg=jnp.exp(gc), e_last=jnp.exp(g_last),
                                e_rest=jnp.exp(g_last - gc))

    items = [(ci, b, h) for ci in range(nck) for b in range(nb) for h in range(nh)]

    a_kk, a_qk, k_beta = [], {}, {}
    for it in items:
        ci, b, h = it
        gt = gates[ci, b]
        k = k_ref[b, rs(ci), ks(h)]
        kb = k * gt["gb"][:, h:h + 1]
        kq = jnp.concatenate([kb, q_ref[b, rs(ci), ks(h)]], axis=0).astype(BF16)
        gram = _dot_nt(kq, k.astype(BF16))
        gl = nh + h
        decay = jnp.exp(jnp.where(causal, gt["gc"][:, gl:gl + 1] - gt["gc_t"][gl:gl + 1, :], NEG_BIG))
        a_kk.append(jnp.where(strict, gram[:c] * decay, 0.0))
        a_qk[it] = (gram[c:] * decay).astype(BF16)
        k_beta[it] = kb

    t_inv = dict(zip(items, _unit_lower_inverse_all(a_kk, row, col)))

    uw = {}
    for it in items:
        ci, b, h = it
        gt = gates[ci, b]
        gl = nh + h
        rhs = jnp.concatenate([v_ref[b, rs(ci), vs(h)] * gt["gb"][:, h:h + 1],
                               k_beta[it] * gt["eg"][:, gl:gl + 1]], axis=1)
        uw[it] = _dot(t_inv[it].astype(BF16), rhs.astype(BF16))

    for ci in range(nck):
        chunk_items = [(ci, b, h) for b in range(nb) for h in range(nh)]

        ws_qs, states = {}, {}
        for it in chunk_items:
            _, b, h = it
            gl = nh + h
            s = s_ref[b * nh + h]
            wq = jnp.concatenate([uw[it][:, GDN_HEAD:],
                                  q_ref[b, rs(ci), ks(h)] * gates[ci, b]["eg"][:, gl:gl + 1]], axis=0)
            ws_qs[it] = _dot(wq.astype(BF16), s.astype(BF16))
            states[it] = s

        for it in chunk_items:
            _, b, h = it
            gt = gates[ci, b]
            gl = nh + h
            v_new = (uw[it][:, :GDN_HEAD] - ws_qs[it][:c]).astype(BF16)
            o = ws_qs[it][c:] + _dot(a_qk[it], v_new)
            k_dec = (k_ref[b, rs(ci), ks(h)] * gt["e_rest"][:, gl:gl + 1]).astype(BF16)
            s_ref[b * nh + h] = states[it] * gt["e_last"][:, gl:gl + 1] + _dot_tn(k_dec, v_new)
            o = o * lax.rsqrt(jnp.mean(o * o, axis=-1, keepdims=True) + GDN_EPS) * nw_ref[...]
            o_ref[b, rs(ci), vs(h)] = (o * z_ref[b, rs(ci), vs(h)]).astype(o_ref.dtype)


def _cast_specs(arrays, n_steps):
    specs, shapes = [], []
    for a in arrays:
        rows_per_step, rem = divmod(a.shape[0], n_steps)
        assert rem == 0 and rows_per_step % (2 * SUBLANES) == 0, a.shape
        specs.append(pl.BlockSpec((rows_per_step, a.shape[1]), lambda j: (j, 0)))
        shapes.append(jax.ShapeDtypeStruct(a.shape, BF16))
    return specs, shapes


def _gdn_chunk(q, k, v, zs, gb, norm_w, batch, seq_len, to_bf16):
    rows_per_step = GDN_CHUNK * GDN_CHUNKS_PER_STEP
    n_steps = seq_len // rows_per_step
    seq = lambda a: a.reshape(batch, seq_len, a.shape[-1])
    rows = lambda width: pl.BlockSpec((batch, rows_per_step, width), lambda j: (0, j, 0))
    cast_specs, cast_shapes = _cast_specs(to_bf16, n_steps)
    out, *casted = pl.pallas_call(
        functools.partial(_gdn_chunk_kernel, nb=batch, nck=GDN_CHUNKS_PER_STEP, n_cast=len(to_bf16)),
        grid=(n_steps,),
        in_specs=[rows(GDN_KDIM), rows(GDN_KDIM), rows(GDN_VDIM), rows(GDN_VDIM), rows(LANES),
                  pl.BlockSpec((1, GDN_HEAD), lambda j: (0, 0))] + cast_specs,
        out_specs=[rows(GDN_VDIM)] + cast_specs,
        out_shape=[jax.ShapeDtypeStruct((batch, seq_len, GDN_VDIM), BF16)] + cast_shapes,
        scratch_shapes=[pltpu.VMEM((batch * GDN_V_HEADS, GDN_HEAD, GDN_HEAD), F32)],
        compiler_params=_params("arbitrary"),
        name="gdn_chunk",
    )(seq(q), seq(k), seq(v), seq(zs), seq(gb), norm_w, *to_bf16)
    return out.reshape(batch * seq_len, GDN_VDIM), casted


def _proj_res_ln(a_ref, w_ref, r_ref, g_ref, b_ref):
    return _layer_norm(ALPHA * r_ref[...] + _dot(a_ref[...], w_ref[...]), g_ref[...], b_ref[...])


def _outproj_ffn_kernel(a_ref, wo_ref, r_ref, g0_ref, b0_ref, wg_ref, wu_ref, wd_ref, g1_ref, b1_ref, o_ref):
    x1 = _proj_res_ln(a_ref, wo_ref, r_ref, g0_ref, b0_ref)
    xb = x1.astype(BF16)
    h = (_silu(_dot(xb, wg_ref[...])) * _dot(xb, wu_ref[...])).astype(BF16)
    o_ref[...] = _layer_norm(ALPHA * x1 + _dot(h, wd_ref[...]), g1_ref[...], b1_ref[...])


def _proj_ln_kernel(a_ref, w_ref, r_ref, g_ref, b_ref, o_ref):
    o_ref[...] = _proj_res_ln(a_ref, w_ref, r_ref, g_ref, b_ref)


def _proj_ln(a, w, res, g, b, name):
    n, kdim = a.shape
    tm = TM_FFN
    return pl.pallas_call(
        _proj_ln_kernel,
        grid=(n // tm,),
        in_specs=[pl.BlockSpec((tm, kdim), lambda i: (i, 0)), pl.BlockSpec(w.shape, lambda i: (0, 0)),
                  pl.BlockSpec((tm, D_MODEL), lambda i: (i, 0)),
                  pl.BlockSpec((1, D_MODEL), lambda i: (0, 0)), pl.BlockSpec((1, D_MODEL), lambda i: (0, 0))],
        out_specs=pl.BlockSpec((tm, D_MODEL), lambda i: (i, 0)),
        out_shape=jax.ShapeDtypeStruct((n, D_MODEL), F32),
        compiler_params=_params("parallel"),
        name=name,
    )(a, w, res, g, b)


def _outproj_ffn(a, w_out, res, g0, b0, wg, wu, wd, g1, b1):
    n, kdim = a.shape
    tm = TM_FFN
    resident = lambda w: pl.BlockSpec(w.shape, lambda i: (0, 0), pipeline_mode=pl.Buffered(1))
    rows = lambda width: pl.BlockSpec((tm, width), lambda i: (i, 0))
    vec = pl.BlockSpec((1, D_MODEL), lambda i: (0, 0))
    return pl.pallas_call(
        _outproj_ffn_kernel,
        grid=(n // tm,),
        in_specs=[rows(kdim), resident(w_out), rows(D_MODEL), vec, vec,
                  resident(wg), resident(wu), resident(wd), vec, vec],
        out_specs=rows(D_MODEL),
        out_shape=jax.ShapeDtypeStruct((n, D_MODEL), F32),
        compiler_params=_params("parallel"),
        name="gdn_outproj_ffn",
    )(a, w_out, res, g0, b0, wg, wu, wd, g1, b1)


def _swa_inproj_kernel(x_ref, w_ref, b_ref, q_ref, kv_ref):
    p = _dot(x_ref[...].astype(BF16), w_ref[...]) + b_ref[...]
    q_ref[...] = (p[:, :SWA_QDIM] * (SWA_HEAD_DIM ** -0.5)).astype(BF16)
    kv_ref[...] = p[:, SWA_QDIM:].astype(BF16)


def _swa_inproj(x, w, b):
    n = x.shape[0]
    tm = TM_SWA_IN
    return pl.pallas_call(
        _swa_inproj_kernel,
        grid=(n // tm,),
        in_specs=[pl.BlockSpec((tm, D_MODEL), lambda i: (i, 0)), pl.BlockSpec(w.shape, lambda i: (0, 0)),
                  pl.BlockSpec(b.shape, lambda i: (0, 0))],
        out_specs=[pl.BlockSpec((tm, SWA_QDIM), lambda i: (i, 0)), pl.BlockSpec((tm, 2 * SWA_KVDIM), lambda i: (i, 0))],
        out_shape=[jax.ShapeDtypeStruct((n, SWA_QDIM), BF16), jax.ShapeDtypeStruct((n, 2 * SWA_KVDIM), BF16)],
        compiler_params=_params("parallel"),
        name="swa_inproj",
    )(x, w, b)


def _band_tables():
    qi = np.arange(SWA_BLOCK)[:, None]
    kj = np.arange(2 * SWA_BLOCK)[None, :]
    dist = qi + SWA_BLOCK - kj
    d = np.maximum(dist, 0)
    max_exact = REL_BUCKETS // 2
    df = np.maximum(d, 1).astype(np.float32)
    large = max_exact + (np.log(df / np.float32(max_exact)) / np.float32(math.log(REL_MAX_DIST / max_exact))
                         * np.float32(REL_BUCKETS - max_exact)).astype(np.int32)
    large = np.minimum(large, REL_BUCKETS - 1)
    bucket = np.where(d < max_exact, d, large).astype(np.int32)
    band_ok = (dist >= 0) & (dist < SWA_WINDOW)
    valid = np.stack([band_ok & (kj >= SWA_BLOCK), band_ok]).astype(np.int32)
    return np.ascontiguousarray(bucket.T), np.ascontiguousarray(valid.transpose(0, 2, 1))


def _bias_kernel(relb_ref, bucket_ref, valid_ref, o_ref):
    bucket = bucket_ref[...]
    in_bucket = [bucket == b for b in range(REL_BUCKETS)]
    for h in range(SWA_Q_HEADS):
        acc = jnp.zeros(bucket.shape, F32)
        for b in range(REL_BUCKETS):
            acc = jnp.where(in_bucket[b], relb_ref[b, h], acc)
        for t in range(2):
            o_ref[t, h] = jnp.where(valid_ref[t] != 0, acc, NEG_BIG)


def _bias_table(rel_bias):
    bucket, valid = _band_tables()
    shape = (2 * SWA_BLOCK, SWA_BLOCK)
    return pl.pallas_call(
        _bias_kernel,
        grid=(1,),
        in_specs=[pl.BlockSpec(memory_space=pltpu.SMEM), pl.BlockSpec(shape, lambda i: (0, 0)),
                  pl.BlockSpec((2,) + shape, lambda i: (0, 0, 0))],
        out_specs=pl.BlockSpec((2, SWA_Q_HEADS) + shape, lambda i: (0, 0, 0, 0)),
        out_shape=jax.ShapeDtypeStruct((2, SWA_Q_HEADS) + shape, F32),
        compiler_params=_params("arbitrary"),
        name="swa_bias_table",
    )(rel_bias, jnp.asarray(bucket), jnp.asarray(valid))


def _swa_kernel(q_ref, kvc_ref, kvp_ref, bias_ref, sink_ref, o_ref):
    kv = jnp.concatenate([kvp_ref[...], kvc_ref[...]], axis=0)
    dh = SWA_HEAD_DIM
    nk = 2 * SWA_BLOCK
    k_all = kv[:, :SWA_KVDIM]
    lane = lax.broadcasted_iota(jnp.int32, k_all.shape, 1)
    zero = jnp.zeros_like(k_all)
    for kh in range(SWA_KV_HEADS):
        k_own = jnp.where((lane >= kh * dh) & (lane < (kh + 1) * dh), k_all, zero)
        k_other = pltpu.roll(k_own.astype(F32), dh, axis=1).astype(BF16)
        k_even, k_odd = (k_own, k_other) if kh == 0 else (k_other, k_own)
        k2 = jnp.concatenate([k_even, k_odd], axis=0)
        v_t = kv[:, SWA_KVDIM + kh * dh:SWA_KVDIM + (kh + 1) * dh].T
        pairs = [kh * (SWA_GROUP // 2) + p for p in range(SWA_GROUP // 2)]
        heads = [2 * p + r for p in pairs for r in range(2)]
        st = [_dot_nt(k2, q_ref[:, p * 2 * dh:(p + 1) * 2 * dh]) for p in pairs]
        scores = [st[i // 2][(i % 2) * nk:(i % 2 + 1) * nk] + bias_ref[0, h] for i, h in enumerate(heads)]
        sinks = [sink_ref[0:1, h:h + 1] for h in heads]
        maxes = [jnp.maximum(jnp.max(s, axis=0, keepdims=True), sk) for s, sk in zip(scores, sinks)]
        expo = [jnp.exp(s - m) for s, m in zip(scores, maxes)]
        sums = [jnp.sum(p, axis=0, keepdims=True) for p in expo]
        inv_den = [1.0 / (t + jnp.exp(sk - m)) for t, sk, m in zip(sums, sinks, maxes)]
        for i, p in enumerate(pairs):
            o_t = _dot(v_t, jnp.concatenate(expo[2 * i:2 * i + 2], axis=1).astype(BF16))
            z = jnp.concatenate([o_t[:, :SWA_BLOCK] * inv_den[2 * i], o_t[:, SWA_BLOCK:] * inv_den[2 * i + 1]],
                                axis=0)
            o_ref[:, p * 2 * dh:(p + 1) * 2 * dh] = z.T.astype(o_ref.dtype)


def _swa_attention(q, kv, bias, sinks, batch, seq_len):
    n = q.shape[0]
    nb = seq_len // SWA_BLOCK
    blk = SWA_BLOCK
    return pl.pallas_call(
        _swa_kernel,
        grid=(batch, nb),
        in_specs=[pl.BlockSpec((blk, SWA_QDIM), lambda b, j: (b * nb + j, 0)),
                  pl.BlockSpec((blk, 2 * SWA_KVDIM), lambda b, j: (b * nb + j, 0)),
                  pl.BlockSpec((blk, 2 * SWA_KVDIM), lambda b, j: (b * nb + jnp.maximum(j - 1, 0), 0)),
                  pl.BlockSpec((1, SWA_Q_HEADS, 2 * blk, blk), lambda b, j: (jnp.minimum(j, 1), 0, 0, 0)),
                  pl.BlockSpec((1, LANES), lambda b, j: (0, 0))],
        out_specs=pl.BlockSpec((blk, SWA_QDIM), lambda b, j: (b * nb + j, 0)),
        out_shape=jax.ShapeDtypeStruct((n, SWA_QDIM), BF16),
        compiler_params=_params("parallel", "arbitrary"),
        name="swa_attention",
    )(q, kv, kv, bias, sinks)


def _router_kernel(x_ref, wr_ref, route_ref, route_t_ref, cnt_ref, run_ref, *, tm):
    @pl.when(pl.program_id(0) == 0)
    def _():
        run_ref[...] = jnp.zeros(run_ref.shape, F32)

    logits = _dot3(x_ref[...], wr_ref[...])
    lane = lax.broadcasted_iota(jnp.int32, logits.shape, 1)
    lane_f = lane.astype(F32)
    lg = jnp.where(lane < N_EXPERTS, logits, NEG_BIG)
    m1 = jnp.max(lg, axis=-1, keepdims=True)
    i1 = jnp.min(jnp.where(lg == m1, lane_f, float(LANES)), axis=-1, keepdims=True)
    oh1 = lane_f == i1
    lg2 = jnp.where(oh1, NEG_BIG, lg)
    m2 = jnp.max(lg2, axis=-1, keepdims=True)
    i2 = jnp.min(jnp.where(lg2 == m2, lane_f, float(LANES)), axis=-1, keepdims=True)
    oh2 = lane_f == i2
    e = jnp.exp(m2 - m1)
    w0 = 1.0 / (1.0 + e)
    w1 = e * w0

    cnt = jnp.where(oh1, 1.0, 0.0) + jnp.where(oh2, 1.0, 0.0)
    r = lax.broadcasted_iota(jnp.int32, (tm, tm), 0)
    c = lax.broadcasted_iota(jnp.int32, (tm, tm), 1)
    before = jnp.where(r > c, 1.0, 0.0).astype(BF16)
    excl = _dot(before, cnt.astype(BF16)) + run_ref[...]
    rank0 = jnp.sum(jnp.where(oh1, excl, 0.0), axis=-1, keepdims=True)
    rank1 = jnp.sum(jnp.where(oh2, excl, 0.0), axis=-1, keepdims=True)
    run = run_ref[...] + jnp.sum(cnt, axis=0, keepdims=True)
    run_ref[...] = run
    cnt_ref[...] = run

    vals = (i1, i2, rank0, rank1, w0, w1)
    out = jnp.zeros(logits.shape, F32)
    for idx, val in enumerate(vals):
        out = jnp.where(lane == idx, val, out)
    route_ref[...] = out
    route_t_ref[...] = out.T[:SUBLANES, :]


def _router(x, w_router_padded):
    n = x.shape[0]
    tm = TM_ROUTER
    return pl.pallas_call(
        functools.partial(_router_kernel, tm=tm),
        grid=(n // tm,),
        in_specs=[pl.BlockSpec((tm, D_MODEL), lambda i: (i, 0)), pl.BlockSpec((D_MODEL, LANES), lambda i: (0, 0))],
        out_specs=[pl.BlockSpec((tm, LANES), lambda i: (i, 0)), pl.BlockSpec((SUBLANES, tm), lambda i: (0, i)),
                   pl.BlockSpec((1, LANES), lambda i: (0, 0))],
        out_shape=[jax.ShapeDtypeStruct((n, LANES), F32), jax.ShapeDtypeStruct((SUBLANES, n), F32),
                   jax.ShapeDtypeStruct((1, LANES), F32)],
        scratch_shapes=[pltpu.VMEM((1, LANES), F32)],
        compiler_params=_params("arbitrary"),
        name="moe_router",
    )(x, w_router_padded)


def _dispatch_kernel(pos0_ref, pos1_ref, meta_ref, x_ref, wsrc_ref, xs_hbm, wdst_ref, zero_ref, sem, pad_sem, *, td):
    i = pl.program_id(0)
    wdst_ref[...] = wsrc_ref[...].astype(BF16)

    def row_copy(t, dst):
        return pltpu.make_async_copy(x_ref.at[pl.ds(t, 1)], xs_hbm.at[pl.ds(dst, 1)], sem)

    def start(t, carry):
        row_copy(t, pos0_ref[0, 0, t]).start()
        row_copy(t, pos1_ref[0, 0, t]).start(priority=1)
        return carry

    lax.fori_loop(0, td, start, 0, unroll=DMA_UNROLL)

    @pl.when(i == pl.num_programs(0) - 1)
    def _():
        zero_ref[...] = jnp.zeros(zero_ref.shape, F32)

        zr = zero_ref.shape[0]

        def row_zero(dst):
            return pltpu.make_async_copy(zero_ref.at[pl.ds(0, 1)], xs_hbm.at[pl.ds(dst, 1)], pad_sem)

        def block_zero(blk):
            return pltpu.make_async_copy(zero_ref, xs_hbm.at[pl.ds(pl.multiple_of(blk * zr, zr), zr)], pad_sem)

        def zero_range(copy, lo, hi):
            lax.fori_loop(lo, hi, lambda r, carry: (copy(r).start(), carry)[1], 0)
            lax.fori_loop(lo, hi, lambda r, carry: (copy(0).wait(), carry)[1], 0)

        for e in range(N_EXPERTS):
            off, cnt, cnt_up, padded = (meta_ref[r, e] for r in range(4))
            zero_range(lambda r, off=off: row_zero(off + r), cnt, cnt_up)
            zero_range(block_zero, (off + cnt_up) // zr, (off + padded) // zr)

        used = meta_ref[0, N_EXPERTS - 1] + meta_ref[3, N_EXPERTS - 1]
        zero_range(block_zero, used // zr, xs_hbm.shape[0] // zr)

    for _ in range(2):
        pltpu.make_async_copy(x_ref, xs_hbm.at[pl.ds(0, td)], sem).wait()


def _dispatch(x, pos, meta, n_rows, to_bf16):
    n = x.shape[0]
    td = TD_DISPATCH
    pos_spec = pl.BlockSpec((1, 1, td), lambda i: (i, 0, 0), memory_space=pltpu.SMEM)
    (cast_spec,), (cast_shape,) = _cast_specs([to_bf16], n // td)
    return pl.pallas_call(
        functools.partial(_dispatch_kernel, td=td),
        grid=(n // td,),
        in_specs=[pos_spec, pos_spec, pl.BlockSpec(memory_space=pltpu.SMEM),
                  pl.BlockSpec((td, D_MODEL), lambda i: (i, 0)), cast_spec],
        out_specs=[pl.BlockSpec(memory_space=pl.ANY), cast_spec],
        out_shape=[jax.ShapeDtypeStruct((n_rows, D_MODEL), F32), cast_shape],
        scratch_shapes=[pltpu.VMEM((ZERO_ROWS, D_MODEL), F32), pltpu.SemaphoreType.DMA, pltpu.SemaphoreType.DMA],
        compiler_params=_params("arbitrary"),
        name="moe_dispatch",
    )(pos[0].reshape(n // td, 1, td), pos[1].reshape(n // td, 1, td), meta, x, to_bf16)


def _moe_kernel(we_ref, wb_ref, nv_ref, cnt_ref, *refs, n_sub, tm):
    xs_refs = refs[:n_sub]
    wg_ref, wu_ref, wd_ref, ys_hbm, acc_ref, xb_ref, sem = refs[n_sub:]
    j = pl.program_id(0)
    f = pl.program_id(1)
    last_f = pl.num_programs(1) - 1
    n_windows, n_tiles = cnt_ref[0], cnt_ref[1]

    def tile_write(slot, block):
        return pltpu.make_async_copy(acc_ref.at[slot], ys_hbm.at[pl.ds(pl.multiple_of(block * tm, tm), tm)],
                                     sem.at[slot])

    @pl.when(j < n_windows)
    def _():
        prev_tiles = jnp.where(j > 0, nv_ref[jnp.maximum(j - 1, 0)], 0)

        for s in range(n_sub):
            @pl.when(s < nv_ref[j])
            def _(s=s):
                @pl.when(f == 0)
                def _():
                    xb_ref[s] = xs_refs[s][...].astype(BF16)

                xb = xb_ref[s]
                h = (_silu(_dot(xb, wg_ref[0])) * _dot(xb, wu_ref[0])).astype(BF16)
                part = _dot(h, wd_ref[0])

                @pl.when(f == 0)
                def _():
                    @pl.when(s < prev_tiles)
                    def _():
                        tile_write(s, 0).wait()

                    acc_ref[s] = part

                @pl.when(f != 0)
                def _():
                    acc_ref[s] += part

                @pl.when(f == last_f)
                def _():
                    tile_write(s, wb_ref[j] + s).start()

            @pl.when((f == 0) & (s >= nv_ref[j]) & (s < prev_tiles))
            def _(s=s):
                tile_write(s, 0).wait()

        @pl.when((f == last_f) & (j == n_windows - 1))
        def _():
            for s in range(n_sub):
                @pl.when(s < nv_ref[j])
                def _(s=s):
                    tile_write(s, 0).wait()

            acc_ref[0] = jnp.zeros(acc_ref.shape[1:], F32)
            total_tiles = ys_hbm.shape[0] // tm
            lax.fori_loop(n_tiles, total_tiles, lambda t, c: (tile_write(0, t).start(), c)[1], 0)
            lax.fori_loop(n_tiles, total_tiles, lambda t, c: (tile_write(0, 0).wait(), c)[1], 0)


def _moe_experts(xs, wg, wu, wd, win_expert, win_block, win_tiles, counts):
    n_rows = xs.shape[0]
    tm, tf, n_sub = TM_MOE, TF_MOE, MOE_TILES_PER_WINDOW
    max_windows = win_expert.shape[0]

    def xs_spec(s):
        return pl.BlockSpec((tm, D_MODEL), lambda j, f, we, wb, nv, cnt: (wb[j] + jnp.minimum(s, nv[j] - 1), 0))

    grid_spec = pltpu.PrefetchScalarGridSpec(
        num_scalar_prefetch=4,
        grid=(max_windows, EXPERT_DIM // tf),
        in_specs=[xs_spec(s) for s in range(n_sub)] + [
            pl.BlockSpec((1, D_MODEL, tf), lambda j, f, we, wb, nv, cnt: (we[j], 0, f)),
            pl.BlockSpec((1, D_MODEL, tf), lambda j, f, we, wb, nv, cnt: (we[j], 0, f)),
            pl.BlockSpec((1, tf, D_MODEL), lambda j, f, we, wb, nv, cnt: (we[j], f, 0))],
        out_specs=pl.BlockSpec(memory_space=pl.ANY),
        scratch_shapes=[pltpu.VMEM((n_sub, tm, D_MODEL), F32), pltpu.VMEM((n_sub, tm, D_MODEL), BF16),
                        pltpu.SemaphoreType.DMA((n_sub,))],
    )
    return pl.pallas_call(
        functools.partial(_moe_kernel, n_sub=n_sub, tm=tm),
        grid_spec=grid_spec,
        out_shape=jax.ShapeDtypeStruct((n_rows, D_MODEL), F32),
        compiler_params=_params("arbitrary", "arbitrary"),
        name="moe_experts",
    )(win_expert, win_block, win_tiles, counts, *([xs] * n_sub), wg, wu, wd)


def _combine_kernel(pos0_ref, pos1_ref, pos0n_ref, pos1n_ref, x_ref, route_ref, ys_hbm, g_ref, b_ref, o_ref,
                    y_ref, sem, *, tc):
    i = pl.program_id(0)
    slot = i % 2

    def issue(p_refs, s):
        def start(t, carry):
            for kk in range(2):
                pltpu.make_async_copy(ys_hbm.at[pl.ds(p_refs[kk][0, 0, t], 1)],
                                      y_ref.at[s, kk, pl.ds(t, 1)], sem.at[s]).start(priority=kk)
            return carry

        lax.fori_loop(0, tc, start, 0, unroll=DMA_UNROLL)

    @pl.when(i == 0)
    def _():
        issue((pos0_ref, pos1_ref), 0)

    @pl.when(i + 1 < pl.num_programs(0))
    def _():
        issue((pos0n_ref, pos1n_ref), 1 - slot)

    for kk in range(2):
        pltpu.make_async_copy(ys_hbm.at[pl.ds(0, tc)], y_ref.at[slot, kk], sem.at[slot]).wait()

    route = route_ref[...]
    f = route[:, 4:5] * y_ref[slot, 0] + route[:, 5:6] * y_ref[slot, 1]
    o_ref[...] = _layer_norm(ALPHA * x_ref[...] + f, g_ref[...], b_ref[...])


def _combine_ln(x, route, pos, ys, g, b):
    n = x.shape[0]
    tc = TC_COMBINE
    nt = n // tc
    pos0, pos1 = pos[0].reshape(nt, 1, tc), pos[1].reshape(nt, 1, tc)
    cur = pl.BlockSpec((1, 1, tc), lambda i: (i, 0, 0), memory_space=pltpu.SMEM)
    nxt = pl.BlockSpec((1, 1, tc), lambda i: (jnp.minimum(i + 1, nt - 1), 0, 0), memory_space=pltpu.SMEM)
    return pl.pallas_call(
        functools.partial(_combine_kernel, tc=tc),
        grid=(nt,),
        in_specs=[cur, cur, nxt, nxt,
                  pl.BlockSpec((tc, D_MODEL), lambda i: (i, 0)),
                  pl.BlockSpec((tc, LANES), lambda i: (i, 0)),
                  pl.BlockSpec(memory_space=pl.ANY),
                  pl.BlockSpec((1, D_MODEL), lambda i: (0, 0)), pl.BlockSpec((1, D_MODEL), lambda i: (0, 0))],
        out_specs=pl.BlockSpec((tc, D_MODEL), lambda i: (i, 0)),
        out_shape=jax.ShapeDtypeStruct((n, D_MODEL), F32),
        scratch_shapes=[pltpu.VMEM((2, 2, tc, D_MODEL), F32), pltpu.SemaphoreType.DMA((2,))],
        compiler_params=_params("arbitrary"),
        name="moe_combine_ln",
    )(pos0, pos1, pos0, pos1, x, route, ys, g, b)


def _moe_layer(x, route, route_t, counts_f, wg, wu, wd_f32, g, b):
    n = x.shape[0]
    tm = TM_MOE

    counts = counts_f[0, :N_EXPERTS].astype(jnp.int32)
    padded = ((counts + tm - 1) // tm) * tm
    offs = jnp.cumsum(padded) - padded
    eids = jnp.arange(N_EXPERTS, dtype=jnp.int32)
    e01 = route_t[0:2].astype(jnp.int32)
    rank01 = route_t[2:4].astype(jnp.int32)
    base = jnp.sum(jnp.where(e01[:, None, :] == eids[None, :, None], offs[None, :, None], 0), axis=1)
    pos = base + rank01
    counts_up = jnp.minimum(padded, ((counts + ZERO_ROWS - 1) // ZERO_ROWS) * ZERO_ROWS)
    meta = jnp.stack([offs, counts, counts_up, padded]).astype(jnp.int32)

    n_rows = 2 * n + N_EXPERTS * tm
    nsub = MOE_TILES_PER_WINDOW
    tiles = padded // tm
    wins = (tiles + nsub - 1) // nsub
    win_end = jnp.cumsum(wins)
    n_windows = win_end[-1]
    max_windows = (n_rows // tm + nsub - 1) // nsub + N_EXPERTS
    wj = jnp.minimum(jnp.arange(max_windows, dtype=jnp.int32), n_windows - 1)
    win_expert = jnp.minimum(jnp.sum((wj[:, None] >= win_end[None, :]).astype(jnp.int32), axis=-1), N_EXPERTS - 1)
    pick = lambda v: jnp.sum(jnp.where(win_expert[:, None] == eids[None, :], v[None, :], 0), axis=-1)
    local = wj - pick(win_end - wins)
    win_block = pick(offs // tm) + nsub * local
    win_tiles = jnp.minimum(nsub, pick(tiles) - nsub * local)
    counts_nt = jnp.stack([n_windows, jnp.sum(tiles)])

    xs, wd = _dispatch(x, pos, meta, n_rows, wd_f32.reshape(-1, D_MODEL))
    wd = wd.reshape(wd_f32.shape)
    i32 = lambda a: a.astype(jnp.int32)
    ys = _moe_experts(xs, wg, wu, wd, i32(win_expert), i32(win_block), i32(win_tiles), i32(counts_nt))
    return _combine_ln(x, route, pos, ys, g, b)


def kernel(x, a_w_in, a_conv_w, a_a_log, a_dt_bias, a_norm_w, a_w_out, b_w_in, b_b_in, b_sinks, b_w_out, rel_bias,
           ffn_w_gate, ffn_w_up, ffn_w_down, moe_router, moe_w_gate, moe_w_up, moe_w_down, ln_g, ln_b):
    batch, seq_len, _ = x.shape
    n = batch * seq_len
    x0 = x.reshape(n, D_MODEL)
    ln_g = ln_g.reshape(DEPTH, 2, 1, D_MODEL)
    ln_b = ln_b.reshape(DEPTH, 2, 1, D_MODEL)

    w_in = a_w_in[0]
    w_all = jnp.pad(w_in, ((0, 0), (0, LANES - 2 * GDN_V_HEADS))).astype(BF16)
    pad_gate = lambda p: jnp.pad(p.reshape(1, GDN_V_HEADS), ((0, 0), (GDN_V_HEADS, LANES - 2 * GDN_V_HEADS)))
    gate_params = jnp.concatenate([pad_gate(a_a_log[0]), pad_gate(a_dt_bias[0])], axis=0)
    q, k, v, zs, gb = _gdn_inproj(x0, w_all, a_conv_w[0], gate_params, seq_len)
    cast_cols = 256
    o, casted = _gdn_chunk(q, k, v, zs, gb, a_norm_w[0].reshape(1, GDN_HEAD), batch, seq_len,
                           [moe_w_gate[0].reshape(-1, EXPERT_DIM), moe_w_up[0].reshape(-1, EXPERT_DIM),
                            ffn_w_gate[0], ffn_w_up[0], ffn_w_down[0].reshape(-1, cast_cols),
                            a_w_out[0], b_w_in[0], b_w_out[0]])
    moe_wg, moe_wu, ffn_wg, ffn_wu, ffn_wd, w_out_a, w_in_b, w_out_b = casted
    x2 = _outproj_ffn(o, w_out_a, x0, ln_g[0, 0], ln_b[0, 0],
                      ffn_wg, ffn_wu, ffn_wd.reshape(FFN_DIM, D_MODEL), ln_g[0, 1], ln_b[0, 1])

    qa, kva = _swa_inproj(x2, w_in_b, b_b_in[0].reshape(1, -1))
    bias = _bias_table(rel_bias)
    sinks = jnp.pad(b_sinks[0].reshape(1, SWA_Q_HEADS), ((0, 0), (0, LANES - SWA_Q_HEADS)))
    oa = _swa_attention(qa, kva, bias, sinks, batch, seq_len)
    w_router = jnp.pad(moe_router[0], ((0, 0), (0, LANES - N_EXPERTS)))
    x3 = _proj_ln(oa, w_out_b, x2, ln_g[1, 0], ln_b[1, 0], "swa_outproj_ln")
    route, route_t, counts = _router(x3, w_router)
    expert_shape = (N_EXPERTS, D_MODEL, EXPERT_DIM)
    x4 = _moe_layer(x3, route, route_t, counts, moe_wg.reshape(expert_shape), moe_wu.reshape(expert_shape),
                    moe_w_down[0], ln_g[1, 1], ln_b[1, 1])
    return x4.reshape(batch, seq_len, D_MODEL)
```

```python
import functools
import math

import numpy as np
import jax
import jax.numpy as jnp
from jax import lax
from jax.experimental import pallas as pl
from jax.experimental.pallas import tpu as pltpu

F32 = jnp.float32
BF16 = jnp.bfloat16

D_MODEL = 1024
DEPTH = 2
ALPHA = (2.0 * DEPTH) ** 0.25
LN_EPS = 1e-5

GDN_K_HEADS = 4
GDN_V_HEADS = 8
GDN_HEAD = 128
GDN_KDIM = GDN_K_HEADS * GDN_HEAD
GDN_VDIM = GDN_V_HEADS * GDN_HEAD
GDN_CONV = 4
GDN_CHUNK = 64
GDN_QKV = 2 * GDN_KDIM + GDN_VDIM
GDN_EPS = 1e-6

SWA_Q_HEADS = 16
SWA_KV_HEADS = 2
SWA_GROUP = SWA_Q_HEADS // SWA_KV_HEADS
SWA_HEAD_DIM = 64
SWA_WINDOW = 128
SWA_BLOCK = 128
SWA_QDIM = SWA_Q_HEADS * SWA_HEAD_DIM
SWA_KVDIM = SWA_KV_HEADS * SWA_HEAD_DIM
REL_BUCKETS = 32
REL_MAX_DIST = 128

FFN_DIM = 2816
N_EXPERTS = 8
EXPERT_DIM = 3584

LANES = 128
SUBLANES = 8
NEG_BIG = -1e30
VMEM_LIMIT = 56 * 1024 * 1024

TM_GDN_IN = 512
TM_FFN = 512
SWA_BLOCKS_PER_STEP = 4
TM_ROUTER = 512
TM_MOE = 512
TF_MOE = 1792
MOE_TILES_PER_WINDOW = 3
TD_DISPATCH = 512
TC_COMBINE = 256
ZERO_ROWS = 64
DMA_UNROLL = 8


def _params(*sem):
    return pltpu.CompilerParams(dimension_semantics=sem, vmem_limit_bytes=VMEM_LIMIT)


def _dot(a, b):
    return jnp.dot(a, b, preferred_element_type=F32)


def _dot_nt(a, b):
    return lax.dot_general(a, b, (((1,), (1,)), ((), ())), preferred_element_type=F32)


def _dot_tn(a, b):
    return lax.dot_general(a, b, (((0,), (0,)), ((), ())), preferred_element_type=F32)


def _split(x):
    hi = x.astype(BF16)
    lo = (x - hi.astype(F32)).astype(BF16)
    return hi, lo


def _dot3(a, b):
    ah, al = _split(a)
    bh, bl = _split(b)
    return _dot(ah, bh) + (_dot(ah, bl) + _dot(al, bh))


def _silu(x):
    return x * jax.nn.sigmoid(x)


def _layer_norm(y, g, b):
    mu = jnp.mean(y, axis=-1, keepdims=True)
    yc = y - mu
    var = jnp.mean(yc * yc, axis=-1, keepdims=True)
    return yc * lax.rsqrt(var + LN_EPS) * g + b


def _gdn_inproj_kernel(x_ref, wf_ref, convw_ref, gp_ref,
                       q_ref, k_ref, v_ref, z_ref, gb_ref, ext_ref, w_ref, *, tm, tiles_per_seq):
    i = pl.program_id(0)
    xb = x_ref[...].astype(BF16)

    @pl.when(i == 0)
    def _():
        n_in = wf_ref.shape[1]
        w_ref[:, :n_in] = wf_ref[...].astype(BF16)
        w_ref[:, n_in:] = jnp.zeros((w_ref.shape[0], w_ref.shape[1] - n_in), BF16)

    @pl.when(i % tiles_per_seq == 0)
    def _():
        ext_ref[0:SUBLANES, :] = jnp.zeros((SUBLANES, GDN_QKV), F32)

    @pl.when(i % tiles_per_seq != 0)
    def _():
        ext_ref[0:SUBLANES, :] = ext_ref[tm:tm + SUBLANES, :]

    ext_ref[SUBLANES:, :] = _dot(xb, w_ref[:, :GDN_QKV])
    z_ref[...] = _silu(_dot(xb, w_ref[:, GDN_QKV:GDN_QKV + GDN_VDIM]))

    n_chunks = GDN_QKV // LANES
    for c in range(n_chunks):
        cs = slice(c * LANES, (c + 1) * LANES)
        acc = convw_ref[GDN_CONV - 1:GDN_CONV, cs] * ext_ref[SUBLANES:SUBLANES + tm, cs]
        for j in range(GDN_CONV - 1):
            off = SUBLANES - (GDN_CONV - 1) + j
            acc = acc + convw_ref[j:j + 1, cs] * ext_ref[off:off + tm, cs]
        y = _silu(acc)
        if c < 2 * GDN_K_HEADS:
            y = y * lax.rsqrt(jnp.sum(y * y, axis=-1, keepdims=True) + GDN_EPS)
            if c < GDN_K_HEADS:
                q_ref[:, cs] = y * (GDN_HEAD ** -0.5)
            else:
                k_ref[:, (c - GDN_K_HEADS) * LANES:(c - GDN_K_HEADS + 1) * LANES] = y
        else:
            cv = c - 2 * GDN_K_HEADS
            v_ref[:, cv * LANES:(cv + 1) * LANES] = y

    ba = _dot(xb, w_ref[:, GDN_QKV + GDN_VDIM:])
    lane = lax.broadcasted_iota(jnp.int32, ba.shape, 1)
    sp = ba + gp_ref[1:2, :]
    softplus = jnp.maximum(sp, 0.0) + jnp.log(1.0 + jnp.exp(-jnp.abs(sp)))
    g = -jnp.exp(gp_ref[0:1, :]) * softplus
    gb_ref[...] = jnp.where(lane < GDN_V_HEADS, jax.nn.sigmoid(ba), g)


def _gdn_inproj(x2d, w_in, conv_w, gate_params, seq_len):
    n = x2d.shape[0]
    tm = TM_GDN_IN
    kern = functools.partial(_gdn_inproj_kernel, tm=tm, tiles_per_seq=seq_len // tm)
    full = lambda shape: pl.BlockSpec(shape, lambda i: (0,) * len(shape))
    rows = lambda width: pl.BlockSpec((tm, width), lambda i: (i, 0))
    return pl.pallas_call(
        kern,
        grid=(n // tm,),
        in_specs=[rows(D_MODEL), pl.BlockSpec(w_in.shape, lambda i: (0, 0), pipeline_mode=pl.Buffered(1)),
                  full(conv_w.shape), full(gate_params.shape)],
        out_specs=[rows(GDN_KDIM), rows(GDN_KDIM), rows(GDN_VDIM), rows(GDN_VDIM), rows(LANES)],
        out_shape=[jax.ShapeDtypeStruct((n, GDN_KDIM), F32), jax.ShapeDtypeStruct((n, GDN_KDIM), F32),
                   jax.ShapeDtypeStruct((n, GDN_VDIM), F32), jax.ShapeDtypeStruct((n, GDN_VDIM), F32),
                   jax.ShapeDtypeStruct((n, LANES), F32)],
        scratch_shapes=[pltpu.VMEM((tm + SUBLANES, GDN_QKV), F32),
                        pltpu.VMEM((D_MODEL, GDN_QKV + GDN_VDIM + LANES), BF16)],
        compiler_params=_params("arbitrary"),
        name="gdn_inproj",
    )(x2d, w_in, conv_w, gate_params)


GDN_CHUNKS_PER_STEP = 2
NEUMANN_BLOCK = 4


def _bf16_all(xs):
    return [x.astype(BF16) for x in xs]


def _dot_all(a_list, b_list):
    return [_dot(a, b) for a, b in zip(a_list, b_list)]


def _unit_lower_inverse_all(a_list, row, col):
    shift = int(math.log2(NEUMANN_BLOCK))
    eye = jnp.where(row == col, 1.0, 0.0).astype(F32)
    on_diag_block = (row >> shift) == (col >> shift)
    d = [jnp.where(on_diag_block, a, 0.0) for a in a_list]
    d_b = _bf16_all(d)
    x = [eye - di for di in d]
    d2_b = _bf16_all(_dot_all(d_b, d_b))
    x = [xi + t for xi, t in zip(x, _dot_all(_bf16_all(x), d2_b))]
    size = NEUMANN_BLOCK
    while size < a_list[0].shape[0]:
        shift = int(math.log2(size))
        rbl, cbl = row >> shift, col >> shift
        below = ((rbl & 1) == 1) & (cbl == rbl - 1)
        l_b = _bf16_all([jnp.where(below, a, 0.0) for a in a_list])
        x_b = _bf16_all(x)
        xl_b = _bf16_all(_dot_all(x_b, l_b))
        x = [xi - t for xi, t in zip(x, _dot_all(xl_b, x_b))]
        size *= 2
    return x


def _gdn_chunk_kernel(q_ref, k_ref, v_ref, z_ref, gb_ref, nw_ref, *rest, nb, nck, n_cast):
    cast_src, (o_ref, *cast_dst), s_ref = rest[:n_cast], rest[n_cast:2 * n_cast + 1], rest[-1]
    for src, dst in zip(cast_src, cast_dst):
        dst[...] = src[...].astype(BF16)
    c = GDN_CHUNK
    nh = GDN_V_HEADS

    @pl.when(pl.program_id(0) == 0)
    def _():
        s_ref[...] = jnp.zeros(s_ref.shape, F32)

    row = lax.broadcasted_iota(jnp.int32, (c, c), 0)
    col = lax.broadcasted_iota(jnp.int32, (c, c), 1)
    causal = row >= col
    strict = row > col
    tril = jnp.where(causal, 1.0, 0.0).astype(BF16)
    rs = lambda ci: slice(ci * c, (ci + 1) * c)
    ks = lambda h: slice((h // (nh // GDN_K_HEADS)) * GDN_HEAD, (h // (nh // GDN_K_HEADS) + 1) * GDN_HEAD)
    vs = lambda h: slice(h * GDN_HEAD, (h + 1) * GDN_HEAD)

    gates = {}
    for ci in range(nck):
        for b in range(nb):
            gb = gb_ref[b, rs(ci), :]
            g_hi, g_lo = _split(gb)
            gc = _dot(tril, g_hi) + _dot(tril, g_lo)
            g_last = gc[c - 1:c, :]
            gates[ci, b] = dict(gb=gb, gc=gc, gc_t=gc.T, eg=jnp.exp(gc), e_last=jnp.exp(g_last),
                                e_rest=jnp.exp(g_last - gc))

    items = [(ci, b, h) for ci in range(nck) for b in range(nb) for h in range(nh)]

    a_kk, a_qk, k_beta = [], {}, {}
    for it in items:
        ci, b, h = it
        gt = gates[ci, b]
        k = k_ref[b, rs(ci), ks(h)]
        kb = k * gt["gb"][:, h:h + 1]
        kq = jnp.concatenate([kb, q_ref[b, rs(ci), ks(h)]], axis=0).astype(BF16)
        gram = _dot_nt(kq, k.astype(BF16))
        gl = nh + h
        decay = jnp.exp(jnp.where(causal, gt["gc"][:, gl:gl + 1] - gt["gc_t"][gl:gl + 1, :], NEG_BIG))
        a_kk.append(jnp.where(strict, gram[:c] * decay, 0.0))
        a_qk[it] = (gram[c:] * decay).astype(BF16)
        k_beta[it] = kb

    t_inv = dict(zip(items, _unit_lower_inverse_all(a_kk, row, col)))

    uw = {}
    for it in items:
        ci, b, h = it
        gt = gates[ci, b]
        gl = nh + h
        rhs = jnp.concatenate([v_ref[b, rs(ci), vs(h)] * gt["gb"][:, h:h + 1],
                               k_beta[it] * gt["eg"][:, gl:gl + 1]], axis=1)
        uw[it] = _dot(t_inv[it].astype(BF16), rhs.astype(BF16))

    for ci in range(nck):
        chunk_items = [(ci, b, h) for b in range(nb) for h in range(nh)]

        ws_qs, states = {}, {}
        for it in chunk_items:
            _, b, h = it
            gl = nh + h
            s = s_ref[b * nh + h]
            wq = jnp.concatenate([uw[it][:, GDN_HEAD:],
                                  q_ref[b, rs(ci), ks(h)] * gates[ci, b]["eg"][:, gl:gl + 1]], axis=0)
            ws_qs[it] = _dot(wq.astype(BF16), s.astype(BF16))
            states[it] = s

        for it in chunk_items:
            _, b, h = it
            gt = gates[ci, b]
            gl = nh + h
            v_new = (uw[it][:, :GDN_HEAD] - ws_qs[it][:c]).astype(BF16)
            o = ws_qs[it][c:] + _dot(a_qk[it], v_new)
            k_dec = (k_ref[b, rs(ci), ks(h)] * gt["e_rest"][:, gl:gl + 1]).astype(BF16)
            s_ref[b * nh + h] = states[it] * gt["e_last"][:, gl:gl + 1] + _dot_tn(k_dec, v_new)
            o = o * lax.rsqrt(jnp.mean(o * o, axis=-1, keepdims=True) + GDN_EPS) * nw_ref[...]
            o_ref[b, rs(ci), vs(h)] = (o * z_ref[b, rs(ci), vs(h)]).astype(o_ref.dtype)


def _cast_specs(arrays, n_steps):
    specs, shapes = [], []
    for a in arrays:
        span = next(d for d in (1, 2, 4, 8)
                    if (a.shape[0] * d) % n_steps == 0 and (a.shape[0] * d // n_steps) % (2 * SUBLANES) == 0)
        specs.append(pl.BlockSpec((a.shape[0] * span // n_steps, a.shape[1]), lambda j, span=span: (j // span, 0)))
        shapes.append(jax.ShapeDtypeStruct(a.shape, BF16))
    return specs, shapes


def _gdn_chunk(q, k, v, zs, gb, norm_w, batch, seq_len, to_bf16):
    rows_per_step = GDN_CHUNK * GDN_CHUNKS_PER_STEP
    n_steps = seq_len // rows_per_step
    seq = lambda a: a.reshape(batch, seq_len, a.shape[-1])
    rows = lambda width: pl.BlockSpec((batch, rows_per_step, width), lambda j: (0, j, 0))
    cast_specs, cast_shapes = _cast_specs(to_bf16, n_steps)
    out, *casted = pl.pallas_call(
        functools.partial(_gdn_chunk_kernel, nb=batch, nck=GDN_CHUNKS_PER_STEP, n_cast=len(to_bf16)),
        grid=(n_steps,),
        in_specs=[rows(GDN_KDIM), rows(GDN_KDIM), rows(GDN_VDIM), rows(GDN_VDIM), rows(LANES),
                  pl.BlockSpec((1, GDN_HEAD), lambda j: (0, 0))] + cast_specs,
        out_specs=[rows(GDN_VDIM)] + cast_specs,
        out_shape=[jax.ShapeDtypeStruct((batch, seq_len, GDN_VDIM), BF16)] + cast_shapes,
        scratch_shapes=[pltpu.VMEM((batch * GDN_V_HEADS, GDN_HEAD, GDN_HEAD), F32)],
        compiler_params=_params("arbitrary"),
        name="gdn_chunk",
    )(seq(q), seq(k), seq(v), seq(zs), seq(gb), norm_w, *to_bf16)
    return out.reshape(batch * seq_len, GDN_VDIM), casted


def _proj_res_ln(a_ref, w_ref, r_ref, g_ref, b_ref):
    return _layer_norm(ALPHA * r_ref[...] + _dot(a_ref[...], w_ref[...]), g_ref[...], b_ref[...])


def _outproj_ffn_kernel(a_ref, wo_ref, r_ref, g0_ref, b0_ref, wg_ref, wu_ref, wd_ref, g1_ref, b1_ref, o_ref):
    x1 = _proj_res_ln(a_ref, wo_ref, r_ref, g0_ref, b0_ref)
    xb = x1.astype(BF16)
    h = (_silu(_dot(xb, wg_ref[...])) * _dot(xb, wu_ref[...])).astype(BF16)
    o_ref[...] = _layer_norm(ALPHA * x1 + _dot(h, wd_ref[...]), g1_ref[...], b1_ref[...])


def _outproj_ffn(a, w_out, res, g0, b0, wg, wu, wd, g1, b1):
    n, kdim = a.shape
    tm = TM_FFN
    resident = lambda w: pl.BlockSpec(w.shape, lambda i: (0, 0), pipeline_mode=pl.Buffered(1))
    rows = lambda width: pl.BlockSpec((tm, width), lambda i: (i, 0))
    vec = pl.BlockSpec((1, D_MODEL), lambda i: (0, 0))
    return pl.pallas_call(
        _outproj_ffn_kernel,
        grid=(n // tm,),
        in_specs=[rows(kdim), resident(w_out), rows(D_MODEL), vec, vec,
                  resident(wg), resident(wu), resident(wd), vec, vec],
        out_specs=rows(D_MODEL),
        out_shape=jax.ShapeDtypeStruct((n, D_MODEL), F32),
        compiler_params=_params("parallel"),
        name="gdn_outproj_ffn",
    )(a, w_out, res, g0, b0, wg, wu, wd, g1, b1)


def _band_tables():
    qi = np.arange(SWA_BLOCK)[:, None]
    kj = np.arange(2 * SWA_BLOCK)[None, :]
    dist = qi + SWA_BLOCK - kj
    d = np.maximum(dist, 0)
    max_exact = REL_BUCKETS // 2
    df = np.maximum(d, 1).astype(np.float32)
    large = max_exact + (np.log(df / np.float32(max_exact)) / np.float32(math.log(REL_MAX_DIST / max_exact))
                         * np.float32(REL_BUCKETS - max_exact)).astype(np.int32)
    large = np.minimum(large, REL_BUCKETS - 1)
    bucket = np.where(d < max_exact, d, large).astype(np.int32)
    band_ok = (dist >= 0) & (dist < SWA_WINDOW)
    valid = np.stack([band_ok & (kj >= SWA_BLOCK), band_ok]).astype(np.int32)
    return np.ascontiguousarray(bucket.T), np.ascontiguousarray(valid.transpose(0, 2, 1))


def _bias_kernel(relb_ref, bucket_ref, valid_ref, o_ref):
    bucket = bucket_ref[...]
    in_bucket = [bucket == b for b in range(REL_BUCKETS)]
    for h in range(SWA_Q_HEADS):
        acc = jnp.zeros(bucket.shape, F32)
        for b in range(REL_BUCKETS):
            acc = jnp.where(in_bucket[b], relb_ref[b, h], acc)
        for t in range(2):
            o_ref[t, h] = jnp.where(valid_ref[t] != 0, acc, NEG_BIG)


def _bias_table(rel_bias):
    bucket, valid = _band_tables()
    shape = (2 * SWA_BLOCK, SWA_BLOCK)
    return pl.pallas_call(
        _bias_kernel,
        grid=(1,),
        in_specs=[pl.BlockSpec(memory_space=pltpu.SMEM), pl.BlockSpec(shape, lambda i: (0, 0)),
                  pl.BlockSpec((2,) + shape, lambda i: (0, 0, 0))],
        out_specs=pl.BlockSpec((2, SWA_Q_HEADS) + shape, lambda i: (0, 0, 0, 0)),
        out_shape=jax.ShapeDtypeStruct((2, SWA_Q_HEADS) + shape, F32),
        compiler_params=_params("arbitrary"),
        name="swa_bias_table",
    )(rel_bias, jnp.asarray(bucket), jnp.asarray(valid))


def _band_attention(q_pair, kv, bias_of, sink_ref, put_pair):
    dh = SWA_HEAD_DIM
    nk = 2 * SWA_BLOCK
    k_all = kv[:, :SWA_KVDIM]
    lane = lax.broadcasted_iota(jnp.int32, k_all.shape, 1)
    zero = jnp.zeros_like(k_all)
    for kh in range(SWA_KV_HEADS):
        k_own = jnp.where((lane >= kh * dh) & (lane < (kh + 1) * dh), k_all, zero)
        k_other = pltpu.roll(k_own.astype(F32), dh, axis=1).astype(BF16)
        k_even, k_odd = (k_own, k_other) if kh == 0 else (k_other, k_own)
        k2 = jnp.concatenate([k_even, k_odd], axis=0)
        v_t = kv[:, SWA_KVDIM + kh * dh:SWA_KVDIM + (kh + 1) * dh].T
        pairs = [kh * (SWA_GROUP // 2) + p for p in range(SWA_GROUP // 2)]
        heads = [2 * p + r for p in pairs for r in range(2)]
        st = [_dot_nt(k2, q_pair(p)) for p in pairs]
        scores = [st[i // 2][(i % 2) * nk:(i % 2 + 1) * nk] + bias_of(h) for i, h in enumerate(heads)]
        sinks = [sink_ref[0:1, h:h + 1] for h in heads]
        maxes = [jnp.maximum(jnp.max(s, axis=0, keepdims=True), sk) for s, sk in zip(scores, sinks)]
        expo = [jnp.exp(s - m) for s, m in zip(scores, maxes)]
        sums = [jnp.sum(p, axis=0, keepdims=True) for p in expo]
        inv_den = [1.0 / (t + jnp.exp(sk - m)) for t, sk, m in zip(sums, sinks, maxes)]
        for i, p in enumerate(pairs):
            o_t = _dot(v_t, jnp.concatenate(expo[2 * i:2 * i + 2], axis=1).astype(BF16))
            z = jnp.concatenate([o_t[:, :SWA_BLOCK] * inv_den[2 * i], o_t[:, SWA_BLOCK:] * inv_den[2 * i + 1]],
                                axis=0)
            put_pair(p, z.T)


def _swa_layer_kernel(x_ref, win_ref, bin_ref, bias_ref, sink_ref, wout_ref, g_ref, b_ref, o_ref,
                      q_ref, kv_ref, att_ref, *, nblk):
    j = pl.program_id(1)
    blk = SWA_BLOCK
    tm = nblk * blk
    dh2 = 2 * SWA_HEAD_DIM

    p = _dot(x_ref[...].astype(BF16), win_ref[...]) + bin_ref[...]
    q_ref[...] = (p[:, :SWA_QDIM] * (SWA_HEAD_DIM ** -0.5)).astype(BF16)

    @pl.when(j == 0)
    def _():
        kv_ref[0:blk, :] = jnp.zeros((blk, 2 * SWA_KVDIM), BF16)

    @pl.when(j != 0)
    def _():
        kv_ref[0:blk, :] = kv_ref[tm:tm + blk, :]

    kv_ref[blk:, :] = p[:, SWA_QDIM:].astype(BF16)

    for i in range(nblk):
        rows = slice(i * blk, (i + 1) * blk)
        table = jnp.where(j == 0, 0, 1) if i == 0 else 1

        def put_pair(pair, o, rows=rows):
            att_ref[rows, pair * dh2:(pair + 1) * dh2] = o.astype(att_ref.dtype)

        _band_attention(lambda pair, rows=rows: q_ref[rows, pair * dh2:(pair + 1) * dh2],
                        kv_ref[i * blk:(i + 2) * blk, :],
                        lambda h, table=table: bias_ref[table, h], sink_ref, put_pair)

    o_ref[...] = _proj_res_ln(att_ref, wout_ref, x_ref, g_ref, b_ref)


def _swa_layer(x, w_in, b_in, bias, sinks, w_out, g, b, batch, seq_len):
    n = x.shape[0]
    nblk = SWA_BLOCKS_PER_STEP
    tm = nblk * SWA_BLOCK
    steps = seq_len // tm
    resident = lambda a: pl.BlockSpec(a.shape, lambda s, j: (0,) * a.ndim, pipeline_mode=pl.Buffered(1))
    rows = pl.BlockSpec((tm, D_MODEL), lambda s, j: (s * steps + j, 0))
    return pl.pallas_call(
        functools.partial(_swa_layer_kernel, nblk=nblk),
        grid=(batch, steps),
        in_specs=[rows, resident(w_in), resident(b_in), resident(bias), resident(sinks), resident(w_out),
                  resident(g), resident(b)],
        out_specs=rows,
        out_shape=jax.ShapeDtypeStruct((n, D_MODEL), F32),
        scratch_shapes=[pltpu.VMEM((tm, SWA_QDIM), BF16), pltpu.VMEM((tm + SWA_BLOCK, 2 * SWA_KVDIM), BF16),
                        pltpu.VMEM((tm, SWA_QDIM), BF16)],
        compiler_params=_params("parallel", "arbitrary"),
        name="swa_layer",
    )(x, w_in, b_in, bias, sinks, w_out, g, b)


def _router_kernel(x_ref, wr_ref, route_ref, route_t_ref, cnt_ref, run_ref, *, tm):
    @pl.when(pl.program_id(0) == 0)
    def _():
        run_ref[...] = jnp.zeros(run_ref.shape, F32)

    logits = _dot3(x_ref[...], wr_ref[...])
    lane = lax.broadcasted_iota(jnp.int32, logits.shape, 1)
    lane_f = lane.astype(F32)
    lg = jnp.where(lane < N_EXPERTS, logits, NEG_BIG)
    m1 = jnp.max(lg, axis=-1, keepdims=True)
    i1 = jnp.min(jnp.where(lg == m1, lane_f, float(LANES)), axis=-1, keepdims=True)
    oh1 = lane_f == i1
    lg2 = jnp.where(oh1, NEG_BIG, lg)
    m2 = jnp.max(lg2, axis=-1, keepdims=True)
    i2 = jnp.min(jnp.where(lg2 == m2, lane_f, float(LANES)), axis=-1, keepdims=True)
    oh2 = lane_f == i2
    e = jnp.exp(m2 - m1)
    w0 = 1.0 / (1.0 + e)
    w1 = e * w0

    cnt = jnp.where(oh1, 1.0, 0.0) + jnp.where(oh2, 1.0, 0.0)
    r = lax.broadcasted_iota(jnp.int32, (tm, tm), 0)
    c = lax.broadcasted_iota(jnp.int32, (tm, tm), 1)
    before = jnp.where(r > c, 1.0, 0.0).astype(BF16)
    excl = _dot(before, cnt.astype(BF16)) + run_ref[...]
    rank0 = jnp.sum(jnp.where(oh1, excl, 0.0), axis=-1, keepdims=True)
    rank1 = jnp.sum(jnp.where(oh2, excl, 0.0), axis=-1, keepdims=True)
    run = run_ref[...] + jnp.sum(cnt, axis=0, keepdims=True)
    run_ref[...] = run
    cnt_ref[...] = run

    vals = (i1, i2, rank0, rank1, w0, w1)
    out = jnp.zeros(logits.shape, F32)
    for idx, val in enumerate(vals):
        out = jnp.where(lane == idx, val, out)
    route_ref[...] = out
    route_t_ref[...] = out.T[:SUBLANES, :]


def _router(x, w_router_padded):
    n = x.shape[0]
    tm = TM_ROUTER
    return pl.pallas_call(
        functools.partial(_router_kernel, tm=tm),
        grid=(n // tm,),
        in_specs=[pl.BlockSpec((tm, D_MODEL), lambda i: (i, 0)), pl.BlockSpec((D_MODEL, LANES), lambda i: (0, 0))],
        out_specs=[pl.BlockSpec((tm, LANES), lambda i: (i, 0)), pl.BlockSpec((SUBLANES, tm), lambda i: (0, i)),
                   pl.BlockSpec((1, LANES), lambda i: (0, 0))],
        out_shape=[jax.ShapeDtypeStruct((n, LANES), F32), jax.ShapeDtypeStruct((SUBLANES, n), F32),
                   jax.ShapeDtypeStruct((1, LANES), F32)],
        scratch_shapes=[pltpu.VMEM((1, LANES), F32)],
        compiler_params=_params("arbitrary"),
        name="moe_router",
    )(x, w_router_padded)


def _dispatch_kernel(pos0_ref, pos1_ref, meta_ref, x_ref, wsrc_ref, xs_hbm, wdst_ref, zero_ref, sem, pad_sem, *, td):
    i = pl.program_id(0)
    wdst_ref[...] = wsrc_ref[...].astype(BF16)

    def row_copy(t, dst):
        return pltpu.make_async_copy(x_ref.at[pl.ds(t, 1)], xs_hbm.at[pl.ds(dst, 1)], sem)

    def start(t, carry):
        row_copy(t, pos0_ref[0, 0, t]).start()
        row_copy(t, pos1_ref[0, 0, t]).start(priority=1)
        return carry

    lax.fori_loop(0, td, start, 0, unroll=DMA_UNROLL)

    @pl.when(i == pl.num_programs(0) - 1)
    def _():
        zero_ref[...] = jnp.zeros(zero_ref.shape, F32)

        zr = zero_ref.shape[0]

        def row_zero(dst):
            return pltpu.make_async_copy(zero_ref.at[pl.ds(0, 1)], xs_hbm.at[pl.ds(dst, 1)], pad_sem)

        def block_zero(blk):
            return pltpu.make_async_copy(zero_ref, xs_hbm.at[pl.ds(pl.multiple_of(blk * zr, zr), zr)], pad_sem)

        def zero_range(copy, lo, hi):
            lax.fori_loop(lo, hi, lambda r, carry: (copy(r).start(), carry)[1], 0)
            lax.fori_loop(lo, hi, lambda r, carry: (copy(0).wait(), carry)[1], 0)

        for e in range(N_EXPERTS):
            off, cnt, cnt_up, padded = (meta_ref[r, e] for r in range(4))
            zero_range(lambda r, off=off: row_zero(off + r), cnt, cnt_up)
            zero_range(block_zero, (off + cnt_up) // zr, (off + padded) // zr)

        used = meta_ref[0, N_EXPERTS - 1] + meta_ref[3, N_EXPERTS - 1]
        zero_range(block_zero, used // zr, xs_hbm.shape[0] // zr)

    for _ in range(2):
        pltpu.make_async_copy(x_ref, xs_hbm.at[pl.ds(0, td)], sem).wait()


def _dispatch(x, pos, meta, n_rows, to_bf16):
    n = x.shape[0]
    td = TD_DISPATCH
    pos_spec = pl.BlockSpec((1, 1, td), lambda i: (i, 0, 0), memory_space=pltpu.SMEM)
    (cast_spec,), (cast_shape,) = _cast_specs([to_bf16], n // td)
    return pl.pallas_call(
        functools.partial(_dispatch_kernel, td=td),
        grid=(n // td,),
        in_specs=[pos_spec, pos_spec, pl.BlockSpec(memory_space=pltpu.SMEM),
                  pl.BlockSpec((td, D_MODEL), lambda i: (i, 0)), cast_spec],
        out_specs=[pl.BlockSpec(memory_space=pl.ANY), cast_spec],
        out_shape=[jax.ShapeDtypeStruct((n_rows, D_MODEL), F32), cast_shape],
        scratch_shapes=[pltpu.VMEM((ZERO_ROWS, D_MODEL), F32), pltpu.SemaphoreType.DMA, pltpu.SemaphoreType.DMA],
        compiler_params=_params("arbitrary"),
        name="moe_dispatch",
    )(pos[0].reshape(n // td, 1, td), pos[1].reshape(n // td, 1, td), meta, x, to_bf16)


def _moe_kernel(we_ref, wb_ref, nv_ref, cnt_ref, *refs, n_sub, tm):
    xs_refs = refs[:n_sub]
    wg_ref, wu_ref, wd_ref, ys_hbm, acc_ref, xb_ref, sem = refs[n_sub:]
    j = pl.program_id(0)
    f = pl.program_id(1)
    last_f = pl.num_programs(1) - 1
    n_windows, n_tiles = cnt_ref[0], cnt_ref[1]

    def tile_write(slot, block):
        return pltpu.make_async_copy(acc_ref.at[slot], ys_hbm.at[pl.ds(pl.multiple_of(block * tm, tm), tm)],
                                     sem.at[slot])

    @pl.when(j < n_windows)
    def _():
        prev_tiles = jnp.where(j > 0, nv_ref[jnp.maximum(j - 1, 0)], 0)

        for s in range(n_sub):
            @pl.when(s < nv_ref[j])
            def _(s=s):
                @pl.when(f == 0)
                def _():
                    xb_ref[s] = xs_refs[s][...].astype(BF16)

                xb = xb_ref[s]
                h = (_silu(_dot(xb, wg_ref[0])) * _dot(xb, wu_ref[0])).astype(BF16)
                part = _dot(h, wd_ref[0])

                @pl.when(f == 0)
                def _():
                    @pl.when(s < prev_tiles)
                    def _():
                        tile_write(s, 0).wait()

                    acc_ref[s] = part

                @pl.when(f != 0)
                def _():
                    acc_ref[s] += part

                @pl.when(f == last_f)
                def _():
                    tile_write(s, wb_ref[j] + s).start()

            @pl.when((f == 0) & (s >= nv_ref[j]) & (s < prev_tiles))
            def _(s=s):
                tile_write(s, 0).wait()

        @pl.when((f == last_f) & (j == n_windows - 1))
        def _():
            for s in range(n_sub):
                @pl.when(s < nv_ref[j])
                def _(s=s):
                    tile_write(s, 0).wait()

            acc_ref[0] = jnp.zeros(acc_ref.shape[1:], F32)
            total_tiles = ys_hbm.shape[0] // tm
            lax.fori_loop(n_tiles, total_tiles, lambda t, c: (tile_write(0, t).start(), c)[1], 0)
            lax.fori_loop(n_tiles, total_tiles, lambda t, c: (tile_write(0, 0).wait(), c)[1], 0)


def _moe_experts(xs, wg, wu, wd, win_expert, win_block, win_tiles, counts):
    n_rows = xs.shape[0]
    tm, tf, n_sub = TM_MOE, TF_MOE, MOE_TILES_PER_WINDOW
    max_windows = win_expert.shape[0]

    def xs_spec(s):
        return pl.BlockSpec((tm, D_MODEL), lambda j, f, we, wb, nv, cnt: (wb[j] + jnp.minimum(s, nv[j] - 1), 0))

    grid_spec = pltpu.PrefetchScalarGridSpec(
        num_scalar_prefetch=4,
        grid=(max_windows, EXPERT_DIM // tf),
        in_specs=[xs_spec(s) for s in range(n_sub)] + [
            pl.BlockSpec((1, D_MODEL, tf), lambda j, f, we, wb, nv, cnt: (we[j], 0, f)),
            pl.BlockSpec((1, D_MODEL, tf), lambda j, f, we, wb, nv, cnt: (we[j], 0, f)),
            pl.BlockSpec((1, tf, D_MODEL), lambda j, f, we, wb, nv, cnt: (we[j], f, 0))],
        out_specs=pl.BlockSpec(memory_space=pl.ANY),
        scratch_shapes=[pltpu.VMEM((n_sub, tm, D_MODEL), F32), pltpu.VMEM((n_sub, tm, D_MODEL), BF16),
                        pltpu.SemaphoreType.DMA((n_sub,))],
    )
    return pl.pallas_call(
        functools.partial(_moe_kernel, n_sub=n_sub, tm=tm),
        grid_spec=grid_spec,
        out_shape=jax.ShapeDtypeStruct((n_rows, D_MODEL), F32),
        compiler_params=_params("arbitrary", "arbitrary"),
        name="moe_experts",
    )(win_expert, win_block, win_tiles, counts, *([xs] * n_sub), wg, wu, wd)


def _combine_kernel(pos0_ref, pos1_ref, pos0n_ref, pos1n_ref, x_ref, route_ref, ys_hbm, g_ref, b_ref, o_ref,
                    y_ref, sem, *, tc):
    i = pl.program_id(0)
    slot = i % 2

    def issue(p_refs, s):
        def start(t, carry):
            for kk in range(2):
                pltpu.make_async_copy(ys_hbm.at[pl.ds(p_refs[kk][0, 0, t], 1)],
                                      y_ref.at[s, kk, pl.ds(t, 1)], sem.at[s]).start(priority=kk)
            return carry

        lax.fori_loop(0, tc, start, 0, unroll=DMA_UNROLL)

    @pl.when(i == 0)
    def _():
        issue((pos0_ref, pos1_ref), 0)

    @pl.when(i + 1 < pl.num_programs(0))
    def _():
        issue((pos0n_ref, pos1n_ref), 1 - slot)

    for kk in range(2):
        pltpu.make_async_copy(ys_hbm.at[pl.ds(0, tc)], y_ref.at[slot, kk], sem.at[slot]).wait()

    route = route_ref[...]
    f = route[:, 4:5] * y_ref[slot, 0] + route[:, 5:6] * y_ref[slot, 1]
    o_ref[...] = _layer_norm(ALPHA * x_ref[...] + f, g_ref[...], b_ref[...])


def _combine_ln(x, route, pos, ys, g, b):
    n = x.shape[0]
    tc = TC_COMBINE
    nt = n // tc
    pos0, pos1 = pos[0].reshape(nt, 1, tc), pos[1].reshape(nt, 1, tc)
    cur = pl.BlockSpec((1, 1, tc), lambda i: (i, 0, 0), memory_space=pltpu.SMEM)
    nxt = pl.BlockSpec((1, 1, tc), lambda i: (jnp.minimum(i + 1, nt - 1), 0, 0), memory_space=pltpu.SMEM)
    return pl.pallas_call(
        functools.partial(_combine_kernel, tc=tc),
        grid=(nt,),
        in_specs=[cur, cur, nxt, nxt,
                  pl.BlockSpec((tc, D_MODEL), lambda i: (i, 0)),
                  pl.BlockSpec((tc, LANES), lambda i: (i, 0)),
                  pl.BlockSpec(memory_space=pl.ANY),
                  pl.BlockSpec((1, D_MODEL), lambda i: (0, 0)), pl.BlockSpec((1, D_MODEL), lambda i: (0, 0))],
        out_specs=pl.BlockSpec((tc, D_MODEL), lambda i: (i, 0)),
        out_shape=jax.ShapeDtypeStruct((n, D_MODEL), F32),
        scratch_shapes=[pltpu.VMEM((2, 2, tc, D_MODEL), F32), pltpu.SemaphoreType.DMA((2,))],
        compiler_params=_params("arbitrary"),
        name="moe_combine_ln",
    )(pos0, pos1, pos0, pos1, x, route, ys, g, b)


def _moe_layer(x, route, route_t, counts_f, wg, wu, wd_f32, g, b):
    n = x.shape[0]
    tm = TM_MOE

    counts = counts_f[0, :N_EXPERTS].astype(jnp.int32)
    padded = ((counts + tm - 1) // tm) * tm
    offs = jnp.cumsum(padded) - padded
    eids = jnp.arange(N_EXPERTS, dtype=jnp.int32)
    e01 = route_t[0:2].astype(jnp.int32)
    rank01 = route_t[2:4].astype(jnp.int32)
    base = jnp.sum(jnp.where(e01[:, None, :] == eids[None, :, None], offs[None, :, None], 0), axis=1)
    pos = base + rank01
    counts_up = jnp.minimum(padded, ((counts + ZERO_ROWS - 1) // ZERO_ROWS) * ZERO_ROWS)
    meta = jnp.stack([offs, counts, counts_up, padded]).astype(jnp.int32)

    n_rows = 2 * n + N_EXPERTS * tm
    nsub = MOE_TILES_PER_WINDOW
    tiles = padded // tm
    wins = (tiles + nsub - 1) // nsub
    win_end = jnp.cumsum(wins)
    n_windows = win_end[-1]
    max_windows = (n_rows // tm + nsub - 1) // nsub + N_EXPERTS
    wj = jnp.minimum(jnp.arange(max_windows, dtype=jnp.int32), n_windows - 1)
    win_expert = jnp.minimum(jnp.sum((wj[:, None] >= win_end[None, :]).astype(jnp.int32), axis=-1), N_EXPERTS - 1)
    pick = lambda v: jnp.sum(jnp.where(win_expert[:, None] == eids[None, :], v[None, :], 0), axis=-1)
    local = wj - pick(win_end - wins)
    win_block = pick(offs // tm) + nsub * local
    win_tiles = jnp.minimum(nsub, pick(tiles) - nsub * local)
    counts_nt = jnp.stack([n_windows, jnp.sum(tiles)])

    xs, wd = _dispatch(x, pos, meta, n_rows, wd_f32.reshape(-1, D_MODEL))
    wd = wd.reshape(wd_f32.shape)
    i32 = lambda a: a.astype(jnp.int32)
    ys = _moe_experts(xs, wg, wu, wd, i32(win_expert), i32(win_block), i32(win_tiles), i32(counts_nt))
    return _combine_ln(x, route, pos, ys, g, b)


def kernel(x, a_w_in, a_conv_w, a_a_log, a_dt_bias, a_norm_w, a_w_out, b_w_in, b_b_in, b_sinks, b_w_out, rel_bias,
           ffn_w_gate, ffn_w_up, ffn_w_down, moe_router, moe_w_gate, moe_w_up, moe_w_down, ln_g, ln_b):
    batch, seq_len, _ = x.shape
    n = batch * seq_len
    x0 = x.reshape(n, D_MODEL)
    ln_g = ln_g.reshape(DEPTH, 2, 1, D_MODEL)
    ln_b = ln_b.reshape(DEPTH, 2, 1, D_MODEL)

    pad_gate = lambda p: jnp.pad(p.reshape(1, GDN_V_HEADS), ((0, 0), (GDN_V_HEADS, LANES - 2 * GDN_V_HEADS)))
    gate_params = jnp.concatenate([pad_gate(a_a_log[0]), pad_gate(a_dt_bias[0])], axis=0)
    q, k, v, zs, gb = _gdn_inproj(x0, a_w_in[0], a_conv_w[0], gate_params, seq_len)
    o, casted = _gdn_chunk(q, k, v, zs, gb, a_norm_w[0].reshape(1, GDN_HEAD), batch, seq_len,
                           [moe_w_gate[0].reshape(-1, EXPERT_DIM), moe_w_up[0].reshape(-1, EXPERT_DIM),
                            ffn_w_gate[0], ffn_w_up[0], ffn_w_down[0],
                            a_w_out[0], b_w_in[0], b_w_out[0]])
    moe_wg, moe_wu, ffn_wg, ffn_wu, ffn_wd, w_out_a, w_in_b, w_out_b = casted
    x2 = _outproj_ffn(o, w_out_a, x0, ln_g[0, 0], ln_b[0, 0],
                      ffn_wg, ffn_wu, ffn_wd, ln_g[0, 1], ln_b[0, 1])

    bias = _bias_table(rel_bias)
    sinks = jnp.pad(b_sinks[0].reshape(1, SWA_Q_HEADS), ((0, 0), (0, LANES - SWA_Q_HEADS)))
    x3 = _swa_layer(x2, w_in_b, b_b_in[0].reshape(1, -1), bias, sinks, w_out_b, ln_g[1, 0], ln_b[1, 0],
                    batch, seq_len)
    w_router = jnp.pad(moe_router[0], ((0, 0), (0, LANES - N_EXPERTS)))
    route, route_t, counts = _router(x3, w_router)
    expert_shape = (N_EXPERTS, D_MODEL, EXPERT_DIM)
    x4 = _moe_layer(x3, route, route_t, counts, moe_wg.reshape(expert_shape), moe_wu.reshape(expert_shape),
                    moe_w_down[0], ln_g[1, 1], ln_b[1, 1])
    return x4.reshape(batch, seq_len, D_MODEL)
```

```python
import functools
import math

import numpy as np
import jax
import jax.numpy as jnp
from jax import lax
from jax.experimental import pallas as pl
from jax.experimental.pallas import tpu as pltpu

F32 = jnp.float32
BF16 = jnp.bfloat16

D_MODEL = 1024
DEPTH = 2
ALPHA = (2.0 * DEPTH) ** 0.25
LN_EPS = 1e-5

GDN_K_HEADS = 4
GDN_V_HEADS = 8
GDN_HEAD = 128
GDN_KDIM = GDN_K_HEADS * GDN_HEAD
GDN_VDIM = GDN_V_HEADS * GDN_HEAD
GDN_CONV = 4
GDN_CHUNK = 64
GDN_QKV = 2 * GDN_KDIM + GDN_VDIM
GDN_EPS = 1e-6

SWA_Q_HEADS = 16
SWA_KV_HEADS = 2
SWA_GROUP = SWA_Q_HEADS // SWA_KV_HEADS
SWA_HEAD_DIM = 64
SWA_WINDOW = 128
SWA_BLOCK = 128
SWA_QDIM = SWA_Q_HEADS * SWA_HEAD_DIM
SWA_KVDIM = SWA_KV_HEADS * SWA_HEAD_DIM
REL_BUCKETS = 32
REL_MAX_DIST = 128

FFN_DIM = 2816
N_EXPERTS = 8
EXPERT_DIM = 3584

LANES = 128
SUBLANES = 8
NEG_BIG = -1e30
VMEM_LIMIT = 56 * 1024 * 1024

TM_GDN_IN = 512
TM_FFN = 512
SWA_BLOCKS_PER_STEP = 4
TM_ROUTER = 1024
TM_MOE = 512
TF_MOE = 1792
MOE_TILES_PER_WINDOW = 3
TD_DISPATCH = 1024
TC_COMBINE = 512
ZERO_ROWS = 64
DMA_UNROLL = 8


def _params(*sem):
    return pltpu.CompilerParams(dimension_semantics=sem, vmem_limit_bytes=VMEM_LIMIT)


def _dot(a, b):
    return jnp.dot(a, b, preferred_element_type=F32)


def _dot_nt(a, b):
    return lax.dot_general(a, b, (((1,), (1,)), ((), ())), preferred_element_type=F32)


def _dot_tn(a, b):
    return lax.dot_general(a, b, (((0,), (0,)), ((), ())), preferred_element_type=F32)


def _split(x):
    hi = x.astype(BF16)
    lo = (x - hi.astype(F32)).astype(BF16)
    return hi, lo


def _dot3(a, b):
    ah, al = _split(a)
    bh, bl = _split(b)
    return _dot(ah, bh) + (_dot(ah, bl) + _dot(al, bh))


def _silu(x):
    return x * jax.nn.sigmoid(x)


def _layer_norm(y, g, b):
    mu = jnp.mean(y, axis=-1, keepdims=True)
    yc = y - mu
    var = jnp.mean(yc * yc, axis=-1, keepdims=True)
    return yc * lax.rsqrt(var + LN_EPS) * g + b


def _gdn_inproj_kernel(x_ref, wf_ref, convw_ref, gp_ref,
                       q_ref, k_ref, v_ref, z_ref, gb_ref, ext_ref, w_ref, *, tm, tiles_per_seq):
    i = pl.program_id(0)
    xb = x_ref[...].astype(BF16)

    @pl.when(i == 0)
    def _():
        n_in = wf_ref.shape[-1]
        w_ref[:, :n_in] = wf_ref[0].astype(BF16)
        w_ref[:, n_in:] = jnp.zeros((w_ref.shape[0], w_ref.shape[1] - n_in), BF16)

    @pl.when(i % tiles_per_seq == 0)
    def _():
        ext_ref[0:SUBLANES, :] = jnp.zeros((SUBLANES, GDN_QKV), F32)

    @pl.when(i % tiles_per_seq != 0)
    def _():
        ext_ref[0:SUBLANES, :] = ext_ref[tm:tm + SUBLANES, :]

    ext_ref[SUBLANES:, :] = _dot(xb, w_ref[:, :GDN_QKV])
    z_ref[...] = _dot(xb, w_ref[:, GDN_QKV:GDN_QKV + GDN_VDIM])

    n_chunks = GDN_QKV // LANES
    for c in range(n_chunks):
        cs = slice(c * LANES, (c + 1) * LANES)
        acc = convw_ref[GDN_CONV - 1:GDN_CONV, cs] * ext_ref[SUBLANES:SUBLANES + tm, cs]
        for j in range(GDN_CONV - 1):
            off = SUBLANES - (GDN_CONV - 1) + j
            acc = acc + convw_ref[j:j + 1, cs] * ext_ref[off:off + tm, cs]
        y = _silu(acc)
        if c < 2 * GDN_K_HEADS:
            y = y * lax.rsqrt(jnp.sum(y * y, axis=-1, keepdims=True) + GDN_EPS)
            if c < GDN_K_HEADS:
                q_ref[:, cs] = y * (GDN_HEAD ** -0.5)
            else:
                k_ref[:, (c - GDN_K_HEADS) * LANES:(c - GDN_K_HEADS + 1) * LANES] = y
        else:
            cv = c - 2 * GDN_K_HEADS
            v_ref[:, cv * LANES:(cv + 1) * LANES] = y

    ba = _dot(xb, w_ref[:, GDN_QKV + GDN_VDIM:])
    lane = lax.broadcasted_iota(jnp.int32, ba.shape, 1)
    sp = ba + gp_ref[1:2, :]
    softplus = jnp.maximum(sp, 0.0) + jnp.log(1.0 + jnp.exp(-jnp.abs(sp)))
    g = -jnp.exp(gp_ref[0:1, :]) * softplus
    gb_ref[...] = jnp.where(lane < GDN_V_HEADS, jax.nn.sigmoid(ba), g)


def _gdn_inproj(x2d, w_in, conv_w, gate_params, seq_len):
    n = x2d.shape[0]
    tm = TM_GDN_IN
    kern = functools.partial(_gdn_inproj_kernel, tm=tm, tiles_per_seq=seq_len // tm)
    full = lambda shape: pl.BlockSpec(shape, lambda i: (0,) * len(shape))
    rows = lambda width: pl.BlockSpec((tm, width), lambda i: (i, 0))
    return pl.pallas_call(
        kern,
        grid=(n // tm,),
        in_specs=[rows(D_MODEL), pl.BlockSpec(w_in.shape, lambda i: (0, 0, 0), pipeline_mode=pl.Buffered(1)),
                  full(conv_w.shape), full(gate_params.shape)],
        out_specs=[rows(GDN_KDIM), rows(GDN_KDIM), rows(GDN_VDIM), rows(GDN_VDIM), rows(LANES)],
        out_shape=[jax.ShapeDtypeStruct((n, GDN_KDIM), F32), jax.ShapeDtypeStruct((n, GDN_KDIM), F32),
                   jax.ShapeDtypeStruct((n, GDN_VDIM), F32), jax.ShapeDtypeStruct((n, GDN_VDIM), F32),
                   jax.ShapeDtypeStruct((n, LANES), F32)],
        scratch_shapes=[pltpu.VMEM((tm + SUBLANES, GDN_QKV), F32),
                        pltpu.VMEM((D_MODEL, GDN_QKV + GDN_VDIM + LANES), BF16)],
        compiler_params=_params("arbitrary"),
        name="gdn_inproj",
    )(x2d, w_in, conv_w, gate_params)


GDN_CHUNKS_PER_STEP = 2
NEUMANN_BLOCK = 4


def _bf16_all(xs):
    return [x.astype(BF16) for x in xs]


def _dot_all(a_list, b_list):
    return [_dot(a, b) for a, b in zip(a_list, b_list)]


def _unit_lower_inverse_all(a_list, row, col):
    shift = int(math.log2(NEUMANN_BLOCK))
    eye = jnp.where(row == col, 1.0, 0.0).astype(F32)
    on_diag_block = (row >> shift) == (col >> shift)
    d = [jnp.where(on_diag_block, a, 0.0) for a in a_list]
    d_b = _bf16_all(d)
    x = [eye - di for di in d]
    d2_b = _bf16_all(_dot_all(d_b, d_b))
    x = [xi + t for xi, t in zip(x, _dot_all(_bf16_all(x), d2_b))]
    size = NEUMANN_BLOCK
    while size < a_list[0].shape[0]:
        shift = int(math.log2(size))
        rbl, cbl = row >> shift, col >> shift
        below = ((rbl & 1) == 1) & (cbl == rbl - 1)
        l_b = _bf16_all([jnp.where(below, a, 0.0) for a in a_list])
        x_b = _bf16_all(x)
        xl_b = _bf16_all(_dot_all(x_b, l_b))
        x = [xi - t for xi, t in zip(x, _dot_all(xl_b, x_b))]
        size *= 2
    return x


def _gdn_chunk_kernel(q_ref, k_ref, v_ref, z_ref, gb_ref, nw_ref, *rest, nb, nck, n_cast):
    cast_src, (o_ref, *cast_dst), s_ref = rest[:n_cast], rest[n_cast:2 * n_cast + 1], rest[-1]
    for src, dst in zip(cast_src, cast_dst):
        dst[...] = src[...].astype(BF16)
    c = GDN_CHUNK
    nh = GDN_V_HEADS

    @pl.when(pl.program_id(0) == 0)
    def _():
        s_ref[...] = jnp.zeros(s_ref.shape, F32)

    row = lax.broadcasted_iota(jnp.int32, (c, c), 0)
    col = lax.broadcasted_iota(jnp.int32, (c, c), 1)
    causal = row >= col
    strict = row > col
    tril = jnp.where(causal, 1.0, 0.0).astype(BF16)
    rs = lambda ci: slice(ci * c, (ci + 1) * c)
    ks = lambda h: slice((h // (nh // GDN_K_HEADS)) * GDN_HEAD, (h // (nh // GDN_K_HEADS) + 1) * GDN_HEAD)
    vs = lambda h: slice(h * GDN_HEAD, (h + 1) * GDN_HEAD)

    gates = {}
    for ci in range(nck):
        for b in range(nb):
            gb = gb_ref[b, rs(ci), :]
            g_hi, g_lo = _split(gb)
            gc = _dot(tril, g_hi) + _dot(tril, g_lo)
            g_last = gc[c - 1:c, :]
            gates[ci, b] = dict(gb=gb, gc=gc, gc_t=gc.T, eg=jnp.exp(gc), e_last=jnp.exp(g_last),
                                e_rest=jnp.exp(g_last - gc))

    items = [(ci, b, h) for ci in range(nck) for b in range(nb) for h in range(nh)]

    a_kk, a_qk, k_beta = [], {}, {}
    for it in items:
        ci, b, h = it
        gt = gates[ci, b]
        k = k_ref[b, rs(ci), ks(h)]
        kb = k * gt["gb"][:, h:h + 1]
        kq = jnp.concatenate([kb, q_ref[b, rs(ci), ks(h)]], axis=0).astype(BF16)
        gram = _dot_nt(kq, k.astype(BF16))
        gl = nh + h
        decay = jnp.exp(jnp.where(causal, gt["gc"][:, gl:gl + 1] - gt["gc_t"][gl:gl + 1, :], NEG_BIG))
        a_kk.append(jnp.where(strict, gram[:c] * decay, 0.0))
        a_qk[it] = (gram[c:] * decay).astype(BF16)
        k_beta[it] = kb

    t_inv = dict(zip(items, _unit_lower_inverse_all(a_kk, row, col)))

    uw = {}
    for it in items:
        ci, b, h = it
        gt = gates[ci, b]
        gl = nh + h
        rhs = jnp.concatenate([v_ref[b, rs(ci), vs(h)] * gt["gb"][:, h:h + 1],
                               k_beta[it] * gt["eg"][:, gl:gl + 1]], axis=1)
        uw[it] = _dot(t_inv[it].astype(BF16), rhs.astype(BF16))

    for ci in range(nck):
        chunk_items = [(ci, b, h) for b in range(nb) for h in range(nh)]

        ws_qs, states = {}, {}
        for it in chunk_items:
            _, b, h = it
            gl = nh + h
            s = s_ref[b * nh + h]
            wq = jnp.concatenate([uw[it][:, GDN_HEAD:],
                                  q_ref[b, rs(ci), ks(h)] * gates[ci, b]["eg"][:, gl:gl + 1]], axis=0)
            ws_qs[it] = _dot(wq.astype(BF16), s.astype(BF16))
            states[it] = s

        for it in chunk_items:
            _, b, h = it
            gt = gates[ci, b]
            gl = nh + h
            v_new = (uw[it][:, :GDN_HEAD] - ws_qs[it][:c]).astype(BF16)
            o = ws_qs[it][c:] + _dot(a_qk[it], v_new)
            k_dec = (k_ref[b, rs(ci), ks(h)] * gt["e_rest"][:, gl:gl + 1]).astype(BF16)
            s_ref[b * nh + h] = states[it] * gt["e_last"][:, gl:gl + 1] + _dot_tn(k_dec, v_new)
            o = o * lax.rsqrt(jnp.mean(o * o, axis=-1, keepdims=True) + GDN_EPS) * nw_ref[...]
            o_ref[b, rs(ci), vs(h)] = (o * _silu(z_ref[b, rs(ci), vs(h)])).astype(o_ref.dtype)


def _cast_specs(arrays, n_steps):
    specs, shapes = [], []
    for a in arrays:
        span = next(d for d in (1, 2, 4, 8)
                    if (a.shape[0] * d) % n_steps == 0 and (a.shape[0] * d // n_steps) % (2 * SUBLANES) == 0)
        specs.append(pl.BlockSpec((a.shape[0] * span // n_steps, a.shape[1]), lambda j, span=span: (j // span, 0)))
        shapes.append(jax.ShapeDtypeStruct(a.shape, BF16))
    return specs, shapes


def _gdn_chunk(q, k, v, zs, gb, norm_w, batch, seq_len, to_bf16):
    rows_per_step = GDN_CHUNK * GDN_CHUNKS_PER_STEP
    n_steps = seq_len // rows_per_step
    seq = lambda a: a.reshape(batch, seq_len, a.shape[-1])
    rows = lambda width: pl.BlockSpec((batch, rows_per_step, width), lambda j: (0, j, 0))
    cast_specs, cast_shapes = _cast_specs(to_bf16, n_steps)
    out, *casted = pl.pallas_call(
        functools.partial(_gdn_chunk_kernel, nb=batch, nck=GDN_CHUNKS_PER_STEP, n_cast=len(to_bf16)),
        grid=(n_steps,),
        in_specs=[rows(GDN_KDIM), rows(GDN_KDIM), rows(GDN_VDIM), rows(GDN_VDIM), rows(LANES),
                  pl.BlockSpec((1, GDN_HEAD), lambda j: (0, 0))] + cast_specs,
        out_specs=[rows(GDN_VDIM)] + cast_specs,
        out_shape=[jax.ShapeDtypeStruct((batch, seq_len, GDN_VDIM), BF16)] + cast_shapes,
        scratch_shapes=[pltpu.VMEM((batch * GDN_V_HEADS, GDN_HEAD, GDN_HEAD), F32)],
        compiler_params=_params("arbitrary"),
        name="gdn_chunk",
    )(seq(q), seq(k), seq(v), seq(zs), seq(gb), norm_w, *to_bf16)
    return out.reshape(batch * seq_len, GDN_VDIM), casted


def _proj_res_ln(a_ref, w_ref, r_ref, g_ref, b_ref):
    return _layer_norm(ALPHA * r_ref[...] + _dot(a_ref[...], w_ref[...]), g_ref[...], b_ref[...])


def _outproj_ffn_kernel(a_ref, wo_ref, r_ref, g0_ref, b0_ref, wg_ref, wu_ref, wd_ref, g1_ref, b1_ref, o_ref):
    x1 = _proj_res_ln(a_ref, wo_ref, r_ref, g0_ref, b0_ref)
    xb = x1.astype(BF16)
    h = (_silu(_dot(xb, wg_ref[...])) * _dot(xb, wu_ref[...])).astype(BF16)
    o_ref[...] = _layer_norm(ALPHA * x1 + _dot(h, wd_ref[...]), g1_ref[...], b1_ref[...])


def _outproj_ffn(a, w_out, res, g0, b0, wg, wu, wd, g1, b1):
    n, kdim = a.shape
    tm = TM_FFN
    resident = lambda w: pl.BlockSpec(w.shape, lambda i: (0, 0), pipeline_mode=pl.Buffered(1))
    rows = lambda width: pl.BlockSpec((tm, width), lambda i: (i, 0))
    vec = pl.BlockSpec((1, D_MODEL), lambda i: (0, 0))
    return pl.pallas_call(
        _outproj_ffn_kernel,
        grid=(n // tm,),
        in_specs=[rows(kdim), resident(w_out), rows(D_MODEL), vec, vec,
                  resident(wg), resident(wu), resident(wd), vec, vec],
        out_specs=rows(D_MODEL),
        out_shape=jax.ShapeDtypeStruct((n, D_MODEL), F32),
        compiler_params=_params("parallel"),
        name="gdn_outproj_ffn",
    )(a, w_out, res, g0, b0, wg, wu, wd, g1, b1)


def _band_tables():
    qi = np.arange(SWA_BLOCK)[:, None]
    kj = np.arange(2 * SWA_BLOCK)[None, :]
    dist = qi + SWA_BLOCK - kj
    d = np.maximum(dist, 0)
    max_exact = REL_BUCKETS // 2
    df = np.maximum(d, 1).astype(np.float32)
    large = max_exact + (np.log(df / np.float32(max_exact)) / np.float32(math.log(REL_MAX_DIST / max_exact))
                         * np.float32(REL_BUCKETS - max_exact)).astype(np.int32)
    large = np.minimum(large, REL_BUCKETS - 1)
    bucket = np.where(d < max_exact, d, large).astype(np.int32)
    band_ok = (dist >= 0) & (dist < SWA_WINDOW)
    valid = np.stack([band_ok & (kj >= SWA_BLOCK), band_ok]).astype(np.int32)
    return np.ascontiguousarray(bucket.T), np.ascontiguousarray(valid.transpose(0, 2, 1))


def _bias_kernel(relb_ref, bucket_ref, valid_ref, o_ref):
    bucket = bucket_ref[...]
    in_bucket = [bucket == b for b in range(REL_BUCKETS)]
    for h in range(SWA_Q_HEADS):
        acc = jnp.zeros(bucket.shape, F32)
        for b in range(REL_BUCKETS):
            acc = jnp.where(in_bucket[b], relb_ref[b, h], acc)
        for t in range(2):
            o_ref[t, h] = jnp.where(valid_ref[t] != 0, acc, NEG_BIG)


def _bias_table(rel_bias):
    bucket, valid = _band_tables()
    shape = (2 * SWA_BLOCK, SWA_BLOCK)
    return pl.pallas_call(
        _bias_kernel,
        grid=(1,),
        in_specs=[pl.BlockSpec(memory_space=pltpu.SMEM), pl.BlockSpec(shape, lambda i: (0, 0)),
                  pl.BlockSpec((2,) + shape, lambda i: (0, 0, 0))],
        out_specs=pl.BlockSpec((2, SWA_Q_HEADS) + shape, lambda i: (0, 0, 0, 0)),
        out_shape=jax.ShapeDtypeStruct((2, SWA_Q_HEADS) + shape, F32),
        compiler_params=_params("arbitrary"),
        name="swa_bias_table",
    )(rel_bias, jnp.asarray(bucket), jnp.asarray(valid))


def _band_attention(q_pair, kv, bias_of, sink_ref, put_pair):
    dh = SWA_HEAD_DIM
    nk = 2 * SWA_BLOCK
    k_all = kv[:, :SWA_KVDIM]
    lane = lax.broadcasted_iota(jnp.int32, k_all.shape, 1)
    zero = jnp.zeros_like(k_all)
    for kh in range(SWA_KV_HEADS):
        k_own = jnp.where((lane >= kh * dh) & (lane < (kh + 1) * dh), k_all, zero)
        k_other = pltpu.roll(k_own.astype(F32), dh, axis=1).astype(BF16)
        k_even, k_odd = (k_own, k_other) if kh == 0 else (k_other, k_own)
        k2 = jnp.concatenate([k_even, k_odd], axis=0)
        v_t = kv[:, SWA_KVDIM + kh * dh:SWA_KVDIM + (kh + 1) * dh].T
        pairs = [kh * (SWA_GROUP // 2) + p for p in range(SWA_GROUP // 2)]
        heads = [2 * p + r for p in pairs for r in range(2)]
        st = [_dot_nt(k2, q_pair(p)) for p in pairs]
        scores = [st[i // 2][(i % 2) * nk:(i % 2 + 1) * nk] + bias_of(h) for i, h in enumerate(heads)]
        sinks = [sink_ref[0:1, h:h + 1] for h in heads]
        maxes = [jnp.maximum(jnp.max(s, axis=0, keepdims=True), sk) for s, sk in zip(scores, sinks)]
        expo = [jnp.exp(s - m) for s, m in zip(scores, maxes)]
        sums = [jnp.sum(p, axis=0, keepdims=True) for p in expo]
        inv_den = [1.0 / (t + jnp.exp(sk - m)) for t, sk, m in zip(sums, sinks, maxes)]
        for i, p in enumerate(pairs):
            o_t = _dot(v_t, jnp.concatenate(expo[2 * i:2 * i + 2], axis=1).astype(BF16))
            z = jnp.concatenate([o_t[:, :SWA_BLOCK] * inv_den[2 * i], o_t[:, SWA_BLOCK:] * inv_den[2 * i + 1]],
                                axis=0)
            put_pair(p, z.T)


def _swa_layer_kernel(x_ref, win_ref, bin_ref, bias_ref, sink_ref, wout_ref, g_ref, b_ref, o_ref,
                      q_ref, kv_ref, att_ref, *, nblk):
    j = pl.program_id(1)
    blk = SWA_BLOCK
    tm = nblk * blk
    dh2 = 2 * SWA_HEAD_DIM

    p = _dot(x_ref[...].astype(BF16), win_ref[...]) + bin_ref[...]
    q_ref[...] = (p[:, :SWA_QDIM] * (SWA_HEAD_DIM ** -0.5)).astype(BF16)

    @pl.when(j == 0)
    def _():
        kv_ref[0:blk, :] = jnp.zeros((blk, 2 * SWA_KVDIM), BF16)

    @pl.when(j != 0)
    def _():
        kv_ref[0:blk, :] = kv_ref[tm:tm + blk, :]

    kv_ref[blk:, :] = p[:, SWA_QDIM:].astype(BF16)

    for i in range(nblk):
        rows = slice(i * blk, (i + 1) * blk)
        table = jnp.where(j == 0, 0, 1) if i == 0 else 1

        def put_pair(pair, o, rows=rows):
            att_ref[rows, pair * dh2:(pair + 1) * dh2] = o.astype(att_ref.dtype)

        _band_attention(lambda pair, rows=rows: q_ref[rows, pair * dh2:(pair + 1) * dh2],
                        kv_ref[i * blk:(i + 2) * blk, :],
                        lambda h, table=table: bias_ref[table, h], sink_ref, put_pair)

    o_ref[...] = _proj_res_ln(att_ref, wout_ref, x_ref, g_ref, b_ref)


def _swa_layer(x, w_in, b_in, bias, sinks, w_out, g, b, batch, seq_len):
    n = x.shape[0]
    nblk = SWA_BLOCKS_PER_STEP
    tm = nblk * SWA_BLOCK
    steps = seq_len // tm
    resident = lambda a: pl.BlockSpec(a.shape, lambda s, j: (0,) * a.ndim, pipeline_mode=pl.Buffered(1))
    rows = pl.BlockSpec((tm, D_MODEL), lambda s, j: (s * steps + j, 0))
    return pl.pallas_call(
        functools.partial(_swa_layer_kernel, nblk=nblk),
        grid=(batch, steps),
        in_specs=[rows, resident(w_in), resident(b_in), resident(bias), resident(sinks), resident(w_out),
                  resident(g), resident(b)],
        out_specs=rows,
        out_shape=jax.ShapeDtypeStruct((n, D_MODEL), F32),
        scratch_shapes=[pltpu.VMEM((tm, SWA_QDIM), BF16), pltpu.VMEM((tm + SWA_BLOCK, 2 * SWA_KVDIM), BF16),
                        pltpu.VMEM((tm, SWA_QDIM), BF16)],
        compiler_params=_params("parallel", "arbitrary"),
        name="swa_layer",
    )(x, w_in, b_in, bias, sinks, w_out, g, b)


def _router_kernel(x_ref, wr_ref, route_ref, route_t_ref, cnt_ref, run_ref, *, tm):
    @pl.when(pl.program_id(0) == 0)
    def _():
        run_ref[...] = jnp.zeros(run_ref.shape, F32)

    logits = _dot3(x_ref[...], wr_ref[...])
    lane = lax.broadcasted_iota(jnp.int32, logits.shape, 1)
    lane_f = lane.astype(F32)
    lg = jnp.where(lane < N_EXPERTS, logits, NEG_BIG)
    m1 = jnp.max(lg, axis=-1, keepdims=True)
    i1 = jnp.min(jnp.where(lg == m1, lane_f, float(LANES)), axis=-1, keepdims=True)
    oh1 = lane_f == i1
    lg2 = jnp.where(oh1, NEG_BIG, lg)
    m2 = jnp.max(lg2, axis=-1, keepdims=True)
    i2 = jnp.min(jnp.where(lg2 == m2, lane_f, float(LANES)), axis=-1, keepdims=True)
    oh2 = lane_f == i2
    e = jnp.exp(m2 - m1)
    w0 = 1.0 / (1.0 + e)
    w1 = e * w0

    cnt = jnp.where(oh1, 1.0, 0.0) + jnp.where(oh2, 1.0, 0.0)
    r = lax.broadcasted_iota(jnp.int32, (tm, tm), 0)
    c = lax.broadcasted_iota(jnp.int32, (tm, tm), 1)
    before = jnp.where(r > c, 1.0, 0.0).astype(BF16)
    excl = _dot(before, cnt.astype(BF16)) + run_ref[...]
    rank0 = jnp.sum(jnp.where(oh1, excl, 0.0), axis=-1, keepdims=True)
    rank1 = jnp.sum(jnp.where(oh2, excl, 0.0), axis=-1, keepdims=True)
    run = run_ref[...] + jnp.sum(cnt, axis=0, keepdims=True)
    run_ref[...] = run
    cnt_ref[...] = run

    vals = (i1, i2, rank0, rank1, w0, w1)
    out = jnp.zeros(logits.shape, F32)
    for idx, val in enumerate(vals):
        out = jnp.where(lane == idx, val, out)
    route_ref[...] = out
    route_t_ref[...] = out.T[:SUBLANES, :]


def _router(x, w_router_padded):
    n = x.shape[0]
    tm = TM_ROUTER
    return pl.pallas_call(
        functools.partial(_router_kernel, tm=tm),
        grid=(n // tm,),
        in_specs=[pl.BlockSpec((tm, D_MODEL), lambda i: (i, 0)), pl.BlockSpec((D_MODEL, LANES), lambda i: (0, 0))],
        out_specs=[pl.BlockSpec((tm, LANES), lambda i: (i, 0)), pl.BlockSpec((SUBLANES, tm), lambda i: (0, i)),
                   pl.BlockSpec((1, LANES), lambda i: (0, 0))],
        out_shape=[jax.ShapeDtypeStruct((n, LANES), F32), jax.ShapeDtypeStruct((SUBLANES, n), F32),
                   jax.ShapeDtypeStruct((1, LANES), F32)],
        scratch_shapes=[pltpu.VMEM((1, LANES), F32)],
        compiler_params=_params("arbitrary"),
        name="moe_router",
    )(x, w_router_padded)


def _dispatch_kernel(pos0_ref, pos1_ref, meta_ref, x_ref, wsrc_ref, xs_hbm, wdst_ref, zero_ref, sem, pad_sem, *, td):
    i = pl.program_id(0)
    wdst_ref[...] = wsrc_ref[...].astype(BF16)

    def row_copy(t, dst):
        return pltpu.make_async_copy(x_ref.at[pl.ds(t, 1)], xs_hbm.at[pl.ds(dst, 1)], sem)

    def start(t, carry):
        row_copy(t, pos0_ref[0, 0, t]).start()
        row_copy(t, pos1_ref[0, 0, t]).start(priority=1)
        return carry

    lax.fori_loop(0, td, start, 0, unroll=DMA_UNROLL)

    @pl.when(i == pl.num_programs(0) - 1)
    def _():
        zero_ref[...] = jnp.zeros(zero_ref.shape, F32)

        zr = zero_ref.shape[0]

        def row_zero(dst):
            return pltpu.make_async_copy(zero_ref.at[pl.ds(0, 1)], xs_hbm.at[pl.ds(dst, 1)], pad_sem)

        def block_zero(blk):
            return pltpu.make_async_copy(zero_ref, xs_hbm.at[pl.ds(pl.multiple_of(blk * zr, zr), zr)], pad_sem)

        def zero_range(copy, lo, hi):
            lax.fori_loop(lo, hi, lambda r, carry: (copy(r).start(), carry)[1], 0)
            lax.fori_loop(lo, hi, lambda r, carry: (copy(0).wait(), carry)[1], 0)

        for e in range(N_EXPERTS):
            off, cnt, cnt_up, padded = (meta_ref[r, e] for r in range(4))
            zero_range(lambda r, off=off: row_zero(off + r), cnt, cnt_up)
            zero_range(block_zero, (off + cnt_up) // zr, (off + padded) // zr)

        used = meta_ref[0, N_EXPERTS - 1] + meta_ref[3, N_EXPERTS - 1]
        zero_range(block_zero, used // zr, xs_hbm.shape[0] // zr)

    for _ in range(2):
        pltpu.make_async_copy(x_ref, xs_hbm.at[pl.ds(0, td)], sem).wait()


def _dispatch(x, pos, meta, n_rows, to_bf16):
    n = x.shape[0]
    td = TD_DISPATCH
    pos_spec = pl.BlockSpec((1, 1, td), lambda i: (i, 0, 0), memory_space=pltpu.SMEM)
    (cast_spec,), (cast_shape,) = _cast_specs([to_bf16], n // td)
    return pl.pallas_call(
        functools.partial(_dispatch_kernel, td=td),
        grid=(n // td,),
        in_specs=[pos_spec, pos_spec, pl.BlockSpec(memory_space=pltpu.SMEM),
                  pl.BlockSpec((td, D_MODEL), lambda i: (i, 0)), cast_spec],
        out_specs=[pl.BlockSpec(memory_space=pl.ANY), cast_spec],
        out_shape=[jax.ShapeDtypeStruct((n_rows, D_MODEL), F32), cast_shape],
        scratch_shapes=[pltpu.VMEM((ZERO_ROWS, D_MODEL), F32), pltpu.SemaphoreType.DMA, pltpu.SemaphoreType.DMA],
        compiler_params=_params("arbitrary"),
        name="moe_dispatch",
    )(pos[0].reshape(n // td, 1, td), pos[1].reshape(n // td, 1, td), meta, x, to_bf16)


def _moe_kernel(we_ref, wb_ref, nv_ref, cnt_ref, *refs, n_sub, tm):
    xs_refs = refs[:n_sub]
    wg_ref, wu_ref, wd_ref, ys_hbm, acc_ref, xb_ref, sem = refs[n_sub:]
    j = pl.program_id(0)
    f = pl.program_id(1)
    last_f = pl.num_programs(1) - 1
    n_windows, n_tiles = cnt_ref[0], cnt_ref[1]

    def tile_write(slot, block):
        return pltpu.make_async_copy(acc_ref.at[slot], ys_hbm.at[pl.ds(pl.multiple_of(block * tm, tm), tm)],
                                     sem.at[slot])

    @pl.when(j < n_windows)
    def _():
        prev_tiles = jnp.where(j > 0, nv_ref[jnp.maximum(j - 1, 0)], 0)

        for s in range(n_sub):
            @pl.when(s < nv_ref[j])
            def _(s=s):
                @pl.when(f == 0)
                def _():
                    xb_ref[s] = xs_refs[s][...].astype(BF16)

                xb = xb_ref[s]
                h = (_silu(_dot(xb, wg_ref[0])) * _dot(xb, wu_ref[0])).astype(BF16)
                part = _dot(h, wd_ref[0])

                @pl.when(f == 0)
                def _():
                    @pl.when(s < prev_tiles)
                    def _():
                        tile_write(s, 0).wait()

                    acc_ref[s] = part

                @pl.when(f != 0)
                def _():
                    acc_ref[s] += part

                @pl.when(f == last_f)
                def _():
                    tile_write(s, wb_ref[j] + s).start()

            @pl.when((f == 0) & (s >= nv_ref[j]) & (s < prev_tiles))
            def _(s=s):
                tile_write(s, 0).wait()

        @pl.when((f == last_f) & (j == n_windows - 1))
        def _():
            for s in range(n_sub):
                @pl.when(s < nv_ref[j])
                def _(s=s):
                    tile_write(s, 0).wait()

            acc_ref[0] = jnp.zeros(acc_ref.shape[1:], F32)
            total_tiles = ys_hbm.shape[0] // tm
            lax.fori_loop(n_tiles, total_tiles, lambda t, c: (tile_write(0, t).start(), c)[1], 0)
            lax.fori_loop(n_tiles, total_tiles, lambda t, c: (tile_write(0, 0).wait(), c)[1], 0)


def _moe_experts(xs, wg, wu, wd, win_expert, win_block, win_tiles, counts):
    n_rows = xs.shape[0]
    tm, tf, n_sub = TM_MOE, TF_MOE, MOE_TILES_PER_WINDOW
    max_windows = win_expert.shape[0]

    def xs_spec(s):
        return pl.BlockSpec((tm, D_MODEL), lambda j, f, we, wb, nv, cnt: (wb[j] + jnp.minimum(s, nv[j] - 1), 0))

    grid_spec = pltpu.PrefetchScalarGridSpec(
        num_scalar_prefetch=4,
        grid=(max_windows, EXPERT_DIM // tf),
        in_specs=[xs_spec(s) for s in range(n_sub)] + [
            pl.BlockSpec((1, D_MODEL, tf), lambda j, f, we, wb, nv, cnt: (we[j], 0, f)),
            pl.BlockSpec((1, D_MODEL, tf), lambda j, f, we, wb, nv, cnt: (we[j], 0, f)),
            pl.BlockSpec((1, tf, D_MODEL), lambda j, f, we, wb, nv, cnt: (we[j], f, 0))],
        out_specs=pl.BlockSpec(memory_space=pl.ANY),
        scratch_shapes=[pltpu.VMEM((n_sub, tm, D_MODEL), F32), pltpu.VMEM((n_sub, tm, D_MODEL), BF16),
                        pltpu.SemaphoreType.DMA((n_sub,))],
    )
    return pl.pallas_call(
        functools.partial(_moe_kernel, n_sub=n_sub, tm=tm),
        grid_spec=grid_spec,
        out_shape=jax.ShapeDtypeStruct((n_rows, D_MODEL), F32),
        compiler_params=_params("arbitrary", "arbitrary"),
        name="moe_experts",
    )(win_expert, win_block, win_tiles, counts, *([xs] * n_sub), wg, wu, wd)


def _combine_kernel(pos0_ref, pos1_ref, pos0n_ref, pos1n_ref, x_ref, route_ref, ys_hbm, g_ref, b_ref, o_ref,
                    y_ref, sem, *, tc):
    i = pl.program_id(0)
    slot = i % 2

    def issue(p_refs, s):
        def start(t, carry):
            for kk in range(2):
                pltpu.make_async_copy(ys_hbm.at[pl.ds(p_refs[kk][0, 0, t], 1)],
                                      y_ref.at[s, kk, pl.ds(t, 1)], sem.at[s]).start(priority=kk)
            return carry

        lax.fori_loop(0, tc, start, 0, unroll=DMA_UNROLL)

    @pl.when(i == 0)
    def _():
        issue((pos0_ref, pos1_ref), 0)

    @pl.when(i + 1 < pl.num_programs(0))
    def _():
        issue((pos0n_ref, pos1n_ref), 1 - slot)

    for kk in range(2):
        pltpu.make_async_copy(ys_hbm.at[pl.ds(0, tc)], y_ref.at[slot, kk], sem.at[slot]).wait()

    route = route_ref[...]
    f = route[:, 4:5] * y_ref[slot, 0] + route[:, 5:6] * y_ref[slot, 1]
    o_ref[...] = _layer_norm(ALPHA * x_ref[...] + f, g_ref[...], b_ref[...])


def _combine_ln(x, route, pos, ys, g, b):
    n = x.shape[0]
    tc = TC_COMBINE
    nt = n // tc
    pos0, pos1 = pos[0].reshape(nt, 1, tc), pos[1].reshape(nt, 1, tc)
    cur = pl.BlockSpec((1, 1, tc), lambda i: (i, 0, 0), memory_space=pltpu.SMEM)
    nxt = pl.BlockSpec((1, 1, tc), lambda i: (jnp.minimum(i + 1, nt - 1), 0, 0), memory_space=pltpu.SMEM)
    return pl.pallas_call(
        functools.partial(_combine_kernel, tc=tc),
        grid=(nt,),
        in_specs=[cur, cur, nxt, nxt,
                  pl.BlockSpec((tc, D_MODEL), lambda i: (i, 0)),
                  pl.BlockSpec((tc, LANES), lambda i: (i, 0)),
                  pl.BlockSpec(memory_space=pl.ANY),
                  pl.BlockSpec((1, D_MODEL), lambda i: (0, 0)), pl.BlockSpec((1, D_MODEL), lambda i: (0, 0))],
        out_specs=pl.BlockSpec((tc, D_MODEL), lambda i: (i, 0)),
        out_shape=jax.ShapeDtypeStruct((n, D_MODEL), F32),
        scratch_shapes=[pltpu.VMEM((2, 2, tc, D_MODEL), F32), pltpu.SemaphoreType.DMA((2,))],
        compiler_params=_params("arbitrary"),
        name="moe_combine_ln",
    )(pos0, pos1, pos0, pos1, x, route, ys, g, b)


def _moe_layer(x, route, route_t, counts_f, wg, wu, wd_f32, g, b):
    n = x.shape[0]
    tm = TM_MOE

    counts = counts_f[0, :N_EXPERTS].astype(jnp.int32)
    padded = ((counts + tm - 1) // tm) * tm
    offs = jnp.cumsum(padded) - padded
    eids = jnp.arange(N_EXPERTS, dtype=jnp.int32)
    e01 = route_t[0:2].astype(jnp.int32)
    rank01 = route_t[2:4].astype(jnp.int32)
    base = jnp.sum(jnp.where(e01[:, None, :] == eids[None, :, None], offs[None, :, None], 0), axis=1)
    pos = base + rank01
    counts_up = jnp.minimum(padded, ((counts + ZERO_ROWS - 1) // ZERO_ROWS) * ZERO_ROWS)
    meta = jnp.stack([offs, counts, counts_up, padded]).astype(jnp.int32)

    n_rows = 2 * n + N_EXPERTS * tm
    nsub = MOE_TILES_PER_WINDOW
    tiles = padded // tm
    wins = (tiles + nsub - 1) // nsub
    win_end = jnp.cumsum(wins)
    n_windows = win_end[-1]
    max_windows = (n_rows // tm + nsub - 1) // nsub + N_EXPERTS
    wj = jnp.minimum(jnp.arange(max_windows, dtype=jnp.int32), n_windows - 1)
    win_expert = jnp.minimum(jnp.sum((wj[:, None] >= win_end[None, :]).astype(jnp.int32), axis=-1), N_EXPERTS - 1)
    pick = lambda v: jnp.sum(jnp.where(win_expert[:, None] == eids[None, :], v[None, :], 0), axis=-1)
    local = wj - pick(win_end - wins)
    win_block = pick(offs // tm) + nsub * local
    win_tiles = jnp.minimum(nsub, pick(tiles) - nsub * local)
    counts_nt = jnp.stack([n_windows, jnp.sum(tiles)])

    xs, wd = _dispatch(x, pos, meta, n_rows, wd_f32.reshape(-1, D_MODEL))
    wd = wd.reshape(wd_f32.shape)
    i32 = lambda a: a.astype(jnp.int32)
    ys = _moe_experts(xs, wg, wu, wd, i32(win_expert), i32(win_block), i32(win_tiles), i32(counts_nt))
    return _combine_ln(x, route, pos, ys, g, b)


def kernel(x, a_w_in, a_conv_w, a_a_log, a_dt_bias, a_norm_w, a_w_out, b_w_in, b_b_in, b_sinks, b_w_out, rel_bias,
           ffn_w_gate, ffn_w_up, ffn_w_down, moe_router, moe_w_gate, moe_w_up, moe_w_down, ln_g, ln_b):
    batch, seq_len, _ = x.shape
    n = batch * seq_len
    x0 = x.reshape(n, D_MODEL)
    ln_g = ln_g.reshape(DEPTH, 2, 1, D_MODEL)
    ln_b = ln_b.reshape(DEPTH, 2, 1, D_MODEL)

    pad_gate = lambda p: jnp.pad(p.reshape(1, GDN_V_HEADS), ((0, 0), (GDN_V_HEADS, LANES - 2 * GDN_V_HEADS)))
    gate_params = jnp.concatenate([pad_gate(a_a_log[0]), pad_gate(a_dt_bias[0])], axis=0)
    q, k, v, zs, gb = _gdn_inproj(x0, a_w_in, a_conv_w[0], gate_params, seq_len)
    o, casted = _gdn_chunk(q, k, v, zs, gb, a_norm_w[0].reshape(1, GDN_HEAD), batch, seq_len,
                           [moe_w_gate[0].reshape(-1, EXPERT_DIM), moe_w_up[0].reshape(-1, EXPERT_DIM),
                            ffn_w_gate[0], ffn_w_up[0], ffn_w_down[0],
                            a_w_out[0], b_w_in[0], b_w_out[0]])
    moe_wg, moe_wu, ffn_wg, ffn_wu, ffn_wd, w_out_a, w_in_b, w_out_b = casted
    x2 = _outproj_ffn(o, w_out_a, x0, ln_g[0, 0], ln_b[0, 0],
                      ffn_wg, ffn_wu, ffn_wd, ln_g[0, 1], ln_b[0, 1])

    bias = _bias_table(rel_bias)
    sinks = jnp.pad(b_sinks[0].reshape(1, SWA_Q_HEADS), ((0, 0), (0, LANES - SWA_Q_HEADS)))
    x3 = _swa_layer(x2, w_in_b, b_b_in[0].reshape(1, -1), bias, sinks, w_out_b, ln_g[1, 0], ln_b[1, 0],
                    batch, seq_len)
    w_router = jnp.pad(moe_router[0], ((0, 0), (0, LANES - N_EXPERTS)))
    route, route_t, counts = _router(x3, w_router)
    expert_shape = (N_EXPERTS, D_MODEL, EXPERT_DIM)
    x4 = _moe_layer(x3, route, route_t, counts, moe_wg.reshape(expert_shape), moe_wu.reshape(expert_shape),
                    moe_w_down[0], ln_g[1, 1], ln_b[1, 1])
    return x4.reshape(batch, seq_len, D_MODEL)
```

```python
import functools
import math

import numpy as np
import jax
import jax.numpy as jnp
from jax import lax
from jax.experimental import pallas as pl
from jax.experimental.pallas import tpu as pltpu

F32 = jnp.float32
BF16 = jnp.bfloat16

D_MODEL = 1024
DEPTH = 2
ALPHA = (2.0 * DEPTH) ** 0.25
LN_EPS = 1e-5

GDN_K_HEADS = 4
GDN_V_HEADS = 8
GDN_HEAD = 128
GDN_KDIM = GDN_K_HEADS * GDN_HEAD
GDN_VDIM = GDN_V_HEADS * GDN_HEAD
GDN_CONV = 4
GDN_CHUNK = 64
GDN_QKV = 2 * GDN_KDIM + GDN_VDIM
GDN_EPS = 1e-6

SWA_Q_HEADS = 16
SWA_KV_HEADS = 2
SWA_GROUP = SWA_Q_HEADS // SWA_KV_HEADS
SWA_HEAD_DIM = 64
SWA_WINDOW = 128
SWA_BLOCK = 128
SWA_QDIM = SWA_Q_HEADS * SWA_HEAD_DIM
SWA_KVDIM = SWA_KV_HEADS * SWA_HEAD_DIM
REL_BUCKETS = 32
REL_MAX_DIST = 128

FFN_DIM = 2816
N_EXPERTS = 8
EXPERT_DIM = 3584

LANES = 128
SUBLANES = 8
NEG_BIG = -1e30
VMEM_LIMIT = 56 * 1024 * 1024

TM_GDN_IN = 512
TM_FFN = 512
SWA_BLOCKS_PER_STEP = 4
TM_ROUTER = 1024
TM_MOE = 512
TF_MOE = 1792
MOE_TILES_PER_WINDOW = 3
TD_DISPATCH = 1024
TC_COMBINE = 512
ZERO_ROWS = 64
COMBINE_ROWS = 64
DMA_UNROLL = 8


def _params(*sem):
    return pltpu.CompilerParams(dimension_semantics=sem, vmem_limit_bytes=VMEM_LIMIT)


def _dot(a, b):
    return jnp.dot(a, b, preferred_element_type=F32)


def _dot_nt(a, b):
    return lax.dot_general(a, b, (((1,), (1,)), ((), ())), preferred_element_type=F32)


def _dot_tn(a, b):
    return lax.dot_general(a, b, (((0,), (0,)), ((), ())), preferred_element_type=F32)


def _split(x):
    hi = x.astype(BF16)
    lo = (x - hi.astype(F32)).astype(BF16)
    return hi, lo


def _dot3(a, b):
    ah, al = _split(a)
    bh, bl = _split(b)
    return _dot(ah, bh) + (_dot(ah, bl) + _dot(al, bh))


def _silu(x):
    return x * jax.nn.sigmoid(x)


def _layer_norm(y, g, b):
    mu = jnp.mean(y, axis=-1, keepdims=True)
    yc = y - mu
    var = jnp.mean(yc * yc, axis=-1, keepdims=True)
    return yc * lax.rsqrt(var + LN_EPS) * g + b


def _gdn_inproj_kernel(x_ref, wf_ref, convw_ref, gp_ref,
                       q_ref, k_ref, v_ref, z_ref, gb_ref, ext_ref, w_ref, *, tm, tiles_per_seq):
    i = pl.program_id(0)
    xb = x_ref[...].astype(BF16)

    @pl.when(i == 0)
    def _():
        n_in = wf_ref.shape[-1]
        w_ref[:, :n_in] = wf_ref[0].astype(BF16)
        w_ref[:, n_in:] = jnp.zeros((w_ref.shape[0], w_ref.shape[1] - n_in), BF16)

    @pl.when(i % tiles_per_seq == 0)
    def _():
        ext_ref[0:SUBLANES, :] = jnp.zeros((SUBLANES, GDN_QKV), F32)

    @pl.when(i % tiles_per_seq != 0)
    def _():
        ext_ref[0:SUBLANES, :] = ext_ref[tm:tm + SUBLANES, :]

    ext_ref[SUBLANES:, :] = _dot(xb, w_ref[:, :GDN_QKV])
    z_ref[...] = _dot(xb, w_ref[:, GDN_QKV:GDN_QKV + GDN_VDIM])

    n_chunks = GDN_QKV // LANES
    for c in range(n_chunks):
        cs = slice(c * LANES, (c + 1) * LANES)
        acc = convw_ref[GDN_CONV - 1:GDN_CONV, cs] * ext_ref[SUBLANES:SUBLANES + tm, cs]
        for j in range(GDN_CONV - 1):
            off = SUBLANES - (GDN_CONV - 1) + j
            acc = acc + convw_ref[j:j + 1, cs] * ext_ref[off:off + tm, cs]
        y = _silu(acc)
        if c < 2 * GDN_K_HEADS:
            y = y * lax.rsqrt(jnp.sum(y * y, axis=-1, keepdims=True) + GDN_EPS)
            if c < GDN_K_HEADS:
                q_ref[:, cs] = y * (GDN_HEAD ** -0.5)
            else:
                k_ref[:, (c - GDN_K_HEADS) * LANES:(c - GDN_K_HEADS + 1) * LANES] = y
        else:
            cv = c - 2 * GDN_K_HEADS
            v_ref[:, cv * LANES:(cv + 1) * LANES] = y

    ba = _dot(xb, w_ref[:, GDN_QKV + GDN_VDIM:])
    lane = lax.broadcasted_iota(jnp.int32, ba.shape, 1)
    sp = ba + gp_ref[1:2, :]
    softplus = jnp.maximum(sp, 0.0) + jnp.log(1.0 + jnp.exp(-jnp.abs(sp)))
    g = -jnp.exp(gp_ref[0:1, :]) * softplus
    gb_ref[...] = jnp.where(lane < GDN_V_HEADS, jax.nn.sigmoid(ba), g)


def _gdn_inproj(x2d, w_in, conv_w, gate_params, seq_len):
    n = x2d.shape[0]
    tm = TM_GDN_IN
    kern = functools.partial(_gdn_inproj_kernel, tm=tm, tiles_per_seq=seq_len // tm)
    full = lambda shape: pl.BlockSpec(shape, lambda i: (0,) * len(shape))
    rows = lambda width: pl.BlockSpec((tm, width), lambda i: (i, 0))
    return pl.pallas_call(
        kern,
        grid=(n // tm,),
        in_specs=[rows(D_MODEL), pl.BlockSpec(w_in.shape, lambda i: (0, 0, 0), pipeline_mode=pl.Buffered(1)),
                  full(conv_w.shape), full(gate_params.shape)],
        out_specs=[rows(GDN_KDIM), rows(GDN_KDIM), rows(GDN_VDIM), rows(GDN_VDIM), rows(LANES)],
        out_shape=[jax.ShapeDtypeStruct((n, GDN_KDIM), F32), jax.ShapeDtypeStruct((n, GDN_KDIM), F32),
                   jax.ShapeDtypeStruct((n, GDN_VDIM), F32), jax.ShapeDtypeStruct((n, GDN_VDIM), F32),
                   jax.ShapeDtypeStruct((n, LANES), F32)],
        scratch_shapes=[pltpu.VMEM((tm + SUBLANES, GDN_QKV), F32),
                        pltpu.VMEM((D_MODEL, GDN_QKV + GDN_VDIM + LANES), BF16)],
        compiler_params=_params("arbitrary"),
        name="gdn_inproj",
    )(x2d, w_in, conv_w, gate_params)


GDN_CHUNKS_PER_STEP = 2
NEUMANN_BLOCK = 4


def _bf16_all(xs):
    return [x.astype(BF16) for x in xs]


def _dot_all(a_list, b_list):
    return [_dot(a, b) for a, b in zip(a_list, b_list)]


def _unit_lower_inverse_all(a_list, row, col):
    shift = int(math.log2(NEUMANN_BLOCK))
    eye = jnp.where(row == col, 1.0, 0.0).astype(F32)
    on_diag_block = (row >> shift) == (col >> shift)
    d = [jnp.where(on_diag_block, a, 0.0) for a in a_list]
    d_b = _bf16_all(d)
    x = [eye - di for di in d]
    d2_b = _bf16_all(_dot_all(d_b, d_b))
    x = [xi + t for xi, t in zip(x, _dot_all(_bf16_all(x), d2_b))]
    size = NEUMANN_BLOCK
    while size < a_list[0].shape[0]:
        shift = int(math.log2(size))
        rbl, cbl = row >> shift, col >> shift
        below = ((rbl & 1) == 1) & (cbl == rbl - 1)
        l_b = _bf16_all([jnp.where(below, a, 0.0) for a in a_list])
        x_b = _bf16_all(x)
        xl_b = _bf16_all(_dot_all(x_b, l_b))
        x = [xi - t for xi, t in zip(x, _dot_all(xl_b, x_b))]
        size *= 2
    return x


def _gdn_chunk_kernel(q_ref, k_ref, v_ref, z_ref, gb_ref, nw_ref, *rest, nb, nck, n_cast):
    cast_src, (o_ref, *cast_dst), s_ref = rest[:n_cast], rest[n_cast:2 * n_cast + 1], rest[-1]
    for src, dst in zip(cast_src, cast_dst):
        dst[...] = src[...].astype(BF16)
    c = GDN_CHUNK
    nh = GDN_V_HEADS

    @pl.when(pl.program_id(0) == 0)
    def _():
        s_ref[...] = jnp.zeros(s_ref.shape, F32)

    row = lax.broadcasted_iota(jnp.int32, (c, c), 0)
    col = lax.broadcasted_iota(jnp.int32, (c, c), 1)
    causal = row >= col
    strict = row > col
    tril = jnp.where(causal, 1.0, 0.0).astype(BF16)
    rs = lambda ci: slice(ci * c, (ci + 1) * c)
    ks = lambda h: slice((h // (nh // GDN_K_HEADS)) * GDN_HEAD, (h // (nh // GDN_K_HEADS) + 1) * GDN_HEAD)
    vs = lambda h: slice(h * GDN_HEAD, (h + 1) * GDN_HEAD)

    gates = {}
    for ci in range(nck):
        for b in range(nb):
            gb = gb_ref[b, rs(ci), :]
            g_hi, g_lo = _split(gb)
            gc = _dot(tril, g_hi) + _dot(tril, g_lo)
            g_last = gc[c - 1:c, :]
            gates[ci, b] = dict(gb=gb, gc=gc, gc_t=gc.T, eg=jnp.exp(gc), e_last=jnp.exp(g_last),
                                e_rest=jnp.exp(g_last - gc))

    items = [(ci, b, h) for ci in range(nck) for b in range(nb) for h in range(nh)]

    a_kk, a_qk, k_beta = [], {}, {}
    for it in items:
        ci, b, h = it
        gt = gates[ci, b]
        k = k_ref[b, rs(ci), ks(h)]
        kb = k * gt["gb"][:, h:h + 1]
        kq = jnp.concatenate([kb, q_ref[b, rs(ci), ks(h)]], axis=0).astype(BF16)
        gram = _dot_nt(kq, k.astype(BF16))
        gl = nh + h
        decay = jnp.exp(jnp.where(causal, gt["gc"][:, gl:gl + 1] - gt["gc_t"][gl:gl + 1, :], NEG_BIG))
        a_kk.append(jnp.where(strict, gram[:c] * decay, 0.0))
        a_qk[it] = (gram[c:] * decay).astype(BF16)
        k_beta[it] = kb

    t_inv = dict(zip(items, _unit_lower_inverse_all(a_kk, row, col)))

    uw = {}
    for it in items:
        ci, b, h = it
        gt = gates[ci, b]
        gl = nh + h
        rhs = jnp.concatenate([v_ref[b, rs(ci), vs(h)] * gt["gb"][:, h:h + 1],
                               k_beta[it] * gt["eg"][:, gl:gl + 1]], axis=1)
        uw[it] = _dot(t_inv[it].astype(BF16), rhs.astype(BF16))

    for ci in range(nck):
        chunk_items = [(ci, b, h) for b in range(nb) for h in range(nh)]

        ws_qs, states = {}, {}
        for it in chunk_items:
            _, b, h = it
            gl = nh + h
            s = s_ref[b * nh + h]
            wq = jnp.concatenate([uw[it][:, GDN_HEAD:],
                                  q_ref[b, rs(ci), ks(h)] * gates[ci, b]["eg"][:, gl:gl + 1]], axis=0)
            ws_qs[it] = _dot(wq.astype(BF16), s.astype(BF16))
            states[it] = s

        for it in chunk_items:
            _, b, h = it
            gt = gates[ci, b]
            gl = nh + h
            v_new = (uw[it][:, :GDN_HEAD] - ws_qs[it][:c]).astype(BF16)
            o = ws_qs[it][c:] + _dot(a_qk[it], v_new)
            k_dec = (k_ref[b, rs(ci), ks(h)] * gt["e_rest"][:, gl:gl + 1]).astype(BF16)
            s_ref[b * nh + h] = states[it] * gt["e_last"][:, gl:gl + 1] + _dot_tn(k_dec, v_new)
            o = o * lax.rsqrt(jnp.mean(o * o, axis=-1, keepdims=True) + GDN_EPS) * nw_ref[...]
            o_ref[b, rs(ci), vs(h)] = (o * _silu(z_ref[b, rs(ci), vs(h)])).astype(o_ref.dtype)


def _cast_specs(arrays, n_steps):
    specs, shapes = [], []
    for a in arrays:
        span = next(d for d in (1, 2, 4, 8)
                    if (a.shape[0] * d) % n_steps == 0 and (a.shape[0] * d // n_steps) % (2 * SUBLANES) == 0)
        specs.append(pl.BlockSpec((a.shape[0] * span // n_steps, a.shape[1]), lambda j, span=span: (j // span, 0)))
        shapes.append(jax.ShapeDtypeStruct(a.shape, BF16))
    return specs, shapes


def _gdn_chunk(q, k, v, zs, gb, norm_w, batch, seq_len, to_bf16):
    rows_per_step = GDN_CHUNK * GDN_CHUNKS_PER_STEP
    n_steps = seq_len // rows_per_step
    seq = lambda a: a.reshape(batch, seq_len, a.shape[-1])
    rows = lambda width: pl.BlockSpec((batch, rows_per_step, width), lambda j: (0, j, 0))
    cast_specs, cast_shapes = _cast_specs(to_bf16, n_steps)
    out, *casted = pl.pallas_call(
        functools.partial(_gdn_chunk_kernel, nb=batch, nck=GDN_CHUNKS_PER_STEP, n_cast=len(to_bf16)),
        grid=(n_steps,),
        in_specs=[rows(GDN_KDIM), rows(GDN_KDIM), rows(GDN_VDIM), rows(GDN_VDIM), rows(LANES),
                  pl.BlockSpec((1, GDN_HEAD), lambda j: (0, 0))] + cast_specs,
        out_specs=[rows(GDN_VDIM)] + cast_specs,
        out_shape=[jax.ShapeDtypeStruct((batch, seq_len, GDN_VDIM), BF16)] + cast_shapes,
        scratch_shapes=[pltpu.VMEM((batch * GDN_V_HEADS, GDN_HEAD, GDN_HEAD), F32)],
        compiler_params=_params("arbitrary"),
        name="gdn_chunk",
    )(seq(q), seq(k), seq(v), seq(zs), seq(gb), norm_w, *to_bf16)
    return out.reshape(batch * seq_len, GDN_VDIM), casted


def _proj_res_ln(a_ref, w_ref, r_ref, g_ref, b_ref):
    return _layer_norm(ALPHA * r_ref[...] + _dot(a_ref[...], w_ref[...]), g_ref[...], b_ref[...])


def _outproj_ffn_kernel(a_ref, wo_ref, r_ref, g0_ref, b0_ref, wg_ref, wu_ref, wd_ref, g1_ref, b1_ref, o_ref):
    x1 = _proj_res_ln(a_ref, wo_ref, r_ref, g0_ref, b0_ref)
    xb = x1.astype(BF16)
    h = (_silu(_dot(xb, wg_ref[...])) * _dot(xb, wu_ref[...])).astype(BF16)
    o_ref[...] = _layer_norm(ALPHA * x1 + _dot(h, wd_ref[...]), g1_ref[...], b1_ref[...])


def _outproj_ffn(a, w_out, res, g0, b0, wg, wu, wd, g1, b1):
    n, kdim = a.shape
    tm = TM_FFN
    resident = lambda w: pl.BlockSpec(w.shape, lambda i: (0, 0), pipeline_mode=pl.Buffered(1))
    rows = lambda width: pl.BlockSpec((tm, width), lambda i: (i, 0))
    vec = pl.BlockSpec((1, D_MODEL), lambda i: (0, 0))
    return pl.pallas_call(
        _outproj_ffn_kernel,
        grid=(n // tm,),
        in_specs=[rows(kdim), resident(w_out), rows(D_MODEL), vec, vec,
                  resident(wg), resident(wu), resident(wd), vec, vec],
        out_specs=rows(D_MODEL),
        out_shape=jax.ShapeDtypeStruct((n, D_MODEL), F32),
        compiler_params=_params("parallel"),
        name="gdn_outproj_ffn",
    )(a, w_out, res, g0, b0, wg, wu, wd, g1, b1)


def _band_tables():
    qi = np.arange(SWA_BLOCK)[:, None]
    kj = np.arange(2 * SWA_BLOCK)[None, :]
    dist = qi + SWA_BLOCK - kj
    d = np.maximum(dist, 0)
    max_exact = REL_BUCKETS // 2
    df = np.maximum(d, 1).astype(np.float32)
    large = max_exact + (np.log(df / np.float32(max_exact)) / np.float32(math.log(REL_MAX_DIST / max_exact))
                         * np.float32(REL_BUCKETS - max_exact)).astype(np.int32)
    large = np.minimum(large, REL_BUCKETS - 1)
    bucket = np.where(d < max_exact, d, large).astype(np.int32)
    band_ok = (dist >= 0) & (dist < SWA_WINDOW)
    valid = np.stack([band_ok & (kj >= SWA_BLOCK), band_ok]).astype(np.int32)
    return np.ascontiguousarray(bucket.T), np.ascontiguousarray(valid.transpose(0, 2, 1))


def _bias_kernel(relb_ref, bucket_ref, valid_ref, o_ref):
    bucket = bucket_ref[...]
    in_bucket = [bucket == b for b in range(REL_BUCKETS)]
    for h in range(SWA_Q_HEADS):
        acc = jnp.zeros(bucket.shape, F32)
        for b in range(REL_BUCKETS):
            acc = jnp.where(in_bucket[b], relb_ref[b, h], acc)
        for t in range(2):
            o_ref[t, h] = jnp.where(valid_ref[t] != 0, acc, NEG_BIG)


def _bias_table(rel_bias):
    bucket, valid = _band_tables()
    shape = (2 * SWA_BLOCK, SWA_BLOCK)
    return pl.pallas_call(
        _bias_kernel,
        grid=(1,),
        in_specs=[pl.BlockSpec(memory_space=pltpu.SMEM), pl.BlockSpec(shape, lambda i: (0, 0)),
                  pl.BlockSpec((2,) + shape, lambda i: (0, 0, 0))],
        out_specs=pl.BlockSpec((2, SWA_Q_HEADS) + shape, lambda i: (0, 0, 0, 0)),
        out_shape=jax.ShapeDtypeStruct((2, SWA_Q_HEADS) + shape, F32),
        compiler_params=_params("arbitrary"),
        name="swa_bias_table",
    )(rel_bias, jnp.asarray(bucket), jnp.asarray(valid))


def _band_attention(q_pair, kv, bias_of, sink_ref, put_pair):
    dh = SWA_HEAD_DIM
    nk = 2 * SWA_BLOCK
    k_all = kv[:, :SWA_KVDIM]
    lane = lax.broadcasted_iota(jnp.int32, k_all.shape, 1)
    zero = jnp.zeros_like(k_all)
    for kh in range(SWA_KV_HEADS):
        k_own = jnp.where((lane >= kh * dh) & (lane < (kh + 1) * dh), k_all, zero)
        k_other = pltpu.roll(k_own.astype(F32), dh, axis=1).astype(BF16)
        k_even, k_odd = (k_own, k_other) if kh == 0 else (k_other, k_own)
        k2 = jnp.concatenate([k_even, k_odd], axis=0)
        v_t = kv[:, SWA_KVDIM + kh * dh:SWA_KVDIM + (kh + 1) * dh].T
        pairs = [kh * (SWA_GROUP // 2) + p for p in range(SWA_GROUP // 2)]
        heads = [2 * p + r for p in pairs for r in range(2)]
        st = [_dot_nt(k2, q_pair(p)) for p in pairs]
        scores = [st[i // 2][(i % 2) * nk:(i % 2 + 1) * nk] + bias_of(h) for i, h in enumerate(heads)]
        sinks = [sink_ref[0:1, h:h + 1] for h in heads]
        maxes = [jnp.maximum(jnp.max(s, axis=0, keepdims=True), sk) for s, sk in zip(scores, sinks)]
        expo = [jnp.exp(s - m) for s, m in zip(scores, maxes)]
        sums = [jnp.sum(p, axis=0, keepdims=True) for p in expo]
        inv_den = [1.0 / (t + jnp.exp(sk - m)) for t, sk, m in zip(sums, sinks, maxes)]
        for i, p in enumerate(pairs):
            o_t = _dot(v_t, jnp.concatenate(expo[2 * i:2 * i + 2], axis=1).astype(BF16))
            z = jnp.concatenate([o_t[:, :SWA_BLOCK] * inv_den[2 * i], o_t[:, SWA_BLOCK:] * inv_den[2 * i + 1]],
                                axis=0)
            put_pair(p, z.T)


def _swa_layer_kernel(x_ref, win_ref, bin_ref, bias_ref, sink_ref, wout_ref, g_ref, b_ref, o_ref,
                      q_ref, kv_ref, att_ref, *, nblk):
    j = pl.program_id(1)
    blk = SWA_BLOCK
    tm = nblk * blk
    dh2 = 2 * SWA_HEAD_DIM

    p = _dot(x_ref[...].astype(BF16), win_ref[...]) + bin_ref[...]
    q_ref[...] = (p[:, :SWA_QDIM] * (SWA_HEAD_DIM ** -0.5)).astype(BF16)

    @pl.when(j == 0)
    def _():
        kv_ref[0:blk, :] = jnp.zeros((blk, 2 * SWA_KVDIM), BF16)

    @pl.when(j != 0)
    def _():
        kv_ref[0:blk, :] = kv_ref[tm:tm + blk, :]

    kv_ref[blk:, :] = p[:, SWA_QDIM:].astype(BF16)

    for i in range(nblk):
        rows = slice(i * blk, (i + 1) * blk)
        table = jnp.where(j == 0, 0, 1) if i == 0 else 1

        def put_pair(pair, o, rows=rows):
            att_ref[rows, pair * dh2:(pair + 1) * dh2] = o.astype(att_ref.dtype)

        _band_attention(lambda pair, rows=rows: q_ref[rows, pair * dh2:(pair + 1) * dh2],
                        kv_ref[i * blk:(i + 2) * blk, :],
                        lambda h, table=table: bias_ref[table, h], sink_ref, put_pair)

    o_ref[...] = _proj_res_ln(att_ref, wout_ref, x_ref, g_ref, b_ref)


def _swa_layer(x, w_in, b_in, bias, sinks, w_out, g, b, batch, seq_len):
    n = x.shape[0]
    nblk = SWA_BLOCKS_PER_STEP
    tm = nblk * SWA_BLOCK
    steps = seq_len // tm
    resident = lambda a: pl.BlockSpec(a.shape, lambda s, j: (0,) * a.ndim, pipeline_mode=pl.Buffered(1))
    rows = pl.BlockSpec((tm, D_MODEL), lambda s, j: (s * steps + j, 0))
    return pl.pallas_call(
        functools.partial(_swa_layer_kernel, nblk=nblk),
        grid=(batch, steps),
        in_specs=[rows, resident(w_in), resident(b_in), resident(bias), resident(sinks), resident(w_out),
                  resident(g), resident(b)],
        out_specs=rows,
        out_shape=jax.ShapeDtypeStruct((n, D_MODEL), F32),
        scratch_shapes=[pltpu.VMEM((tm, SWA_QDIM), BF16), pltpu.VMEM((tm + SWA_BLOCK, 2 * SWA_KVDIM), BF16),
                        pltpu.VMEM((tm, SWA_QDIM), BF16)],
        compiler_params=_params("parallel", "arbitrary"),
        name="swa_layer",
    )(x, w_in, b_in, bias, sinks, w_out, g, b)


def _router_kernel(x_ref, wr_ref, route_ref, route_t_ref, cnt_ref, run_ref, before_ref, *, tm):
    @pl.when(pl.program_id(0) == 0)
    def _():
        run_ref[...] = jnp.zeros(run_ref.shape, F32)
        r = lax.broadcasted_iota(jnp.int32, (tm, tm), 0)
        c = lax.broadcasted_iota(jnp.int32, (tm, tm), 1)
        before_ref[...] = jnp.where(r > c, 1.0, 0.0).astype(BF16)

    logits = _dot3(x_ref[...], wr_ref[...])
    lane = lax.broadcasted_iota(jnp.int32, logits.shape, 1)
    lane_f = lane.astype(F32)
    lg = jnp.where(lane < N_EXPERTS, logits, NEG_BIG)
    m1 = jnp.max(lg, axis=-1, keepdims=True)
    i1 = jnp.min(jnp.where(lg == m1, lane_f, float(LANES)), axis=-1, keepdims=True)
    oh1 = lane_f == i1
    lg2 = jnp.where(oh1, NEG_BIG, lg)
    m2 = jnp.max(lg2, axis=-1, keepdims=True)
    i2 = jnp.min(jnp.where(lg2 == m2, lane_f, float(LANES)), axis=-1, keepdims=True)
    oh2 = lane_f == i2
    e = jnp.exp(m2 - m1)
    w0 = 1.0 / (1.0 + e)
    w1 = e * w0

    cnt = jnp.where(oh1, 1.0, 0.0) + jnp.where(oh2, 1.0, 0.0)
    excl = _dot(before_ref[...], cnt.astype(BF16)) + run_ref[...]
    rank0 = jnp.sum(jnp.where(oh1, excl, 0.0), axis=-1, keepdims=True)
    rank1 = jnp.sum(jnp.where(oh2, excl, 0.0), axis=-1, keepdims=True)
    run = run_ref[...] + jnp.sum(cnt, axis=0, keepdims=True)
    run_ref[...] = run
    cnt_ref[...] = run

    vals = (i1, i2, rank0, rank1, w0, w1)
    out = jnp.zeros(logits.shape, F32)
    for idx, val in enumerate(vals):
        out = jnp.where(lane == idx, val, out)
    route_ref[...] = out
    route_t_ref[...] = out.T[:SUBLANES, :]


def _router(x, w_router_padded):
    n = x.shape[0]
    tm = TM_ROUTER
    return pl.pallas_call(
        functools.partial(_router_kernel, tm=tm),
        grid=(n // tm,),
        in_specs=[pl.BlockSpec((tm, D_MODEL), lambda i: (i, 0)), pl.BlockSpec((D_MODEL, LANES), lambda i: (0, 0))],
        out_specs=[pl.BlockSpec((tm, LANES), lambda i: (i, 0)), pl.BlockSpec((SUBLANES, tm), lambda i: (0, i)),
                   pl.BlockSpec((1, LANES), lambda i: (0, 0))],
        out_shape=[jax.ShapeDtypeStruct((n, LANES), F32), jax.ShapeDtypeStruct((SUBLANES, n), F32),
                   jax.ShapeDtypeStruct((1, LANES), F32)],
        scratch_shapes=[pltpu.VMEM((1, LANES), F32), pltpu.VMEM((tm, tm), BF16)],
        compiler_params=_params("arbitrary"),
        name="moe_router",
    )(x, w_router_padded)


def _dispatch_kernel(pos0_ref, pos1_ref, meta_ref, x_ref, wsrc_ref, xs_hbm, wdst_ref, zero_ref, sem, pad_sem, *, td):
    i = pl.program_id(0)
    wdst_ref[...] = wsrc_ref[...].astype(BF16)

    def row_copy(t, dst):
        return pltpu.make_async_copy(x_ref.at[pl.ds(t, 1)], xs_hbm.at[pl.ds(dst, 1)], sem)

    def start(t, carry):
        row_copy(t, pos0_ref[0, 0, t]).start()
        row_copy(t, pos1_ref[0, 0, t]).start(priority=1)
        return carry

    lax.fori_loop(0, td, start, 0, unroll=DMA_UNROLL)

    @pl.when(i == pl.num_programs(0) - 1)
    def _():
        zero_ref[...] = jnp.zeros(zero_ref.shape, F32)

        zr = zero_ref.shape[0]

        def row_zero(dst):
            return pltpu.make_async_copy(zero_ref.at[pl.ds(0, 1)], xs_hbm.at[pl.ds(dst, 1)], pad_sem)

        def block_zero(blk):
            return pltpu.make_async_copy(zero_ref, xs_hbm.at[pl.ds(pl.multiple_of(blk * zr, zr), zr)], pad_sem)

        def zero_range(copy, lo, hi):
            lax.fori_loop(lo, hi, lambda r, carry: (copy(r).start(), carry)[1], 0)
            lax.fori_loop(lo, hi, lambda r, carry: (copy(0).wait(), carry)[1], 0)

        for e in range(N_EXPERTS):
            off, cnt, cnt_up, padded = (meta_ref[r, e] for r in range(4))
            zero_range(lambda r, off=off: row_zero(off + r), cnt, cnt_up)
            zero_range(block_zero, (off + cnt_up) // zr, (off + padded) // zr)

        used = meta_ref[0, N_EXPERTS - 1] + meta_ref[3, N_EXPERTS - 1]
        zero_range(block_zero, used // zr, xs_hbm.shape[0] // zr)

    for _ in range(2):
        pltpu.make_async_copy(x_ref, xs_hbm.at[pl.ds(0, td)], sem).wait()


def _dispatch(x, pos, meta, n_rows, to_bf16):
    n = x.shape[0]
    td = TD_DISPATCH
    pos_spec = pl.BlockSpec((1, 1, td), lambda i: (i, 0, 0), memory_space=pltpu.SMEM)
    (cast_spec,), (cast_shape,) = _cast_specs([to_bf16], n // td)
    return pl.pallas_call(
        functools.partial(_dispatch_kernel, td=td),
        grid=(n // td,),
        in_specs=[pos_spec, pos_spec, pl.BlockSpec(memory_space=pltpu.SMEM),
                  pl.BlockSpec((td, D_MODEL), lambda i: (i, 0)), cast_spec],
        out_specs=[pl.BlockSpec(memory_space=pl.ANY), cast_spec],
        out_shape=[jax.ShapeDtypeStruct((n_rows, D_MODEL), F32), cast_shape],
        scratch_shapes=[pltpu.VMEM((ZERO_ROWS, D_MODEL), F32), pltpu.SemaphoreType.DMA, pltpu.SemaphoreType.DMA],
        compiler_params=_params("arbitrary"),
        name="moe_dispatch",
    )(pos[0].reshape(n // td, 1, td), pos[1].reshape(n // td, 1, td), meta, x, to_bf16)


def _moe_kernel(we_ref, wb_ref, nv_ref, cnt_ref, *refs, n_sub, tm):
    xs_refs = refs[:n_sub]
    wg_ref, wu_ref, wd_ref, ys_hbm, acc_ref, xb_ref, sem = refs[n_sub:]
    j = pl.program_id(0)
    f = pl.program_id(1)
    last_f = pl.num_programs(1) - 1
    n_windows, n_tiles = cnt_ref[0], cnt_ref[1]

    def tile_write(slot, block):
        return pltpu.make_async_copy(acc_ref.at[slot], ys_hbm.at[pl.ds(pl.multiple_of(block * tm, tm), tm)],
                                     sem.at[slot])

    @pl.when(j < n_windows)
    def _():
        prev_tiles = jnp.where(j > 0, nv_ref[jnp.maximum(j - 1, 0)], 0)

        for s in range(n_sub):
            @pl.when(s < nv_ref[j])
            def _(s=s):
                @pl.when(f == 0)
                def _():
                    xb_ref[s] = xs_refs[s][...].astype(BF16)

                xb = xb_ref[s]
                h = (_silu(_dot(xb, wg_ref[0])) * _dot(xb, wu_ref[0])).astype(BF16)
                part = _dot(h, wd_ref[0])

                @pl.when(f == 0)
                def _():
                    @pl.when(s < prev_tiles)
                    def _():
                        tile_write(s, 0).wait()

                    acc_ref[s] = part

                @pl.when(f != 0)
                def _():
                    acc_ref[s] += part

                @pl.when(f == last_f)
                def _():
                    tile_write(s, wb_ref[j] + s).start()

            @pl.when((f == 0) & (s >= nv_ref[j]) & (s < prev_tiles))
            def _(s=s):
                tile_write(s, 0).wait()

        @pl.when((f == last_f) & (j == n_windows - 1))
        def _():
            for s in range(n_sub):
                @pl.when(s < nv_ref[j])
                def _(s=s):
                    tile_write(s, 0).wait()

            acc_ref[0] = jnp.zeros(acc_ref.shape[1:], F32)
            total_tiles = ys_hbm.shape[0] // tm
            lax.fori_loop(n_tiles, total_tiles, lambda t, c: (tile_write(0, t).start(), c)[1], 0)
            lax.fori_loop(n_tiles, total_tiles, lambda t, c: (tile_write(0, 0).wait(), c)[1], 0)


def _moe_experts(xs, wg, wu, wd, win_expert, win_block, win_tiles, counts):
    n_rows = xs.shape[0]
    tm, tf, n_sub = TM_MOE, TF_MOE, MOE_TILES_PER_WINDOW
    max_windows = win_expert.shape[0]

    def xs_spec(s):
        return pl.BlockSpec((tm, D_MODEL), lambda j, f, we, wb, nv, cnt: (wb[j] + jnp.minimum(s, nv[j] - 1), 0))

    grid_spec = pltpu.PrefetchScalarGridSpec(
        num_scalar_prefetch=4,
        grid=(max_windows, EXPERT_DIM // tf),
        in_specs=[xs_spec(s) for s in range(n_sub)] + [
            pl.BlockSpec((1, D_MODEL, tf), lambda j, f, we, wb, nv, cnt: (we[j], 0, f)),
            pl.BlockSpec((1, D_MODEL, tf), lambda j, f, we, wb, nv, cnt: (we[j], 0, f)),
            pl.BlockSpec((1, tf, D_MODEL), lambda j, f, we, wb, nv, cnt: (we[j], f, 0))],
        out_specs=pl.BlockSpec(memory_space=pl.ANY),
        scratch_shapes=[pltpu.VMEM((n_sub, tm, D_MODEL), F32), pltpu.VMEM((n_sub, tm, D_MODEL), BF16),
                        pltpu.SemaphoreType.DMA((n_sub,))],
    )
    return pl.pallas_call(
        functools.partial(_moe_kernel, n_sub=n_sub, tm=tm),
        grid_spec=grid_spec,
        out_shape=jax.ShapeDtypeStruct((n_rows, D_MODEL), F32),
        compiler_params=_params("arbitrary", "arbitrary"),
        name="moe_experts",
    )(win_expert, win_block, win_tiles, counts, *([xs] * n_sub), wg, wu, wd)


def _combine_kernel(pos0_ref, pos1_ref, pos0n_ref, pos1n_ref, x_ref, route_ref, ys_hbm, g_ref, b_ref, o_ref,
                    ya_ref, yb_ref, sem, *, tc):
    i = pl.program_id(0)
    is_last = i + 1 == pl.num_programs(0)

    def gather_rows(p_refs, y_ref, s, t0):
        for r in range(COMBINE_ROWS):
            for kk in range(2):
                pltpu.make_async_copy(ys_hbm.at[pl.ds(p_refs[kk][0, 0, t0 + r], 1)],
                                      y_ref.at[kk, pl.ds(t0 + r, 1)], sem.at[s]).start(priority=kk)

    def combine_rows(y_ref, t0):
        rows = pl.ds(pl.multiple_of(t0, COMBINE_ROWS), COMBINE_ROWS)
        route = route_ref[rows, :]
        f = route[:, 4:5] * y_ref[0, rows, :] + route[:, 5:6] * y_ref[1, rows, :]
        o_ref[rows, :] = _layer_norm(ALPHA * x_ref[rows, :] + f, g_ref[...], b_ref[...])

    def over_row_groups(body):
        lax.fori_loop(0, tc // COMBINE_ROWS, lambda grp, carry: (body(grp * COMBINE_ROWS), carry)[1], 0)

    @pl.when(i == 0)
    def _():
        over_row_groups(lambda t0: gather_rows((pos0_ref, pos1_ref), ya_ref, 0, t0))

    def step(cur_ref, cur_sem, nxt_ref, nxt_sem):
        for kk in range(2):
            pltpu.make_async_copy(ys_hbm.at[pl.ds(0, tc)], cur_ref.at[kk], sem.at[cur_sem]).wait()

        @pl.when(jnp.logical_not(is_last))
        def _():
            def body(t0):
                gather_rows((pos0n_ref, pos1n_ref), nxt_ref, nxt_sem, t0)
                combine_rows(cur_ref, t0)

            over_row_groups(body)

        @pl.when(is_last)
        def _():
            over_row_groups(lambda t0: combine_rows(cur_ref, t0))

    @pl.when(i % 2 == 0)
    def _():
        step(ya_ref, 0, yb_ref, 1)

    @pl.when(i % 2 == 1)
    def _():
        step(yb_ref, 1, ya_ref, 0)


def _combine_ln(x, route, pos, ys, g, b):
    n = x.shape[0]
    tc = TC_COMBINE
    nt = n // tc
    pos0, pos1 = pos[0].reshape(nt, 1, tc), pos[1].reshape(nt, 1, tc)
    cur = pl.BlockSpec((1, 1, tc), lambda i: (i, 0, 0), memory_space=pltpu.SMEM)
    nxt = pl.BlockSpec((1, 1, tc), lambda i: (jnp.minimum(i + 1, nt - 1), 0, 0), memory_space=pltpu.SMEM)
    return pl.pallas_call(
        functools.partial(_combine_kernel, tc=tc),
        grid=(nt,),
        in_specs=[cur, cur, nxt, nxt,
                  pl.BlockSpec((tc, D_MODEL), lambda i: (i, 0)),
                  pl.BlockSpec((tc, LANES), lambda i: (i, 0)),
                  pl.BlockSpec(memory_space=pl.ANY),
                  pl.BlockSpec((1, D_MODEL), lambda i: (0, 0)), pl.BlockSpec((1, D_MODEL), lambda i: (0, 0))],
        out_specs=pl.BlockSpec((tc, D_MODEL), lambda i: (i, 0)),
        out_shape=jax.ShapeDtypeStruct((n, D_MODEL), F32),
        scratch_shapes=[pltpu.VMEM((2, tc, D_MODEL), F32), pltpu.VMEM((2, tc, D_MODEL), F32),
                        pltpu.SemaphoreType.DMA((2,))],
        compiler_params=_params("arbitrary"),
        name="moe_combine_ln",
    )(pos0, pos1, pos0, pos1, x, route, ys, g, b)


def _moe_layer(x, route, route_t, counts_f, wg, wu, wd_f32, g, b):
    n = x.shape[0]
    tm = TM_MOE

    counts = counts_f[0, :N_EXPERTS].astype(jnp.int32)
    padded = ((counts + tm - 1) // tm) * tm
    offs = jnp.cumsum(padded) - padded
    eids = jnp.arange(N_EXPERTS, dtype=jnp.int32)
    e01 = route_t[0:2].astype(jnp.int32)
    rank01 = route_t[2:4].astype(jnp.int32)
    base = jnp.sum(jnp.where(e01[:, None, :] == eids[None, :, None], offs[None, :, None], 0), axis=1)
    pos = base + rank01
    counts_up = jnp.minimum(padded, ((counts + ZERO_ROWS - 1) // ZERO_ROWS) * ZERO_ROWS)
    meta = jnp.stack([offs, counts, counts_up, padded]).astype(jnp.int32)

    n_rows = 2 * n + N_EXPERTS * tm
    nsub = MOE_TILES_PER_WINDOW
    tiles = padded // tm
    wins = (tiles + nsub - 1) // nsub
    win_end = jnp.cumsum(wins)
    n_windows = win_end[-1]
    max_windows = (n_rows // tm + nsub - 1) // nsub + N_EXPERTS
    wj = jnp.minimum(jnp.arange(max_windows, dtype=jnp.int32), n_windows - 1)
    win_expert = jnp.minimum(jnp.sum((wj[:, None] >= win_end[None, :]).astype(jnp.int32), axis=-1), N_EXPERTS - 1)
    pick = lambda v: jnp.sum(jnp.where(win_expert[:, None] == eids[None, :], v[None, :], 0), axis=-1)
    local = wj - pick(win_end - wins)
    win_block = pick(offs // tm) + nsub * local
    win_tiles = jnp.minimum(nsub, pick(tiles) - nsub * local)
    counts_nt = jnp.stack([n_windows, jnp.sum(tiles)])

    xs, wd = _dispatch(x, pos, meta, n_rows, wd_f32.reshape(-1, D_MODEL))
    wd = wd.reshape(wd_f32.shape)
    i32 = lambda a: a.astype(jnp.int32)
    ys = _moe_experts(xs, wg, wu, wd, i32(win_expert), i32(win_block), i32(win_tiles), i32(counts_nt))
    return _combine_ln(x, route, pos, ys, g, b)


def kernel(x, a_w_in, a_conv_w, a_a_log, a_dt_bias, a_norm_w, a_w_out, b_w_in, b_b_in, b_sinks, b_w_out, rel_bias,
           ffn_w_gate, ffn_w_up, ffn_w_down, moe_router, moe_w_gate, moe_w_up, moe_w_down, ln_g, ln_b):
    batch, seq_len, _ = x.shape
    n = batch * seq_len
    x0 = x.reshape(n, D_MODEL)
    ln_g = ln_g.reshape(DEPTH, 2, 1, D_MODEL)
    ln_b = ln_b.reshape(DEPTH, 2, 1, D_MODEL)

    pad_gate = lambda p: jnp.pad(p.reshape(1, GDN_V_HEADS), ((0, 0), (GDN_V_HEADS, LANES - 2 * GDN_V_HEADS)))
    gate_params = jnp.concatenate([pad_gate(a_a_log[0]), pad_gate(a_dt_bias[0])], axis=0)
    q, k, v, zs, gb = _gdn_inproj(x0, a_w_in, a_conv_w[0], gate_params, seq_len)
    o, casted = _gdn_chunk(q, k, v, zs, gb, a_norm_w[0].reshape(1, GDN_HEAD), batch, seq_len,
                           [moe_w_gate[0].reshape(-1, EXPERT_DIM), moe_w_up[0].reshape(-1, EXPERT_DIM),
                            ffn_w_gate[0], ffn_w_up[0], ffn_w_down[0],
                            a_w_out[0], b_w_in[0], b_w_out[0]])
    moe_wg, moe_wu, ffn_wg, ffn_wu, ffn_wd, w_out_a, w_in_b, w_out_b = casted
    x2 = _outproj_ffn(o, w_out_a, x0, ln_g[0, 0], ln_b[0, 0],
                      ffn_wg, ffn_wu, ffn_wd, ln_g[0, 1], ln_b[0, 1])

    bias = _bias_table(rel_bias)
    sinks = jnp.pad(b_sinks[0].reshape(1, SWA_Q_HEADS), ((0, 0), (0, LANES - SWA_Q_HEADS)))
    x3 = _swa_layer(x2, w_in_b, b_b_in[0].reshape(1, -1), bias, sinks, w_out_b, ln_g[1, 0], ln_b[1, 0],
                    batch, seq_len)
    w_router = jnp.pad(moe_router[0], ((0, 0), (0, LANES - N_EXPERTS)))
    route, route_t, counts = _router(x3, w_router)
    expert_shape = (N_EXPERTS, D_MODEL, EXPERT_DIM)
    x4 = _moe_layer(x3, route, route_t, counts, moe_wg.reshape(expert_shape), moe_wu.reshape(expert_shape),
                    moe_w_down[0], ln_g[1, 1], ln_b[1, 1])
    return x4.reshape(batch, seq_len, D_MODEL)
```

```python
import functools
import math

import numpy as np
import jax
import jax.numpy as jnp
from jax import lax
from jax.experimental import pallas as pl
from jax.experimental.pallas import tpu as pltpu

F32 = jnp.float32
BF16 = jnp.bfloat16

D_MODEL = 1024
DEPTH = 2
ALPHA = (2.0 * DEPTH) ** 0.25
LN_EPS = 1e-5

GDN_K_HEADS = 4
GDN_V_HEADS = 8
GDN_HEAD = 128
GDN_KDIM = GDN_K_HEADS * GDN_HEAD
GDN_VDIM = GDN_V_HEADS * GDN_HEAD
GDN_CONV = 4
GDN_CHUNK = 64
GDN_QKV = 2 * GDN_KDIM + GDN_VDIM
GDN_EPS = 1e-6

SWA_Q_HEADS = 16
SWA_KV_HEADS = 2
SWA_GROUP = SWA_Q_HEADS // SWA_KV_HEADS
SWA_HEAD_DIM = 64
SWA_WINDOW = 128
SWA_BLOCK = 128
SWA_QDIM = SWA_Q_HEADS * SWA_HEAD_DIM
SWA_KVDIM = SWA_KV_HEADS * SWA_HEAD_DIM
REL_BUCKETS = 32
REL_MAX_DIST = 128

FFN_DIM = 2816
N_EXPERTS = 8
EXPERT_DIM = 3584

LANES = 128
SUBLANES = 8
NEG_BIG = -1e30
VMEM_LIMIT = 56 * 1024 * 1024

TM_GDN_IN = 512
TM_FFN = 512
SWA_BLOCKS_PER_STEP = 4
TM_ROUTER = 1024
TM_MOE = 512
TF_MOE = 1792
MOE_TILES_PER_WINDOW = 3
TD_DISPATCH = 1024
TC_COMBINE = 512
ZERO_ROWS = 64
DMA_UNROLL = 8


def _params(*sem):
    return pltpu.CompilerParams(dimension_semantics=sem, vmem_limit_bytes=VMEM_LIMIT)


def _dot(a, b):
    return jnp.dot(a, b, preferred_element_type=F32)


def _dot_nt(a, b):
    return lax.dot_general(a, b, (((1,), (1,)), ((), ())), preferred_element_type=F32)


def _dot_tn(a, b):
    return lax.dot_general(a, b, (((0,), (0,)), ((), ())), preferred_element_type=F32)


def _split(x):
    hi = x.astype(BF16)
    lo = (x - hi.astype(F32)).astype(BF16)
    return hi, lo


def _dot3(a, b):
    ah, al = _split(a)
    bh, bl = _split(b)
    return _dot(ah, bh) + (_dot(ah, bl) + _dot(al, bh))


def _silu(x):
    return x * jax.nn.sigmoid(x)


def _layer_norm(y, g, b):
    mu = jnp.mean(y, axis=-1, keepdims=True)
    yc = y - mu
    var = jnp.mean(yc * yc, axis=-1, keepdims=True)
    return yc * lax.rsqrt(var + LN_EPS) * g + b


def _gdn_inproj_kernel(x_ref, wf_ref, convw_ref, gp_ref,
                       q_ref, k_ref, v_ref, z_ref, gb_ref, ext_ref, w_ref, u_ref, *, tm, tiles_per_seq):
    i = pl.program_id(0)
    xb = x_ref[...].astype(BF16)

    @pl.when(i == 0)
    def _():
        n_in = wf_ref.shape[-1]
        w_ref[:, :n_in] = wf_ref[0].astype(BF16)
        w_ref[:, n_in:] = jnp.zeros((w_ref.shape[0], w_ref.shape[1] - n_in), BF16)

    halo = 2 * SUBLANES

    @pl.when(i % tiles_per_seq == 0)
    def _():
        ext_ref[0:halo, :] = jnp.zeros((halo, GDN_QKV), F32)

    @pl.when(i % tiles_per_seq != 0)
    def _():
        ext_ref[SUBLANES:halo, :] = ext_ref[tm + SUBLANES:tm + halo, :]

    ext_ref[halo:, :] = _dot(xb, w_ref[:, :GDN_QKV])
    z_ref[...] = _dot(xb, w_ref[:, GDN_QKV:GDN_QKV + GDN_VDIM])

    n_chunks = GDN_QKV // LANES
    for c in range(n_chunks):
        cs = slice(c * LANES, (c + 1) * LANES)
        p_t = ext_ref[SUBLANES:halo + tm, cs]
        p_tm1 = ext_ref[SUBLANES - 1:halo + tm - 1, cs]
        u_ref[...] = convw_ref[1:2, cs] * p_t + convw_ref[0:1, cs] * p_tm1
        acc = (convw_ref[3:4, cs] * p_t[SUBLANES:] + convw_ref[2:3, cs] * p_tm1[SUBLANES:]
               + u_ref[SUBLANES - 2:SUBLANES - 2 + tm, :])
        y = _silu(acc)
        if c < 2 * GDN_K_HEADS:
            y = y * lax.rsqrt(jnp.sum(y * y, axis=-1, keepdims=True) + GDN_EPS)
            if c < GDN_K_HEADS:
                q_ref[:, cs] = y * (GDN_HEAD ** -0.5)
            else:
                k_ref[:, (c - GDN_K_HEADS) * LANES:(c - GDN_K_HEADS + 1) * LANES] = y
        else:
            cv = c - 2 * GDN_K_HEADS
            v_ref[:, cv * LANES:(cv + 1) * LANES] = y

    ba = _dot(xb, w_ref[:, GDN_QKV + GDN_VDIM:])
    lane = lax.broadcasted_iota(jnp.int32, ba.shape, 1)
    sp = ba + gp_ref[1:2, :]
    softplus = jnp.maximum(sp, 0.0) + jnp.log(1.0 + jnp.exp(-jnp.abs(sp)))
    g = -jnp.exp(gp_ref[0:1, :]) * softplus
    gb_ref[...] = jnp.where(lane < GDN_V_HEADS, jax.nn.sigmoid(ba), g)


def _gdn_inproj(x2d, w_in, conv_w, gate_params, seq_len):
    n = x2d.shape[0]
    tm = TM_GDN_IN
    kern = functools.partial(_gdn_inproj_kernel, tm=tm, tiles_per_seq=seq_len // tm)
    full = lambda shape: pl.BlockSpec(shape, lambda i: (0,) * len(shape))
    rows = lambda width: pl.BlockSpec((tm, width), lambda i: (i, 0))
    return pl.pallas_call(
        kern,
        grid=(n // tm,),
        in_specs=[rows(D_MODEL), pl.BlockSpec(w_in.shape, lambda i: (0, 0, 0), pipeline_mode=pl.Buffered(1)),
                  full(conv_w.shape), full(gate_params.shape)],
        out_specs=[rows(GDN_KDIM), rows(GDN_KDIM), rows(GDN_VDIM), rows(GDN_VDIM), rows(LANES)],
        out_shape=[jax.ShapeDtypeStruct((n, GDN_KDIM), F32), jax.ShapeDtypeStruct((n, GDN_KDIM), F32),
                   jax.ShapeDtypeStruct((n, GDN_VDIM), F32), jax.ShapeDtypeStruct((n, GDN_VDIM), F32),
                   jax.ShapeDtypeStruct((n, LANES), F32)],
        scratch_shapes=[pltpu.VMEM((tm + 2 * SUBLANES, GDN_QKV), F32),
                        pltpu.VMEM((D_MODEL, GDN_QKV + GDN_VDIM + LANES), BF16),
                        pltpu.VMEM((tm + SUBLANES, LANES), F32)],
        compiler_params=_params("arbitrary"),
        name="gdn_inproj",
    )(x2d, w_in, conv_w, gate_params)


GDN_CHUNKS_PER_STEP = 2
NEUMANN_BLOCK = 4


def _bf16_all(xs):
    return [x.astype(BF16) for x in xs]


def _dot_all(a_list, b_list):
    return [_dot(a, b) for a, b in zip(a_list, b_list)]


def _unit_lower_inverse_all(a_list, row, col):
    shift = int(math.log2(NEUMANN_BLOCK))
    eye = jnp.where(row == col, 1.0, 0.0).astype(F32)
    on_diag_block = (row >> shift) == (col >> shift)
    d = [jnp.where(on_diag_block, a, 0.0) for a in a_list]
    d_b = _bf16_all(d)
    x = [eye - di for di in d]
    d2_b = _bf16_all(_dot_all(d_b, d_b))
    x = [xi + t for xi, t in zip(x, _dot_all(_bf16_all(x), d2_b))]
    size = NEUMANN_BLOCK
    while size < a_list[0].shape[0]:
        shift = int(math.log2(size))
        rbl, cbl = row >> shift, col >> shift
        below = ((rbl & 1) == 1) & (cbl == rbl - 1)
        l_b = _bf16_all([jnp.where(below, a, 0.0) for a in a_list])
        x_b = _bf16_all(x)
        xl_b = _bf16_all(_dot_all(x_b, l_b))
        x = [xi - t for xi, t in zip(x, _dot_all(xl_b, x_b))]
        size *= 2
    return x


def _gdn_chunk_kernel(q_ref, k_ref, v_ref, z_ref, gb_ref, nw_ref, *rest, nb, nck, n_cast):
    cast_src, (o_ref, *cast_dst), s_ref = rest[:n_cast], rest[n_cast:2 * n_cast + 1], rest[-1]
    for src, dst in zip(cast_src, cast_dst):
        dst[...] = src[...].astype(BF16)
    c = GDN_CHUNK
    nh = GDN_V_HEADS

    @pl.when(pl.program_id(0) == 0)
    def _():
        s_ref[...] = jnp.zeros(s_ref.shape, F32)

    row = lax.broadcasted_iota(jnp.int32, (c, c), 0)
    col = lax.broadcasted_iota(jnp.int32, (c, c), 1)
    causal = row >= col
    strict = row > col
    tril = jnp.where(causal, 1.0, 0.0).astype(BF16)
    rs = lambda ci: slice(ci * c, (ci + 1) * c)
    ks = lambda h: slice((h // (nh // GDN_K_HEADS)) * GDN_HEAD, (h // (nh // GDN_K_HEADS) + 1) * GDN_HEAD)
    vs = lambda h: slice(h * GDN_HEAD, (h + 1) * GDN_HEAD)

    gates = {}
    for ci in range(nck):
        for b in range(nb):
            gb = gb_ref[b, rs(ci), :]
            g_hi, g_lo = _split(gb)
            gc = _dot(tril, g_hi) + _dot(tril, g_lo)
            g_last = gc[c - 1:c, :]
            gates[ci, b] = dict(gb=gb, gc=gc, gc_t=gc.T, eg=jnp.exp(gc), e_last=jnp.exp(g_last),
                                e_rest=jnp.exp(g_last - gc))

    items = [(ci, b, h) for ci in range(nck) for b in range(nb) for h in range(nh)]

    a_kk, a_qk, k_beta = [], {}, {}
    for it in items:
        ci, b, h = it
        gt = gates[ci, b]
        k = k_ref[b, rs(ci), ks(h)]
        kb = k * gt["gb"][:, h:h + 1]
        kq = jnp.concatenate([kb, q_ref[b, rs(ci), ks(h)]], axis=0).astype(BF16)
        gram = _dot_nt(kq, k.astype(BF16))
        gl = nh + h
        decay = jnp.exp(jnp.where(causal, gt["gc"][:, gl:gl + 1] - gt["gc_t"][gl:gl + 1, :], NEG_BIG))
        a_kk.append(jnp.where(strict, gram[:c] * decay, 0.0))
        a_qk[it] = (gram[c:] * decay).astype(BF16)
        k_beta[it] = kb

    t_inv = dict(zip(items, _unit_lower_inverse_all(a_kk, row, col)))

    uw = {}
    for it in items:
        ci, b, h = it
        gt = gates[ci, b]
        gl = nh + h
        rhs = jnp.concatenate([v_ref[b, rs(ci), vs(h)] * gt["gb"][:, h:h + 1],
                               k_beta[it] * gt["eg"][:, gl:gl + 1]], axis=1)
        uw[it] = _dot(t_inv[it].astype(BF16), rhs.astype(BF16))

    for ci in range(nck):
        chunk_items = [(ci, b, h) for b in range(nb) for h in range(nh)]

        ws_qs, states = {}, {}
        for it in chunk_items:
            _, b, h = it
            gl = nh + h
            s = s_ref[b * nh + h]
            wq = jnp.concatenate([uw[it][:, GDN_HEAD:],
                                  q_ref[b, rs(ci), ks(h)] * gates[ci, b]["eg"][:, gl:gl + 1]], axis=0)
            ws_qs[it] = _dot(wq.astype(BF16), s.astype(BF16))
            states[it] = s

        for it in chunk_items:
            _, b, h = it
            gt = gates[ci, b]
            gl = nh + h
            v_new = (uw[it][:, :GDN_HEAD] - ws_qs[it][:c]).astype(BF16)
            o = ws_qs[it][c:] + _dot(a_qk[it], v_new)
            k_dec = (k_ref[b, rs(ci), ks(h)] * gt["e_rest"][:, gl:gl + 1]).astype(BF16)
            s_ref[b * nh + h] = states[it] * gt["e_last"][:, gl:gl + 1] + _dot_tn(k_dec, v_new)
            o = o * lax.rsqrt(jnp.mean(o * o, axis=-1, keepdims=True) + GDN_EPS) * nw_ref[...]
            o_ref[b, rs(ci), vs(h)] = (o * _silu(z_ref[b, rs(ci), vs(h)])).astype(o_ref.dtype)


def _cast_specs(arrays, n_steps):
    specs, shapes = [], []
    for a in arrays:
        span = next(d for d in (1, 2, 4, 8)
                    if (a.shape[0] * d) % n_steps == 0 and (a.shape[0] * d // n_steps) % (2 * SUBLANES) == 0)
        specs.append(pl.BlockSpec((a.shape[0] * span // n_steps, a.shape[1]), lambda j, span=span: (j // span, 0)))
        shapes.append(jax.ShapeDtypeStruct(a.shape, BF16))
    return specs, shapes


def _gdn_chunk(q, k, v, zs, gb, norm_w, batch, seq_len, to_bf16):
    rows_per_step = GDN_CHUNK * GDN_CHUNKS_PER_STEP
    n_steps = seq_len // rows_per_step
    seq = lambda a: a.reshape(batch, seq_len, a.shape[-1])
    rows = lambda width: pl.BlockSpec((batch, rows_per_step, width), lambda j: (0, j, 0))
    cast_specs, cast_shapes = _cast_specs(to_bf16, n_steps)
    out, *casted = pl.pallas_call(
        functools.partial(_gdn_chunk_kernel, nb=batch, nck=GDN_CHUNKS_PER_STEP, n_cast=len(to_bf16)),
        grid=(n_steps,),
        in_specs=[rows(GDN_KDIM), rows(GDN_KDIM), rows(GDN_VDIM), rows(GDN_VDIM), rows(LANES),
                  pl.BlockSpec((1, GDN_HEAD), lambda j: (0, 0))] + cast_specs,
        out_specs=[rows(GDN_VDIM)] + cast_specs,
        out_shape=[jax.ShapeDtypeStruct((batch, seq_len, GDN_VDIM), BF16)] + cast_shapes,
        scratch_shapes=[pltpu.VMEM((batch * GDN_V_HEADS, GDN_HEAD, GDN_HEAD), F32)],
        compiler_params=_params("arbitrary"),
        name="gdn_chunk",
    )(seq(q), seq(k), seq(v), seq(zs), seq(gb), norm_w, *to_bf16)
    return out.reshape(batch * seq_len, GDN_VDIM), casted


def _proj_res_ln(a_ref, w_ref, r_ref, g_ref, b_ref):
    return _layer_norm(ALPHA * r_ref[...] + _dot(a_ref[...], w_ref[...]), g_ref[...], b_ref[...])


def _outproj_ffn_kernel(a_ref, wo_ref, r_ref, g0_ref, b0_ref, wg_ref, wu_ref, wd_ref, g1_ref, b1_ref, o_ref):
    x1 = _proj_res_ln(a_ref, wo_ref, r_ref, g0_ref, b0_ref)
    xb = x1.astype(BF16)
    h = (_silu(_dot(xb, wg_ref[...])) * _dot(xb, wu_ref[...])).astype(BF16)
    o_ref[...] = _layer_norm(ALPHA * x1 + _dot(h, wd_ref[...]), g1_ref[...], b1_ref[...])


def _outproj_ffn(a, w_out, res, g0, b0, wg, wu, wd, g1, b1):
    n, kdim = a.shape
    tm = TM_FFN
    resident = lambda w: pl.BlockSpec(w.shape, lambda i: (0, 0), pipeline_mode=pl.Buffered(1))
    rows = lambda width: pl.BlockSpec((tm, width), lambda i: (i, 0))
    vec = pl.BlockSpec((1, D_MODEL), lambda i: (0, 0))
    return pl.pallas_call(
        _outproj_ffn_kernel,
        grid=(n // tm,),
        in_specs=[rows(kdim), resident(w_out), rows(D_MODEL), vec, vec,
                  resident(wg), resident(wu), resident(wd), vec, vec],
        out_specs=rows(D_MODEL),
        out_shape=jax.ShapeDtypeStruct((n, D_MODEL), F32),
        compiler_params=_params("parallel"),
        name="gdn_outproj_ffn",
    )(a, w_out, res, g0, b0, wg, wu, wd, g1, b1)


def _band_tables():
    assert SWA_WINDOW == SWA_BLOCK
    r = np.arange(SWA_BLOCK)[:, None]
    q = np.arange(SWA_BLOCK)[None, :]
    d = (q - r) % SWA_BLOCK
    max_exact = REL_BUCKETS // 2
    df = np.maximum(d, 1).astype(np.float32)
    large = max_exact + (np.log(df / np.float32(max_exact)) / np.float32(math.log(REL_MAX_DIST / max_exact))
                         * np.float32(REL_BUCKETS - max_exact)).astype(np.int32)
    large = np.minimum(large, REL_BUCKETS - 1)
    bucket = np.where(d < max_exact, d, large).astype(np.int32)
    valid = np.stack([q >= r, np.ones_like(q >= r)]).astype(np.int32)
    return bucket, valid


def _bias_kernel(relb_ref, bucket_ref, valid_ref, o_ref):
    bucket = bucket_ref[...]
    in_bucket = [bucket == b for b in range(REL_BUCKETS)]
    for h in range(SWA_Q_HEADS):
        acc = jnp.zeros(bucket.shape, F32)
        for b in range(REL_BUCKETS):
            acc = jnp.where(in_bucket[b], relb_ref[b, h], acc)
        for t in range(2):
            o_ref[t, h] = jnp.where(valid_ref[t] != 0, acc, NEG_BIG)


def _bias_table(rel_bias):
    bucket, valid = _band_tables()
    shape = (SWA_BLOCK, SWA_BLOCK)
    return pl.pallas_call(
        _bias_kernel,
        grid=(1,),
        in_specs=[pl.BlockSpec(memory_space=pltpu.SMEM), pl.BlockSpec(shape, lambda i: (0, 0)),
                  pl.BlockSpec((2,) + shape, lambda i: (0, 0, 0))],
        out_specs=pl.BlockSpec((2, SWA_Q_HEADS) + shape, lambda i: (0, 0, 0, 0)),
        out_shape=jax.ShapeDtypeStruct((2, SWA_Q_HEADS) + shape, F32),
        compiler_params=_params("arbitrary"),
        name="swa_bias_table",
    )(rel_bias, jnp.asarray(bucket), jnp.asarray(valid))


def _band_attention(q_pair, kv, bias_of, sink_ref, put_pair):
    dh = SWA_HEAD_DIM
    blk = SWA_BLOCK
    nk = 2 * blk
    from_prev = lax.broadcasted_iota(jnp.int32, (blk, blk), 1) < lax.broadcasted_iota(jnp.int32, (blk, blk), 0)
    k_all = kv[:, :SWA_KVDIM]
    lane = lax.broadcasted_iota(jnp.int32, k_all.shape, 1)
    zero = jnp.zeros_like(k_all)
    for kh in range(SWA_KV_HEADS):
        k_own = jnp.where((lane >= kh * dh) & (lane < (kh + 1) * dh), k_all, zero)
        k_other = pltpu.roll(k_own.astype(F32), dh, axis=1).astype(BF16)
        k_even, k_odd = (k_own, k_other) if kh == 0 else (k_other, k_own)
        k2 = jnp.concatenate([k_even, k_odd], axis=0)
        v_t = kv[:, SWA_KVDIM + kh * dh:SWA_KVDIM + (kh + 1) * dh].T
        pairs = [kh * (SWA_GROUP // 2) + p for p in range(SWA_GROUP // 2)]
        heads = [2 * p + r for p in pairs for r in range(2)]
        st = [_dot_nt(k2, q_pair(p)) for p in pairs]
        scores = [jnp.where(from_prev, st[i // 2][(i % 2) * nk:(i % 2) * nk + blk],
                            st[i // 2][(i % 2) * nk + blk:(i % 2 + 1) * nk]) + bias_of(h)
                  for i, h in enumerate(heads)]
        sinks = [sink_ref[0:1, h:h + 1] for h in heads]
        maxes = [jnp.maximum(jnp.max(s, axis=0, keepdims=True), sk) for s, sk in zip(scores, sinks)]
        expo = [jnp.exp(s - m) for s, m in zip(scores, maxes)]
        sums = [jnp.sum(p, axis=0, keepdims=True) for p in expo]
        inv_den = [1.0 / (t + jnp.exp(sk - m)) for t, sk, m in zip(sums, sinks, maxes)]
        expo = [jnp.concatenate([jnp.where(from_prev, p, 0.0), jnp.where(from_prev, 0.0, p)], axis=0) for p in expo]
        for i, p in enumerate(pairs):
            o_t = _dot(v_t, jnp.concatenate(expo[2 * i:2 * i + 2], axis=1).astype(BF16))
            z = jnp.concatenate([o_t[:, :SWA_BLOCK] * inv_den[2 * i], o_t[:, SWA_BLOCK:] * inv_den[2 * i + 1]],
                                axis=0)
            put_pair(p, z.T)


def _swa_layer_kernel(x_ref, win_ref, bin_ref, bias_ref, sink_ref, wout_ref, g_ref, b_ref, o_ref,
                      q_ref, kv_ref, att_ref, *, nblk):
    j = pl.program_id(1)
    blk = SWA_BLOCK
    tm = nblk * blk
    dh2 = 2 * SWA_HEAD_DIM

    p = _dot(x_ref[...].astype(BF16), win_ref[...]) + bin_ref[...]
    q_ref[...] = (p[:, :SWA_QDIM] * (SWA_HEAD_DIM ** -0.5)).astype(BF16)

    @pl.when(j == 0)
    def _():
        kv_ref[0:blk, :] = jnp.zeros((blk, 2 * SWA_KVDIM), BF16)

    @pl.when(j != 0)
    def _():
        kv_ref[0:blk, :] = kv_ref[tm:tm + blk, :]

    kv_ref[blk:, :] = p[:, SWA_QDIM:].astype(BF16)

    for i in range(nblk):
        rows = slice(i * blk, (i + 1) * blk)
        table = jnp.where(j == 0, 0, 1) if i == 0 else 1

        def put_pair(pair, o, rows=rows):
            att_ref[rows, pair * dh2:(pair + 1) * dh2] = o.astype(att_ref.dtype)

        _band_attention(lambda pair, rows=rows: q_ref[rows, pair * dh2:(pair + 1) * dh2],
                        kv_ref[i * blk:(i + 2) * blk, :],
                        lambda h, table=table: bias_ref[table, h], sink_ref, put_pair)

    o_ref[...] = _proj_res_ln(att_ref, wout_ref, x_ref, g_ref, b_ref)


def _swa_layer(x, w_in, b_in, bias, sinks, w_out, g, b, batch, seq_len):
    n = x.shape[0]
    nblk = SWA_BLOCKS_PER_STEP
    tm = nblk * SWA_BLOCK
    steps = seq_len // tm
    resident = lambda a: pl.BlockSpec(a.shape, lambda s, j: (0,) * a.ndim, pipeline_mode=pl.Buffered(1))
    rows = pl.BlockSpec((tm, D_MODEL), lambda s, j: (s * steps + j, 0))
    return pl.pallas_call(
        functools.partial(_swa_layer_kernel, nblk=nblk),
        grid=(batch, steps),
        in_specs=[rows, resident(w_in), resident(b_in), resident(bias), resident(sinks), resident(w_out),
                  resident(g), resident(b)],
        out_specs=rows,
        out_shape=jax.ShapeDtypeStruct((n, D_MODEL), F32),
        scratch_shapes=[pltpu.VMEM((tm, SWA_QDIM), BF16), pltpu.VMEM((tm + SWA_BLOCK, 2 * SWA_KVDIM), BF16),
                        pltpu.VMEM((tm, SWA_QDIM), BF16)],
        compiler_params=_params("parallel", "arbitrary"),
        name="swa_layer",
    )(x, w_in, b_in, bias, sinks, w_out, g, b)


def _router_kernel(x_ref, wr_ref, route_ref, route_t_ref, cnt_ref, run_ref, *, tm):
    @pl.when(pl.program_id(0) == 0)
    def _():
        run_ref[...] = jnp.zeros(run_ref.shape, F32)

    logits = _dot3(x_ref[...], wr_ref[...])
    lane = lax.broadcasted_iota(jnp.int32, logits.shape, 1)
    lane_f = lane.astype(F32)
    lg = jnp.where(lane < N_EXPERTS, logits, NEG_BIG)
    m1 = jnp.max(lg, axis=-1, keepdims=True)
    i1 = jnp.min(jnp.where(lg == m1, lane_f, float(LANES)), axis=-1, keepdims=True)
    oh1 = lane_f == i1
    lg2 = jnp.where(oh1, NEG_BIG, lg)
    m2 = jnp.max(lg2, axis=-1, keepdims=True)
    i2 = jnp.min(jnp.where(lg2 == m2, lane_f, float(LANES)), axis=-1, keepdims=True)
    oh2 = lane_f == i2
    e = jnp.exp(m2 - m1)
    w0 = 1.0 / (1.0 + e)
    w1 = e * w0

    cnt = jnp.where(oh1, 1.0, 0.0) + jnp.where(oh2, 1.0, 0.0)
    r = lax.broadcasted_iota(jnp.int32, (tm, tm), 0)
    c = lax.broadcasted_iota(jnp.int32, (tm, tm), 1)
    before = jnp.where(r > c, 1.0, 0.0).astype(BF16)
    excl = _dot(before, cnt.astype(BF16)) + run_ref[...]
    rank0 = jnp.sum(jnp.where(oh1, excl, 0.0), axis=-1, keepdims=True)
    rank1 = jnp.sum(jnp.where(oh2, excl, 0.0), axis=-1, keepdims=True)
    run = run_ref[...] + jnp.sum(cnt, axis=0, keepdims=True)
    run_ref[...] = run
    cnt_ref[...] = run

    vals = (i1, i2, rank0, rank1, w0, w1)
    out = jnp.zeros(logits.shape, F32)
    for idx, val in enumerate(vals):
        out = jnp.where(lane == idx, val, out)
    route_ref[...] = out
    route_t_ref[...] = out.T[:SUBLANES, :]


def _router(x, w_router_padded):
    n = x.shape[0]
    tm = TM_ROUTER
    return pl.pallas_call(
        functools.partial(_router_kernel, tm=tm),
        grid=(n // tm,),
        in_specs=[pl.BlockSpec((tm, D_MODEL), lambda i: (i, 0)), pl.BlockSpec((D_MODEL, LANES), lambda i: (0, 0))],
        out_specs=[pl.BlockSpec((tm, LANES), lambda i: (i, 0)), pl.BlockSpec((SUBLANES, tm), lambda i: (0, i)),
                   pl.BlockSpec((1, LANES), lambda i: (0, 0))],
        out_shape=[jax.ShapeDtypeStruct((n, LANES), F32), jax.ShapeDtypeStruct((SUBLANES, n), F32),
                   jax.ShapeDtypeStruct((1, LANES), F32)],
        scratch_shapes=[pltpu.VMEM((1, LANES), F32)],
        compiler_params=_params("arbitrary"),
        name="moe_router",
    )(x, w_router_padded)


def _dispatch_kernel(pos0_ref, pos1_ref, meta_ref, x_ref, wsrc_ref, xs_hbm, wdst_ref, zero_ref, sem, pad_sem, *, td):
    i = pl.program_id(0)
    wdst_ref[...] = wsrc_ref[...].astype(BF16)

    def row_copy(t, dst):
        return pltpu.make_async_copy(x_ref.at[pl.ds(t, 1)], xs_hbm.at[pl.ds(dst, 1)], sem)

    def start(t, carry):
        row_copy(t, pos0_ref[0, 0, t]).start()
        row_copy(t, pos1_ref[0, 0, t]).start(priority=1)
        return carry

    lax.fori_loop(0, td, start, 0, unroll=DMA_UNROLL)

    @pl.when(i == pl.num_programs(0) - 1)
    def _():
        zero_ref[...] = jnp.zeros(zero_ref.shape, F32)

        zr = zero_ref.shape[0]

        def row_zero(dst):
            return pltpu.make_async_copy(zero_ref.at[pl.ds(0, 1)], xs_hbm.at[pl.ds(dst, 1)], pad_sem)

        def block_zero(blk):
            return pltpu.make_async_copy(zero_ref, xs_hbm.at[pl.ds(pl.multiple_of(blk * zr, zr), zr)], pad_sem)

        def zero_range(copy, lo, hi):
            lax.fori_loop(lo, hi, lambda r, carry: (copy(r).start(), carry)[1], 0)
            lax.fori_loop(lo, hi, lambda r, carry: (copy(0).wait(), carry)[1], 0)

        for e in range(N_EXPERTS):
            off, cnt, cnt_up, padded = (meta_ref[r, e] for r in range(4))
            zero_range(lambda r, off=off: row_zero(off + r), cnt, cnt_up)
            zero_range(block_zero, (off + cnt_up) // zr, (off + padded) // zr)

        used = meta_ref[0, N_EXPERTS - 1] + meta_ref[3, N_EXPERTS - 1]
        zero_range(block_zero, used // zr, xs_hbm.shape[0] // zr)

    for _ in range(2):
        pltpu.make_async_copy(x_ref, xs_hbm.at[pl.ds(0, td)], sem).wait()


def _dispatch(x, pos, meta, n_rows, to_bf16):
    n = x.shape[0]
    td = TD_DISPATCH
    pos_spec = pl.BlockSpec((1, 1, td), lambda i: (i, 0, 0), memory_space=pltpu.SMEM)
    (cast_spec,), (cast_shape,) = _cast_specs([to_bf16], n // td)
    return pl.pallas_call(
        functools.partial(_dispatch_kernel, td=td),
        grid=(n // td,),
        in_specs=[pos_spec, pos_spec, pl.BlockSpec(memory_space=pltpu.SMEM),
                  pl.BlockSpec((td, D_MODEL), lambda i: (i, 0)), cast_spec],
        out_specs=[pl.BlockSpec(memory_space=pl.ANY), cast_spec],
        out_shape=[jax.ShapeDtypeStruct((n_rows, D_MODEL), F32), cast_shape],
        scratch_shapes=[pltpu.VMEM((ZERO_ROWS, D_MODEL), F32), pltpu.SemaphoreType.DMA, pltpu.SemaphoreType.DMA],
        compiler_params=_params("arbitrary"),
        name="moe_dispatch",
    )(pos[0].reshape(n // td, 1, td), pos[1].reshape(n // td, 1, td), meta, x, to_bf16)


def _moe_kernel(we_ref, wb_ref, nv_ref, cnt_ref, *refs, n_sub, tm):
    xs_refs = refs[:n_sub]
    wg_ref, wu_ref, wd_ref, ys_hbm, acc_ref, xb_ref, sem = refs[n_sub:]
    j = pl.program_id(0)
    f = pl.program_id(1)
    last_f = pl.num_programs(1) - 1
    n_windows, n_tiles = cnt_ref[0], cnt_ref[1]

    def tile_write(slot, block):
        return pltpu.make_async_copy(acc_ref.at[slot], ys_hbm.at[pl.ds(pl.multiple_of(block * tm, tm), tm)],
                                     sem.at[slot])

    @pl.when(j < n_windows)
    def _():
        prev_tiles = jnp.where(j > 0, nv_ref[jnp.maximum(j - 1, 0)], 0)

        for s in range(n_sub):
            @pl.when(s < nv_ref[j])
            def _(s=s):
                @pl.when(f == 0)
                def _():
                    xb_ref[s] = xs_refs[s][...].astype(BF16)

                xb = xb_ref[s]
                h = (_silu(_dot(xb, wg_ref[0])) * _dot(xb, wu_ref[0])).astype(BF16)
                part = _dot(h, wd_ref[0])

                @pl.when(f == 0)
                def _():
                    @pl.when(s < prev_tiles)
                    def _():
                        tile_write(s, 0).wait()

                    acc_ref[s] = part

                @pl.when(f != 0)
                def _():
                    acc_ref[s] += part

                @pl.when(f == last_f)
                def _():
                    tile_write(s, wb_ref[j] + s).start()

            @pl.when((f == 0) & (s >= nv_ref[j]) & (s < prev_tiles))
            def _(s=s):
                tile_write(s, 0).wait()

        @pl.when((f == last_f) & (j == n_windows - 1))
        def _():
            for s in range(n_sub):
                @pl.when(s < nv_ref[j])
                def _(s=s):
                    tile_write(s, 0).wait()

            acc_ref[0] = jnp.zeros(acc_ref.shape[1:], F32)
            total_tiles = ys_hbm.shape[0] // tm
            lax.fori_loop(n_tiles, total_tiles, lambda t, c: (tile_write(0, t).start(), c)[1], 0)
            lax.fori_loop(n_tiles, total_tiles, lambda t, c: (tile_write(0, 0).wait(), c)[1], 0)


def _moe_experts(xs, wg, wu, wd, win_expert, win_block, win_tiles, counts):
    n_rows = xs.shape[0]
    tm, tf, n_sub = TM_MOE, TF_MOE, MOE_TILES_PER_WINDOW
    max_windows = win_expert.shape[0]

    def xs_spec(s):
        return pl.BlockSpec((tm, D_MODEL), lambda j, f, we, wb, nv, cnt: (wb[j] + jnp.minimum(s, nv[j] - 1), 0))

    grid_spec = pltpu.PrefetchScalarGridSpec(
        num_scalar_prefetch=4,
        grid=(max_windows, EXPERT_DIM // tf),
        in_specs=[xs_spec(s) for s in range(n_sub)] + [
            pl.BlockSpec((1, D_MODEL, tf), lambda j, f, we, wb, nv, cnt: (we[j], 0, f)),
            pl.BlockSpec((1, D_MODEL, tf), lambda j, f, we, wb, nv, cnt: (we[j], 0, f)),
            pl.BlockSpec((1, tf, D_MODEL), lambda j, f, we, wb, nv, cnt: (we[j], f, 0))],
        out_specs=pl.BlockSpec(memory_space=pl.ANY),
        scratch_shapes=[pltpu.VMEM((n_sub, tm, D_MODEL), F32), pltpu.VMEM((n_sub, tm, D_MODEL), BF16),
                        pltpu.SemaphoreType.DMA((n_sub,))],
    )
    return pl.pallas_call(
        functools.partial(_moe_kernel, n_sub=n_sub, tm=tm),
        grid_spec=grid_spec,
        out_shape=jax.ShapeDtypeStruct((n_rows, D_MODEL), F32),
        compiler_params=_params("arbitrary", "arbitrary"),
        name="moe_experts",
    )(win_expert, win_block, win_tiles, counts, *([xs] * n_sub), wg, wu, wd)


def _combine_kernel(pos0_ref, pos1_ref, pos0n_ref, pos1n_ref, x_ref, route_ref, ys_hbm, g_ref, b_ref, o_ref,
                    y_ref, sem, *, tc):
    i = pl.program_id(0)
    slot = i % 2

    def issue(p_refs, s):
        def start(t, carry):
            for kk in range(2):
                pltpu.make_async_copy(ys_hbm.at[pl.ds(p_refs[kk][0, 0, t], 1)],
                                      y_ref.at[s, kk, pl.ds(t, 1)], sem.at[s]).start(priority=kk)
            return carry

        lax.fori_loop(0, tc, start, 0, unroll=DMA_UNROLL)

    @pl.when(i == 0)
    def _():
        issue((pos0_ref, pos1_ref), 0)

    @pl.when(i + 1 < pl.num_programs(0))
    def _():
        issue((pos0n_ref, pos1n_ref), 1 - slot)

    for kk in range(2):
        pltpu.make_async_copy(ys_hbm.at[pl.ds(0, tc)], y_ref.at[slot, kk], sem.at[slot]).wait()

    route = route_ref[...]
    f = route[:, 4:5] * y_ref[slot, 0] + route[:, 5:6] * y_ref[slot, 1]
    o_ref[...] = _layer_norm(ALPHA * x_ref[...] + f, g_ref[...], b_ref[...])


def _combine_ln(x, route, pos, ys, g, b):
    n = x.shape[0]
    tc = TC_COMBINE
    nt = n // tc
    pos0, pos1 = pos[0].reshape(nt, 1, tc), pos[1].reshape(nt, 1, tc)
    cur = pl.BlockSpec((1, 1, tc), lambda i: (i, 0, 0), memory_space=pltpu.SMEM)
    nxt = pl.BlockSpec((1, 1, tc), lambda i: (jnp.minimum(i + 1, nt - 1), 0, 0), memory_space=pltpu.SMEM)
    return pl.pallas_call(
        functools.partial(_combine_kernel, tc=tc),
        grid=(nt,),
        in_specs=[cur, cur, nxt, nxt,
                  pl.BlockSpec((tc, D_MODEL), lambda i: (i, 0)),
                  pl.BlockSpec((tc, LANES), lambda i: (i, 0)),
                  pl.BlockSpec(memory_space=pl.ANY),
                  pl.BlockSpec((1, D_MODEL), lambda i: (0, 0)), pl.BlockSpec((1, D_MODEL), lambda i: (0, 0))],
        out_specs=pl.BlockSpec((tc, D_MODEL), lambda i: (i, 0)),
        out_shape=jax.ShapeDtypeStruct((n, D_MODEL), F32),
        scratch_shapes=[pltpu.VMEM((2, 2, tc, D_MODEL), F32), pltpu.SemaphoreType.DMA((2,))],
        compiler_params=_params("arbitrary"),
        name="moe_combine_ln",
    )(pos0, pos1, pos0, pos1, x, route, ys, g, b)


def _moe_layer(x, route, route_t, counts_f, wg, wu, wd_f32, g, b):
    n = x.shape[0]
    tm = TM_MOE

    counts = counts_f[0, :N_EXPERTS].astype(jnp.int32)
    padded = ((counts + tm - 1) // tm) * tm
    offs = jnp.cumsum(padded) - padded
    eids = jnp.arange(N_EXPERTS, dtype=jnp.int32)
    e01 = route_t[0:2].astype(jnp.int32)
    rank01 = route_t[2:4].astype(jnp.int32)
    base = jnp.sum(jnp.where(e01[:, None, :] == eids[None, :, None], offs[None, :, None], 0), axis=1)
    pos = base + rank01
    counts_up = jnp.minimum(padded, ((counts + ZERO_ROWS - 1) // ZERO_ROWS) * ZERO_ROWS)
    meta = jnp.stack([offs, counts, counts_up, padded]).astype(jnp.int32)

    n_rows = 2 * n + N_EXPERTS * tm
    nsub = MOE_TILES_PER_WINDOW
    tiles = padded // tm
    wins = (tiles + nsub - 1) // nsub
    win_end = jnp.cumsum(wins)
    n_windows = win_end[-1]
    max_windows = (n_rows // tm + nsub - 1) // nsub + N_EXPERTS
    wj = jnp.minimum(jnp.arange(max_windows, dtype=jnp.int32), n_windows - 1)
    win_expert = jnp.minimum(jnp.sum((wj[:, None] >= win_end[None, :]).astype(jnp.int32), axis=-1), N_EXPERTS - 1)
    pick = lambda v: jnp.sum(jnp.where(win_expert[:, None] == eids[None, :], v[None, :], 0), axis=-1)
    local = wj - pick(win_end - wins)
    win_block = pick(offs // tm) + nsub * local
    win_tiles = jnp.minimum(nsub, pick(tiles) - nsub * local)
    counts_nt = jnp.stack([n_windows, jnp.sum(tiles)])

    xs, wd = _dispatch(x, pos, meta, n_rows, wd_f32.reshape(-1, D_MODEL))
    wd = wd.reshape(wd_f32.shape)
    i32 = lambda a: a.astype(jnp.int32)
    ys = _moe_experts(xs, wg, wu, wd, i32(win_expert), i32(win_block), i32(win_tiles), i32(counts_nt))
    return _combine_ln(x, route, pos, ys, g, b)


def kernel(x, a_w_in, a_conv_w, a_a_log, a_dt_bias, a_norm_w, a_w_out, b_w_in, b_b_in, b_sinks, b_w_out, rel_bias,
           ffn_w_gate, ffn_w_up, ffn_w_down, moe_router, moe_w_gate, moe_w_up, moe_w_down, ln_g, ln_b):
    batch, seq_len, _ = x.shape
    n = batch * seq_len
    x0 = x.reshape(n, D_MODEL)
    ln_g = ln_g.reshape(DEPTH, 2, 1, D_MODEL)
    ln_b = ln_b.reshape(DEPTH, 2, 1, D_MODEL)

    pad_gate = lambda p: jnp.pad(p.reshape(1, GDN_V_HEADS), ((0, 0), (GDN_V_HEADS, LANES - 2 * GDN_V_HEADS)))
    gate_params = jnp.concatenate([pad_gate(a_a_log[0]), pad_gate(a_dt_bias[0])], axis=0)
    q, k, v, zs, gb = _gdn_inproj(x0, a_w_in, a_conv_w[0], gate_params, seq_len)
    o, casted = _gdn_chunk(q, k, v, zs, gb, a_norm_w[0].reshape(1, GDN_HEAD), batch, seq_len,
                           [moe_w_gate[0].reshape(-1, EXPERT_DIM), moe_w_up[0].reshape(-1, EXPERT_DIM),
                            ffn_w_gate[0], ffn_w_up[0], ffn_w_down[0],
                            a_w_out[0], b_w_in[0], b_w_out[0]])
    moe_wg, moe_wu, ffn_wg, ffn_wu, ffn_wd, w_out_a, w_in_b, w_out_b = casted
    x2 = _outproj_ffn(o, w_out_a, x0, ln_g[0, 0], ln_b[0, 0],
                      ffn_wg, ffn_wu, ffn_wd, ln_g[0, 1], ln_b[0, 1])

    bias = _bias_table(rel_bias)
    sinks = jnp.pad(b_sinks[0].reshape(1, SWA_Q_HEADS), ((0, 0), (0, LANES - SWA_Q_HEADS)))
    x3 = _swa_layer(x2, w_in_b, b_b_in[0].reshape(1, -1), bias, sinks, w_out_b, ln_g[1, 0], ln_b[1, 0],
                    batch, seq_len)
    w_router = jnp.pad(moe_router[0], ((0, 0), (0, LANES - N_EXPERTS)))
    route, route_t, counts = _router(x3, w_router)
    expert_shape = (N_EXPERTS, D_MODEL, EXPERT_DIM)
    x4 = _moe_layer(x3, route, route_t, counts, moe_wg.reshape(expert_shape), moe_wu.reshape(expert_shape),
                    moe_w_down[0], ln_g[1, 1], ln_b[1, 1])
    return x4.reshape(batch, seq_len, D_MODEL)
```

```python
import functools
import math

import numpy as np
import jax
import jax.numpy as jnp
from jax import lax
from jax.experimental import pallas as pl
from jax.experimental.pallas import tpu as pltpu

F32 = jnp.float32
BF16 = jnp.bfloat16

D_MODEL = 1024
DEPTH = 2
ALPHA = (2.0 * DEPTH) ** 0.25
LN_EPS = 1e-5

GDN_K_HEADS = 4
GDN_V_HEADS = 8
GDN_HEAD = 128
GDN_KDIM = GDN_K_HEADS * GDN_HEAD
GDN_VDIM = GDN_V_HEADS * GDN_HEAD
GDN_CONV = 4
GDN_CHUNK = 64
GDN_QKV = 2 * GDN_KDIM + GDN_VDIM
GDN_EPS = 1e-6

SWA_Q_HEADS = 16
SWA_KV_HEADS = 2
SWA_GROUP = SWA_Q_HEADS // SWA_KV_HEADS
SWA_HEAD_DIM = 64
SWA_WINDOW = 128
SWA_BLOCK = 128
SWA_QDIM = SWA_Q_HEADS * SWA_HEAD_DIM
SWA_KVDIM = SWA_KV_HEADS * SWA_HEAD_DIM
REL_BUCKETS = 32
REL_MAX_DIST = 128

FFN_DIM = 2816
N_EXPERTS = 8
EXPERT_DIM = 3584

LANES = 128
SUBLANES = 8
NEG_BIG = -1e30
VMEM_LIMIT = 56 * 1024 * 1024

TM_GDN_IN = 512
TM_FFN = 512
SWA_BLOCKS_PER_STEP = 8
TM_ROUTER = 1024
TM_MOE = 512
TF_MOE = 1792
MOE_TILES_PER_WINDOW = 3
TD_DISPATCH = 1024
TC_COMBINE = 512
ZERO_ROWS = 64
DMA_UNROLL = 8


def _params(*sem):
    return pltpu.CompilerParams(dimension_semantics=sem, vmem_limit_bytes=VMEM_LIMIT)


def _dot(a, b):
    return jnp.dot(a, b, preferred_element_type=F32)


def _dot_nt(a, b):
    return lax.dot_general(a, b, (((1,), (1,)), ((), ())), preferred_element_type=F32)


def _dot_tn(a, b):
    return lax.dot_general(a, b, (((0,), (0,)), ((), ())), preferred_element_type=F32)


def _split(x):
    hi = x.astype(BF16)
    lo = (x - hi.astype(F32)).astype(BF16)
    return hi, lo


def _dot3(a, b):
    ah, al = _split(a)
    bh, bl = _split(b)
    return _dot(ah, bh) + (_dot(ah, bl) + _dot(al, bh))


def _silu(x):
    return x * jax.nn.sigmoid(x)


def _layer_norm(y, g, b):
    mu = jnp.mean(y, axis=-1, keepdims=True)
    yc = y - mu
    var = jnp.mean(yc * yc, axis=-1, keepdims=True)
    return yc * lax.rsqrt(var + LN_EPS) * g + b


def _gdn_inproj_kernel(x_ref, wf_ref, convw_ref, gp_ref,
                       q_ref, k_ref, v_ref, z_ref, gb_ref, ext_ref, w_ref, *, tm, tiles_per_seq):
    i = pl.program_id(0)
    xb = x_ref[...].astype(BF16)

    @pl.when(i == 0)
    def _():
        n_in = wf_ref.shape[-1]
        w_ref[:, :n_in] = wf_ref[0].astype(BF16)
        w_ref[:, n_in:] = jnp.zeros((w_ref.shape[0], w_ref.shape[1] - n_in), BF16)

    @pl.when(i % tiles_per_seq == 0)
    def _():
        ext_ref[0:SUBLANES, :] = jnp.zeros((SUBLANES, GDN_QKV), F32)

    @pl.when(i % tiles_per_seq != 0)
    def _():
        ext_ref[0:SUBLANES, :] = ext_ref[tm:tm + SUBLANES, :]

    ext_ref[SUBLANES:, :] = _dot(xb, w_ref[:, :GDN_QKV])
    z_ref[...] = _dot(xb, w_ref[:, GDN_QKV:GDN_QKV + GDN_VDIM])

    n_chunks = GDN_QKV // LANES
    for c in range(n_chunks):
        cs = slice(c * LANES, (c + 1) * LANES)
        acc = convw_ref[GDN_CONV - 1:GDN_CONV, cs] * ext_ref[SUBLANES:SUBLANES + tm, cs]
        for j in range(GDN_CONV - 1):
            off = SUBLANES - (GDN_CONV - 1) + j
            acc = acc + convw_ref[j:j + 1, cs] * ext_ref[off:off + tm, cs]
        y = _silu(acc)
        if c < 2 * GDN_K_HEADS:
            y = y * lax.rsqrt(jnp.sum(y * y, axis=-1, keepdims=True) + GDN_EPS)
            if c < GDN_K_HEADS:
                q_ref[:, cs] = y * (GDN_HEAD ** -0.5)
            else:
                k_ref[:, (c - GDN_K_HEADS) * LANES:(c - GDN_K_HEADS + 1) * LANES] = y
        else:
            cv = c - 2 * GDN_K_HEADS
            v_ref[:, cv * LANES:(cv + 1) * LANES] = y

    ba = _dot(xb, w_ref[:, GDN_QKV + GDN_VDIM:])
    lane = lax.broadcasted_iota(jnp.int32, ba.shape, 1)
    sp = ba + gp_ref[1:2, :]
    softplus = jnp.maximum(sp, 0.0) + jnp.log(1.0 + jnp.exp(-jnp.abs(sp)))
    g = -jnp.exp(gp_ref[0:1, :]) * softplus
    gb_ref[...] = jnp.where(lane < GDN_V_HEADS, jax.nn.sigmoid(ba), g)


def _gdn_inproj(x2d, w_in, conv_w, gate_params, seq_len):
    n = x2d.shape[0]
    tm = TM_GDN_IN
    kern = functools.partial(_gdn_inproj_kernel, tm=tm, tiles_per_seq=seq_len // tm)
    full = lambda shape: pl.BlockSpec(shape, lambda i: (0,) * len(shape))
    rows = lambda width: pl.BlockSpec((tm, width), lambda i: (i, 0))
    return pl.pallas_call(
        kern,
        grid=(n // tm,),
        in_specs=[rows(D_MODEL), pl.BlockSpec(w_in.shape, lambda i: (0, 0, 0), pipeline_mode=pl.Buffered(1)),
                  full(conv_w.shape), full(gate_params.shape)],
        out_specs=[rows(GDN_KDIM), rows(GDN_KDIM), rows(GDN_VDIM), rows(GDN_VDIM), rows(LANES)],
        out_shape=[jax.ShapeDtypeStruct((n, GDN_KDIM), F32), jax.ShapeDtypeStruct((n, GDN_KDIM), F32),
                   jax.ShapeDtypeStruct((n, GDN_VDIM), F32), jax.ShapeDtypeStruct((n, GDN_VDIM), F32),
                   jax.ShapeDtypeStruct((n, LANES), F32)],
        scratch_shapes=[pltpu.VMEM((tm + SUBLANES, GDN_QKV), F32),
                        pltpu.VMEM((D_MODEL, GDN_QKV + GDN_VDIM + LANES), BF16)],
        compiler_params=_params("arbitrary"),
        name="gdn_inproj",
    )(x2d, w_in, conv_w, gate_params)


GDN_CHUNKS_PER_STEP = 2
NEUMANN_BLOCK = 4


def _bf16_all(xs):
    return [x.astype(BF16) for x in xs]


def _dot_all(a_list, b_list):
    return [_dot(a, b) for a, b in zip(a_list, b_list)]


def _unit_lower_inverse_all(a_list, row, col):
    shift = int(math.log2(NEUMANN_BLOCK))
    eye = jnp.where(row == col, 1.0, 0.0).astype(F32)
    on_diag_block = (row >> shift) == (col >> shift)
    d = [jnp.where(on_diag_block, a, 0.0) for a in a_list]
    d_b = _bf16_all(d)
    x = [eye - di for di in d]
    d2_b = _bf16_all(_dot_all(d_b, d_b))
    x = [xi + t for xi, t in zip(x, _dot_all(_bf16_all(x), d2_b))]
    size = NEUMANN_BLOCK
    while size < a_list[0].shape[0]:
        shift = int(math.log2(size))
        rbl, cbl = row >> shift, col >> shift
        below = ((rbl & 1) == 1) & (cbl == rbl - 1)
        l_b = _bf16_all([jnp.where(below, a, 0.0) for a in a_list])
        x_b = _bf16_all(x)
        xl_b = _bf16_all(_dot_all(x_b, l_b))
        x = [xi - t for xi, t in zip(x, _dot_all(xl_b, x_b))]
        size *= 2
    return x


def _gdn_chunk_kernel(q_ref, k_ref, v_ref, z_ref, gb_ref, nw_ref, *rest, nb, nck, n_cast):
    cast_src, (o_ref, *cast_dst), s_ref = rest[:n_cast], rest[n_cast:2 * n_cast + 1], rest[-1]
    for src, dst in zip(cast_src, cast_dst):
        dst[...] = src[...].astype(BF16)
    c = GDN_CHUNK
    nh = GDN_V_HEADS

    @pl.when(pl.program_id(0) == 0)
    def _():
        s_ref[...] = jnp.zeros(s_ref.shape, F32)

    row = lax.broadcasted_iota(jnp.int32, (c, c), 0)
    col = lax.broadcasted_iota(jnp.int32, (c, c), 1)
    causal = row >= col
    strict = row > col
    tril = jnp.where(causal, 1.0, 0.0).astype(BF16)
    rs = lambda ci: slice(ci * c, (ci + 1) * c)
    ks = lambda h: slice((h // (nh // GDN_K_HEADS)) * GDN_HEAD, (h // (nh // GDN_K_HEADS) + 1) * GDN_HEAD)
    vs = lambda h: slice(h * GDN_HEAD, (h + 1) * GDN_HEAD)

    gates = {}
    for ci in range(nck):
        for b in range(nb):
            gb = gb_ref[b, rs(ci), :]
            g_hi, g_lo = _split(gb)
            gc = _dot(tril, g_hi) + _dot(tril, g_lo)
            g_last = gc[c - 1:c, :]
            gates[ci, b] = dict(gb=gb, gc=gc, gc_t=gc.T, eg=jnp.exp(gc), e_last=jnp.exp(g_last),
                                e_rest=jnp.exp(g_last - gc))

    items = [(ci, b, h) for ci in range(nck) for b in range(nb) for h in range(nh)]

    a_kk, a_qk, k_beta = [], {}, {}
    for it in items:
        ci, b, h = it
        gt = gates[ci, b]
        k = k_ref[b, rs(ci), ks(h)]
        kb = k * gt["gb"][:, h:h + 1]
        kq = jnp.concatenate([kb, q_ref[b, rs(ci), ks(h)]], axis=0).astype(BF16)
        gram = _dot_nt(kq, k.astype(BF16))
        gl = nh + h
        decay = jnp.exp(jnp.where(causal, gt["gc"][:, gl:gl + 1] - gt["gc_t"][gl:gl + 1, :], NEG_BIG))
        a_kk.append(jnp.where(strict, gram[:c] * decay, 0.0))
        a_qk[it] = (gram[c:] * decay).astype(BF16)
        k_beta[it] = kb

    t_inv = dict(zip(items, _unit_lower_inverse_all(a_kk, row, col)))

    uw = {}
    for it in items:
        ci, b, h = it
        gt = gates[ci, b]
        gl = nh + h
        rhs = jnp.concatenate([v_ref[b, rs(ci), vs(h)] * gt["gb"][:, h:h + 1],
                               k_beta[it] * gt["eg"][:, gl:gl + 1]], axis=1)
        uw[it] = _dot(t_inv[it].astype(BF16), rhs.astype(BF16))

    for ci in range(nck):
        chunk_items = [(ci, b, h) for b in range(nb) for h in range(nh)]

        ws_qs, states = {}, {}
        for it in chunk_items:
            _, b, h = it
            gl = nh + h
            s = s_ref[b * nh + h]
            wq = jnp.concatenate([uw[it][:, GDN_HEAD:],
                                  q_ref[b, rs(ci), ks(h)] * gates[ci, b]["eg"][:, gl:gl + 1]], axis=0)
            ws_qs[it] = _dot(wq.astype(BF16), s.astype(BF16))
            states[it] = s

        for it in chunk_items:
            _, b, h = it
            gt = gates[ci, b]
            gl = nh + h
            v_new = (uw[it][:, :GDN_HEAD] - ws_qs[it][:c]).astype(BF16)
            o = ws_qs[it][c:] + _dot(a_qk[it], v_new)
            k_dec = (k_ref[b, rs(ci), ks(h)] * gt["e_rest"][:, gl:gl + 1]).astype(BF16)
            s_ref[b * nh + h] = states[it] * gt["e_last"][:, gl:gl + 1] + _dot_tn(k_dec, v_new)
            o = o * lax.rsqrt(jnp.mean(o * o, axis=-1, keepdims=True) + GDN_EPS) * nw_ref[...]
            o_ref[b, rs(ci), vs(h)] = (o * _silu(z_ref[b, rs(ci), vs(h)])).astype(o_ref.dtype)


def _cast_specs(arrays, n_steps):
    specs, shapes = [], []
    for a in arrays:
        span = next(d for d in (1, 2, 4, 8)
                    if (a.shape[0] * d) % n_steps == 0 and (a.shape[0] * d // n_steps) % (2 * SUBLANES) == 0)
        specs.append(pl.BlockSpec((a.shape[0] * span // n_steps, a.shape[1]), lambda j, span=span: (j // span, 0)))
        shapes.append(jax.ShapeDtypeStruct(a.shape, BF16))
    return specs, shapes


def _gdn_chunk(q, k, v, zs, gb, norm_w, batch, seq_len, to_bf16):
    rows_per_step = GDN_CHUNK * GDN_CHUNKS_PER_STEP
    n_steps = seq_len // rows_per_step
    seq = lambda a: a.reshape(batch, seq_len, a.shape[-1])
    rows = lambda width: pl.BlockSpec((batch, rows_per_step, width), lambda j: (0, j, 0))
    cast_specs, cast_shapes = _cast_specs(to_bf16, n_steps)
    out, *casted = pl.pallas_call(
        functools.partial(_gdn_chunk_kernel, nb=batch, nck=GDN_CHUNKS_PER_STEP, n_cast=len(to_bf16)),
        grid=(n_steps,),
        in_specs=[rows(GDN_KDIM), rows(GDN_KDIM), rows(GDN_VDIM), rows(GDN_VDIM), rows(LANES),
                  pl.BlockSpec((1, GDN_HEAD), lambda j: (0, 0))] + cast_specs,
        out_specs=[rows(GDN_VDIM)] + cast_specs,
        out_shape=[jax.ShapeDtypeStruct((batch, seq_len, GDN_VDIM), BF16)] + cast_shapes,
        scratch_shapes=[pltpu.VMEM((batch * GDN_V_HEADS, GDN_HEAD, GDN_HEAD), F32)],
        compiler_params=_params("arbitrary"),
        name="gdn_chunk",
    )(seq(q), seq(k), seq(v), seq(zs), seq(gb), norm_w, *to_bf16)
    return out.reshape(batch * seq_len, GDN_VDIM), casted


def _proj_res_ln(a_ref, w_ref, r_ref, g_ref, b_ref):
    return _layer_norm(ALPHA * r_ref[...] + _dot(a_ref[...], w_ref[...]), g_ref[...], b_ref[...])


def _outproj_ffn_kernel(a_ref, wo_ref, r_ref, g0_ref, b0_ref, wg_ref, wu_ref, wd_ref, g1_ref, b1_ref, o_ref):
    x1 = _proj_res_ln(a_ref, wo_ref, r_ref, g0_ref, b0_ref)
    xb = x1.astype(BF16)
    h = (_silu(_dot(xb, wg_ref[...])) * _dot(xb, wu_ref[...])).astype(BF16)
    o_ref[...] = _layer_norm(ALPHA * x1 + _dot(h, wd_ref[...]), g1_ref[...], b1_ref[...])


def _outproj_ffn(a, w_out, res, g0, b0, wg, wu, wd, g1, b1):
    n, kdim = a.shape
    tm = TM_FFN
    resident = lambda w: pl.BlockSpec(w.shape, lambda i: (0, 0), pipeline_mode=pl.Buffered(1))
    rows = lambda width: pl.BlockSpec((tm, width), lambda i: (i, 0))
    vec = pl.BlockSpec((1, D_MODEL), lambda i: (0, 0))
    return pl.pallas_call(
        _outproj_ffn_kernel,
        grid=(n // tm,),
        in_specs=[rows(kdim), resident(w_out), rows(D_MODEL), vec, vec,
                  resident(wg), resident(wu), resident(wd), vec, vec],
        out_specs=rows(D_MODEL),
        out_shape=jax.ShapeDtypeStruct((n, D_MODEL), F32),
        compiler_params=_params("parallel"),
        name="gdn_outproj_ffn",
    )(a, w_out, res, g0, b0, wg, wu, wd, g1, b1)


def _band_tables():
    assert SWA_WINDOW == SWA_BLOCK
    r = np.arange(SWA_BLOCK)[:, None]
    q = np.arange(SWA_BLOCK)[None, :]
    d = (q - r) % SWA_BLOCK
    max_exact = REL_BUCKETS // 2
    df = np.maximum(d, 1).astype(np.float32)
    large = max_exact + (np.log(df / np.float32(max_exact)) / np.float32(math.log(REL_MAX_DIST / max_exact))
                         * np.float32(REL_BUCKETS - max_exact)).astype(np.int32)
    large = np.minimum(large, REL_BUCKETS - 1)
    bucket = np.where(d < max_exact, d, large).astype(np.int32)
    valid = np.stack([q >= r, np.ones_like(q >= r)]).astype(np.int32)
    return bucket, valid


def _bias_kernel(relb_ref, bucket_ref, valid_ref, o_ref):
    bucket = bucket_ref[...]
    in_bucket = [bucket == b for b in range(REL_BUCKETS)]
    for h in range(SWA_Q_HEADS):
        acc = jnp.zeros(bucket.shape, F32)
        for b in range(REL_BUCKETS):
            acc = jnp.where(in_bucket[b], relb_ref[b, h], acc)
        for t in range(2):
            o_ref[t, h] = jnp.where(valid_ref[t] != 0, acc, NEG_BIG)


def _bias_table(rel_bias):
    bucket, valid = _band_tables()
    shape = (SWA_BLOCK, SWA_BLOCK)
    return pl.pallas_call(
        _bias_kernel,
        grid=(1,),
        in_specs=[pl.BlockSpec(memory_space=pltpu.SMEM), pl.BlockSpec(shape, lambda i: (0, 0)),
                  pl.BlockSpec((2,) + shape, lambda i: (0, 0, 0))],
        out_specs=pl.BlockSpec((2, SWA_Q_HEADS) + shape, lambda i: (0, 0, 0, 0)),
        out_shape=jax.ShapeDtypeStruct((2, SWA_Q_HEADS) + shape, F32),
        compiler_params=_params("arbitrary"),
        name="swa_bias_table",
    )(rel_bias, jnp.asarray(bucket), jnp.asarray(valid))


def _band_attention(q_pair, kv, bias_of, sink_ref, put_pair):
    dh = SWA_HEAD_DIM
    blk = SWA_BLOCK
    nk = 2 * blk
    from_prev = lax.broadcasted_iota(jnp.int32, (blk, blk), 1) < lax.broadcasted_iota(jnp.int32, (blk, blk), 0)
    k_all = kv[:, :SWA_KVDIM]
    lane = lax.broadcasted_iota(jnp.int32, k_all.shape, 1)
    zero = jnp.zeros_like(k_all)
    for kh in range(SWA_KV_HEADS):
        k_own = jnp.where((lane >= kh * dh) & (lane < (kh + 1) * dh), k_all, zero)
        k_other = pltpu.roll(k_own.astype(F32), dh, axis=1).astype(BF16)
        k_even, k_odd = (k_own, k_other) if kh == 0 else (k_other, k_own)
        k2 = jnp.concatenate([k_even, k_odd], axis=0)
        v_t = kv[:, SWA_KVDIM + kh * dh:SWA_KVDIM + (kh + 1) * dh].T
        pairs = [kh * (SWA_GROUP // 2) + p for p in range(SWA_GROUP // 2)]
        heads = [2 * p + r for p in pairs for r in range(2)]
        st = [_dot_nt(k2, q_pair(p)) for p in pairs]
        scores = [jnp.where(from_prev, st[i // 2][(i % 2) * nk:(i % 2) * nk + blk],
                            st[i // 2][(i % 2) * nk + blk:(i % 2 + 1) * nk]) + bias_of(h)
                  for i, h in enumerate(heads)]
        sinks = [sink_ref[0:1, h:h + 1] for h in heads]
        maxes = [jnp.maximum(jnp.max(s, axis=0, keepdims=True), sk) for s, sk in zip(scores, sinks)]
        expo = [jnp.exp(s - m) for s, m in zip(scores, maxes)]
        sums = [jnp.sum(p, axis=0, keepdims=True) for p in expo]
        inv_den = [1.0 / (t + jnp.exp(sk - m)) for t, sk, m in zip(sums, sinks, maxes)]
        expo = [jnp.concatenate([jnp.where(from_prev, p, 0.0), jnp.where(from_prev, 0.0, p)], axis=0) for p in expo]
        for i, p in enumerate(pairs):
            o_t = _dot(v_t, jnp.concatenate(expo[2 * i:2 * i + 2], axis=1).astype(BF16))
            z = jnp.concatenate([o_t[:, :SWA_BLOCK] * inv_den[2 * i], o_t[:, SWA_BLOCK:] * inv_den[2 * i + 1]],
                                axis=0)
            put_pair(p, z.T)


def _swa_layer_kernel(x_ref, win_ref, bin_ref, bias_ref, sink_ref, wout_ref, g_ref, b_ref, o_ref,
                      q_ref, kv_ref, att_ref, *, nblk):
    j = pl.program_id(1)
    blk = SWA_BLOCK
    tm = nblk * blk
    dh2 = 2 * SWA_HEAD_DIM

    p = _dot(x_ref[...].astype(BF16), win_ref[...]) + bin_ref[...]
    q_ref[...] = (p[:, :SWA_QDIM] * (SWA_HEAD_DIM ** -0.5)).astype(BF16)

    @pl.when(j == 0)
    def _():
        kv_ref[0:blk, :] = jnp.zeros((blk, 2 * SWA_KVDIM), BF16)

    @pl.when(j != 0)
    def _():
        kv_ref[0:blk, :] = kv_ref[tm:tm + blk, :]

    kv_ref[blk:, :] = p[:, SWA_QDIM:].astype(BF16)

    for i in range(nblk):
        rows = slice(i * blk, (i + 1) * blk)
        table = jnp.where(j == 0, 0, 1) if i == 0 else 1

        def put_pair(pair, o, rows=rows):
            att_ref[rows, pair * dh2:(pair + 1) * dh2] = o.astype(att_ref.dtype)

        _band_attention(lambda pair, rows=rows: q_ref[rows, pair * dh2:(pair + 1) * dh2],
                        kv_ref[i * blk:(i + 2) * blk, :],
                        lambda h, table=table: bias_ref[table, h], sink_ref, put_pair)

    o_ref[...] = _proj_res_ln(att_ref, wout_ref, x_ref, g_ref, b_ref)


def _swa_layer(x, w_in, b_in, bias, sinks, w_out, g, b, batch, seq_len):
    n = x.shape[0]
    nblk = SWA_BLOCKS_PER_STEP
    tm = nblk * SWA_BLOCK
    steps = seq_len // tm
    resident = lambda a: pl.BlockSpec(a.shape, lambda s, j: (0,) * a.ndim, pipeline_mode=pl.Buffered(1))
    rows = pl.BlockSpec((tm, D_MODEL), lambda s, j: (s * steps + j, 0))
    return pl.pallas_call(
        functools.partial(_swa_layer_kernel, nblk=nblk),
        grid=(batch, steps),
        in_specs=[rows, resident(w_in), resident(b_in), resident(bias), resident(sinks), resident(w_out),
                  resident(g), resident(b)],
        out_specs=rows,
        out_shape=jax.ShapeDtypeStruct((n, D_MODEL), F32),
        scratch_shapes=[pltpu.VMEM((tm, SWA_QDIM), BF16), pltpu.VMEM((tm + SWA_BLOCK, 2 * SWA_KVDIM), BF16),
                        pltpu.VMEM((tm, SWA_QDIM), BF16)],
        compiler_params=_params("parallel", "arbitrary"),
        name="swa_layer",
    )(x, w_in, b_in, bias, sinks, w_out, g, b)


def _router_kernel(x_ref, wr_ref, route_ref, route_t_ref, cnt_ref, run_ref, *, tm):
    @pl.when(pl.program_id(0) == 0)
    def _():
        run_ref[...] = jnp.zeros(run_ref.shape, F32)

    logits = _dot3(x_ref[...], wr_ref[...])
    lane = lax.broadcasted_iota(jnp.int32, logits.shape, 1)
    lane_f = lane.astype(F32)
    lg = jnp.where(lane < N_EXPERTS, logits, NEG_BIG)
    m1 = jnp.max(lg, axis=-1, keepdims=True)
    i1 = jnp.min(jnp.where(lg == m1, lane_f, float(LANES)), axis=-1, keepdims=True)
    oh1 = lane_f == i1
    lg2 = jnp.where(oh1, NEG_BIG, lg)
    m2 = jnp.max(lg2, axis=-1, keepdims=True)
    i2 = jnp.min(jnp.where(lg2 == m2, lane_f, float(LANES)), axis=-1, keepdims=True)
    oh2 = lane_f == i2
    e = jnp.exp(m2 - m1)
    w0 = 1.0 / (1.0 + e)
    w1 = e * w0

    cnt = jnp.where(oh1, 1.0, 0.0) + jnp.where(oh2, 1.0, 0.0)
    r = lax.broadcasted_iota(jnp.int32, (tm, tm), 0)
    c = lax.broadcasted_iota(jnp.int32, (tm, tm), 1)
    before = jnp.where(r > c, 1.0, 0.0).astype(BF16)
    excl = _dot(before, cnt.astype(BF16)) + run_ref[...]
    rank0 = jnp.sum(jnp.where(oh1, excl, 0.0), axis=-1, keepdims=True)
    rank1 = jnp.sum(jnp.where(oh2, excl, 0.0), axis=-1, keepdims=True)
    run = run_ref[...] + jnp.sum(cnt, axis=0, keepdims=True)
    run_ref[...] = run
    cnt_ref[...] = run

    vals = (i1, i2, rank0, rank1, w0, w1)
    out = jnp.zeros(logits.shape, F32)
    for idx, val in enumerate(vals):
        out = jnp.where(lane == idx, val, out)
    route_ref[...] = out
    route_t_ref[...] = out.T[:SUBLANES, :]


def _router(x, w_router_padded):
    n = x.shape[0]
    tm = TM_ROUTER
    return pl.pallas_call(
        functools.partial(_router_kernel, tm=tm),
        grid=(n // tm,),
        in_specs=[pl.BlockSpec((tm, D_MODEL), lambda i: (i, 0)), pl.BlockSpec((D_MODEL, LANES), lambda i: (0, 0))],
        out_specs=[pl.BlockSpec((tm, LANES), lambda i: (i, 0)), pl.BlockSpec((SUBLANES, tm), lambda i: (0, i)),
                   pl.BlockSpec((1, LANES), lambda i: (0, 0))],
        out_shape=[jax.ShapeDtypeStruct((n, LANES), F32), jax.ShapeDtypeStruct((SUBLANES, n), F32),
                   jax.ShapeDtypeStruct((1, LANES), F32)],
        scratch_shapes=[pltpu.VMEM((1, LANES), F32)],
        compiler_params=_params("arbitrary"),
        name="moe_router",
    )(x, w_router_padded)


def _dispatch_kernel(pos0_ref, pos1_ref, meta_ref, x_ref, wsrc_ref, xs_hbm, wdst_ref, zero_ref, sem, pad_sem, *, td):
    i = pl.program_id(0)
    wdst_ref[...] = wsrc_ref[...].astype(BF16)

    def row_copy(t, dst):
        return pltpu.make_async_copy(x_ref.at[pl.ds(t, 1)], xs_hbm.at[pl.ds(dst, 1)], sem)

    def start(t, carry):
        row_copy(t, pos0_ref[0, 0, t]).start()
        row_copy(t, pos1_ref[0, 0, t]).start(priority=1)
        return carry

    lax.fori_loop(0, td, start, 0, unroll=DMA_UNROLL)

    @pl.when(i == pl.num_programs(0) - 1)
    def _():
        zero_ref[...] = jnp.zeros(zero_ref.shape, F32)

        zr = zero_ref.shape[0]

        def row_zero(dst):
            return pltpu.make_async_copy(zero_ref.at[pl.ds(0, 1)], xs_hbm.at[pl.ds(dst, 1)], pad_sem)

        def block_zero(blk):
            return pltpu.make_async_copy(zero_ref, xs_hbm.at[pl.ds(pl.multiple_of(blk * zr, zr), zr)], pad_sem)

        def zero_range(copy, lo, hi):
            lax.fori_loop(lo, hi, lambda r, carry: (copy(r).start(), carry)[1], 0)
            lax.fori_loop(lo, hi, lambda r, carry: (copy(0).wait(), carry)[1], 0)

        for e in range(N_EXPERTS):
            off, cnt, cnt_up, padded = (meta_ref[r, e] for r in range(4))
            zero_range(lambda r, off=off: row_zero(off + r), cnt, cnt_up)
            zero_range(block_zero, (off + cnt_up) // zr, (off + padded) // zr)

        used = meta_ref[0, N_EXPERTS - 1] + meta_ref[3, N_EXPERTS - 1]
        zero_range(block_zero, used // zr, xs_hbm.shape[0] // zr)

    for _ in range(2):
        pltpu.make_async_copy(x_ref, xs_hbm.at[pl.ds(0, td)], sem).wait()


def _dispatch(x, pos, meta, n_rows, to_bf16):
    n = x.shape[0]
    td = TD_DISPATCH
    pos_spec = pl.BlockSpec((1, 1, td), lambda i: (i, 0, 0), memory_space=pltpu.SMEM)
    (cast_spec,), (cast_shape,) = _cast_specs([to_bf16], n // td)
    return pl.pallas_call(
        functools.partial(_dispatch_kernel, td=td),
        grid=(n // td,),
        in_specs=[pos_spec, pos_spec, pl.BlockSpec(memory_space=pltpu.SMEM),
                  pl.BlockSpec((td, D_MODEL), lambda i: (i, 0)), cast_spec],
        out_specs=[pl.BlockSpec(memory_space=pl.ANY), cast_spec],
        out_shape=[jax.ShapeDtypeStruct((n_rows, D_MODEL), F32), cast_shape],
        scratch_shapes=[pltpu.VMEM((ZERO_ROWS, D_MODEL), F32), pltpu.SemaphoreType.DMA, pltpu.SemaphoreType.DMA],
        compiler_params=_params("arbitrary"),
        name="moe_dispatch",
    )(pos[0].reshape(n // td, 1, td), pos[1].reshape(n // td, 1, td), meta, x, to_bf16)


def _moe_kernel(we_ref, wb_ref, nv_ref, cnt_ref, *refs, n_sub, tm):
    xs_refs = refs[:n_sub]
    wg_ref, wu_ref, wd_ref, ys_hbm, acc_ref, xb_ref, sem = refs[n_sub:]
    j = pl.program_id(0)
    f = pl.program_id(1)
    last_f = pl.num_programs(1) - 1
    n_windows, n_tiles = cnt_ref[0], cnt_ref[1]

    def tile_write(slot, block):
        return pltpu.make_async_copy(acc_ref.at[slot], ys_hbm.at[pl.ds(pl.multiple_of(block * tm, tm), tm)],
                                     sem.at[slot])

    @pl.when(j < n_windows)
    def _():
        prev_tiles = jnp.where(j > 0, nv_ref[jnp.maximum(j - 1, 0)], 0)

        for s in range(n_sub):
            @pl.when(s < nv_ref[j])
            def _(s=s):
                @pl.when(f == 0)
                def _():
                    xb_ref[s] = xs_refs[s][...].astype(BF16)

                xb = xb_ref[s]
                h = (_silu(_dot(xb, wg_ref[0])) * _dot(xb, wu_ref[0])).astype(BF16)
                part = _dot(h, wd_ref[0])

                @pl.when(f == 0)
                def _():
                    @pl.when(s < prev_tiles)
                    def _():
                        tile_write(s, 0).wait()

                    acc_ref[s] = part

                @pl.when(f != 0)
                def _():
                    acc_ref[s] += part

                @pl.when(f == last_f)
                def _():
                    tile_write(s, wb_ref[j] + s).start()

            @pl.when((f == 0) & (s >= nv_ref[j]) & (s < prev_tiles))
            def _(s=s):
                tile_write(s, 0).wait()

        @pl.when((f == last_f) & (j == n_windows - 1))
        def _():
            for s in range(n_sub):
                @pl.when(s < nv_ref[j])
                def _(s=s):
                    tile_write(s, 0).wait()

            acc_ref[0] = jnp.zeros(acc_ref.shape[1:], F32)
            total_tiles = ys_hbm.shape[0] // tm
            lax.fori_loop(n_tiles, total_tiles, lambda t, c: (tile_write(0, t).start(), c)[1], 0)
            lax.fori_loop(n_tiles, total_tiles, lambda t, c: (tile_write(0, 0).wait(), c)[1], 0)


def _moe_experts(xs, wg, wu, wd, win_expert, win_block, win_tiles, counts):
    n_rows = xs.shape[0]
    tm, tf, n_sub = TM_MOE, TF_MOE, MOE_TILES_PER_WINDOW
    max_windows = win_expert.shape[0]

    def xs_spec(s):
        return pl.BlockSpec((tm, D_MODEL), lambda j, f, we, wb, nv, cnt: (wb[j] + jnp.minimum(s, nv[j] - 1), 0))

    grid_spec = pltpu.PrefetchScalarGridSpec(
        num_scalar_prefetch=4,
        grid=(max_windows, EXPERT_DIM // tf),
        in_specs=[xs_spec(s) for s in range(n_sub)] + [
            pl.BlockSpec((1, D_MODEL, tf), lambda j, f, we, wb, nv, cnt: (we[j], 0, f)),
            pl.BlockSpec((1, D_MODEL, tf), lambda j, f, we, wb, nv, cnt: (we[j], 0, f)),
            pl.BlockSpec((1, tf, D_MODEL), lambda j, f, we, wb, nv, cnt: (we[j], f, 0))],
        out_specs=pl.BlockSpec(memory_space=pl.ANY),
        scratch_shapes=[pltpu.VMEM((n_sub, tm, D_MODEL), F32), pltpu.VMEM((n_sub, tm, D_MODEL), BF16),
                        pltpu.SemaphoreType.DMA((n_sub,))],
    )
    return pl.pallas_call(
        functools.partial(_moe_kernel, n_sub=n_sub, tm=tm),
        grid_spec=grid_spec,
        out_shape=jax.ShapeDtypeStruct((n_rows, D_MODEL), F32),
        compiler_params=_params("arbitrary", "arbitrary"),
        name="moe_experts",
    )(win_expert, win_block, win_tiles, counts, *([xs] * n_sub), wg, wu, wd)


def _combine_kernel(pos0_ref, pos1_ref, pos0n_ref, pos1n_ref, x_ref, route_ref, ys_hbm, g_ref, b_ref, o_ref,
                    y_ref, sem, *, tc):
    i = pl.program_id(0)
    slot = i % 2

    def issue(p_refs, s):
        def start(t, carry):
            for kk in range(2):
                pltpu.make_async_copy(ys_hbm.at[pl.ds(p_refs[kk][0, 0, t], 1)],
                                      y_ref.at[s, kk, pl.ds(t, 1)], sem.at[s]).start(priority=kk)
            return carry

        lax.fori_loop(0, tc, start, 0, unroll=DMA_UNROLL)

    @pl.when(i == 0)
    def _():
        issue((pos0_ref, pos1_ref), 0)

    @pl.when(i + 1 < pl.num_programs(0))
    def _():
        issue((pos0n_ref, pos1n_ref), 1 - slot)

    for kk in range(2):
        pltpu.make_async_copy(ys_hbm.at[pl.ds(0, tc)], y_ref.at[slot, kk], sem.at[slot]).wait()

    route = route_ref[...]
    f = route[:, 4:5] * y_ref[slot, 0] + route[:, 5:6] * y_ref[slot, 1]
    o_ref[...] = _layer_norm(ALPHA * x_ref[...] + f, g_ref[...], b_ref[...])


def _combine_ln(x, route, pos, ys, g, b):
    n = x.shape[0]
    tc = TC_COMBINE
    nt = n // tc
    pos0, pos1 = pos[0].reshape(nt, 1, tc), pos[1].reshape(nt, 1, tc)
    cur = pl.BlockSpec((1, 1, tc), lambda i: (i, 0, 0), memory_space=pltpu.SMEM)
    nxt = pl.BlockSpec((1, 1, tc), lambda i: (jnp.minimum(i + 1, nt - 1), 0, 0), memory_space=pltpu.SMEM)
    return pl.pallas_call(
        functools.partial(_combine_kernel, tc=tc),
        grid=(nt,),
        in_specs=[cur, cur, nxt, nxt,
                  pl.BlockSpec((tc, D_MODEL), lambda i: (i, 0)),
                  pl.BlockSpec((tc, LANES), lambda i: (i, 0)),
                  pl.BlockSpec(memory_space=pl.ANY),
                  pl.BlockSpec((1, D_MODEL), lambda i: (0, 0)), pl.BlockSpec((1, D_MODEL), lambda i: (0, 0))],
        out_specs=pl.BlockSpec((tc, D_MODEL), lambda i: (i, 0)),
        out_shape=jax.ShapeDtypeStruct((n, D_MODEL), F32),
        scratch_shapes=[pltpu.VMEM((2, 2, tc, D_MODEL), F32), pltpu.SemaphoreType.DMA((2,))],
        compiler_params=_params("arbitrary"),
        name="moe_combine_ln",
    )(pos0, pos1, pos0, pos1, x, route, ys, g, b)


def _moe_layer(x, route, route_t, counts_f, wg, wu, wd_f32, g, b):
    n = x.shape[0]
    tm = TM_MOE

    counts = counts_f[0, :N_EXPERTS].astype(jnp.int32)
    padded = ((counts + tm - 1) // tm) * tm
    offs = jnp.cumsum(padded) - padded
    eids = jnp.arange(N_EXPERTS, dtype=jnp.int32)
    e01 = route_t[0:2].astype(jnp.int32)
    rank01 = route_t[2:4].astype(jnp.int32)
    base = jnp.sum(jnp.where(e01[:, None, :] == eids[None, :, None], offs[None, :, None], 0), axis=1)
    pos = base + rank01
    counts_up = jnp.minimum(padded, ((counts + ZERO_ROWS - 1) // ZERO_ROWS) * ZERO_ROWS)
    meta = jnp.stack([offs, counts, counts_up, padded]).astype(jnp.int32)

    n_rows = 2 * n + N_EXPERTS * tm
    nsub = MOE_TILES_PER_WINDOW
    tiles = padded // tm
    wins = (tiles + nsub - 1) // nsub
    win_end = jnp.cumsum(wins)
    n_windows = win_end[-1]
    max_windows = (n_rows // tm + nsub - 1) // nsub + N_EXPERTS
    wj = jnp.minimum(jnp.arange(max_windows, dtype=jnp.int32), n_windows - 1)
    win_expert = jnp.minimum(jnp.sum((wj[:, None] >= win_end[None, :]).astype(jnp.int32), axis=-1), N_EXPERTS - 1)
    pick = lambda v: jnp.sum(jnp.where(win_expert[:, None] == eids[None, :], v[None, :], 0), axis=-1)
    local = wj - pick(win_end - wins)
    win_block = pick(offs // tm) + nsub * local
    win_tiles = jnp.minimum(nsub, pick(tiles) - nsub * local)
    counts_nt = jnp.stack([n_windows, jnp.sum(tiles)])

    xs, wd = _dispatch(x, pos, meta, n_rows, wd_f32.reshape(-1, D_MODEL))
    wd = wd.reshape(wd_f32.shape)
    i32 = lambda a: a.astype(jnp.int32)
    ys = _moe_experts(xs, wg, wu, wd, i32(win_expert), i32(win_block), i32(win_tiles), i32(counts_nt))
    return _combine_ln(x, route, pos, ys, g, b)


def kernel(x, a_w_in, a_conv_w, a_a_log, a_dt_bias, a_norm_w, a_w_out, b_w_in, b_b_in, b_sinks, b_w_out, rel_bias,
           ffn_w_gate, ffn_w_up, ffn_w_down, moe_router, moe_w_gate, moe_w_up, moe_w_down, ln_g, ln_b):
    batch, seq_len, _ = x.shape
    n = batch * seq_len
    x0 = x.reshape(n, D_MODEL)
    ln_g = ln_g.reshape(DEPTH, 2, 1, D_MODEL)
    ln_b = ln_b.reshape(DEPTH, 2, 1, D_MODEL)

    pad_gate = lambda p: jnp.pad(p.reshape(1, GDN_V_HEADS), ((0, 0), (GDN_V_HEADS, LANES - 2 * GDN_V_HEADS)))
    gate_params = jnp.concatenate([pad_gate(a_a_log[0]), pad_gate(a_dt_bias[0])], axis=0)
    q, k, v, zs, gb = _gdn_inproj(x0, a_w_in, a_conv_w[0], gate_params, seq_len)
    o, casted = _gdn_chunk(q, k, v, zs, gb, a_norm_w[0].reshape(1, GDN_HEAD), batch, seq_len,
                           [moe_w_gate[0].reshape(-1, EXPERT_DIM), moe_w_up[0].reshape(-1, EXPERT_DIM),
                            ffn_w_gate[0], ffn_w_up[0], ffn_w_down[0],
                            a_w_out[0], b_w_in[0], b_w_out[0]])
    moe_wg, moe_wu, ffn_wg, ffn_wu, ffn_wd, w_out_a, w_in_b, w_out_b = casted
    x2 = _outproj_ffn(o, w_out_a, x0, ln_g[0, 0], ln_b[0, 0],
                      ffn_wg, ffn_wu, ffn_wd, ln_g[0, 1], ln_b[0, 1])

    bias = _bias_table(rel_bias)
    sinks = jnp.pad(b_sinks[0].reshape(1, SWA_Q_HEADS), ((0, 0), (0, LANES - SWA_Q_HEADS)))
    x3 = _swa_layer(x2, w_in_b, b_b_in[0].reshape(1, -1), bias, sinks, w_out_b, ln_g[1, 0], ln_b[1, 0],
                    batch, seq_len)
    w_router = jnp.pad(moe_router[0], ((0, 0), (0, LANES - N_EXPERTS)))
    route, route_t, counts = _router(x3, w_router)
    expert_shape = (N_EXPERTS, D_MODEL, EXPERT_DIM)
    x4 = _moe_layer(x3, route, route_t, counts, moe_wg.reshape(expert_shape), moe_wu.reshape(expert_shape),
                    moe_w_down[0], ln_g[1, 1], ln_b[1, 1])
    return x4.reshape(batch, seq_len, D_MODEL)
```

```python
import functools
import math

import numpy as np
import jax
import jax.numpy as jnp
from jax import lax
from jax.experimental import pallas as pl
from jax.experimental.pallas import tpu as pltpu

F32 = jnp.float32
BF16 = jnp.bfloat16

D_MODEL = 1024
DEPTH = 2
ALPHA = (2.0 * DEPTH) ** 0.25
LN_EPS = 1e-5

GDN_K_HEADS = 4
GDN_V_HEADS = 8
GDN_HEAD = 128
GDN_KDIM = GDN_K_HEADS * GDN_HEAD
GDN_VDIM = GDN_V_HEADS * GDN_HEAD
GDN_CONV = 4
GDN_CHUNK = 64
GDN_QKV = 2 * GDN_KDIM + GDN_VDIM
GDN_EPS = 1e-6

SWA_Q_HEADS = 16
SWA_KV_HEADS = 2
SWA_GROUP = SWA_Q_HEADS // SWA_KV_HEADS
SWA_HEAD_DIM = 64
SWA_WINDOW = 128
SWA_BLOCK = 128
SWA_QDIM = SWA_Q_HEADS * SWA_HEAD_DIM
SWA_KVDIM = SWA_KV_HEADS * SWA_HEAD_DIM
REL_BUCKETS = 32
REL_MAX_DIST = 128

FFN_DIM = 2816
N_EXPERTS = 8
EXPERT_DIM = 3584

LANES = 128
SUBLANES = 8
NEG_BIG = -1e30
VMEM_LIMIT = 56 * 1024 * 1024

TM_GDN_IN = 512
TM_FFN = 512
SWA_BLOCKS_PER_STEP = 8
TM_ROUTER = 1024
TM_MOE = 512
TF_MOE = 1792
MOE_TILES_PER_WINDOW = 3
TD_DISPATCH = 1024
TC_COMBINE = 512
ZERO_ROWS = 64
DMA_UNROLL = 8


def _params(*sem):
    return pltpu.CompilerParams(dimension_semantics=sem, vmem_limit_bytes=VMEM_LIMIT)


def _dot(a, b):
    return jnp.dot(a, b, preferred_element_type=F32)


def _dot_nt(a, b):
    return lax.dot_general(a, b, (((1,), (1,)), ((), ())), preferred_element_type=F32)


def _dot_tn(a, b):
    return lax.dot_general(a, b, (((0,), (0,)), ((), ())), preferred_element_type=F32)


def _split(x):
    hi = x.astype(BF16)
    lo = (x - hi.astype(F32)).astype(BF16)
    return hi, lo


def _dot3(a, b):
    ah, al = _split(a)
    bh, bl = _split(b)
    return _dot(ah, bh) + (_dot(ah, bl) + _dot(al, bh))


def _silu(x):
    return x * jax.nn.sigmoid(x)


def _layer_norm(y, g, b):
    mu = jnp.mean(y, axis=-1, keepdims=True)
    yc = y - mu
    var = jnp.mean(yc * yc, axis=-1, keepdims=True)
    return yc * lax.rsqrt(var + LN_EPS) * g + b


def _gdn_inproj_kernel(x_ref, wf_ref, convw_ref, gp_ref,
                       q_ref, k_ref, v_ref, z_ref, gb_ref, ext_ref, w_ref, *, tm, tiles_per_seq):
    i = pl.program_id(0)
    xb = x_ref[...].astype(BF16)

    @pl.when(i == 0)
    def _():
        n_in = wf_ref.shape[-1]
        w_ref[:, :n_in] = wf_ref[0].astype(BF16)
        w_ref[:, n_in:] = jnp.zeros((w_ref.shape[0], w_ref.shape[1] - n_in), BF16)

    @pl.when(i % tiles_per_seq == 0)
    def _():
        ext_ref[0:SUBLANES, :] = jnp.zeros((SUBLANES, GDN_QKV), F32)

    @pl.when(i % tiles_per_seq != 0)
    def _():
        ext_ref[0:SUBLANES, :] = ext_ref[tm:tm + SUBLANES, :]

    ext_ref[SUBLANES:, :] = _dot(xb, w_ref[:, :GDN_QKV])
    z_ref[...] = _dot(xb, w_ref[:, GDN_QKV:GDN_QKV + GDN_VDIM])

    n_chunks = GDN_QKV // LANES
    for c in range(n_chunks):
        cs = slice(c * LANES, (c + 1) * LANES)
        acc = convw_ref[GDN_CONV - 1:GDN_CONV, cs] * ext_ref[SUBLANES:SUBLANES + tm, cs]
        for j in range(GDN_CONV - 1):
            off = SUBLANES - (GDN_CONV - 1) + j
            acc = acc + convw_ref[j:j + 1, cs] * ext_ref[off:off + tm, cs]
        y = _silu(acc)
        if c < 2 * GDN_K_HEADS:
            y = y * lax.rsqrt(jnp.sum(y * y, axis=-1, keepdims=True) + GDN_EPS)
            if c < GDN_K_HEADS:
                q_ref[:, cs] = y * (GDN_HEAD ** -0.5)
            else:
                k_ref[:, (c - GDN_K_HEADS) * LANES:(c - GDN_K_HEADS + 1) * LANES] = y
        else:
            cv = c - 2 * GDN_K_HEADS
            v_ref[:, cv * LANES:(cv + 1) * LANES] = y

    ba = _dot(xb, w_ref[:, GDN_QKV + GDN_VDIM:])
    lane = lax.broadcasted_iota(jnp.int32, ba.shape, 1)
    sp = ba + gp_ref[1:2, :]
    softplus = jnp.maximum(sp, 0.0) + jnp.log(1.0 + jnp.exp(-jnp.abs(sp)))
    g = -jnp.exp(gp_ref[0:1, :]) * softplus
    gb_ref[...] = jnp.where(lane < GDN_V_HEADS, jax.nn.sigmoid(ba), g)


def _gdn_inproj(x2d, w_in, conv_w, gate_params, seq_len):
    n = x2d.shape[0]
    tm = TM_GDN_IN
    kern = functools.partial(_gdn_inproj_kernel, tm=tm, tiles_per_seq=seq_len // tm)
    full = lambda shape: pl.BlockSpec(shape, lambda i: (0,) * len(shape))
    rows = lambda width: pl.BlockSpec((tm, width), lambda i: (i, 0))
    return pl.pallas_call(
        kern,
        grid=(n // tm,),
        in_specs=[rows(D_MODEL), pl.BlockSpec(w_in.shape, lambda i: (0, 0, 0), pipeline_mode=pl.Buffered(1)),
                  full(conv_w.shape), full(gate_params.shape)],
        out_specs=[rows(GDN_KDIM), rows(GDN_KDIM), rows(GDN_VDIM), rows(GDN_VDIM), rows(LANES)],
        out_shape=[jax.ShapeDtypeStruct((n, GDN_KDIM), F32), jax.ShapeDtypeStruct((n, GDN_KDIM), F32),
                   jax.ShapeDtypeStruct((n, GDN_VDIM), F32), jax.ShapeDtypeStruct((n, GDN_VDIM), F32),
                   jax.ShapeDtypeStruct((n, LANES), F32)],
        scratch_shapes=[pltpu.VMEM((tm + SUBLANES, GDN_QKV), F32),
                        pltpu.VMEM((D_MODEL, GDN_QKV + GDN_VDIM + LANES), BF16)],
        compiler_params=_params("arbitrary"),
        name="gdn_inproj",
    )(x2d, w_in, conv_w, gate_params)


GDN_CHUNKS_PER_STEP = 4
NEUMANN_BLOCK = 4


def _bf16_all(xs):
    return [x.astype(BF16) for x in xs]


def _dot_all(a_list, b_list):
    return [_dot(a, b) for a, b in zip(a_list, b_list)]


def _unit_lower_inverse_all(a_list, row, col):
    shift = int(math.log2(NEUMANN_BLOCK))
    eye = jnp.where(row == col, 1.0, 0.0).astype(F32)
    on_diag_block = (row >> shift) == (col >> shift)
    d = [jnp.where(on_diag_block, a, 0.0) for a in a_list]
    d_b = _bf16_all(d)
    x = [eye - di for di in d]
    d2_b = _bf16_all(_dot_all(d_b, d_b))
    x = [xi + t for xi, t in zip(x, _dot_all(_bf16_all(x), d2_b))]
    size = NEUMANN_BLOCK
    while size < a_list[0].shape[0]:
        shift = int(math.log2(size))
        rbl, cbl = row >> shift, col >> shift
        below = ((rbl & 1) == 1) & (cbl == rbl - 1)
        l_b = _bf16_all([jnp.where(below, a, 0.0) for a in a_list])
        x_b = _bf16_all(x)
        xl_b = _bf16_all(_dot_all(x_b, l_b))
        x = [xi - t for xi, t in zip(x, _dot_all(xl_b, x_b))]
        size *= 2
    return x


def _gdn_chunk_kernel(q_ref, k_ref, v_ref, z_ref, gb_ref, nw_ref, *rest, nb, nck, n_cast):
    cast_src, (o_ref, *cast_dst), s_ref = rest[:n_cast], rest[n_cast:2 * n_cast + 1], rest[-1]
    for src, dst in zip(cast_src, cast_dst):
        dst[...] = src[...].astype(BF16)
    c = GDN_CHUNK
    nh = GDN_V_HEADS

    @pl.when(pl.program_id(0) == 0)
    def _():
        s_ref[...] = jnp.zeros(s_ref.shape, F32)

    row = lax.broadcasted_iota(jnp.int32, (c, c), 0)
    col = lax.broadcasted_iota(jnp.int32, (c, c), 1)
    causal = row >= col
    strict = row > col
    tril = jnp.where(causal, 1.0, 0.0).astype(BF16)
    rs = lambda ci: slice(ci * c, (ci + 1) * c)
    ks = lambda h: slice((h // (nh // GDN_K_HEADS)) * GDN_HEAD, (h // (nh // GDN_K_HEADS) + 1) * GDN_HEAD)
    vs = lambda h: slice(h * GDN_HEAD, (h + 1) * GDN_HEAD)

    gates = {}
    for ci in range(nck):
        for b in range(nb):
            gb = gb_ref[b, rs(ci), :]
            g_hi, g_lo = _split(gb)
            gc = _dot(tril, g_hi) + _dot(tril, g_lo)
            g_last = gc[c - 1:c, :]
            gates[ci, b] = dict(gb=gb, gc=gc, gc_t=gc.T, eg=jnp.exp(gc), e_last=jnp.exp(g_last),
                                e_rest=jnp.exp(g_last - gc))

    items = [(ci, b, h) for ci in range(nck) for b in range(nb) for h in range(nh)]

    a_kk, a_qk, k_beta = [], {}, {}
    for it in items:
        ci, b, h = it
        gt = gates[ci, b]
        k = k_ref[b, rs(ci), ks(h)]
        kb = k * gt["gb"][:, h:h + 1]
        kq = jnp.concatenate([kb, q_ref[b, rs(ci), ks(h)]], axis=0).astype(BF16)
        gram = _dot_nt(kq, k.astype(BF16))
        gl = nh + h
        decay = jnp.exp(jnp.where(causal, gt["gc"][:, gl:gl + 1] - gt["gc_t"][gl:gl + 1, :], NEG_BIG))
        a_kk.append(jnp.where(strict, gram[:c] * decay, 0.0))
        a_qk[it] = (gram[c:] * decay).astype(BF16)
        k_beta[it] = kb

    t_inv = dict(zip(items, _unit_lower_inverse_all(a_kk, row, col)))

    uw = {}
    for it in items:
        ci, b, h = it
        gt = gates[ci, b]
        gl = nh + h
        rhs = jnp.concatenate([v_ref[b, rs(ci), vs(h)] * gt["gb"][:, h:h + 1],
                               k_beta[it] * gt["eg"][:, gl:gl + 1]], axis=1)
        uw[it] = _dot(t_inv[it].astype(BF16), rhs.astype(BF16))

    for ci in range(nck):
        chunk_items = [(ci, b, h) for b in range(nb) for h in range(nh)]

        ws_qs, states = {}, {}
        for it in chunk_items:
            _, b, h = it
            gl = nh + h
            s = s_ref[b * nh + h]
            wq = jnp.concatenate([uw[it][:, GDN_HEAD:],
                                  q_ref[b, rs(ci), ks(h)] * gates[ci, b]["eg"][:, gl:gl + 1]], axis=0)
            ws_qs[it] = _dot(wq.astype(BF16), s.astype(BF16))
            states[it] = s

        for it in chunk_items:
            _, b, h = it
            gt = gates[ci, b]
            gl = nh + h
            v_new = (uw[it][:, :GDN_HEAD] - ws_qs[it][:c]).astype(BF16)
            o = ws_qs[it][c:] + _dot(a_qk[it], v_new)
            k_dec = (k_ref[b, rs(ci), ks(h)] * gt["e_rest"][:, gl:gl + 1]).astype(BF16)
            s_ref[b * nh + h] = states[it] * gt["e_last"][:, gl:gl + 1] + _dot_tn(k_dec, v_new)
            o = o * lax.rsqrt(jnp.mean(o * o, axis=-1, keepdims=True) + GDN_EPS) * nw_ref[...]
            o_ref[b, rs(ci), vs(h)] = (o * _silu(z_ref[b, rs(ci), vs(h)])).astype(o_ref.dtype)


def _cast_specs(arrays, n_steps, flat_step=lambda j: j):
    specs, shapes = [], []
    for a in arrays:
        span = next(d for d in (1, 2, 4, 8)
                    if (a.shape[0] * d) % n_steps == 0 and (a.shape[0] * d // n_steps) % (2 * SUBLANES) == 0)
        specs.append(pl.BlockSpec((a.shape[0] * span // n_steps, a.shape[1]),
                                  lambda *idx, span=span: (flat_step(*idx) // span, 0)))
        shapes.append(jax.ShapeDtypeStruct(a.shape, BF16))
    return specs, shapes


def _gdn_chunk(q, k, v, zs, gb, norm_w, batch, seq_len, to_bf16):
    rows_per_step = GDN_CHUNK * GDN_CHUNKS_PER_STEP
    n_steps = seq_len // rows_per_step
    seq = lambda a: a.reshape(batch, seq_len, a.shape[-1])
    rows = lambda width: pl.BlockSpec((batch, rows_per_step, width), lambda j: (0, j, 0))
    cast_specs, cast_shapes = _cast_specs(to_bf16, n_steps)
    out, *casted = pl.pallas_call(
        functools.partial(_gdn_chunk_kernel, nb=batch, nck=GDN_CHUNKS_PER_STEP, n_cast=len(to_bf16)),
        grid=(n_steps,),
        in_specs=[rows(GDN_KDIM), rows(GDN_KDIM), rows(GDN_VDIM), rows(GDN_VDIM), rows(LANES),
                  pl.BlockSpec((1, GDN_HEAD), lambda j: (0, 0))] + cast_specs,
        out_specs=[rows(GDN_VDIM)] + cast_specs,
        out_shape=[jax.ShapeDtypeStruct((batch, seq_len, GDN_VDIM), BF16)] + cast_shapes,
        scratch_shapes=[pltpu.VMEM((batch * GDN_V_HEADS, GDN_HEAD, GDN_HEAD), F32)],
        compiler_params=_params("arbitrary"),
        name="gdn_chunk",
    )(seq(q), seq(k), seq(v), seq(zs), seq(gb), norm_w, *to_bf16)
    return out.reshape(batch * seq_len, GDN_VDIM), casted


def _proj_res_ln(a_ref, w_ref, r_ref, g_ref, b_ref):
    return _layer_norm(ALPHA * r_ref[...] + _dot(a_ref[...], w_ref[...]), g_ref[...], b_ref[...])


def _outproj_ffn_kernel(a_ref, wo_ref, r_ref, g0_ref, b0_ref, wg_ref, wu_ref, wd_ref, g1_ref, b1_ref, csrc_ref,
                        o_ref, cdst_ref):
    cdst_ref[...] = csrc_ref[...].astype(BF16)
    x1 = _proj_res_ln(a_ref, wo_ref, r_ref, g0_ref, b0_ref)
    xb = x1.astype(BF16)
    h = (_silu(_dot(xb, wg_ref[...])) * _dot(xb, wu_ref[...])).astype(BF16)
    o_ref[...] = _layer_norm(ALPHA * x1 + _dot(h, wd_ref[...]), g1_ref[...], b1_ref[...])


def _outproj_ffn(a, w_out, res, g0, b0, wg, wu, wd, g1, b1, to_bf16):
    n, kdim = a.shape
    tm = TM_FFN
    (cast_spec,), (cast_shape,) = _cast_specs([to_bf16], n // tm)
    resident = lambda w: pl.BlockSpec(w.shape, lambda i: (0, 0), pipeline_mode=pl.Buffered(1))
    rows = lambda width: pl.BlockSpec((tm, width), lambda i: (i, 0))
    vec = pl.BlockSpec((1, D_MODEL), lambda i: (0, 0))
    return pl.pallas_call(
        _outproj_ffn_kernel,
        grid=(n // tm,),
        in_specs=[rows(kdim), resident(w_out), rows(D_MODEL), vec, vec,
                  resident(wg), resident(wu), resident(wd), vec, vec, cast_spec],
        out_specs=[rows(D_MODEL), cast_spec],
        out_shape=[jax.ShapeDtypeStruct((n, D_MODEL), F32), cast_shape],
        compiler_params=_params("parallel"),
        name="gdn_outproj_ffn",
    )(a, w_out, res, g0, b0, wg, wu, wd, g1, b1, to_bf16)


def _band_tables():
    assert SWA_WINDOW == SWA_BLOCK
    r = np.arange(SWA_BLOCK)[:, None]
    q = np.arange(SWA_BLOCK)[None, :]
    d = (q - r) % SWA_BLOCK
    max_exact = REL_BUCKETS // 2
    df = np.maximum(d, 1).astype(np.float32)
    large = max_exact + (np.log(df / np.float32(max_exact)) / np.float32(math.log(REL_MAX_DIST / max_exact))
                         * np.float32(REL_BUCKETS - max_exact)).astype(np.int32)
    large = np.minimum(large, REL_BUCKETS - 1)
    bucket = np.where(d < max_exact, d, large).astype(np.int32)
    valid = np.stack([q >= r, np.ones_like(q >= r)]).astype(np.int32)
    return bucket, valid


def _bias_kernel(relb_ref, bucket_ref, valid_ref, o_ref):
    bucket = bucket_ref[...]
    in_bucket = [bucket == b for b in range(REL_BUCKETS)]
    for h in range(SWA_Q_HEADS):
        acc = jnp.zeros(bucket.shape, F32)
        for b in range(REL_BUCKETS):
            acc = jnp.where(in_bucket[b], relb_ref[b, h], acc)
        for t in range(2):
            o_ref[t, h] = jnp.where(valid_ref[t] != 0, acc, NEG_BIG)


def _bias_table(rel_bias):
    bucket, valid = _band_tables()
    shape = (SWA_BLOCK, SWA_BLOCK)
    return pl.pallas_call(
        _bias_kernel,
        grid=(1,),
        in_specs=[pl.BlockSpec(memory_space=pltpu.SMEM), pl.BlockSpec(shape, lambda i: (0, 0)),
                  pl.BlockSpec((2,) + shape, lambda i: (0, 0, 0))],
        out_specs=pl.BlockSpec((2, SWA_Q_HEADS) + shape, lambda i: (0, 0, 0, 0)),
        out_shape=jax.ShapeDtypeStruct((2, SWA_Q_HEADS) + shape, F32),
        compiler_params=_params("arbitrary"),
        name="swa_bias_table",
    )(rel_bias, jnp.asarray(bucket), jnp.asarray(valid))


def _band_attention(q_pair, kv, bias_of, sink_ref, put_pair):
    dh = SWA_HEAD_DIM
    blk = SWA_BLOCK
    nk = 2 * blk
    from_prev = lax.broadcasted_iota(jnp.int32, (blk, blk), 1) < lax.broadcasted_iota(jnp.int32, (blk, blk), 0)
    k_all = kv[:, :SWA_KVDIM]
    lane = lax.broadcasted_iota(jnp.int32, k_all.shape, 1)
    zero = jnp.zeros_like(k_all)
    for kh in range(SWA_KV_HEADS):
        k_own = jnp.where((lane >= kh * dh) & (lane < (kh + 1) * dh), k_all, zero)
        k_other = pltpu.roll(k_own.astype(F32), dh, axis=1).astype(BF16)
        k_even, k_odd = (k_own, k_other) if kh == 0 else (k_other, k_own)
        k2 = jnp.concatenate([k_even, k_odd], axis=0)
        v_t = kv[:, SWA_KVDIM + kh * dh:SWA_KVDIM + (kh + 1) * dh].T
        pairs = [kh * (SWA_GROUP // 2) + p for p in range(SWA_GROUP // 2)]
        heads = [2 * p + r for p in pairs for r in range(2)]
        st = [_dot_nt(k2, q_pair(p)) for p in pairs]
        scores = [jnp.where(from_prev, st[i // 2][(i % 2) * nk:(i % 2) * nk + blk],
                            st[i // 2][(i % 2) * nk + blk:(i % 2 + 1) * nk]) + bias_of(h)
                  for i, h in enumerate(heads)]
        sinks = [sink_ref[0:1, h:h + 1] for h in heads]
        maxes = [jnp.maximum(jnp.max(s, axis=0, keepdims=True), sk) for s, sk in zip(scores, sinks)]
        expo = [jnp.exp(s - m) for s, m in zip(scores, maxes)]
        sums = [jnp.sum(p, axis=0, keepdims=True) for p in expo]
        inv_den = [1.0 / (t + jnp.exp(sk - m)) for t, sk, m in zip(sums, sinks, maxes)]
        expo = [jnp.concatenate([jnp.where(from_prev, p, 0.0), jnp.where(from_prev, 0.0, p)], axis=0) for p in expo]
        for i, p in enumerate(pairs):
            o_t = _dot(v_t, jnp.concatenate(expo[2 * i:2 * i + 2], axis=1).astype(BF16))
            z = jnp.concatenate([o_t[:, :SWA_BLOCK] * inv_den[2 * i], o_t[:, SWA_BLOCK:] * inv_den[2 * i + 1]],
                                axis=0)
            put_pair(p, z.T)


def _swa_layer_kernel(x_ref, win_ref, bin_ref, bias_ref, sink_ref, wout_ref, g_ref, b_ref, csrc_ref, o_ref, cdst_ref,
                      q_ref, kv_ref, att_ref, *, nblk):
    j = pl.program_id(1)
    blk = SWA_BLOCK
    tm = nblk * blk
    dh2 = 2 * SWA_HEAD_DIM
    cdst_ref[...] = csrc_ref[...].astype(BF16)

    p = _dot(x_ref[...].astype(BF16), win_ref[...]) + bin_ref[...]
    q_ref[...] = (p[:, :SWA_QDIM] * (SWA_HEAD_DIM ** -0.5)).astype(BF16)

    @pl.when(j == 0)
    def _():
        kv_ref[0:blk, :] = jnp.zeros((blk, 2 * SWA_KVDIM), BF16)

    @pl.when(j != 0)
    def _():
        kv_ref[0:blk, :] = kv_ref[tm:tm + blk, :]

    kv_ref[blk:, :] = p[:, SWA_QDIM:].astype(BF16)

    for i in range(nblk):
        rows = slice(i * blk, (i + 1) * blk)
        table = jnp.where(j == 0, 0, 1) if i == 0 else 1

        def put_pair(pair, o, rows=rows):
            att_ref[rows, pair * dh2:(pair + 1) * dh2] = o.astype(att_ref.dtype)

        _band_attention(lambda pair, rows=rows: q_ref[rows, pair * dh2:(pair + 1) * dh2],
                        kv_ref[i * blk:(i + 2) * blk, :],
                        lambda h, table=table: bias_ref[table, h], sink_ref, put_pair)

    o_ref[...] = _proj_res_ln(att_ref, wout_ref, x_ref, g_ref, b_ref)


def _swa_layer(x, w_in, b_in, bias, sinks, w_out, g, b, batch, seq_len, to_bf16):
    n = x.shape[0]
    nblk = SWA_BLOCKS_PER_STEP
    tm = nblk * SWA_BLOCK
    steps = seq_len // tm
    (cast_spec,), (cast_shape,) = _cast_specs([to_bf16], batch * steps, lambda s, j: s * steps + j)
    resident = lambda a: pl.BlockSpec(a.shape, lambda s, j: (0,) * a.ndim, pipeline_mode=pl.Buffered(1))
    rows = pl.BlockSpec((tm, D_MODEL), lambda s, j: (s * steps + j, 0))
    return pl.pallas_call(
        functools.partial(_swa_layer_kernel, nblk=nblk),
        grid=(batch, steps),
        in_specs=[rows, resident(w_in), resident(b_in), resident(bias), resident(sinks), resident(w_out),
                  resident(g), resident(b), cast_spec],
        out_specs=[rows, cast_spec],
        out_shape=[jax.ShapeDtypeStruct((n, D_MODEL), F32), cast_shape],
        scratch_shapes=[pltpu.VMEM((tm, SWA_QDIM), BF16), pltpu.VMEM((tm + SWA_BLOCK, 2 * SWA_KVDIM), BF16),
                        pltpu.VMEM((tm, SWA_QDIM), BF16)],
        compiler_params=_params("parallel", "arbitrary"),
        name="swa_layer",
    )(x, w_in, b_in, bias, sinks, w_out, g, b, to_bf16)


def _router_kernel(x_ref, wr_ref, route_ref, route_t_ref, cnt_ref, run_ref, *, tm):
    @pl.when(pl.program_id(0) == 0)
    def _():
        run_ref[...] = jnp.zeros(run_ref.shape, F32)

    logits = _dot3(x_ref[...], wr_ref[...])
    lane = lax.broadcasted_iota(jnp.int32, logits.shape, 1)
    lane_f = lane.astype(F32)
    lg = jnp.where(lane < N_EXPERTS, logits, NEG_BIG)
    m1 = jnp.max(lg, axis=-1, keepdims=True)
    i1 = jnp.min(jnp.where(lg == m1, lane_f, float(LANES)), axis=-1, keepdims=True)
    oh1 = lane_f == i1
    lg2 = jnp.where(oh1, NEG_BIG, lg)
    m2 = jnp.max(lg2, axis=-1, keepdims=True)
    i2 = jnp.min(jnp.where(lg2 == m2, lane_f, float(LANES)), axis=-1, keepdims=True)
    oh2 = lane_f == i2
    e = jnp.exp(m2 - m1)
    w0 = 1.0 / (1.0 + e)
    w1 = e * w0

    cnt = jnp.where(oh1, 1.0, 0.0) + jnp.where(oh2, 1.0, 0.0)
    r = lax.broadcasted_iota(jnp.int32, (tm, tm), 0)
    c = lax.broadcasted_iota(jnp.int32, (tm, tm), 1)
    before = jnp.where(r > c, 1.0, 0.0).astype(BF16)
    excl = _dot(before, cnt.astype(BF16)) + run_ref[...]
    rank0 = jnp.sum(jnp.where(oh1, excl, 0.0), axis=-1, keepdims=True)
    rank1 = jnp.sum(jnp.where(oh2, excl, 0.0), axis=-1, keepdims=True)
    run = run_ref[...] + jnp.sum(cnt, axis=0, keepdims=True)
    run_ref[...] = run
    cnt_ref[...] = run

    vals = (i1, i2, rank0, rank1, w0, w1)
    out = jnp.zeros(logits.shape, F32)
    for idx, val in enumerate(vals):
        out = jnp.where(lane == idx, val, out)
    route_ref[...] = out
    route_t_ref[...] = out.T[:SUBLANES, :]


def _router(x, w_router_padded):
    n = x.shape[0]
    tm = TM_ROUTER
    return pl.pallas_call(
        functools.partial(_router_kernel, tm=tm),
        grid=(n // tm,),
        in_specs=[pl.BlockSpec((tm, D_MODEL), lambda i: (i, 0)), pl.BlockSpec((D_MODEL, LANES), lambda i: (0, 0))],
        out_specs=[pl.BlockSpec((tm, LANES), lambda i: (i, 0)), pl.BlockSpec((SUBLANES, tm), lambda i: (0, i)),
                   pl.BlockSpec((1, LANES), lambda i: (0, 0))],
        out_shape=[jax.ShapeDtypeStruct((n, LANES), F32), jax.ShapeDtypeStruct((SUBLANES, n), F32),
                   jax.ShapeDtypeStruct((1, LANES), F32)],
        scratch_shapes=[pltpu.VMEM((1, LANES), F32)],
        compiler_params=_params("arbitrary"),
        name="moe_router",
    )(x, w_router_padded)


def _dispatch_kernel(pos0_ref, pos1_ref, meta_ref, x_ref, wsrc_ref, xs_hbm, wdst_ref, zero_ref, sem, pad_sem, *, td):
    i = pl.program_id(0)
    wdst_ref[...] = wsrc_ref[...].astype(BF16)

    def row_copy(t, dst):
        return pltpu.make_async_copy(x_ref.at[pl.ds(t, 1)], xs_hbm.at[pl.ds(dst, 1)], sem)

    def start(t, carry):
        row_copy(t, pos0_ref[0, 0, t]).start()
        row_copy(t, pos1_ref[0, 0, t]).start(priority=1)
        return carry

    lax.fori_loop(0, td, start, 0, unroll=DMA_UNROLL)

    @pl.when(i == pl.num_programs(0) - 1)
    def _():
        zero_ref[...] = jnp.zeros(zero_ref.shape, F32)

        zr = zero_ref.shape[0]

        def row_zero(dst):
            return pltpu.make_async_copy(zero_ref.at[pl.ds(0, 1)], xs_hbm.at[pl.ds(dst, 1)], pad_sem)

        def block_zero(blk):
            return pltpu.make_async_copy(zero_ref, xs_hbm.at[pl.ds(pl.multiple_of(blk * zr, zr), zr)], pad_sem)

        def zero_range(copy, lo, hi):
            lax.fori_loop(lo, hi, lambda r, carry: (copy(r).start(), carry)[1], 0)
            lax.fori_loop(lo, hi, lambda r, carry: (copy(0).wait(), carry)[1], 0)

        for e in range(N_EXPERTS):
            off, cnt, cnt_up, padded = (meta_ref[r, e] for r in range(4))
            zero_range(lambda r, off=off: row_zero(off + r), cnt, cnt_up)
            zero_range(block_zero, (off + cnt_up) // zr, (off + padded) // zr)

        used = meta_ref[0, N_EXPERTS - 1] + meta_ref[3, N_EXPERTS - 1]
        zero_range(block_zero, used // zr, xs_hbm.shape[0] // zr)

    for _ in range(2):
        pltpu.make_async_copy(x_ref, xs_hbm.at[pl.ds(0, td)], sem).wait()


def _dispatch(x, pos, meta, n_rows, to_bf16):
    n = x.shape[0]
    td = TD_DISPATCH
    pos_spec = pl.BlockSpec((1, 1, td), lambda i: (i, 0, 0), memory_space=pltpu.SMEM)
    (cast_spec,), (cast_shape,) = _cast_specs([to_bf16], n // td)
    return pl.pallas_call(
        functools.partial(_dispatch_kernel, td=td),
        grid=(n // td,),
        in_specs=[pos_spec, pos_spec, pl.BlockSpec(memory_space=pltpu.SMEM),
                  pl.BlockSpec((td, D_MODEL), lambda i: (i, 0)), cast_spec],
        out_specs=[pl.BlockSpec(memory_space=pl.ANY), cast_spec],
        out_shape=[jax.ShapeDtypeStruct((n_rows, D_MODEL), F32), cast_shape],
        scratch_shapes=[pltpu.VMEM((ZERO_ROWS, D_MODEL), F32), pltpu.SemaphoreType.DMA, pltpu.SemaphoreType.DMA],
        compiler_params=_params("arbitrary"),
        name="moe_dispatch",
    )(pos[0].reshape(n // td, 1, td), pos[1].reshape(n // td, 1, td), meta, x, to_bf16)


def _moe_kernel(we_ref, wb_ref, nv_ref, cnt_ref, *refs, n_sub, tm):
    xs_refs = refs[:n_sub]
    wg_ref, wu_ref, wd_ref, ys_hbm, acc_ref, xb_ref, sem = refs[n_sub:]
    j = pl.program_id(0)
    f = pl.program_id(1)
    last_f = pl.num_programs(1) - 1
    n_windows, n_tiles = cnt_ref[0], cnt_ref[1]

    def tile_write(slot, block):
        return pltpu.make_async_copy(acc_ref.at[slot], ys_hbm.at[pl.ds(pl.multiple_of(block * tm, tm), tm)],
                                     sem.at[slot])

    @pl.when(j < n_windows)
    def _():
        prev_tiles = jnp.where(j > 0, nv_ref[jnp.maximum(j - 1, 0)], 0)

        for s in range(n_sub):
            @pl.when(s < nv_ref[j])
            def _(s=s):
                @pl.when(f == 0)
                def _():
                    xb_ref[s] = xs_refs[s][...].astype(BF16)

                xb = xb_ref[s]
                h = (_silu(_dot(xb, wg_ref[0])) * _dot(xb, wu_ref[0])).astype(BF16)
                part = _dot(h, wd_ref[0])

                @pl.when(f == 0)
                def _():
                    @pl.when(s < prev_tiles)
                    def _():
                        tile_write(s, 0).wait()

                    acc_ref[s] = part

                @pl.when(f != 0)
                def _():
                    acc_ref[s] += part

                @pl.when(f == last_f)
                def _():
                    tile_write(s, wb_ref[j] + s).start()

            @pl.when((f == 0) & (s >= nv_ref[j]) & (s < prev_tiles))
            def _(s=s):
                tile_write(s, 0).wait()

        @pl.when((f == last_f) & (j == n_windows - 1))
        def _():
            for s in range(n_sub):
                @pl.when(s < nv_ref[j])
                def _(s=s):
                    tile_write(s, 0).wait()

            acc_ref[0] = jnp.zeros(acc_ref.shape[1:], F32)
            total_tiles = ys_hbm.shape[0] // tm
            lax.fori_loop(n_tiles, total_tiles, lambda t, c: (tile_write(0, t).start(), c)[1], 0)
            lax.fori_loop(n_tiles, total_tiles, lambda t, c: (tile_write(0, 0).wait(), c)[1], 0)


def _moe_experts(xs, wg, wu, wd, win_expert, win_block, win_tiles, counts):
    n_rows = xs.shape[0]
    tm, tf, n_sub = TM_MOE, TF_MOE, MOE_TILES_PER_WINDOW
    max_windows = win_expert.shape[0]

    def xs_spec(s):
        return pl.BlockSpec((tm, D_MODEL), lambda j, f, we, wb, nv, cnt: (wb[j] + jnp.minimum(s, nv[j] - 1), 0))

    grid_spec = pltpu.PrefetchScalarGridSpec(
        num_scalar_prefetch=4,
        grid=(max_windows, EXPERT_DIM // tf),
        in_specs=[xs_spec(s) for s in range(n_sub)] + [
            pl.BlockSpec((1, D_MODEL, tf), lambda j, f, we, wb, nv, cnt: (we[j], 0, f)),
            pl.BlockSpec((1, D_MODEL, tf), lambda j, f, we, wb, nv, cnt: (we[j], 0, f)),
            pl.BlockSpec((1, tf, D_MODEL), lambda j, f, we, wb, nv, cnt: (we[j], f, 0))],
        out_specs=pl.BlockSpec(memory_space=pl.ANY),
        scratch_shapes=[pltpu.VMEM((n_sub, tm, D_MODEL), F32), pltpu.VMEM((n_sub, tm, D_MODEL), BF16),
                        pltpu.SemaphoreType.DMA((n_sub,))],
    )
    return pl.pallas_call(
        functools.partial(_moe_kernel, n_sub=n_sub, tm=tm),
        grid_spec=grid_spec,
        out_shape=jax.ShapeDtypeStruct((n_rows, D_MODEL), F32),
        compiler_params=_params("arbitrary", "arbitrary"),
        name="moe_experts",
    )(win_expert, win_block, win_tiles, counts, *([xs] * n_sub), wg, wu, wd)


def _combine_kernel(pos0_ref, pos1_ref, pos0n_ref, pos1n_ref, x_ref, route_ref, ys_hbm, g_ref, b_ref, o_ref,
                    y_ref, sem, *, tc):
    i = pl.program_id(0)
    slot = i % 2

    def issue(p_refs, s):
        def start(t, carry):
            for kk in range(2):
                pltpu.make_async_copy(ys_hbm.at[pl.ds(p_refs[kk][0, 0, t], 1)],
                                      y_ref.at[s, kk, pl.ds(t, 1)], sem.at[s]).start(priority=kk)
            return carry

        lax.fori_loop(0, tc, start, 0, unroll=DMA_UNROLL)

    @pl.when(i == 0)
    def _():
        issue((pos0_ref, pos1_ref), 0)

    @pl.when(i + 1 < pl.num_programs(0))
    def _():
        issue((pos0n_ref, pos1n_ref), 1 - slot)

    for kk in range(2):
        pltpu.make_async_copy(ys_hbm.at[pl.ds(0, tc)], y_ref.at[slot, kk], sem.at[slot]).wait()

    route = route_ref[...]
    f = route[:, 4:5] * y_ref[slot, 0] + route[:, 5:6] * y_ref[slot, 1]
    o_ref[...] = _layer_norm(ALPHA * x_ref[...] + f, g_ref[...], b_ref[...])


def _combine_ln(x, route, pos, ys, g, b):
    n = x.shape[0]
    tc = TC_COMBINE
    nt = n // tc
    pos0, pos1 = pos[0].reshape(nt, 1, tc), pos[1].reshape(nt, 1, tc)
    cur = pl.BlockSpec((1, 1, tc), lambda i: (i, 0, 0), memory_space=pltpu.SMEM)
    nxt = pl.BlockSpec((1, 1, tc), lambda i: (jnp.minimum(i + 1, nt - 1), 0, 0), memory_space=pltpu.SMEM)
    return pl.pallas_call(
        functools.partial(_combine_kernel, tc=tc),
        grid=(nt,),
        in_specs=[cur, cur, nxt, nxt,
                  pl.BlockSpec((tc, D_MODEL), lambda i: (i, 0)),
                  pl.BlockSpec((tc, LANES), lambda i: (i, 0)),
                  pl.BlockSpec(memory_space=pl.ANY),
                  pl.BlockSpec((1, D_MODEL), lambda i: (0, 0)), pl.BlockSpec((1, D_MODEL), lambda i: (0, 0))],
        out_specs=pl.BlockSpec((tc, D_MODEL), lambda i: (i, 0)),
        out_shape=jax.ShapeDtypeStruct((n, D_MODEL), F32),
        scratch_shapes=[pltpu.VMEM((2, 2, tc, D_MODEL), F32), pltpu.SemaphoreType.DMA((2,))],
        compiler_params=_params("arbitrary"),
        name="moe_combine_ln",
    )(pos0, pos1, pos0, pos1, x, route, ys, g, b)


def _moe_layer(x, route, route_t, counts_f, wg, wu, wd_f32, g, b):
    n = x.shape[0]
    tm = TM_MOE

    counts = counts_f[0, :N_EXPERTS].astype(jnp.int32)
    padded = ((counts + tm - 1) // tm) * tm
    offs = jnp.cumsum(padded) - padded
    eids = jnp.arange(N_EXPERTS, dtype=jnp.int32)
    e01 = route_t[0:2].astype(jnp.int32)
    rank01 = route_t[2:4].astype(jnp.int32)
    base = jnp.sum(jnp.where(e01[:, None, :] == eids[None, :, None], offs[None, :, None], 0), axis=1)
    pos = base + rank01
    counts_up = jnp.minimum(padded, ((counts + ZERO_ROWS - 1) // ZERO_ROWS) * ZERO_ROWS)
    meta = jnp.stack([offs, counts, counts_up, padded]).astype(jnp.int32)

    n_rows = 2 * n + N_EXPERTS * tm
    nsub = MOE_TILES_PER_WINDOW
    tiles = padded // tm
    wins = (tiles + nsub - 1) // nsub
    win_end = jnp.cumsum(wins)
    n_windows = win_end[-1]
    max_windows = (n_rows // tm + nsub - 1) // nsub + N_EXPERTS
    wj = jnp.minimum(jnp.arange(max_windows, dtype=jnp.int32), n_windows - 1)
    win_expert = jnp.minimum(jnp.sum((wj[:, None] >= win_end[None, :]).astype(jnp.int32), axis=-1), N_EXPERTS - 1)
    pick = lambda v: jnp.sum(jnp.where(win_expert[:, None] == eids[None, :], v[None, :], 0), axis=-1)
    local = wj - pick(win_end - wins)
    win_block = pick(offs // tm) + nsub * local
    win_tiles = jnp.minimum(nsub, pick(tiles) - nsub * local)
    counts_nt = jnp.stack([n_windows, jnp.sum(tiles)])

    xs, wd = _dispatch(x, pos, meta, n_rows, wd_f32.reshape(-1, D_MODEL))
    wd = wd.reshape(wd_f32.shape)
    i32 = lambda a: a.astype(jnp.int32)
    ys = _moe_experts(xs, wg, wu, wd, i32(win_expert), i32(win_block), i32(win_tiles), i32(counts_nt))
    return _combine_ln(x, route, pos, ys, g, b)


def kernel(x, a_w_in, a_conv_w, a_a_log, a_dt_bias, a_norm_w, a_w_out, b_w_in, b_b_in, b_sinks, b_w_out, rel_bias,
           ffn_w_gate, ffn_w_up, ffn_w_down, moe_router, moe_w_gate, moe_w_up, moe_w_down, ln_g, ln_b):
    batch, seq_len, _ = x.shape
    n = batch * seq_len
    x0 = x.reshape(n, D_MODEL)
    ln_g = ln_g.reshape(DEPTH, 2, 1, D_MODEL)
    ln_b = ln_b.reshape(DEPTH, 2, 1, D_MODEL)

    pad_gate = lambda p: jnp.pad(p.reshape(1, GDN_V_HEADS), ((0, 0), (GDN_V_HEADS, LANES - 2 * GDN_V_HEADS)))
    gate_params = jnp.concatenate([pad_gate(a_a_log[0]), pad_gate(a_dt_bias[0])], axis=0)
    q, k, v, zs, gb = _gdn_inproj(x0, a_w_in, a_conv_w[0], gate_params, seq_len)
    o, casted = _gdn_chunk(q, k, v, zs, gb, a_norm_w[0].reshape(1, GDN_HEAD), batch, seq_len,
                           [ffn_w_gate[0], ffn_w_up[0], ffn_w_down[0], a_w_out[0], b_w_in[0], b_w_out[0]])
    ffn_wg, ffn_wu, ffn_wd, w_out_a, w_in_b, w_out_b = casted
    x2, moe_wu = _outproj_ffn(o, w_out_a, x0, ln_g[0, 0], ln_b[0, 0], ffn_wg, ffn_wu, ffn_wd,
                              ln_g[0, 1], ln_b[0, 1], moe_w_up[0].reshape(-1, EXPERT_DIM))

    bias = _bias_table(rel_bias)
    sinks = jnp.pad(b_sinks[0].reshape(1, SWA_Q_HEADS), ((0, 0), (0, LANES - SWA_Q_HEADS)))
    x3, moe_wg = _swa_layer(x2, w_in_b, b_b_in[0].reshape(1, -1), bias, sinks, w_out_b, ln_g[1, 0], ln_b[1, 0],
                            batch, seq_len, moe_w_gate[0].reshape(-1, EXPERT_DIM))
    w_router = jnp.pad(moe_router[0], ((0, 0), (0, LANES - N_EXPERTS)))
    route, route_t, counts = _router(x3, w_router)
    expert_shape = (N_EXPERTS, D_MODEL, EXPERT_DIM)
    x4 = _moe_layer(x3, route, route_t, counts, moe_wg.reshape(expert_shape), moe_wu.reshape(expert_shape),
                    moe_w_down[0], ln_g[1, 1], ln_b[1, 1])
    return x4.reshape(batch, seq_len, D_MODEL)
```

```python
import functools
import math

import numpy as np
import jax
import jax.numpy as jnp
from jax import lax
from jax.experimental import pallas as pl
from jax.experimental.pallas import tpu as pltpu

F32 = jnp.float32
BF16 = jnp.bfloat16

D_MODEL = 1024
DEPTH = 2
ALPHA = (2.0 * DEPTH) ** 0.25
LN_EPS = 1e-5

GDN_K_HEADS = 4
GDN_V_HEADS = 8
GDN_HEAD = 128
GDN_KDIM = GDN_K_HEADS * GDN_HEAD
GDN_VDIM = GDN_V_HEADS * GDN_HEAD
GDN_CONV = 4
GDN_CHUNK = 64
GDN_QKV = 2 * GDN_KDIM + GDN_VDIM
GDN_EPS = 1e-6

SWA_Q_HEADS = 16
SWA_KV_HEADS = 2
SWA_GROUP = SWA_Q_HEADS // SWA_KV_HEADS
SWA_HEAD_DIM = 64
SWA_WINDOW = 128
SWA_BLOCK = 128
SWA_QDIM = SWA_Q_HEADS * SWA_HEAD_DIM
SWA_KVDIM = SWA_KV_HEADS * SWA_HEAD_DIM
REL_BUCKETS = 32
REL_MAX_DIST = 128

FFN_DIM = 2816
N_EXPERTS = 8
EXPERT_DIM = 3584

LANES = 128
SUBLANES = 8
NEG_BIG = -1e30
VMEM_LIMIT = 56 * 1024 * 1024

TM_GDN_IN = 512
TM_FFN = 512
SWA_BLOCKS_PER_STEP = 8
TM_ROUTER = 1024
TM_MOE = 512
TF_MOE = 1792
MOE_TILES_PER_WINDOW = 3
TD_DISPATCH = 1024
TC_COMBINE = 512
ZERO_ROWS = 64
DMA_UNROLL = 8


def _params(*sem):
    return pltpu.CompilerParams(dimension_semantics=sem, vmem_limit_bytes=VMEM_LIMIT)


def _dot(a, b):
    return jnp.dot(a, b, preferred_element_type=F32)


def _dot_nt(a, b):
    return lax.dot_general(a, b, (((1,), (1,)), ((), ())), preferred_element_type=F32)


def _dot_tn(a, b):
    return lax.dot_general(a, b, (((0,), (0,)), ((), ())), preferred_element_type=F32)


def _split(x):
    hi = x.astype(BF16)
    lo = (x - hi.astype(F32)).astype(BF16)
    return hi, lo


def _dot3(a, b):
    ah, al = _split(a)
    bh, bl = _split(b)
    return _dot(ah, bh) + (_dot(ah, bl) + _dot(al, bh))


def _silu(x):
    return x * jax.nn.sigmoid(x)


def _layer_norm(y, g, b):
    mu = jnp.mean(y, axis=-1, keepdims=True)
    yc = y - mu
    var = jnp.mean(yc * yc, axis=-1, keepdims=True)
    return yc * lax.rsqrt(var + LN_EPS) * g + b


def _gdn_inproj_kernel(x_ref, wf_ref, convw_ref, gp_ref,
                       q_ref, k_ref, v_ref, z_ref, gb_ref, ext_ref, w_ref, *, tm, tiles_per_seq):
    i = pl.program_id(0)
    xb = x_ref[...].astype(BF16)

    @pl.when(i == 0)
    def _():
        n_in = wf_ref.shape[1]
        w_ref[:n_in, :] = wf_ref[0].astype(BF16)
        w_ref[n_in:, :] = jnp.zeros((w_ref.shape[0] - n_in, w_ref.shape[1]), BF16)

    @pl.when(i % tiles_per_seq == 0)
    def _():
        ext_ref[0:SUBLANES, :] = jnp.zeros((SUBLANES, GDN_QKV), F32)

    @pl.when(i % tiles_per_seq != 0)
    def _():
        ext_ref[0:SUBLANES, :] = ext_ref[tm:tm + SUBLANES, :]

    ext_ref[SUBLANES:, :] = _dot_nt(xb, w_ref[:GDN_QKV, :])
    z_ref[...] = _dot_nt(xb, w_ref[GDN_QKV:GDN_QKV + GDN_VDIM, :])

    n_chunks = GDN_QKV // LANES
    for c in range(n_chunks):
        cs = slice(c * LANES, (c + 1) * LANES)
        acc = convw_ref[GDN_CONV - 1:GDN_CONV, cs] * ext_ref[SUBLANES:SUBLANES + tm, cs]
        for j in range(GDN_CONV - 1):
            off = SUBLANES - (GDN_CONV - 1) + j
            acc = acc + convw_ref[j:j + 1, cs] * ext_ref[off:off + tm, cs]
        y = _silu(acc)
        if c < 2 * GDN_K_HEADS:
            y = y * lax.rsqrt(jnp.sum(y * y, axis=-1, keepdims=True) + GDN_EPS)
            if c < GDN_K_HEADS:
                q_ref[:, cs] = y * (GDN_HEAD ** -0.5)
            else:
                k_ref[:, (c - GDN_K_HEADS) * LANES:(c - GDN_K_HEADS + 1) * LANES] = y
        else:
            cv = c - 2 * GDN_K_HEADS
            v_ref[:, cv * LANES:(cv + 1) * LANES] = y

    ba = _dot_nt(xb, w_ref[GDN_QKV + GDN_VDIM:, :])
    lane = lax.broadcasted_iota(jnp.int32, ba.shape, 1)
    sp = ba + gp_ref[1:2, :]
    softplus = jnp.maximum(sp, 0.0) + jnp.log(1.0 + jnp.exp(-jnp.abs(sp)))
    g = -jnp.exp(gp_ref[0:1, :]) * softplus
    gb_ref[...] = jnp.where(lane < GDN_V_HEADS, jax.nn.sigmoid(ba), g)


def _gdn_inproj(x2d, w_in, conv_w, gate_params, seq_len):
    n = x2d.shape[0]
    tm = TM_GDN_IN
    kern = functools.partial(_gdn_inproj_kernel, tm=tm, tiles_per_seq=seq_len // tm)
    full = lambda shape: pl.BlockSpec(shape, lambda i: (0,) * len(shape))
    rows = lambda width: pl.BlockSpec((tm, width), lambda i: (i, 0))
    return pl.pallas_call(
        kern,
        grid=(n // tm,),
        in_specs=[rows(D_MODEL), pl.BlockSpec(w_in.shape, lambda i: (0, 0, 0), pipeline_mode=pl.Buffered(1)),
                  full(conv_w.shape), full(gate_params.shape)],
        out_specs=[rows(GDN_KDIM), rows(GDN_KDIM), rows(GDN_VDIM), rows(GDN_VDIM), rows(LANES)],
        out_shape=[jax.ShapeDtypeStruct((n, GDN_KDIM), F32), jax.ShapeDtypeStruct((n, GDN_KDIM), F32),
                   jax.ShapeDtypeStruct((n, GDN_VDIM), F32), jax.ShapeDtypeStruct((n, GDN_VDIM), F32),
                   jax.ShapeDtypeStruct((n, LANES), F32)],
        scratch_shapes=[pltpu.VMEM((tm + SUBLANES, GDN_QKV), F32),
                        pltpu.VMEM((GDN_QKV + GDN_VDIM + LANES, D_MODEL), BF16)],
        compiler_params=_params("arbitrary"),
        name="gdn_inproj",
    )(x2d, w_in, conv_w, gate_params)


GDN_CHUNKS_PER_STEP = 2
NEUMANN_BLOCK = 4


def _bf16_all(xs):
    return [x.astype(BF16) for x in xs]


def _dot_all(a_list, b_list):
    return [_dot(a, b) for a, b in zip(a_list, b_list)]


def _unit_lower_inverse_all(a_list, row, col):
    shift = int(math.log2(NEUMANN_BLOCK))
    eye = jnp.where(row == col, 1.0, 0.0).astype(F32)
    on_diag_block = (row >> shift) == (col >> shift)
    d = [jnp.where(on_diag_block, a, 0.0) for a in a_list]
    d_b = _bf16_all(d)
    x = [eye - di for di in d]
    d2_b = _bf16_all(_dot_all(d_b, d_b))
    x = [xi + t for xi, t in zip(x, _dot_all(_bf16_all(x), d2_b))]
    size = NEUMANN_BLOCK
    while size < a_list[0].shape[0]:
        shift = int(math.log2(size))
        rbl, cbl = row >> shift, col >> shift
        below = ((rbl & 1) == 1) & (cbl == rbl - 1)
        l_b = _bf16_all([jnp.where(below, a, 0.0) for a in a_list])
        x_b = _bf16_all(x)
        xl_b = _bf16_all(_dot_all(x_b, l_b))
        x = [xi - t for xi, t in zip(x, _dot_all(xl_b, x_b))]
        size *= 2
    return x


def _gdn_chunk_kernel(q_ref, k_ref, v_ref, z_ref, gb_ref, nw_ref, *rest, nb, nck, n_cast):
    cast_src, (o_ref, *cast_dst), s_ref = rest[:n_cast], rest[n_cast:2 * n_cast + 1], rest[-1]
    for src, dst in zip(cast_src, cast_dst):
        dst[...] = src[...].astype(BF16)
    c = GDN_CHUNK
    nh = GDN_V_HEADS

    @pl.when(pl.program_id(0) == 0)
    def _():
        s_ref[...] = jnp.zeros(s_ref.shape, F32)

    row = lax.broadcasted_iota(jnp.int32, (c, c), 0)
    col = lax.broadcasted_iota(jnp.int32, (c, c), 1)
    causal = row >= col
    strict = row > col
    tril = jnp.where(causal, 1.0, 0.0).astype(BF16)
    rs = lambda ci: slice(ci * c, (ci + 1) * c)
    ks = lambda h: slice((h // (nh // GDN_K_HEADS)) * GDN_HEAD, (h // (nh // GDN_K_HEADS) + 1) * GDN_HEAD)
    vs = lambda h: slice(h * GDN_HEAD, (h + 1) * GDN_HEAD)

    gates = {}
    for ci in range(nck):
        for b in range(nb):
            gb = gb_ref[b, rs(ci), :]
            g_hi, g_lo = _split(gb)
            gc = _dot(tril, g_hi) + _dot(tril, g_lo)
            g_last = gc[c - 1:c, :]
            gates[ci, b] = dict(gb=gb, gc=gc, gc_t=gc.T, eg=jnp.exp(gc), e_last=jnp.exp(g_last),
                                e_rest=jnp.exp(g_last - gc))

    items = [(ci, b, h) for ci in range(nck) for b in range(nb) for h in range(nh)]

    a_kk, a_qk, k_beta = [], {}, {}
    for it in items:
        ci, b, h = it
        gt = gates[ci, b]
        k = k_ref[b, rs(ci), ks(h)]
        kb = k * gt["gb"][:, h:h + 1]
        kq = jnp.concatenate([kb, q_ref[b, rs(ci), ks(h)]], axis=0).astype(BF16)
        gram = _dot_nt(kq, k.astype(BF16))
        gl = nh + h
        decay = jnp.exp(jnp.where(causal, gt["gc"][:, gl:gl + 1] - gt["gc_t"][gl:gl + 1, :], NEG_BIG))
        a_kk.append(jnp.where(strict, gram[:c] * decay, 0.0))
        a_qk[it] = (gram[c:] * decay).astype(BF16)
        k_beta[it] = kb

    t_inv = dict(zip(items, _unit_lower_inverse_all(a_kk, row, col)))

    uw = {}
    for it in items:
        ci, b, h = it
        gt = gates[ci, b]
        gl = nh + h
        rhs = jnp.concatenate([v_ref[b, rs(ci), vs(h)] * gt["gb"][:, h:h + 1],
                               k_beta[it] * gt["eg"][:, gl:gl + 1]], axis=1)
        uw[it] = _dot(t_inv[it].astype(BF16), rhs.astype(BF16))

    for ci in range(nck):
        chunk_items = [(ci, b, h) for b in range(nb) for h in range(nh)]

        ws_qs, states = {}, {}
        for it in chunk_items:
            _, b, h = it
            gl = nh + h
            s = s_ref[b * nh + h]
            wq = jnp.concatenate([uw[it][:, GDN_HEAD:],
                                  q_ref[b, rs(ci), ks(h)] * gates[ci, b]["eg"][:, gl:gl + 1]], axis=0)
            ws_qs[it] = _dot(wq.astype(BF16), s.astype(BF16))
            states[it] = s

        for it in chunk_items:
            _, b, h = it
            gt = gates[ci, b]
            gl = nh + h
            v_new = (uw[it][:, :GDN_HEAD] - ws_qs[it][:c]).astype(BF16)
            o = ws_qs[it][c:] + _dot(a_qk[it], v_new)
            k_dec = (k_ref[b, rs(ci), ks(h)] * gt["e_rest"][:, gl:gl + 1]).astype(BF16)
            s_ref[b * nh + h] = states[it] * gt["e_last"][:, gl:gl + 1] + _dot_tn(k_dec, v_new)
            o = o * lax.rsqrt(jnp.mean(o * o, axis=-1, keepdims=True) + GDN_EPS) * nw_ref[...]
            o_ref[b, rs(ci), vs(h)] = (o * _silu(z_ref[b, rs(ci), vs(h)])).astype(o_ref.dtype)


def _cast_specs(arrays, n_steps):
    specs, shapes = [], []
    for a in arrays:
        span = next(d for d in (1, 2, 4, 8)
                    if (a.shape[0] * d) % n_steps == 0 and (a.shape[0] * d // n_steps) % (2 * SUBLANES) == 0)
        specs.append(pl.BlockSpec((a.shape[0] * span // n_steps, a.shape[1]), lambda j, span=span: (j // span, 0)))
        shapes.append(jax.ShapeDtypeStruct(a.shape, BF16))
    return specs, shapes


def _gdn_chunk(q, k, v, zs, gb, norm_w, batch, seq_len, to_bf16):
    rows_per_step = GDN_CHUNK * GDN_CHUNKS_PER_STEP
    n_steps = seq_len // rows_per_step
    seq = lambda a: a.reshape(batch, seq_len, a.shape[-1])
    rows = lambda width: pl.BlockSpec((batch, rows_per_step, width), lambda j: (0, j, 0))
    cast_specs, cast_shapes = _cast_specs(to_bf16, n_steps)
    out, *casted = pl.pallas_call(
        functools.partial(_gdn_chunk_kernel, nb=batch, nck=GDN_CHUNKS_PER_STEP, n_cast=len(to_bf16)),
        grid=(n_steps,),
        in_specs=[rows(GDN_KDIM), rows(GDN_KDIM), rows(GDN_VDIM), rows(GDN_VDIM), rows(LANES),
                  pl.BlockSpec((1, GDN_HEAD), lambda j: (0, 0))] + cast_specs,
        out_specs=[rows(GDN_VDIM)] + cast_specs,
        out_shape=[jax.ShapeDtypeStruct((batch, seq_len, GDN_VDIM), BF16)] + cast_shapes,
        scratch_shapes=[pltpu.VMEM((batch * GDN_V_HEADS, GDN_HEAD, GDN_HEAD), F32)],
        compiler_params=_params("arbitrary"),
        name="gdn_chunk",
    )(seq(q), seq(k), seq(v), seq(zs), seq(gb), norm_w, *to_bf16)
    return out.reshape(batch * seq_len, GDN_VDIM), casted


def _proj_res_ln(a_ref, w_ref, r_ref, g_ref, b_ref):
    return _layer_norm(ALPHA * r_ref[...] + _dot(a_ref[...], w_ref[...]), g_ref[...], b_ref[...])


def _outproj_ffn_kernel(a_ref, wo_ref, r_ref, g0_ref, b0_ref, wg_ref, wu_ref, wd_ref, g1_ref, b1_ref, o_ref):
    x1 = _proj_res_ln(a_ref, wo_ref, r_ref, g0_ref, b0_ref)
    xb = x1.astype(BF16)
    h = (_silu(_dot(xb, wg_ref[...])) * _dot(xb, wu_ref[...])).astype(BF16)
    o_ref[...] = _layer_norm(ALPHA * x1 + _dot(h, wd_ref[...]), g1_ref[...], b1_ref[...])


def _outproj_ffn(a, w_out, res, g0, b0, wg, wu, wd, g1, b1):
    n, kdim = a.shape
    tm = TM_FFN
    resident = lambda w: pl.BlockSpec(w.shape, lambda i: (0, 0), pipeline_mode=pl.Buffered(1))
    rows = lambda width: pl.BlockSpec((tm, width), lambda i: (i, 0))
    vec = pl.BlockSpec((1, D_MODEL), lambda i: (0, 0))
    return pl.pallas_call(
        _outproj_ffn_kernel,
        grid=(n // tm,),
        in_specs=[rows(kdim), resident(w_out), rows(D_MODEL), vec, vec,
                  resident(wg), resident(wu), resident(wd), vec, vec],
        out_specs=rows(D_MODEL),
        out_shape=jax.ShapeDtypeStruct((n, D_MODEL), F32),
        compiler_params=_params("parallel"),
        name="gdn_outproj_ffn",
    )(a, w_out, res, g0, b0, wg, wu, wd, g1, b1)


def _band_tables():
    assert SWA_WINDOW == SWA_BLOCK
    r = np.arange(SWA_BLOCK)[:, None]
    q = np.arange(SWA_BLOCK)[None, :]
    d = (q - r) % SWA_BLOCK
    max_exact = REL_BUCKETS // 2
    df = np.maximum(d, 1).astype(np.float32)
    large = max_exact + (np.log(df / np.float32(max_exact)) / np.float32(math.log(REL_MAX_DIST / max_exact))
                         * np.float32(REL_BUCKETS - max_exact)).astype(np.int32)
    large = np.minimum(large, REL_BUCKETS - 1)
    bucket = np.where(d < max_exact, d, large).astype(np.int32)
    valid = np.stack([q >= r, np.ones_like(q >= r)]).astype(np.int32)
    return bucket, valid


def _bias_kernel(relb_ref, bucket_ref, valid_ref, o_ref):
    bucket = bucket_ref[...]
    in_bucket = [bucket == b for b in range(REL_BUCKETS)]
    for h in range(SWA_Q_HEADS):
        acc = jnp.zeros(bucket.shape, F32)
        for b in range(REL_BUCKETS):
            acc = jnp.where(in_bucket[b], relb_ref[b, h], acc)
        for t in range(2):
            o_ref[t, h] = jnp.where(valid_ref[t] != 0, acc, NEG_BIG)


def _bias_table(rel_bias):
    bucket, valid = _band_tables()
    shape = (SWA_BLOCK, SWA_BLOCK)
    return pl.pallas_call(
        _bias_kernel,
        grid=(1,),
        in_specs=[pl.BlockSpec(memory_space=pltpu.SMEM), pl.BlockSpec(shape, lambda i: (0, 0)),
                  pl.BlockSpec((2,) + shape, lambda i: (0, 0, 0))],
        out_specs=pl.BlockSpec((2, SWA_Q_HEADS) + shape, lambda i: (0, 0, 0, 0)),
        out_shape=jax.ShapeDtypeStruct((2, SWA_Q_HEADS) + shape, F32),
        compiler_params=_params("arbitrary"),
        name="swa_bias_table",
    )(rel_bias, jnp.asarray(bucket), jnp.asarray(valid))


def _band_attention(q_pair, kv, bias_of, sink_ref, put_pair):
    dh = SWA_HEAD_DIM
    blk = SWA_BLOCK
    nk = 2 * blk
    from_prev = lax.broadcasted_iota(jnp.int32, (blk, blk), 1) < lax.broadcasted_iota(jnp.int32, (blk, blk), 0)
    k_all = kv[:, :SWA_KVDIM]
    lane = lax.broadcasted_iota(jnp.int32, k_all.shape, 1)
    zero = jnp.zeros_like(k_all)
    for kh in range(SWA_KV_HEADS):
        k_own = jnp.where((lane >= kh * dh) & (lane < (kh + 1) * dh), k_all, zero)
        k_other = pltpu.roll(k_own.astype(F32), dh, axis=1).astype(BF16)
        k_even, k_odd = (k_own, k_other) if kh == 0 else (k_other, k_own)
        k2 = jnp.concatenate([k_even, k_odd], axis=0)
        v_t = kv[:, SWA_KVDIM + kh * dh:SWA_KVDIM + (kh + 1) * dh].T
        pairs = [kh * (SWA_GROUP // 2) + p for p in range(SWA_GROUP // 2)]
        heads = [2 * p + r for p in pairs for r in range(2)]
        st = [_dot_nt(k2, q_pair(p)) for p in pairs]
        scores = [jnp.where(from_prev, st[i // 2][(i % 2) * nk:(i % 2) * nk + blk],
                            st[i // 2][(i % 2) * nk + blk:(i % 2 + 1) * nk]) + bias_of(h)
                  for i, h in enumerate(heads)]
        sinks = [sink_ref[0:1, h:h + 1] for h in heads]
        maxes = [jnp.maximum(jnp.max(s, axis=0, keepdims=True), sk) for s, sk in zip(scores, sinks)]
        expo = [jnp.exp(s - m) for s, m in zip(scores, maxes)]
        sums = [jnp.sum(p, axis=0, keepdims=True) for p in expo]
        inv_den = [1.0 / (t + jnp.exp(sk - m)) for t, sk, m in zip(sums, sinks, maxes)]
        expo = [jnp.concatenate([jnp.where(from_prev, p, 0.0), jnp.where(from_prev, 0.0, p)], axis=0) for p in expo]
        for i, p in enumerate(pairs):
            o_t = _dot(v_t, jnp.concatenate(expo[2 * i:2 * i + 2], axis=1).astype(BF16))
            z = jnp.concatenate([o_t[:, :SWA_BLOCK] * inv_den[2 * i], o_t[:, SWA_BLOCK:] * inv_den[2 * i + 1]],
                                axis=0)
            put_pair(p, z.T)


def _swa_layer_kernel(x_ref, win_ref, bin_ref, bias_ref, sink_ref, wout_ref, g_ref, b_ref, o_ref,
                      q_ref, kv_ref, att_ref, *, nblk):
    j = pl.program_id(1)
    blk = SWA_BLOCK
    tm = nblk * blk
    dh2 = 2 * SWA_HEAD_DIM

    p = _dot(x_ref[...].astype(BF16), win_ref[...]) + bin_ref[...]
    q_ref[...] = (p[:, :SWA_QDIM] * (SWA_HEAD_DIM ** -0.5)).astype(BF16)

    @pl.when(j == 0)
    def _():
        kv_ref[0:blk, :] = jnp.zeros((blk, 2 * SWA_KVDIM), BF16)

    @pl.when(j != 0)
    def _():
        kv_ref[0:blk, :] = kv_ref[tm:tm + blk, :]

    kv_ref[blk:, :] = p[:, SWA_QDIM:].astype(BF16)

    for i in range(nblk):
        rows = slice(i * blk, (i + 1) * blk)
        table = jnp.where(j == 0, 0, 1) if i == 0 else 1

        def put_pair(pair, o, rows=rows):
            att_ref[rows, pair * dh2:(pair + 1) * dh2] = o.astype(att_ref.dtype)

        _band_attention(lambda pair, rows=rows: q_ref[rows, pair * dh2:(pair + 1) * dh2],
                        kv_ref[i * blk:(i + 2) * blk, :],
                        lambda h, table=table: bias_ref[table, h], sink_ref, put_pair)

    o_ref[...] = _proj_res_ln(att_ref, wout_ref, x_ref, g_ref, b_ref)


def _swa_layer(x, w_in, b_in, bias, sinks, w_out, g, b, batch, seq_len):
    n = x.shape[0]
    nblk = SWA_BLOCKS_PER_STEP
    tm = nblk * SWA_BLOCK
    steps = seq_len // tm
    resident = lambda a: pl.BlockSpec(a.shape, lambda s, j: (0,) * a.ndim, pipeline_mode=pl.Buffered(1))
    rows = pl.BlockSpec((tm, D_MODEL), lambda s, j: (s * steps + j, 0))
    return pl.pallas_call(
        functools.partial(_swa_layer_kernel, nblk=nblk),
        grid=(batch, steps),
        in_specs=[rows, resident(w_in), resident(b_in), resident(bias), resident(sinks), resident(w_out),
                  resident(g), resident(b)],
        out_specs=rows,
        out_shape=jax.ShapeDtypeStruct((n, D_MODEL), F32),
        scratch_shapes=[pltpu.VMEM((tm, SWA_QDIM), BF16), pltpu.VMEM((tm + SWA_BLOCK, 2 * SWA_KVDIM), BF16),
                        pltpu.VMEM((tm, SWA_QDIM), BF16)],
        compiler_params=_params("parallel", "arbitrary"),
        name="swa_layer",
    )(x, w_in, b_in, bias, sinks, w_out, g, b)


def _router_kernel(x_ref, wr_ref, route_ref, route_t_ref, cnt_ref, run_ref, *, tm):
    @pl.when(pl.program_id(0) == 0)
    def _():
        run_ref[...] = jnp.zeros(run_ref.shape, F32)

    logits = _dot3(x_ref[...], wr_ref[...])
    lane = lax.broadcasted_iota(jnp.int32, logits.shape, 1)
    lane_f = lane.astype(F32)
    lg = jnp.where(lane < N_EXPERTS, logits, NEG_BIG)
    m1 = jnp.max(lg, axis=-1, keepdims=True)
    i1 = jnp.min(jnp.where(lg == m1, lane_f, float(LANES)), axis=-1, keepdims=True)
    oh1 = lane_f == i1
    lg2 = jnp.where(oh1, NEG_BIG, lg)
    m2 = jnp.max(lg2, axis=-1, keepdims=True)
    i2 = jnp.min(jnp.where(lg2 == m2, lane_f, float(LANES)), axis=-1, keepdims=True)
    oh2 = lane_f == i2
    e = jnp.exp(m2 - m1)
    w0 = 1.0 / (1.0 + e)
    w1 = e * w0

    cnt = jnp.where(oh1, 1.0, 0.0) + jnp.where(oh2, 1.0, 0.0)
    r = lax.broadcasted_iota(jnp.int32, (tm, tm), 0)
    c = lax.broadcasted_iota(jnp.int32, (tm, tm), 1)
    before = jnp.where(r > c, 1.0, 0.0).astype(BF16)
    excl = _dot(before, cnt.astype(BF16)) + run_ref[...]
    rank0 = jnp.sum(jnp.where(oh1, excl, 0.0), axis=-1, keepdims=True)
    rank1 = jnp.sum(jnp.where(oh2, excl, 0.0), axis=-1, keepdims=True)
    run = run_ref[...] + jnp.sum(cnt, axis=0, keepdims=True)
    run_ref[...] = run
    cnt_ref[...] = run

    vals = (i1, i2, rank0, rank1, w0, w1)
    out = jnp.zeros(logits.shape, F32)
    for idx, val in enumerate(vals):
        out = jnp.where(lane == idx, val, out)
    route_ref[...] = out
    route_t_ref[...] = out.T[:SUBLANES, :]


def _router(x, w_router_padded):
    n = x.shape[0]
    tm = TM_ROUTER
    return pl.pallas_call(
        functools.partial(_router_kernel, tm=tm),
        grid=(n // tm,),
        in_specs=[pl.BlockSpec((tm, D_MODEL), lambda i: (i, 0)), pl.BlockSpec((D_MODEL, LANES), lambda i: (0, 0))],
        out_specs=[pl.BlockSpec((tm, LANES), lambda i: (i, 0)), pl.BlockSpec((SUBLANES, tm), lambda i: (0, i)),
                   pl.BlockSpec((1, LANES), lambda i: (0, 0))],
        out_shape=[jax.ShapeDtypeStruct((n, LANES), F32), jax.ShapeDtypeStruct((SUBLANES, n), F32),
                   jax.ShapeDtypeStruct((1, LANES), F32)],
        scratch_shapes=[pltpu.VMEM((1, LANES), F32)],
        compiler_params=_params("arbitrary"),
        name="moe_router",
    )(x, w_router_padded)


def _dispatch_kernel(pos0_ref, pos1_ref, meta_ref, x_ref, wsrc_ref, xs_hbm, wdst_ref, zero_ref, sem, pad_sem, *, td):
    i = pl.program_id(0)
    wdst_ref[...] = wsrc_ref[...].astype(BF16)

    def row_copy(t, dst):
        return pltpu.make_async_copy(x_ref.at[pl.ds(t, 1)], xs_hbm.at[pl.ds(dst, 1)], sem)

    def start(t, carry):
        row_copy(t, pos0_ref[0, 0, t]).start()
        row_copy(t, pos1_ref[0, 0, t]).start(priority=1)
        return carry

    lax.fori_loop(0, td, start, 0, unroll=DMA_UNROLL)

    @pl.when(i == pl.num_programs(0) - 1)
    def _():
        zero_ref[...] = jnp.zeros(zero_ref.shape, F32)

        zr = zero_ref.shape[0]

        def row_zero(dst):
            return pltpu.make_async_copy(zero_ref.at[pl.ds(0, 1)], xs_hbm.at[pl.ds(dst, 1)], pad_sem)

        def block_zero(blk):
            return pltpu.make_async_copy(zero_ref, xs_hbm.at[pl.ds(pl.multiple_of(blk * zr, zr), zr)], pad_sem)

        def zero_range(copy, lo, hi):
            lax.fori_loop(lo, hi, lambda r, carry: (copy(r).start(), carry)[1], 0)
            lax.fori_loop(lo, hi, lambda r, carry: (copy(0).wait(), carry)[1], 0)

        for e in range(N_EXPERTS):
            off, cnt, cnt_up, padded = (meta_ref[r, e] for r in range(4))
            zero_range(lambda r, off=off: row_zero(off + r), cnt, cnt_up)
            zero_range(block_zero, (off + cnt_up) // zr, (off + padded) // zr)

        used = meta_ref[0, N_EXPERTS - 1] + meta_ref[3, N_EXPERTS - 1]
        zero_range(block_zero, used // zr, xs_hbm.shape[0] // zr)

    for _ in range(2):
        pltpu.make_async_copy(x_ref, xs_hbm.at[pl.ds(0, td)], sem).wait()


def _dispatch(x, pos, meta, n_rows, to_bf16):
    n = x.shape[0]
    td = TD_DISPATCH
    pos_spec = pl.BlockSpec((1, 1, td), lambda i: (i, 0, 0), memory_space=pltpu.SMEM)
    (cast_spec,), (cast_shape,) = _cast_specs([to_bf16], n // td)
    return pl.pallas_call(
        functools.partial(_dispatch_kernel, td=td),
        grid=(n // td,),
        in_specs=[pos_spec, pos_spec, pl.BlockSpec(memory_space=pltpu.SMEM),
                  pl.BlockSpec((td, D_MODEL), lambda i: (i, 0)), cast_spec],
        out_specs=[pl.BlockSpec(memory_space=pl.ANY), cast_spec],
        out_shape=[jax.ShapeDtypeStruct((n_rows, D_MODEL), F32), cast_shape],
        scratch_shapes=[pltpu.VMEM((ZERO_ROWS, D_MODEL), F32), pltpu.SemaphoreType.DMA, pltpu.SemaphoreType.DMA],
        compiler_params=_params("arbitrary"),
        name="moe_dispatch",
    )(pos[0].reshape(n // td, 1, td), pos[1].reshape(n // td, 1, td), meta, x, to_bf16)


def _moe_kernel(we_ref, wb_ref, nv_ref, cnt_ref, *refs, n_sub, tm):
    xs_refs = refs[:n_sub]
    wg_ref, wu_ref, wd_ref, ys_hbm, acc_ref, xb_ref, sem = refs[n_sub:]
    j = pl.program_id(0)
    f = pl.program_id(1)
    last_f = pl.num_programs(1) - 1
    n_windows, n_tiles = cnt_ref[0], cnt_ref[1]

    def tile_write(slot, block):
        return pltpu.make_async_copy(acc_ref.at[slot], ys_hbm.at[pl.ds(pl.multiple_of(block * tm, tm), tm)],
                                     sem.at[slot])

    @pl.when(j < n_windows)
    def _():
        prev_tiles = jnp.where(j > 0, nv_ref[jnp.maximum(j - 1, 0)], 0)

        for s in range(n_sub):
            @pl.when(s < nv_ref[j])
            def _(s=s):
                @pl.when(f == 0)
                def _():
                    xb_ref[s] = xs_refs[s][...].astype(BF16)

                xb = xb_ref[s]
                h = (_silu(_dot(xb, wg_ref[0])) * _dot(xb, wu_ref[0])).astype(BF16)
                part = _dot(h, wd_ref[0])

                @pl.when(f == 0)
                def _():
                    @pl.when(s < prev_tiles)
                    def _():
                        tile_write(s, 0).wait()

                    acc_ref[s] = part

                @pl.when(f != 0)
                def _():
                    acc_ref[s] += part

                @pl.when(f == last_f)
                def _():
                    tile_write(s, wb_ref[j] + s).start()

            @pl.when((f == 0) & (s >= nv_ref[j]) & (s < prev_tiles))
            def _(s=s):
                tile_write(s, 0).wait()

        @pl.when((f == last_f) & (j == n_windows - 1))
        def _():
            for s in range(n_sub):
                @pl.when(s < nv_ref[j])
                def _(s=s):
                    tile_write(s, 0).wait()

            acc_ref[0] = jnp.zeros(acc_ref.shape[1:], F32)
            total_tiles = ys_hbm.shape[0] // tm
            lax.fori_loop(n_tiles, total_tiles, lambda t, c: (tile_write(0, t).start(), c)[1], 0)
            lax.fori_loop(n_tiles, total_tiles, lambda t, c: (tile_write(0, 0).wait(), c)[1], 0)


def _moe_experts(xs, wg, wu, wd, win_expert, win_block, win_tiles, counts):
    n_rows = xs.shape[0]
    tm, tf, n_sub = TM_MOE, TF_MOE, MOE_TILES_PER_WINDOW
    max_windows = win_expert.shape[0]

    def xs_spec(s):
        return pl.BlockSpec((tm, D_MODEL), lambda j, f, we, wb, nv, cnt: (wb[j] + jnp.minimum(s, nv[j] - 1), 0))

    grid_spec = pltpu.PrefetchScalarGridSpec(
        num_scalar_prefetch=4,
        grid=(max_windows, EXPERT_DIM // tf),
        in_specs=[xs_spec(s) for s in range(n_sub)] + [
            pl.BlockSpec((1, D_MODEL, tf), lambda j, f, we, wb, nv, cnt: (we[j], 0, f)),
            pl.BlockSpec((1, D_MODEL, tf), lambda j, f, we, wb, nv, cnt: (we[j], 0, f)),
            pl.BlockSpec((1, tf, D_MODEL), lambda j, f, we, wb, nv, cnt: (we[j], f, 0))],
        out_specs=pl.BlockSpec(memory_space=pl.ANY),
        scratch_shapes=[pltpu.VMEM((n_sub, tm, D_MODEL), F32), pltpu.VMEM((n_sub, tm, D_MODEL), BF16),
                        pltpu.SemaphoreType.DMA((n_sub,))],
    )
    return pl.pallas_call(
        functools.partial(_moe_kernel, n_sub=n_sub, tm=tm),
        grid_spec=grid_spec,
        out_shape=jax.ShapeDtypeStruct((n_rows, D_MODEL), F32),
        compiler_params=_params("arbitrary", "arbitrary"),
        name="moe_experts",
    )(win_expert, win_block, win_tiles, counts, *([xs] * n_sub), wg, wu, wd)


def _combine_kernel(pos0_ref, pos1_ref, pos0n_ref, pos1n_ref, x_ref, route_ref, ys_hbm, g_ref, b_ref, o_ref,
                    y_ref, sem, *, tc):
    i = pl.program_id(0)
    slot = i % 2

    def issue(p_refs, s):
        def start(t, carry):
            for kk in range(2):
                pltpu.make_async_copy(ys_hbm.at[pl.ds(p_refs[kk][0, 0, t], 1)],
                                      y_ref.at[s, kk, pl.ds(t, 1)], sem.at[s]).start(priority=kk)
            return carry

        lax.fori_loop(0, tc, start, 0, unroll=DMA_UNROLL)

    @pl.when(i == 0)
    def _():
        issue((pos0_ref, pos1_ref), 0)

    @pl.when(i + 1 < pl.num_programs(0))
    def _():
        issue((pos0n_ref, pos1n_ref), 1 - slot)

    for kk in range(2):
        pltpu.make_async_copy(ys_hbm.at[pl.ds(0, tc)], y_ref.at[slot, kk], sem.at[slot]).wait()

    route = route_ref[...]
    f = route[:, 4:5] * y_ref[slot, 0] + route[:, 5:6] * y_ref[slot, 1]
    o_ref[...] = _layer_norm(ALPHA * x_ref[...] + f, g_ref[...], b_ref[...])


def _combine_ln(x, route, pos, ys, g, b):
    n = x.shape[0]
    tc = TC_COMBINE
    nt = n // tc
    pos0, pos1 = pos[0].reshape(nt, 1, tc), pos[1].reshape(nt, 1, tc)
    cur = pl.BlockSpec((1, 1, tc), lambda i: (i, 0, 0), memory_space=pltpu.SMEM)
    nxt = pl.BlockSpec((1, 1, tc), lambda i: (jnp.minimum(i + 1, nt - 1), 0, 0), memory_space=pltpu.SMEM)
    return pl.pallas_call(
        functools.partial(_combine_kernel, tc=tc),
        grid=(nt,),
        in_specs=[cur, cur, nxt, nxt,
                  pl.BlockSpec((tc, D_MODEL), lambda i: (i, 0)),
                  pl.BlockSpec((tc, LANES), lambda i: (i, 0)),
                  pl.BlockSpec(memory_space=pl.ANY),
                  pl.BlockSpec((1, D_MODEL), lambda i: (0, 0)), pl.BlockSpec((1, D_MODEL), lambda i: (0, 0))],
        out_specs=pl.BlockSpec((tc, D_MODEL), lambda i: (i, 0)),
        out_shape=jax.ShapeDtypeStruct((n, D_MODEL), F32),
        scratch_shapes=[pltpu.VMEM((2, 2, tc, D_MODEL), F32), pltpu.SemaphoreType.DMA((2,))],
        compiler_params=_params("arbitrary"),
        name="moe_combine_ln",
    )(pos0, pos1, pos0, pos1, x, route, ys, g, b)


def _moe_layer(x, route, route_t, counts_f, wg, wu, wd_f32, g, b):
    n = x.shape[0]
    tm = TM_MOE

    counts = counts_f[0, :N_EXPERTS].astype(jnp.int32)
    padded = ((counts + tm - 1) // tm) * tm
    offs = jnp.cumsum(padded) - padded
    eids = jnp.arange(N_EXPERTS, dtype=jnp.int32)
    e01 = route_t[0:2].astype(jnp.int32)
    rank01 = route_t[2:4].astype(jnp.int32)
    base = jnp.sum(jnp.where(e01[:, None, :] == eids[None, :, None], offs[None, :, None], 0), axis=1)
    pos = base + rank01
    counts_up = jnp.minimum(padded, ((counts + ZERO_ROWS - 1) // ZERO_ROWS) * ZERO_ROWS)
    meta = jnp.stack([offs, counts, counts_up, padded]).astype(jnp.int32)

    n_rows = 2 * n + N_EXPERTS * tm
    nsub = MOE_TILES_PER_WINDOW
    tiles = padded // tm
    wins = (tiles + nsub - 1) // nsub
    win_end = jnp.cumsum(wins)
    n_windows = win_end[-1]
    max_windows = (n_rows // tm + nsub - 1) // nsub + N_EXPERTS
    wj = jnp.minimum(jnp.arange(max_windows, dtype=jnp.int32), n_windows - 1)
    win_expert = jnp.minimum(jnp.sum((wj[:, None] >= win_end[None, :]).astype(jnp.int32), axis=-1), N_EXPERTS - 1)
    pick = lambda v: jnp.sum(jnp.where(win_expert[:, None] == eids[None, :], v[None, :], 0), axis=-1)
    local = wj - pick(win_end - wins)
    win_block = pick(offs // tm) + nsub * local
    win_tiles = jnp.minimum(nsub, pick(tiles) - nsub * local)
    counts_nt = jnp.stack([n_windows, jnp.sum(tiles)])

    xs, wd = _dispatch(x, pos, meta, n_rows, wd_f32.reshape(-1, D_MODEL))
    wd = wd.reshape(wd_f32.shape)
    i32 = lambda a: a.astype(jnp.int32)
    ys = _moe_experts(xs, wg, wu, wd, i32(win_expert), i32(win_block), i32(win_tiles), i32(counts_nt))
    return _combine_ln(x, route, pos, ys, g, b)


def kernel(x, a_w_in, a_conv_w, a_a_log, a_dt_bias, a_norm_w, a_w_out, b_w_in, b_b_in, b_sinks, b_w_out, rel_bias,
           ffn_w_gate, ffn_w_up, ffn_w_down, moe_router, moe_w_gate, moe_w_up, moe_w_down, ln_g, ln_b):
    batch, seq_len, _ = x.shape
    n = batch * seq_len
    x0 = x.reshape(n, D_MODEL)
    ln_g = ln_g.reshape(DEPTH, 2, 1, D_MODEL)
    ln_b = ln_b.reshape(DEPTH, 2, 1, D_MODEL)

    pad_gate = lambda p: jnp.pad(p.reshape(1, GDN_V_HEADS), ((0, 0), (GDN_V_HEADS, LANES - 2 * GDN_V_HEADS)))
    gate_params = jnp.concatenate([pad_gate(a_a_log[0]), pad_gate(a_dt_bias[0])], axis=0)
    q, k, v, zs, gb = _gdn_inproj(x0, jnp.swapaxes(a_w_in, 1, 2), a_conv_w[0], gate_params, seq_len)
    o, casted = _gdn_chunk(q, k, v, zs, gb, a_norm_w[0].reshape(1, GDN_HEAD), batch, seq_len,
                           [moe_w_gate[0].reshape(-1, EXPERT_DIM), moe_w_up[0].reshape(-1, EXPERT_DIM),
                            ffn_w_gate[0], ffn_w_up[0], ffn_w_down[0],
                            a_w_out[0], b_w_in[0], b_w_out[0]])
    moe_wg, moe_wu, ffn_wg, ffn_wu, ffn_wd, w_out_a, w_in_b, w_out_b = casted
    x2 = _outproj_ffn(o, w_out_a, x0, ln_g[0, 0], ln_b[0, 0],
                      ffn_wg, ffn_wu, ffn_wd, ln_g[0, 1], ln_b[0, 1])

    bias = _bias_table(rel_bias)
    sinks = jnp.pad(b_sinks[0].reshape(1, SWA_Q_HEADS), ((0, 0), (0, LANES - SWA_Q_HEADS)))
    x3 = _swa_layer(x2, w_in_b, b_b_in[0].reshape(1, -1), bias, sinks, w_out_b, ln_g[1, 0], ln_b[1, 0],
                    batch, seq_len)
    w_router = jnp.pad(moe_router[0], ((0, 0), (0, LANES - N_EXPERTS)))
    route, route_t, counts = _router(x3, w_router)
    expert_shape = (N_EXPERTS, D_MODEL, EXPERT_DIM)
    x4 = _moe_layer(x3, route, route_t, counts, moe_wg.reshape(expert_shape), moe_wu.reshape(expert_shape),
                    moe_w_down[0], ln_g[1, 1], ln_b[1, 1])
    return x4.reshape(batch, seq_len, D_MODEL)
```

```python
import functools
import math

import numpy as np
import jax
import jax.numpy as jnp
from jax import lax
from jax.experimental import pallas as pl
from jax.experimental.pallas import tpu as pltpu

F32 = jnp.float32
BF16 = jnp.bfloat16

D_MODEL = 1024
DEPTH = 2
ALPHA = (2.0 * DEPTH) ** 0.25
LN_EPS = 1e-5

GDN_K_HEADS = 4
GDN_V_HEADS = 8
GDN_HEAD = 128
GDN_KDIM = GDN_K_HEADS * GDN_HEAD
GDN_VDIM = GDN_V_HEADS * GDN_HEAD
GDN_CONV = 4
GDN_CHUNK = 64
GDN_QKV = 2 * GDN_KDIM + GDN_VDIM
GDN_EPS = 1e-6

SWA_Q_HEADS = 16
SWA_KV_HEADS = 2
SWA_GROUP = SWA_Q_HEADS // SWA_KV_HEADS
SWA_HEAD_DIM = 64
SWA_WINDOW = 128
SWA_BLOCK = 128
SWA_QDIM = SWA_Q_HEADS * SWA_HEAD_DIM
SWA_KVDIM = SWA_KV_HEADS * SWA_HEAD_DIM
REL_BUCKETS = 32
REL_MAX_DIST = 128

FFN_DIM = 2816
N_EXPERTS = 8
EXPERT_DIM = 3584

LANES = 128
SUBLANES = 8
NEG_BIG = -1e30
VMEM_LIMIT = 56 * 1024 * 1024

TM_GDN_IN = 512
TM_FFN = 512
SWA_BLOCKS_PER_STEP = 8
TM_ROUTER = 1024
TM_MOE = 512
TF_MOE = 1792
MOE_TILES_PER_WINDOW = 3
TD_DISPATCH = 1024
TC_COMBINE = 512
ZERO_ROWS = 64
DMA_UNROLL = 8


def _params(*sem):
    return pltpu.CompilerParams(dimension_semantics=sem, vmem_limit_bytes=VMEM_LIMIT)


def _dot(a, b):
    return jnp.dot(a, b, preferred_element_type=F32)


def _dot_nt(a, b):
    return lax.dot_general(a, b, (((1,), (1,)), ((), ())), preferred_element_type=F32)


def _dot_tn(a, b):
    return lax.dot_general(a, b, (((0,), (0,)), ((), ())), preferred_element_type=F32)


def _split(x):
    hi = x.astype(BF16)
    lo = (x - hi.astype(F32)).astype(BF16)
    return hi, lo


def _dot3(a, b):
    ah, al = _split(a)
    bh, bl = _split(b)
    return _dot(ah, bh) + (_dot(ah, bl) + _dot(al, bh))


def _silu(x):
    return x * jax.nn.sigmoid(x)


def _layer_norm(y, g, b):
    mu = jnp.mean(y, axis=-1, keepdims=True)
    yc = y - mu
    var = jnp.mean(yc * yc, axis=-1, keepdims=True)
    return yc * lax.rsqrt(var + LN_EPS) * g + b


def _gdn_inproj_kernel(x_ref, wf_ref, convw_ref, gp_ref,
                       q_ref, k_ref, v_ref, z_ref, gb_ref, ext_ref, w_ref, *, tm, tiles_per_seq):
    i = pl.program_id(0)
    xb = x_ref[...].astype(BF16)

    @pl.when(i == 0)
    def _():
        n_in = wf_ref.shape[1]
        n_full = n_in // LANES
        for c in range(n_full):
            cs = slice(c * LANES, (c + 1) * LANES)
            w_ref[:, cs] = wf_ref[0, cs, :].T.astype(BF16)
        tail = jnp.concatenate([wf_ref[0, n_full * LANES:, :],
                                jnp.zeros(((n_full + 1) * LANES - n_in, wf_ref.shape[2]), F32)], axis=0)
        w_ref[:, n_full * LANES:] = tail.T.astype(BF16)

    @pl.when(i % tiles_per_seq == 0)
    def _():
        ext_ref[0:SUBLANES, :] = jnp.zeros((SUBLANES, GDN_QKV), F32)

    @pl.when(i % tiles_per_seq != 0)
    def _():
        ext_ref[0:SUBLANES, :] = ext_ref[tm:tm + SUBLANES, :]

    ext_ref[SUBLANES:, :] = _dot(xb, w_ref[:, :GDN_QKV])
    z_ref[...] = _dot(xb, w_ref[:, GDN_QKV:GDN_QKV + GDN_VDIM])

    n_chunks = GDN_QKV // LANES
    for c in range(n_chunks):
        cs = slice(c * LANES, (c + 1) * LANES)
        acc = convw_ref[GDN_CONV - 1:GDN_CONV, cs] * ext_ref[SUBLANES:SUBLANES + tm, cs]
        for j in range(GDN_CONV - 1):
            off = SUBLANES - (GDN_CONV - 1) + j
            acc = acc + convw_ref[j:j + 1, cs] * ext_ref[off:off + tm, cs]
        y = _silu(acc)
        if c < 2 * GDN_K_HEADS:
            y = y * lax.rsqrt(jnp.sum(y * y, axis=-1, keepdims=True) + GDN_EPS)
            if c < GDN_K_HEADS:
                q_ref[:, cs] = y * (GDN_HEAD ** -0.5)
            else:
                k_ref[:, (c - GDN_K_HEADS) * LANES:(c - GDN_K_HEADS + 1) * LANES] = y
        else:
            cv = c - 2 * GDN_K_HEADS
            v_ref[:, cv * LANES:(cv + 1) * LANES] = y

    ba = _dot(xb, w_ref[:, GDN_QKV + GDN_VDIM:])
    lane = lax.broadcasted_iota(jnp.int32, ba.shape, 1)
    sp = ba + gp_ref[1:2, :]
    softplus = jnp.maximum(sp, 0.0) + jnp.log(1.0 + jnp.exp(-jnp.abs(sp)))
    g = -jnp.exp(gp_ref[0:1, :]) * softplus
    gb_ref[...] = jnp.where(lane < GDN_V_HEADS, jax.nn.sigmoid(ba), g)


def _gdn_inproj(x2d, w_in, conv_w, gate_params, seq_len):
    n = x2d.shape[0]
    tm = TM_GDN_IN
    kern = functools.partial(_gdn_inproj_kernel, tm=tm, tiles_per_seq=seq_len // tm)
    full = lambda shape: pl.BlockSpec(shape, lambda i: (0,) * len(shape))
    rows = lambda width: pl.BlockSpec((tm, width), lambda i: (i, 0))
    return pl.pallas_call(
        kern,
        grid=(n // tm,),
        in_specs=[rows(D_MODEL), pl.BlockSpec(w_in.shape, lambda i: (0, 0, 0), pipeline_mode=pl.Buffered(1)),
                  full(conv_w.shape), full(gate_params.shape)],
        out_specs=[rows(GDN_KDIM), rows(GDN_KDIM), rows(GDN_VDIM), rows(GDN_VDIM), rows(LANES)],
        out_shape=[jax.ShapeDtypeStruct((n, GDN_KDIM), F32), jax.ShapeDtypeStruct((n, GDN_KDIM), F32),
                   jax.ShapeDtypeStruct((n, GDN_VDIM), F32), jax.ShapeDtypeStruct((n, GDN_VDIM), F32),
                   jax.ShapeDtypeStruct((n, LANES), F32)],
        scratch_shapes=[pltpu.VMEM((tm + SUBLANES, GDN_QKV), F32),
                        pltpu.VMEM((D_MODEL, GDN_QKV + GDN_VDIM + LANES), BF16)],
        compiler_params=_params("arbitrary"),
        name="gdn_inproj",
    )(x2d, w_in, conv_w, gate_params)


GDN_CHUNKS_PER_STEP = 2
NEUMANN_BLOCK = 4


def _bf16_all(xs):
    return [x.astype(BF16) for x in xs]


def _dot_all(a_list, b_list):
    return [_dot(a, b) for a, b in zip(a_list, b_list)]


def _unit_lower_inverse_all(a_list, row, col):
    shift = int(math.log2(NEUMANN_BLOCK))
    eye = jnp.where(row == col, 1.0, 0.0).astype(F32)
    on_diag_block = (row >> shift) == (col >> shift)
    d = [jnp.where(on_diag_block, a, 0.0) for a in a_list]
    d_b = _bf16_all(d)
    x = [eye - di for di in d]
    d2_b = _bf16_all(_dot_all(d_b, d_b))
    x = [xi + t for xi, t in zip(x, _dot_all(_bf16_all(x), d2_b))]
    size = NEUMANN_BLOCK
    while size < a_list[0].shape[0]:
        shift = int(math.log2(size))
        rbl, cbl = row >> shift, col >> shift
        below = ((rbl & 1) == 1) & (cbl == rbl - 1)
        l_b = _bf16_all([jnp.where(below, a, 0.0) for a in a_list])
        x_b = _bf16_all(x)
        xl_b = _bf16_all(_dot_all(x_b, l_b))
        x = [xi - t for xi, t in zip(x, _dot_all(xl_b, x_b))]
        size *= 2
    return x


def _gdn_chunk_kernel(q_ref, k_ref, v_ref, z_ref, gb_ref, nw_ref, *rest, nb, nck, n_cast):
    cast_src, (o_ref, *cast_dst), s_ref = rest[:n_cast], rest[n_cast:2 * n_cast + 1], rest[-1]
    for src, dst in zip(cast_src, cast_dst):
        dst[...] = src[...].astype(BF16)
    c = GDN_CHUNK
    nh = GDN_V_HEADS

    @pl.when(pl.program_id(0) == 0)
    def _():
        s_ref[...] = jnp.zeros(s_ref.shape, F32)

    row = lax.broadcasted_iota(jnp.int32, (c, c), 0)
    col = lax.broadcasted_iota(jnp.int32, (c, c), 1)
    causal = row >= col
    strict = row > col
    tril = jnp.where(causal, 1.0, 0.0).astype(BF16)
    rs = lambda ci: slice(ci * c, (ci + 1) * c)
    ks = lambda h: slice((h // (nh // GDN_K_HEADS)) * GDN_HEAD, (h // (nh // GDN_K_HEADS) + 1) * GDN_HEAD)
    vs = lambda h: slice(h * GDN_HEAD, (h + 1) * GDN_HEAD)

    gates = {}
    for ci in range(nck):
        for b in range(nb):
            gb = gb_ref[b, rs(ci), :]
            g_hi, g_lo = _split(gb)
            gc = _dot(tril, g_hi) + _dot(tril, g_lo)
            g_last = gc[c - 1:c, :]
            gates[ci, b] = dict(gb=gb, gc=gc, gc_t=gc.T, eg=jnp.exp(gc), e_last=jnp.exp(g_last),
                                e_rest=jnp.exp(g_last - gc))

    items = [(ci, b, h) for ci in range(nck) for b in range(nb) for h in range(nh)]

    a_kk, a_qk, k_beta = [], {}, {}
    for it in items:
        ci, b, h = it
        gt = gates[ci, b]
        k = k_ref[b, rs(ci), ks(h)]
        kb = k * gt["gb"][:, h:h + 1]
        kq = jnp.concatenate([kb, q_ref[b, rs(ci), ks(h)]], axis=0).astype(BF16)
        gram = _dot_nt(kq, k.astype(BF16))
        gl = nh + h
        decay = jnp.exp(jnp.where(causal, gt["gc"][:, gl:gl + 1] - gt["gc_t"][gl:gl + 1, :], NEG_BIG))
        a_kk.append(jnp.where(strict, gram[:c] * decay, 0.0))
        a_qk[it] = (gram[c:] * decay).astype(BF16)
        k_beta[it] = kb

    t_inv = dict(zip(items, _unit_lower_inverse_all(a_kk, row, col)))

    uw = {}
    for it in items:
        ci, b, h = it
        gt = gates[ci, b]
        gl = nh + h
        rhs = jnp.concatenate([v_ref[b, rs(ci), vs(h)] * gt["gb"][:, h:h + 1],
                               k_beta[it] * gt["eg"][:, gl:gl + 1]], axis=1)
        uw[it] = _dot(t_inv[it].astype(BF16), rhs.astype(BF16))

    for ci in range(nck):
        chunk_items = [(ci, b, h) for b in range(nb) for h in range(nh)]

        ws_qs, states = {}, {}
        for it in chunk_items:
            _, b, h = it
            gl = nh + h
            s = s_ref[b * nh + h]
            wq = jnp.concatenate([uw[it][:, GDN_HEAD:],
                                  q_ref[b, rs(ci), ks(h)] * gates[ci, b]["eg"][:, gl:gl + 1]], axis=0)
            ws_qs[it] = _dot(wq.astype(BF16), s.astype(BF16))
            states[it] = s

        for it in chunk_items:
            _, b, h = it
            gt = gates[ci, b]
            gl = nh + h
            v_new = (uw[it][:, :GDN_HEAD] - ws_qs[it][:c]).astype(BF16)
            o = ws_qs[it][c:] + _dot(a_qk[it], v_new)
            k_dec = (k_ref[b, rs(ci), ks(h)] * gt["e_rest"][:, gl:gl + 1]).astype(BF16)
            s_ref[b * nh + h] = states[it] * gt["e_last"][:, gl:gl + 1] + _dot_tn(k_dec, v_new)
            o = o * lax.rsqrt(jnp.mean(o * o, axis=-1, keepdims=True) + GDN_EPS) * nw_ref[...]
            o_ref[b, rs(ci), vs(h)] = (o * _silu(z_ref[b, rs(ci), vs(h)])).astype(o_ref.dtype)


def _cast_specs(arrays, n_steps):
    specs, shapes = [], []
    for a in arrays:
        span = next(d for d in (1, 2, 4, 8)
                    if (a.shape[0] * d) % n_steps == 0 and (a.shape[0] * d // n_steps) % (2 * SUBLANES) == 0)
        specs.append(pl.BlockSpec((a.shape[0] * span // n_steps, a.shape[1]), lambda j, span=span: (j // span, 0)))
        shapes.append(jax.ShapeDtypeStruct(a.shape, BF16))
    return specs, shapes


def _gdn_chunk(q, k, v, zs, gb, norm_w, batch, seq_len, to_bf16):
    rows_per_step = GDN_CHUNK * GDN_CHUNKS_PER_STEP
    n_steps = seq_len // rows_per_step
    seq = lambda a: a.reshape(batch, seq_len, a.shape[-1])
    rows = lambda width: pl.BlockSpec((batch, rows_per_step, width), lambda j: (0, j, 0))
    cast_specs, cast_shapes = _cast_specs(to_bf16, n_steps)
    out, *casted = pl.pallas_call(
        functools.partial(_gdn_chunk_kernel, nb=batch, nck=GDN_CHUNKS_PER_STEP, n_cast=len(to_bf16)),
        grid=(n_steps,),
        in_specs=[rows(GDN_KDIM), rows(GDN_KDIM), rows(GDN_VDIM), rows(GDN_VDIM), rows(LANES),
                  pl.BlockSpec((1, GDN_HEAD), lambda j: (0, 0))] + cast_specs,
        out_specs=[rows(GDN_VDIM)] + cast_specs,
        out_shape=[jax.ShapeDtypeStruct((batch, seq_len, GDN_VDIM), BF16)] + cast_shapes,
        scratch_shapes=[pltpu.VMEM((batch * GDN_V_HEADS, GDN_HEAD, GDN_HEAD), F32)],
        compiler_params=_params("arbitrary"),
        name="gdn_chunk",
    )(seq(q), seq(k), seq(v), seq(zs), seq(gb), norm_w, *to_bf16)
    return out.reshape(batch * seq_len, GDN_VDIM), casted


def _proj_res_ln(a_ref, w_ref, r_ref, g_ref, b_ref):
    return _layer_norm(ALPHA * r_ref[...] + _dot(a_ref[...], w_ref[...]), g_ref[...], b_ref[...])


def _outproj_ffn_kernel(a_ref, wo_ref, r_ref, g0_ref, b0_ref, wg_ref, wu_ref, wd_ref, g1_ref, b1_ref, o_ref):
    x1 = _proj_res_ln(a_ref, wo_ref, r_ref, g0_ref, b0_ref)
    xb = x1.astype(BF16)
    h = (_silu(_dot(xb, wg_ref[...])) * _dot(xb, wu_ref[...])).astype(BF16)
    o_ref[...] = _layer_norm(ALPHA * x1 + _dot(h, wd_ref[...]), g1_ref[...], b1_ref[...])


def _outproj_ffn(a, w_out, res, g0, b0, wg, wu, wd, g1, b1):
    n, kdim = a.shape
    tm = TM_FFN
    resident = lambda w: pl.BlockSpec(w.shape, lambda i: (0, 0), pipeline_mode=pl.Buffered(1))
    rows = lambda width: pl.BlockSpec((tm, width), lambda i: (i, 0))
    vec = pl.BlockSpec((1, D_MODEL), lambda i: (0, 0))
    return pl.pallas_call(
        _outproj_ffn_kernel,
        grid=(n // tm,),
        in_specs=[rows(kdim), resident(w_out), rows(D_MODEL), vec, vec,
                  resident(wg), resident(wu), resident(wd), vec, vec],
        out_specs=rows(D_MODEL),
        out_shape=jax.ShapeDtypeStruct((n, D_MODEL), F32),
        compiler_params=_params("parallel"),
        name="gdn_outproj_ffn",
    )(a, w_out, res, g0, b0, wg, wu, wd, g1, b1)


def _band_tables():
    assert SWA_WINDOW == SWA_BLOCK
    r = np.arange(SWA_BLOCK)[:, None]
    q = np.arange(SWA_BLOCK)[None, :]
    d = (q - r) % SWA_BLOCK
    max_exact = REL_BUCKETS // 2
    df = np.maximum(d, 1).astype(np.float32)
    large = max_exact + (np.log(df / np.float32(max_exact)) / np.float32(math.log(REL_MAX_DIST / max_exact))
                         * np.float32(REL_BUCKETS - max_exact)).astype(np.int32)
    large = np.minimum(large, REL_BUCKETS - 1)
    bucket = np.where(d < max_exact, d, large).astype(np.int32)
    valid = np.stack([q >= r, np.ones_like(q >= r)]).astype(np.int32)
    return bucket, valid


def _bias_kernel(relb_ref, bucket_ref, valid_ref, o_ref):
    bucket = bucket_ref[...]
    in_bucket = [bucket == b for b in range(REL_BUCKETS)]
    for h in range(SWA_Q_HEADS):
        acc = jnp.zeros(bucket.shape, F32)
        for b in range(REL_BUCKETS):
            acc = jnp.where(in_bucket[b], relb_ref[b, h], acc)
        for t in range(2):
            o_ref[t, h] = jnp.where(valid_ref[t] != 0, acc, NEG_BIG)


def _bias_table(rel_bias):
    bucket, valid = _band_tables()
    shape = (SWA_BLOCK, SWA_BLOCK)
    return pl.pallas_call(
        _bias_kernel,
        grid=(1,),
        in_specs=[pl.BlockSpec(memory_space=pltpu.SMEM), pl.BlockSpec(shape, lambda i: (0, 0)),
                  pl.BlockSpec((2,) + shape, lambda i: (0, 0, 0))],
        out_specs=pl.BlockSpec((2, SWA_Q_HEADS) + shape, lambda i: (0, 0, 0, 0)),
        out_shape=jax.ShapeDtypeStruct((2, SWA_Q_HEADS) + shape, F32),
        compiler_params=_params("arbitrary"),
        name="swa_bias_table",
    )(rel_bias, jnp.asarray(bucket), jnp.asarray(valid))


def _band_attention(q_pair, kv, bias_of, sink_ref, put_pair):
    dh = SWA_HEAD_DIM
    blk = SWA_BLOCK
    nk = 2 * blk
    from_prev = lax.broadcasted_iota(jnp.int32, (blk, blk), 1) < lax.broadcasted_iota(jnp.int32, (blk, blk), 0)
    k_all = kv[:, :SWA_KVDIM]
    lane = lax.broadcasted_iota(jnp.int32, k_all.shape, 1)
    zero = jnp.zeros_like(k_all)
    for kh in range(SWA_KV_HEADS):
        k_own = jnp.where((lane >= kh * dh) & (lane < (kh + 1) * dh), k_all, zero)
        k_other = pltpu.roll(k_own.astype(F32), dh, axis=1).astype(BF16)
        k_even, k_odd = (k_own, k_other) if kh == 0 else (k_other, k_own)
        k2 = jnp.concatenate([k_even, k_odd], axis=0)
        v_t = kv[:, SWA_KVDIM + kh * dh:SWA_KVDIM + (kh + 1) * dh].T
        pairs = [kh * (SWA_GROUP // 2) + p for p in range(SWA_GROUP // 2)]
        heads = [2 * p + r for p in pairs for r in range(2)]
        st = [_dot_nt(k2, q_pair(p)) for p in pairs]
        scores = [jnp.where(from_prev, st[i // 2][(i % 2) * nk:(i % 2) * nk + blk],
                            st[i // 2][(i % 2) * nk + blk:(i % 2 + 1) * nk]) + bias_of(h)
                  for i, h in enumerate(heads)]
        sinks = [sink_ref[0:1, h:h + 1] for h in heads]
        maxes = [jnp.maximum(jnp.max(s, axis=0, keepdims=True), sk) for s, sk in zip(scores, sinks)]
        expo = [jnp.exp(s - m) for s, m in zip(scores, maxes)]
        sums = [jnp.sum(p, axis=0, keepdims=True) for p in expo]
        inv_den = [1.0 / (t + jnp.exp(sk - m)) for t, sk, m in zip(sums, sinks, maxes)]
        expo = [jnp.concatenate([jnp.where(from_prev, p, 0.0), jnp.where(from_prev, 0.0, p)], axis=0) for p in expo]
        for i, p in enumerate(pairs):
            o_t = _dot(v_t, jnp.concatenate(expo[2 * i:2 * i + 2], axis=1).astype(BF16))
            z = jnp.concatenate([o_t[:, :SWA_BLOCK] * inv_den[2 * i], o_t[:, SWA_BLOCK:] * inv_den[2 * i + 1]],
                                axis=0)
            put_pair(p, z.T)


def _swa_layer_kernel(x_ref, win_ref, bin_ref, bias_ref, sink_ref, wout_ref, g_ref, b_ref, o_ref,
                      q_ref, kv_ref, att_ref, *, nblk):
    j = pl.program_id(1)
    blk = SWA_BLOCK
    tm = nblk * blk
    dh2 = 2 * SWA_HEAD_DIM

    p = _dot(x_ref[...].astype(BF16), win_ref[...]) + bin_ref[...]
    q_ref[...] = (p[:, :SWA_QDIM] * (SWA_HEAD_DIM ** -0.5)).astype(BF16)

    @pl.when(j == 0)
    def _():
        kv_ref[0:blk, :] = jnp.zeros((blk, 2 * SWA_KVDIM), BF16)

    @pl.when(j != 0)
    def _():
        kv_ref[0:blk, :] = kv_ref[tm:tm + blk, :]

    kv_ref[blk:, :] = p[:, SWA_QDIM:].astype(BF16)

    for i in range(nblk):
        rows = slice(i * blk, (i + 1) * blk)
        table = jnp.where(j == 0, 0, 1) if i == 0 else 1

        def put_pair(pair, o, rows=rows):
            att_ref[rows, pair * dh2:(pair + 1) * dh2] = o.astype(att_ref.dtype)

        _band_attention(lambda pair, rows=rows: q_ref[rows, pair * dh2:(pair + 1) * dh2],
                        kv_ref[i * blk:(i + 2) * blk, :],
                        lambda h, table=table: bias_ref[table, h], sink_ref, put_pair)

    o_ref[...] = _proj_res_ln(att_ref, wout_ref, x_ref, g_ref, b_ref)


def _swa_layer(x, w_in, b_in, bias, sinks, w_out, g, b, batch, seq_len):
    n = x.shape[0]
    nblk = SWA_BLOCKS_PER_STEP
    tm = nblk * SWA_BLOCK
    steps = seq_len // tm
    resident = lambda a: pl.BlockSpec(a.shape, lambda s, j: (0,) * a.ndim, pipeline_mode=pl.Buffered(1))
    rows = pl.BlockSpec((tm, D_MODEL), lambda s, j: (s * steps + j, 0))
    return pl.pallas_call(
        functools.partial(_swa_layer_kernel, nblk=nblk),
        grid=(batch, steps),
        in_specs=[rows, resident(w_in), resident(b_in), resident(bias), resident(sinks), resident(w_out),
                  resident(g), resident(b)],
        out_specs=rows,
        out_shape=jax.ShapeDtypeStruct((n, D_MODEL), F32),
        scratch_shapes=[pltpu.VMEM((tm, SWA_QDIM), BF16), pltpu.VMEM((tm + SWA_BLOCK, 2 * SWA_KVDIM), BF16),
                        pltpu.VMEM((tm, SWA_QDIM), BF16)],
        compiler_params=_params("parallel", "arbitrary"),
        name="swa_layer",
    )(x, w_in, b_in, bias, sinks, w_out, g, b)


def _router_kernel(x_ref, wr_ref, route_ref, route_t_ref, cnt_ref, run_ref, *, tm):
    @pl.when(pl.program_id(0) == 0)
    def _():
        run_ref[...] = jnp.zeros(run_ref.shape, F32)

    logits = _dot3(x_ref[...], wr_ref[...])
    lane = lax.broadcasted_iota(jnp.int32, logits.shape, 1)
    lane_f = lane.astype(F32)
    lg = jnp.where(lane < N_EXPERTS, logits, NEG_BIG)
    m1 = jnp.max(lg, axis=-1, keepdims=True)
    i1 = jnp.min(jnp.where(lg == m1, lane_f, float(LANES)), axis=-1, keepdims=True)
    oh1 = lane_f == i1
    lg2 = jnp.where(oh1, NEG_BIG, lg)
    m2 = jnp.max(lg2, axis=-1, keepdims=True)
    i2 = jnp.min(jnp.where(lg2 == m2, lane_f, float(LANES)), axis=-1, keepdims=True)
    oh2 = lane_f == i2
    e = jnp.exp(m2 - m1)
    w0 = 1.0 / (1.0 + e)
    w1 = e * w0

    cnt = jnp.where(oh1, 1.0, 0.0) + jnp.where(oh2, 1.0, 0.0)
    r = lax.broadcasted_iota(jnp.int32, (tm, tm), 0)
    c = lax.broadcasted_iota(jnp.int32, (tm, tm), 1)
    before = jnp.where(r > c, 1.0, 0.0).astype(BF16)
    excl = _dot(before, cnt.astype(BF16)) + run_ref[...]
    rank0 = jnp.sum(jnp.where(oh1, excl, 0.0), axis=-1, keepdims=True)
    rank1 = jnp.sum(jnp.where(oh2, excl, 0.0), axis=-1, keepdims=True)
    run = run_ref[...] + jnp.sum(cnt, axis=0, keepdims=True)
    run_ref[...] = run
    cnt_ref[...] = run

    vals = (i1, i2, rank0, rank1, w0, w1)
    out = jnp.zeros(logits.shape, F32)
    for idx, val in enumerate(vals):
        out = jnp.where(lane == idx, val, out)
    route_ref[...] = out
    route_t_ref[...] = out.T[:SUBLANES, :]


def _router(x, w_router_padded):
    n = x.shape[0]
    tm = TM_ROUTER
    return pl.pallas_call(
        functools.partial(_router_kernel, tm=tm),
        grid=(n // tm,),
        in_specs=[pl.BlockSpec((tm, D_MODEL), lambda i: (i, 0)), pl.BlockSpec((D_MODEL, LANES), lambda i: (0, 0))],
        out_specs=[pl.BlockSpec((tm, LANES), lambda i: (i, 0)), pl.BlockSpec((SUBLANES, tm), lambda i: (0, i)),
                   pl.BlockSpec((1, LANES), lambda i: (0, 0))],
        out_shape=[jax.ShapeDtypeStruct((n, LANES), F32), jax.ShapeDtypeStruct((SUBLANES, n), F32),
                   jax.ShapeDtypeStruct((1, LANES), F32)],
        scratch_shapes=[pltpu.VMEM((1, LANES), F32)],
        compiler_params=_params("arbitrary"),
        name="moe_router",
    )(x, w_router_padded)


def _dispatch_kernel(pos0_ref, pos1_ref, meta_ref, x_ref, wsrc_ref, xs_hbm, wdst_ref, zero_ref, sem, pad_sem, *, td):
    i = pl.program_id(0)
    wdst_ref[...] = wsrc_ref[...].astype(BF16)

    def row_copy(t, dst):
        return pltpu.make_async_copy(x_ref.at[pl.ds(t, 1)], xs_hbm.at[pl.ds(dst, 1)], sem)

    def start(t, carry):
        row_copy(t, pos0_ref[0, 0, t]).start()
        row_copy(t, pos1_ref[0, 0, t]).start(priority=1)
        return carry

    lax.fori_loop(0, td, start, 0, unroll=DMA_UNROLL)

    @pl.when(i == pl.num_programs(0) - 1)
    def _():
        zero_ref[...] = jnp.zeros(zero_ref.shape, F32)

        zr = zero_ref.shape[0]

        def row_zero(dst):
            return pltpu.make_async_copy(zero_ref.at[pl.ds(0, 1)], xs_hbm.at[pl.ds(dst, 1)], pad_sem)

        def block_zero(blk):
            return pltpu.make_async_copy(zero_ref, xs_hbm.at[pl.ds(pl.multiple_of(blk * zr, zr), zr)], pad_sem)

        def zero_range(copy, lo, hi):
            lax.fori_loop(lo, hi, lambda r, carry: (copy(r).start(), carry)[1], 0)
            lax.fori_loop(lo, hi, lambda r, carry: (copy(0).wait(), carry)[1], 0)

        for e in range(N_EXPERTS):
            off, cnt, cnt_up, padded = (meta_ref[r, e] for r in range(4))
            zero_range(lambda r, off=off: row_zero(off + r), cnt, cnt_up)
            zero_range(block_zero, (off + cnt_up) // zr, (off + padded) // zr)

        used = meta_ref[0, N_EXPERTS - 1] + meta_ref[3, N_EXPERTS - 1]
        zero_range(block_zero, used // zr, xs_hbm.shape[0] // zr)

    for _ in range(2):
        pltpu.make_async_copy(x_ref, xs_hbm.at[pl.ds(0, td)], sem).wait()


def _dispatch(x, pos, meta, n_rows, to_bf16):
    n = x.shape[0]
    td = TD_DISPATCH
    pos_spec = pl.BlockSpec((1, 1, td), lambda i: (i, 0, 0), memory_space=pltpu.SMEM)
    (cast_spec,), (cast_shape,) = _cast_specs([to_bf16], n // td)
    return pl.pallas_call(
        functools.partial(_dispatch_kernel, td=td),
        grid=(n // td,),
        in_specs=[pos_spec, pos_spec, pl.BlockSpec(memory_space=pltpu.SMEM),
                  pl.BlockSpec((td, D_MODEL), lambda i: (i, 0)), cast_spec],
        out_specs=[pl.BlockSpec(memory_space=pl.ANY), cast_spec],
        out_shape=[jax.ShapeDtypeStruct((n_rows, D_MODEL), F32), cast_shape],
        scratch_shapes=[pltpu.VMEM((ZERO_ROWS, D_MODEL), F32), pltpu.SemaphoreType.DMA, pltpu.SemaphoreType.DMA],
        compiler_params=_params("arbitrary"),
        name="moe_dispatch",
    )(pos[0].reshape(n // td, 1, td), pos[1].reshape(n // td, 1, td), meta, x, to_bf16)


def _moe_kernel(we_ref, wb_ref, nv_ref, cnt_ref, *refs, n_sub, tm):
    xs_refs = refs[:n_sub]
    wg_ref, wu_ref, wd_ref, ys_hbm, acc_ref, xb_ref, sem = refs[n_sub:]
    j = pl.program_id(0)
    f = pl.program_id(1)
    last_f = pl.num_programs(1) - 1
    n_windows, n_tiles = cnt_ref[0], cnt_ref[1]

    def tile_write(slot, block):
        return pltpu.make_async_copy(acc_ref.at[slot], ys_hbm.at[pl.ds(pl.multiple_of(block * tm, tm), tm)],
                                     sem.at[slot])

    @pl.when(j < n_windows)
    def _():
        prev_tiles = jnp.where(j > 0, nv_ref[jnp.maximum(j - 1, 0)], 0)

        for s in range(n_sub):
            @pl.when(s < nv_ref[j])
            def _(s=s):
                @pl.when(f == 0)
                def _():
                    xb_ref[s] = xs_refs[s][...].astype(BF16)

                xb = xb_ref[s]
                h = (_silu(_dot(xb, wg_ref[0])) * _dot(xb, wu_ref[0])).astype(BF16)
                part = _dot(h, wd_ref[0])

                @pl.when(f == 0)
                def _():
                    @pl.when(s < prev_tiles)
                    def _():
                        tile_write(s, 0).wait()

                    acc_ref[s] = part

                @pl.when(f != 0)
                def _():
                    acc_ref[s] += part

                @pl.when(f == last_f)
                def _():
                    tile_write(s, wb_ref[j] + s).start()

            @pl.when((f == 0) & (s >= nv_ref[j]) & (s < prev_tiles))
            def _(s=s):
                tile_write(s, 0).wait()

        @pl.when((f == last_f) & (j == n_windows - 1))
        def _():
            for s in range(n_sub):
                @pl.when(s < nv_ref[j])
                def _(s=s):
                    tile_write(s, 0).wait()

            acc_ref[0] = jnp.zeros(acc_ref.shape[1:], F32)
            total_tiles = ys_hbm.shape[0] // tm
            lax.fori_loop(n_tiles, total_tiles, lambda t, c: (tile_write(0, t).start(), c)[1], 0)
            lax.fori_loop(n_tiles, total_tiles, lambda t, c: (tile_write(0, 0).wait(), c)[1], 0)


def _moe_experts(xs, wg, wu, wd, win_expert, win_block, win_tiles, counts):
    n_rows = xs.shape[0]
    tm, tf, n_sub = TM_MOE, TF_MOE, MOE_TILES_PER_WINDOW
    max_windows = win_expert.shape[0]

    def xs_spec(s):
        return pl.BlockSpec((tm, D_MODEL), lambda j, f, we, wb, nv, cnt: (wb[j] + jnp.minimum(s, nv[j] - 1), 0))

    grid_spec = pltpu.PrefetchScalarGridSpec(
        num_scalar_prefetch=4,
        grid=(max_windows, EXPERT_DIM // tf),
        in_specs=[xs_spec(s) for s in range(n_sub)] + [
            pl.BlockSpec((1, D_MODEL, tf), lambda j, f, we, wb, nv, cnt: (we[j], 0, f)),
            pl.BlockSpec((1, D_MODEL, tf), lambda j, f, we, wb, nv, cnt: (we[j], 0, f)),
            pl.BlockSpec((1, tf, D_MODEL), lambda j, f, we, wb, nv, cnt: (we[j], f, 0))],
        out_specs=pl.BlockSpec(memory_space=pl.ANY),
        scratch_shapes=[pltpu.VMEM((n_sub, tm, D_MODEL), F32), pltpu.VMEM((n_sub, tm, D_MODEL), BF16),
                        pltpu.SemaphoreType.DMA((n_sub,))],
    )
    return pl.pallas_call(
        functools.partial(_moe_kernel, n_sub=n_sub, tm=tm),
        grid_spec=grid_spec,
        out_shape=jax.ShapeDtypeStruct((n_rows, D_MODEL), F32),
        compiler_params=_params("arbitrary", "arbitrary"),
        name="moe_experts",
    )(win_expert, win_block, win_tiles, counts, *([xs] * n_sub), wg, wu, wd)


def _combine_kernel(pos0_ref, pos1_ref, pos0n_ref, pos1n_ref, x_ref, route_ref, ys_hbm, g_ref, b_ref, o_ref,
                    y_ref, sem, *, tc):
    i = pl.program_id(0)
    slot = i % 2

    def issue(p_refs, s):
        def start(t, carry):
            for kk in range(2):
                pltpu.make_async_copy(ys_hbm.at[pl.ds(p_refs[kk][0, 0, t], 1)],
                                      y_ref.at[s, kk, pl.ds(t, 1)], sem.at[s]).start(priority=kk)
            return carry

        lax.fori_loop(0, tc, start, 0, unroll=DMA_UNROLL)

    @pl.when(i == 0)
    def _():
        issue((pos0_ref, pos1_ref), 0)

    @pl.when(i + 1 < pl.num_programs(0))
    def _():
        issue((pos0n_ref, pos1n_ref), 1 - slot)

    for kk in range(2):
        pltpu.make_async_copy(ys_hbm.at[pl.ds(0, tc)], y_ref.at[slot, kk], sem.at[slot]).wait()

    route = route_ref[...]
    f = route[:, 4:5] * y_ref[slot, 0] + route[:, 5:6] * y_ref[slot, 1]
    o_ref[...] = _layer_norm(ALPHA * x_ref[...] + f, g_ref[...], b_ref[...])


def _combine_ln(x, route, pos, ys, g, b):
    n = x.shape[0]
    tc = TC_COMBINE
    nt = n // tc
    pos0, pos1 = pos[0].reshape(nt, 1, tc), pos[1].reshape(nt, 1, tc)
    cur = pl.BlockSpec((1, 1, tc), lambda i: (i, 0, 0), memory_space=pltpu.SMEM)
    nxt = pl.BlockSpec((1, 1, tc), lambda i: (jnp.minimum(i + 1, nt - 1), 0, 0), memory_space=pltpu.SMEM)
    return pl.pallas_call(
        functools.partial(_combine_kernel, tc=tc),
        grid=(nt,),
        in_specs=[cur, cur, nxt, nxt,
                  pl.BlockSpec((tc, D_MODEL), lambda i: (i, 0)),
                  pl.BlockSpec((tc, LANES), lambda i: (i, 0)),
                  pl.BlockSpec(memory_space=pl.ANY),
                  pl.BlockSpec((1, D_MODEL), lambda i: (0, 0)), pl.BlockSpec((1, D_MODEL), lambda i: (0, 0))],
        out_specs=pl.BlockSpec((tc, D_MODEL), lambda i: (i, 0)),
        out_shape=jax.ShapeDtypeStruct((n, D_MODEL), F32),
        scratch_shapes=[pltpu.VMEM((2, 2, tc, D_MODEL), F32), pltpu.SemaphoreType.DMA((2,))],
        compiler_params=_params("arbitrary"),
        name="moe_combine_ln",
    )(pos0, pos1, pos0, pos1, x, route, ys, g, b)


def _moe_layer(x, route, route_t, counts_f, wg, wu, wd_f32, g, b):
    n = x.shape[0]
    tm = TM_MOE

    counts = counts_f[0, :N_EXPERTS].astype(jnp.int32)
    padded = ((counts + tm - 1) // tm) * tm
    offs = jnp.cumsum(padded) - padded
    eids = jnp.arange(N_EXPERTS, dtype=jnp.int32)
    e01 = route_t[0:2].astype(jnp.int32)
    rank01 = route_t[2:4].astype(jnp.int32)
    base = jnp.sum(jnp.where(e01[:, None, :] == eids[None, :, None], offs[None, :, None], 0), axis=1)
    pos = base + rank01
    counts_up = jnp.minimum(padded, ((counts + ZERO_ROWS - 1) // ZERO_ROWS) * ZERO_ROWS)
    meta = jnp.stack([offs, counts, counts_up, padded]).astype(jnp.int32)

    n_rows = 2 * n + N_EXPERTS * tm
    nsub = MOE_TILES_PER_WINDOW
    tiles = padded // tm
    wins = (tiles + nsub - 1) // nsub
    win_end = jnp.cumsum(wins)
    n_windows = win_end[-1]
    max_windows = (n_rows // tm + nsub - 1) // nsub + N_EXPERTS
    wj = jnp.minimum(jnp.arange(max_windows, dtype=jnp.int32), n_windows - 1)
    win_expert = jnp.minimum(jnp.sum((wj[:, None] >= win_end[None, :]).astype(jnp.int32), axis=-1), N_EXPERTS - 1)
    pick = lambda v: jnp.sum(jnp.where(win_expert[:, None] == eids[None, :], v[None, :], 0), axis=-1)
    local = wj - pick(win_end - wins)
    win_block = pick(offs // tm) + nsub * local
    win_tiles = jnp.minimum(nsub, pick(tiles) - nsub * local)
    counts_nt = jnp.stack([n_windows, jnp.sum(tiles)])

    xs, wd = _dispatch(x, pos, meta, n_rows, wd_f32.reshape(-1, D_MODEL))
    wd = wd.reshape(wd_f32.shape)
    i32 = lambda a: a.astype(jnp.int32)
    ys = _moe_experts(xs, wg, wu, wd, i32(win_expert), i32(win_block), i32(win_tiles), i32(counts_nt))
    return _combine_ln(x, route, pos, ys, g, b)


def kernel(x, a_w_in, a_conv_w, a_a_log, a_dt_bias, a_norm_w, a_w_out, b_w_in, b_b_in, b_sinks, b_w_out, rel_bias,
           ffn_w_gate, ffn_w_up, ffn_w_down, moe_router, moe_w_gate, moe_w_up, moe_w_down, ln_g, ln_b):
    batch, seq_len, _ = x.shape
    n = batch * seq_len
    x0 = x.reshape(n, D_MODEL)
    ln_g = ln_g.reshape(DEPTH, 2, 1, D_MODEL)
    ln_b = ln_b.reshape(DEPTH, 2, 1, D_MODEL)

    pad_gate = lambda p: jnp.pad(p.reshape(1, GDN_V_HEADS), ((0, 0), (GDN_V_HEADS, LANES - 2 * GDN_V_HEADS)))
    gate_params = jnp.concatenate([pad_gate(a_a_log[0]), pad_gate(a_dt_bias[0])], axis=0)
    q, k, v, zs, gb = _gdn_inproj(x0, jnp.swapaxes(a_w_in, 1, 2), a_conv_w[0], gate_params, seq_len)
    o, casted = _gdn_chunk(q, k, v, zs, gb, a_norm_w[0].reshape(1, GDN_HEAD), batch, seq_len,
                           [moe_w_gate[0].reshape(-1, EXPERT_DIM), moe_w_up[0].reshape(-1, EXPERT_DIM),
                            ffn_w_gate[0], ffn_w_up[0], ffn_w_down[0],
                            a_w_out[0], b_w_in[0], b_w_out[0]])
    moe_wg, moe_wu, ffn_wg, ffn_wu, ffn_wd, w_out_a, w_in_b, w_out_b = casted
    x2 = _outproj_ffn(o, w_out_a, x0, ln_g[0, 0], ln_b[0, 0],
                      ffn_wg, ffn_wu, ffn_wd, ln_g[0, 1], ln_b[0, 1])

    bias = _bias_table(rel_bias)
    sinks = jnp.pad(b_sinks[0].reshape(1, SWA_Q_HEADS), ((0, 0), (0, LANES - SWA_Q_HEADS)))
    x3 = _swa_layer(x2, w_in_b, b_b_in[0].reshape(1, -1), bias, sinks, w_out_b, ln_g[1, 0], ln_b[1, 0],
                    batch, seq_len)
    w_router = jnp.pad(moe_router[0], ((0, 0), (0, LANES - N_EXPERTS)))
    route, route_t, counts = _router(x3, w_router)
    expert_shape = (N_EXPERTS, D_MODEL, EXPERT_DIM)
    x4 = _moe_layer(x3, route, route_t, counts, moe_wg.reshape(expert_shape), moe_wu.reshape(expert_shape),
                    moe_w_down[0], ln_g[1, 1], ln_b[1, 1])
    return x4.reshape(batch, seq_len, D_MODEL)
```

```python
import functools
import math

import numpy as np
import jax
import jax.numpy as jnp
from jax import lax
from jax.experimental import pallas as pl
from jax.experimental.pallas import tpu as pltpu

F32 = jnp.float32
BF16 = jnp.bfloat16

D_MODEL = 1024
DEPTH = 2
ALPHA = (2.0 * DEPTH) ** 0.25
LN_EPS = 1e-5

GDN_K_HEADS = 4
GDN_V_HEADS = 8
GDN_HEAD = 128
GDN_KDIM = GDN_K_HEADS * GDN_HEAD
GDN_VDIM = GDN_V_HEADS * GDN_HEAD
GDN_CONV = 4
GDN_CHUNK = 64
GDN_QKV = 2 * GDN_KDIM + GDN_VDIM
GDN_EPS = 1e-6

SWA_Q_HEADS = 16
SWA_KV_HEADS = 2
SWA_GROUP = SWA_Q_HEADS // SWA_KV_HEADS
SWA_HEAD_DIM = 64
SWA_WINDOW = 128
SWA_BLOCK = 128
SWA_QDIM = SWA_Q_HEADS * SWA_HEAD_DIM
SWA_KVDIM = SWA_KV_HEADS * SWA_HEAD_DIM
REL_BUCKETS = 32
REL_MAX_DIST = 128

FFN_DIM = 2816
N_EXPERTS = 8
EXPERT_DIM = 3584

LANES = 128
SUBLANES = 8
NEG_BIG = -1e30
VMEM_LIMIT = 56 * 1024 * 1024

TM_GDN_IN = 512
TM_FFN = 512
SWA_BLOCKS_PER_STEP = 8
TM_ROUTER = 2048
ROUTER_PARTS = 8
TM_MOE = 512
TF_MOE = 1792
MOE_TILES_PER_WINDOW = 3
TD_DISPATCH = 1024
TC_COMBINE = 512
ZERO_ROWS = 64
DMA_UNROLL = 8


def _params(*sem):
    return pltpu.CompilerParams(dimension_semantics=sem, vmem_limit_bytes=VMEM_LIMIT)


def _dot(a, b):
    return jnp.dot(a, b, preferred_element_type=F32)


def _dot_nt(a, b):
    return lax.dot_general(a, b, (((1,), (1,)), ((), ())), preferred_element_type=F32)


def _dot_tn(a, b):
    return lax.dot_general(a, b, (((0,), (0,)), ((), ())), preferred_element_type=F32)


def _split(x):
    hi = x.astype(BF16)
    lo = (x - hi.astype(F32)).astype(BF16)
    return hi, lo


def _dot3(a, b):
    ah, al = _split(a)
    bh, bl = _split(b)
    return _dot(ah, bh) + (_dot(ah, bl) + _dot(al, bh))


def _silu(x):
    return x * jax.nn.sigmoid(x)


def _layer_norm(y, g, b):
    mu = jnp.mean(y, axis=-1, keepdims=True)
    yc = y - mu
    var = jnp.mean(yc * yc, axis=-1, keepdims=True)
    return yc * lax.rsqrt(var + LN_EPS) * g + b


def _gdn_inproj_kernel(x_ref, wf_ref, convw_ref, gp_ref,
                       q_ref, k_ref, v_ref, z_ref, gb_ref, ext_ref, w_ref, *, tm, tiles_per_seq):
    i = pl.program_id(0)
    xb = x_ref[...].astype(BF16)

    @pl.when(i == 0)
    def _():
        n_in = wf_ref.shape[1]
        n_full = n_in // LANES
        for c in range(n_full):
            cs = slice(c * LANES, (c + 1) * LANES)
            w_ref[:, cs] = wf_ref[0, cs, :].T.astype(BF16)
        tail = jnp.concatenate([wf_ref[0, n_full * LANES:, :],
                                jnp.zeros(((n_full + 1) * LANES - n_in, wf_ref.shape[2]), F32)], axis=0)
        w_ref[:, n_full * LANES:] = tail.T.astype(BF16)

    @pl.when(i % tiles_per_seq == 0)
    def _():
        ext_ref[0:SUBLANES, :] = jnp.zeros((SUBLANES, GDN_QKV), F32)

    @pl.when(i % tiles_per_seq != 0)
    def _():
        ext_ref[0:SUBLANES, :] = ext_ref[tm:tm + SUBLANES, :]

    ext_ref[SUBLANES:, :] = _dot(xb, w_ref[:, :GDN_QKV])
    z_ref[...] = _dot(xb, w_ref[:, GDN_QKV:GDN_QKV + GDN_VDIM])

    n_chunks = GDN_QKV // LANES
    for c in range(n_chunks):
        cs = slice(c * LANES, (c + 1) * LANES)
        acc = convw_ref[GDN_CONV - 1:GDN_CONV, cs] * ext_ref[SUBLANES:SUBLANES + tm, cs]
        for j in range(GDN_CONV - 1):
            off = SUBLANES - (GDN_CONV - 1) + j
            acc = acc + convw_ref[j:j + 1, cs] * ext_ref[off:off + tm, cs]
        y = _silu(acc)
        if c < 2 * GDN_K_HEADS:
            y = y * lax.rsqrt(jnp.sum(y * y, axis=-1, keepdims=True) + GDN_EPS)
            if c < GDN_K_HEADS:
                q_ref[:, cs] = y * (GDN_HEAD ** -0.5)
            else:
                k_ref[:, (c - GDN_K_HEADS) * LANES:(c - GDN_K_HEADS + 1) * LANES] = y
        else:
            cv = c - 2 * GDN_K_HEADS
            v_ref[:, cv * LANES:(cv + 1) * LANES] = y

    ba = _dot(xb, w_ref[:, GDN_QKV + GDN_VDIM:])
    lane = lax.broadcasted_iota(jnp.int32, ba.shape, 1)
    sp = ba + gp_ref[1:2, :]
    softplus = jnp.maximum(sp, 0.0) + jnp.log(1.0 + jnp.exp(-jnp.abs(sp)))
    g = -jnp.exp(gp_ref[0:1, :]) * softplus
    gb_ref[...] = jnp.where(lane < GDN_V_HEADS, jax.nn.sigmoid(ba), g)


def _gdn_inproj(x2d, w_in, conv_w, gate_params, seq_len):
    n = x2d.shape[0]
    tm = TM_GDN_IN
    kern = functools.partial(_gdn_inproj_kernel, tm=tm, tiles_per_seq=seq_len // tm)
    full = lambda shape: pl.BlockSpec(shape, lambda i: (0,) * len(shape))
    rows = lambda width: pl.BlockSpec((tm, width), lambda i: (i, 0))
    return pl.pallas_call(
        kern,
        grid=(n // tm,),
        in_specs=[rows(D_MODEL), pl.BlockSpec(w_in.shape, lambda i: (0, 0, 0), pipeline_mode=pl.Buffered(1)),
                  full(conv_w.shape), full(gate_params.shape)],
        out_specs=[rows(GDN_KDIM), rows(GDN_KDIM), rows(GDN_VDIM), rows(GDN_VDIM), rows(LANES)],
        out_shape=[jax.ShapeDtypeStruct((n, GDN_KDIM), F32), jax.ShapeDtypeStruct((n, GDN_KDIM), F32),
                   jax.ShapeDtypeStruct((n, GDN_VDIM), F32), jax.ShapeDtypeStruct((n, GDN_VDIM), F32),
                   jax.ShapeDtypeStruct((n, LANES), F32)],
        scratch_shapes=[pltpu.VMEM((tm + SUBLANES, GDN_QKV), F32),
                        pltpu.VMEM((D_MODEL, GDN_QKV + GDN_VDIM + LANES), BF16)],
        compiler_params=_params("arbitrary"),
        name="gdn_inproj",
    )(x2d, w_in, conv_w, gate_params)


GDN_CHUNKS_PER_STEP = 2
NEUMANN_BLOCK = 4


def _bf16_all(xs):
    return [x.astype(BF16) for x in xs]


def _dot_all(a_list, b_list):
    return [_dot(a, b) for a, b in zip(a_list, b_list)]


def _unit_lower_inverse_all(a_list, row, col):
    shift = int(math.log2(NEUMANN_BLOCK))
    eye = jnp.where(row == col, 1.0, 0.0).astype(F32)
    on_diag_block = (row >> shift) == (col >> shift)
    d = [jnp.where(on_diag_block, a, 0.0) for a in a_list]
    d_b = _bf16_all(d)
    x = [eye - di for di in d]
    d2_b = _bf16_all(_dot_all(d_b, d_b))
    x = [xi + t for xi, t in zip(x, _dot_all(_bf16_all(x), d2_b))]
    size = NEUMANN_BLOCK
    while size < a_list[0].shape[0]:
        shift = int(math.log2(size))
        rbl, cbl = row >> shift, col >> shift
        below = ((rbl & 1) == 1) & (cbl == rbl - 1)
        l_b = _bf16_all([jnp.where(below, a, 0.0) for a in a_list])
        x_b = _bf16_all(x)
        xl_b = _bf16_all(_dot_all(x_b, l_b))
        x = [xi - t for xi, t in zip(x, _dot_all(xl_b, x_b))]
        size *= 2
    return x


def _gdn_chunk_kernel(q_ref, k_ref, v_ref, z_ref, gb_ref, nw_ref, *rest, nb, nck, n_cast):
    cast_src, (o_ref, *cast_dst), s_ref = rest[:n_cast], rest[n_cast:2 * n_cast + 1], rest[-1]
    for src, dst in zip(cast_src, cast_dst):
        dst[...] = src[...].astype(BF16)
    c = GDN_CHUNK
    nh = GDN_V_HEADS

    @pl.when(pl.program_id(0) == 0)
    def _():
        s_ref[...] = jnp.zeros(s_ref.shape, F32)

    row = lax.broadcasted_iota(jnp.int32, (c, c), 0)
    col = lax.broadcasted_iota(jnp.int32, (c, c), 1)
    causal = row >= col
    strict = row > col
    tril = jnp.where(causal, 1.0, 0.0).astype(BF16)
    rs = lambda ci: slice(ci * c, (ci + 1) * c)
    ks = lambda h: slice((h // (nh // GDN_K_HEADS)) * GDN_HEAD, (h // (nh // GDN_K_HEADS) + 1) * GDN_HEAD)
    vs = lambda h: slice(h * GDN_HEAD, (h + 1) * GDN_HEAD)

    gates = {}
    for ci in range(nck):
        for b in range(nb):
            gb = gb_ref[b, rs(ci), :]
            g_hi, g_lo = _split(gb)
            gc = _dot(tril, g_hi) + _dot(tril, g_lo)
            g_last = gc[c - 1:c, :]
            gates[ci, b] = dict(gb=gb, gc=gc, gc_t=gc.T, eg=jnp.exp(gc), e_last=jnp.exp(g_last),
                                e_rest=jnp.exp(g_last - gc))

    items = [(ci, b, h) for ci in range(nck) for b in range(nb) for h in range(nh)]

    a_kk, a_qk, k_beta = [], {}, {}
    for it in items:
        ci, b, h = it
        gt = gates[ci, b]
        k = k_ref[b, rs(ci), ks(h)]
        kb = k * gt["gb"][:, h:h + 1]
        kq = jnp.concatenate([kb, q_ref[b, rs(ci), ks(h)]], axis=0).astype(BF16)
        gram = _dot_nt(kq, k.astype(BF16))
        gl = nh + h
        decay = jnp.exp(jnp.where(causal, gt["gc"][:, gl:gl + 1] - gt["gc_t"][gl:gl + 1, :], NEG_BIG))
        a_kk.append(jnp.where(strict, gram[:c] * decay, 0.0))
        a_qk[it] = (gram[c:] * decay).astype(BF16)
        k_beta[it] = kb

    t_inv = dict(zip(items, _unit_lower_inverse_all(a_kk, row, col)))

    uw = {}
    for it in items:
        ci, b, h = it
        gt = gates[ci, b]
        gl = nh + h
        rhs = jnp.concatenate([v_ref[b, rs(ci), vs(h)] * gt["gb"][:, h:h + 1],
                               k_beta[it] * gt["eg"][:, gl:gl + 1]], axis=1)
        uw[it] = _dot(t_inv[it].astype(BF16), rhs.astype(BF16))

    for ci in range(nck):
        chunk_items = [(ci, b, h) for b in range(nb) for h in range(nh)]

        ws_qs, states = {}, {}
        for it in chunk_items:
            _, b, h = it
            gl = nh + h
            s = s_ref[b * nh + h]
            wq = jnp.concatenate([uw[it][:, GDN_HEAD:],
                                  q_ref[b, rs(ci), ks(h)] * gates[ci, b]["eg"][:, gl:gl + 1]], axis=0)
            ws_qs[it] = _dot(wq.astype(BF16), s.astype(BF16))
            states[it] = s

        for it in chunk_items:
            _, b, h = it
            gt = gates[ci, b]
            gl = nh + h
            v_new = (uw[it][:, :GDN_HEAD] - ws_qs[it][:c]).astype(BF16)
            o = ws_qs[it][c:] + _dot(a_qk[it], v_new)
            k_dec = (k_ref[b, rs(ci), ks(h)] * gt["e_rest"][:, gl:gl + 1]).astype(BF16)
            s_ref[b * nh + h] = states[it] * gt["e_last"][:, gl:gl + 1] + _dot_tn(k_dec, v_new)
            o = o * lax.rsqrt(jnp.mean(o * o, axis=-1, keepdims=True) + GDN_EPS) * nw_ref[...]
            o_ref[b, rs(ci), vs(h)] = (o * _silu(z_ref[b, rs(ci), vs(h)])).astype(o_ref.dtype)


def _cast_specs(arrays, n_steps):
    specs, shapes = [], []
    for a in arrays:
        span = next(d for d in (1, 2, 4, 8)
                    if (a.shape[0] * d) % n_steps == 0 and (a.shape[0] * d // n_steps) % (2 * SUBLANES) == 0)
        specs.append(pl.BlockSpec((a.shape[0] * span // n_steps, a.shape[1]), lambda j, span=span: (j // span, 0)))
        shapes.append(jax.ShapeDtypeStruct(a.shape, BF16))
    return specs, shapes


def _gdn_chunk(q, k, v, zs, gb, norm_w, batch, seq_len, to_bf16):
    rows_per_step = GDN_CHUNK * GDN_CHUNKS_PER_STEP
    n_steps = seq_len // rows_per_step
    seq = lambda a: a.reshape(batch, seq_len, a.shape[-1])
    rows = lambda width: pl.BlockSpec((batch, rows_per_step, width), lambda j: (0, j, 0))
    cast_specs, cast_shapes = _cast_specs(to_bf16, n_steps)
    out, *casted = pl.pallas_call(
        functools.partial(_gdn_chunk_kernel, nb=batch, nck=GDN_CHUNKS_PER_STEP, n_cast=len(to_bf16)),
        grid=(n_steps,),
        in_specs=[rows(GDN_KDIM), rows(GDN_KDIM), rows(GDN_VDIM), rows(GDN_VDIM), rows(LANES),
                  pl.BlockSpec((1, GDN_HEAD), lambda j: (0, 0))] + cast_specs,
        out_specs=[rows(GDN_VDIM)] + cast_specs,
        out_shape=[jax.ShapeDtypeStruct((batch, seq_len, GDN_VDIM), BF16)] + cast_shapes,
        scratch_shapes=[pltpu.VMEM((batch * GDN_V_HEADS, GDN_HEAD, GDN_HEAD), F32)],
        compiler_params=_params("arbitrary"),
        name="gdn_chunk",
    )(seq(q), seq(k), seq(v), seq(zs), seq(gb), norm_w, *to_bf16)
    return out.reshape(batch * seq_len, GDN_VDIM), casted


def _proj_res_ln(a_ref, w_ref, r_ref, g_ref, b_ref):
    return _layer_norm(ALPHA * r_ref[...] + _dot(a_ref[...], w_ref[...]), g_ref[...], b_ref[...])


def _outproj_ffn_kernel(a_ref, wo_ref, r_ref, g0_ref, b0_ref, wg_ref, wu_ref, wd_ref, g1_ref, b1_ref, o_ref):
    x1 = _proj_res_ln(a_ref, wo_ref, r_ref, g0_ref, b0_ref)
    xb = x1.astype(BF16)
    h = (_silu(_dot(xb, wg_ref[...])) * _dot(xb, wu_ref[...])).astype(BF16)
    o_ref[...] = _layer_norm(ALPHA * x1 + _dot(h, wd_ref[...]), g1_ref[...], b1_ref[...])


def _outproj_ffn(a, w_out, res, g0, b0, wg, wu, wd, g1, b1):
    n, kdim = a.shape
    tm = TM_FFN
    resident = lambda w: pl.BlockSpec(w.shape, lambda i: (0, 0), pipeline_mode=pl.Buffered(1))
    rows = lambda width: pl.BlockSpec((tm, width), lambda i: (i, 0))
    vec = pl.BlockSpec((1, D_MODEL), lambda i: (0, 0))
    return pl.pallas_call(
        _outproj_ffn_kernel,
        grid=(n // tm,),
        in_specs=[rows(kdim), resident(w_out), rows(D_MODEL), vec, vec,
                  resident(wg), resident(wu), resident(wd), vec, vec],
        out_specs=rows(D_MODEL),
        out_shape=jax.ShapeDtypeStruct((n, D_MODEL), F32),
        compiler_params=_params("parallel"),
        name="gdn_outproj_ffn",
    )(a, w_out, res, g0, b0, wg, wu, wd, g1, b1)


def _band_tables():
    assert SWA_WINDOW == SWA_BLOCK
    r = np.arange(SWA_BLOCK)[:, None]
    q = np.arange(SWA_BLOCK)[None, :]
    d = (q - r) % SWA_BLOCK
    max_exact = REL_BUCKETS // 2
    df = np.maximum(d, 1).astype(np.float32)
    large = max_exact + (np.log(df / np.float32(max_exact)) / np.float32(math.log(REL_MAX_DIST / max_exact))
                         * np.float32(REL_BUCKETS - max_exact)).astype(np.int32)
    large = np.minimum(large, REL_BUCKETS - 1)
    bucket = np.where(d < max_exact, d, large).astype(np.int32)
    valid = np.stack([q >= r, np.ones_like(q >= r)]).astype(np.int32)
    return bucket, valid


def _bias_kernel(relb_ref, bucket_ref, valid_ref, o_ref):
    bucket = bucket_ref[...]
    in_bucket = [bucket == b for b in range(REL_BUCKETS)]
    for h in range(SWA_Q_HEADS):
        acc = jnp.zeros(bucket.shape, F32)
        for b in range(REL_BUCKETS):
            acc = jnp.where(in_bucket[b], relb_ref[b, h], acc)
        for t in range(2):
            o_ref[t, h] = jnp.where(valid_ref[t] != 0, acc, NEG_BIG)


def _bias_table(rel_bias):
    bucket, valid = _band_tables()
    shape = (SWA_BLOCK, SWA_BLOCK)
    return pl.pallas_call(
        _bias_kernel,
        grid=(1,),
        in_specs=[pl.BlockSpec(memory_space=pltpu.SMEM), pl.BlockSpec(shape, lambda i: (0, 0)),
                  pl.BlockSpec((2,) + shape, lambda i: (0, 0, 0))],
        out_specs=pl.BlockSpec((2, SWA_Q_HEADS) + shape, lambda i: (0, 0, 0, 0)),
        out_shape=jax.ShapeDtypeStruct((2, SWA_Q_HEADS) + shape, F32),
        compiler_params=_params("arbitrary"),
        name="swa_bias_table",
    )(rel_bias, jnp.asarray(bucket), jnp.asarray(valid))


def _band_attention(q_pair, kv, bias_of, sink_ref, put_pair):
    dh = SWA_HEAD_DIM
    blk = SWA_BLOCK
    nk = 2 * blk
    from_prev = lax.broadcasted_iota(jnp.int32, (blk, blk), 1) < lax.broadcasted_iota(jnp.int32, (blk, blk), 0)
    k_all = kv[:, :SWA_KVDIM]
    lane = lax.broadcasted_iota(jnp.int32, k_all.shape, 1)
    zero = jnp.zeros_like(k_all)
    for kh in range(SWA_KV_HEADS):
        k_own = jnp.where((lane >= kh * dh) & (lane < (kh + 1) * dh), k_all, zero)
        k_other = pltpu.roll(k_own.astype(F32), dh, axis=1).astype(BF16)
        k_even, k_odd = (k_own, k_other) if kh == 0 else (k_other, k_own)
        k2 = jnp.concatenate([k_even, k_odd], axis=0)
        v_t = kv[:, SWA_KVDIM + kh * dh:SWA_KVDIM + (kh + 1) * dh].T
        pairs = [kh * (SWA_GROUP // 2) + p for p in range(SWA_GROUP // 2)]
        heads = [2 * p + r for p in pairs for r in range(2)]
        st = [_dot_nt(k2, q_pair(p)) for p in pairs]
        scores = [jnp.where(from_prev, st[i // 2][(i % 2) * nk:(i % 2) * nk + blk],
                            st[i // 2][(i % 2) * nk + blk:(i % 2 + 1) * nk]) + bias_of(h)
                  for i, h in enumerate(heads)]
        sinks = [sink_ref[0:1, h:h + 1] for h in heads]
        maxes = [jnp.maximum(jnp.max(s, axis=0, keepdims=True), sk) for s, sk in zip(scores, sinks)]
        expo = [jnp.exp(s - m) for s, m in zip(scores, maxes)]
        sums = [jnp.sum(p, axis=0, keepdims=True) for p in expo]
        inv_den = [1.0 / (t + jnp.exp(sk - m)) for t, sk, m in zip(sums, sinks, maxes)]
        expo = [jnp.concatenate([jnp.where(from_prev, p, 0.0), jnp.where(from_prev, 0.0, p)], axis=0) for p in expo]
        for i, p in enumerate(pairs):
            o_t = _dot(v_t, jnp.concatenate(expo[2 * i:2 * i + 2], axis=1).astype(BF16))
            z = jnp.concatenate([o_t[:, :SWA_BLOCK] * inv_den[2 * i], o_t[:, SWA_BLOCK:] * inv_den[2 * i + 1]],
                                axis=0)
            put_pair(p, z.T)


def _swa_layer_kernel(x_ref, win_ref, bin_ref, bias_ref, sink_ref, wout_ref, g_ref, b_ref, o_ref,
                      q_ref, kv_ref, att_ref, *, nblk):
    j = pl.program_id(1)
    blk = SWA_BLOCK
    tm = nblk * blk
    dh2 = 2 * SWA_HEAD_DIM

    p = _dot(x_ref[...].astype(BF16), win_ref[...]) + bin_ref[...]
    q_ref[...] = (p[:, :SWA_QDIM] * (SWA_HEAD_DIM ** -0.5)).astype(BF16)

    @pl.when(j == 0)
    def _():
        kv_ref[0:blk, :] = jnp.zeros((blk, 2 * SWA_KVDIM), BF16)

    @pl.when(j != 0)
    def _():
        kv_ref[0:blk, :] = kv_ref[tm:tm + blk, :]

    kv_ref[blk:, :] = p[:, SWA_QDIM:].astype(BF16)

    for i in range(nblk):
        rows = slice(i * blk, (i + 1) * blk)
        table = jnp.where(j == 0, 0, 1) if i == 0 else 1

        def put_pair(pair, o, rows=rows):
            att_ref[rows, pair * dh2:(pair + 1) * dh2] = o.astype(att_ref.dtype)

        _band_attention(lambda pair, rows=rows: q_ref[rows, pair * dh2:(pair + 1) * dh2],
                        kv_ref[i * blk:(i + 2) * blk, :],
                        lambda h, table=table: bias_ref[table, h], sink_ref, put_pair)

    o_ref[...] = _proj_res_ln(att_ref, wout_ref, x_ref, g_ref, b_ref)


def _swa_layer(x, w_in, b_in, bias, sinks, w_out, g, b, batch, seq_len):
    n = x.shape[0]
    nblk = SWA_BLOCKS_PER_STEP
    tm = nblk * SWA_BLOCK
    steps = seq_len // tm
    resident = lambda a: pl.BlockSpec(a.shape, lambda s, j: (0,) * a.ndim, pipeline_mode=pl.Buffered(1))
    rows = pl.BlockSpec((tm, D_MODEL), lambda s, j: (s * steps + j, 0))
    return pl.pallas_call(
        functools.partial(_swa_layer_kernel, nblk=nblk),
        grid=(batch, steps),
        in_specs=[rows, resident(w_in), resident(b_in), resident(bias), resident(sinks), resident(w_out),
                  resident(g), resident(b)],
        out_specs=rows,
        out_shape=jax.ShapeDtypeStruct((n, D_MODEL), F32),
        scratch_shapes=[pltpu.VMEM((tm, SWA_QDIM), BF16), pltpu.VMEM((tm + SWA_BLOCK, 2 * SWA_KVDIM), BF16),
                        pltpu.VMEM((tm, SWA_QDIM), BF16)],
        compiler_params=_params("parallel", "arbitrary"),
        name="swa_layer",
    )(x, w_in, b_in, bias, sinks, w_out, g, b)


def _router_kernel(x_ref, wr_ref, route_ref, route_t_ref, cnt_ref, run_ref, *, tm, n_parts):
    @pl.when(pl.program_id(0) == 0)
    def _():
        run_ref[...] = jnp.zeros(run_ref.shape, F32)

    pm = tm // n_parts
    parts = [slice(p * pm, (p + 1) * pm) for p in range(n_parts)]
    lane = lax.broadcasted_iota(jnp.int32, (pm, LANES), 1)
    lane_f = lane.astype(F32)

    wh, wl = _split(wr_ref[...])
    w_cat = jnp.concatenate([wh, wl], axis=1)
    logits = []
    for ps in parts:
        xh, xl = _split(x_ref[ps, :])
        hi = _dot(xh, w_cat)
        logits.append(hi[:, :LANES] + (hi[:, LANES:] + _dot(xl, wh)))

    picks = []
    for lgt in logits:
        lg = jnp.where(lane < N_EXPERTS, lgt, NEG_BIG)
        m1 = jnp.max(lg, axis=-1, keepdims=True)
        i1 = jnp.min(jnp.where(lg == m1, lane_f, float(LANES)), axis=-1, keepdims=True)
        oh1 = lane_f == i1
        lg2 = jnp.where(oh1, NEG_BIG, lg)
        m2 = jnp.max(lg2, axis=-1, keepdims=True)
        i2 = jnp.min(jnp.where(lg2 == m2, lane_f, float(LANES)), axis=-1, keepdims=True)
        oh2 = lane_f == i2
        e = jnp.exp(m2 - m1)
        w0 = 1.0 / (1.0 + e)
        cnt = jnp.where(oh1, 1.0, 0.0) + jnp.where(oh2, 1.0, 0.0)
        picks.append((i1, i2, oh1, oh2, w0, e * w0, cnt))

    r = lax.broadcasted_iota(jnp.int32, (pm, pm), 0)
    c = lax.broadcasted_iota(jnp.int32, (pm, pm), 1)
    before = jnp.where(r > c, 1.0, 0.0).astype(BF16)
    within = [_dot(before, pk[-1].astype(BF16)) for pk in picks]

    run = run_ref[...]
    for ps, (i1, i2, oh1, oh2, w0, w1, cnt), earlier in zip(parts, picks, within):
        excl = earlier + run
        rank0 = jnp.sum(jnp.where(oh1, excl, 0.0), axis=-1, keepdims=True)
        rank1 = jnp.sum(jnp.where(oh2, excl, 0.0), axis=-1, keepdims=True)
        run = run + jnp.sum(cnt, axis=0, keepdims=True)
        out = jnp.zeros((pm, LANES), F32)
        for idx, val in enumerate((i1, i2, rank0, rank1, w0, w1)):
            out = jnp.where(lane == idx, val, out)
        route_ref[ps, :] = out
        route_t_ref[:, ps] = out.T[:SUBLANES, :]
    run_ref[...] = run
    cnt_ref[...] = run


def _router(x, w_router_padded):
    n = x.shape[0]
    tm = TM_ROUTER
    return pl.pallas_call(
        functools.partial(_router_kernel, tm=tm, n_parts=ROUTER_PARTS),
        grid=(n // tm,),
        in_specs=[pl.BlockSpec((tm, D_MODEL), lambda i: (i, 0)), pl.BlockSpec((D_MODEL, LANES), lambda i: (0, 0))],
        out_specs=[pl.BlockSpec((tm, LANES), lambda i: (i, 0)), pl.BlockSpec((SUBLANES, tm), lambda i: (0, i)),
                   pl.BlockSpec((1, LANES), lambda i: (0, 0))],
        out_shape=[jax.ShapeDtypeStruct((n, LANES), F32), jax.ShapeDtypeStruct((SUBLANES, n), F32),
                   jax.ShapeDtypeStruct((1, LANES), F32)],
        scratch_shapes=[pltpu.VMEM((1, LANES), F32)],
        compiler_params=_params("arbitrary"),
        name="moe_router",
    )(x, w_router_padded)


def _dispatch_kernel(pos0_ref, pos1_ref, meta_ref, x_ref, wsrc_ref, xs_hbm, wdst_ref, zero_ref, sem, pad_sem, *, td):
    i = pl.program_id(0)
    wdst_ref[...] = wsrc_ref[...].astype(BF16)

    def row_copy(t, dst):
        return pltpu.make_async_copy(x_ref.at[pl.ds(t, 1)], xs_hbm.at[pl.ds(dst, 1)], sem)

    def start(t, carry):
        row_copy(t, pos0_ref[0, 0, t]).start()
        row_copy(t, pos1_ref[0, 0, t]).start(priority=1)
        return carry

    lax.fori_loop(0, td, start, 0, unroll=DMA_UNROLL)

    @pl.when(i == pl.num_programs(0) - 1)
    def _():
        zero_ref[...] = jnp.zeros(zero_ref.shape, F32)

        zr = zero_ref.shape[0]

        def row_zero(dst):
            return pltpu.make_async_copy(zero_ref.at[pl.ds(0, 1)], xs_hbm.at[pl.ds(dst, 1)], pad_sem)

        def block_zero(blk):
            return pltpu.make_async_copy(zero_ref, xs_hbm.at[pl.ds(pl.multiple_of(blk * zr, zr), zr)], pad_sem)

        def zero_range(copy, lo, hi):
            lax.fori_loop(lo, hi, lambda r, carry: (copy(r).start(), carry)[1], 0)
            lax.fori_loop(lo, hi, lambda r, carry: (copy(0).wait(), carry)[1], 0)

        for e in range(N_EXPERTS):
            off, cnt, cnt_up, padded = (meta_ref[r, e] for r in range(4))
            zero_range(lambda r, off=off: row_zero(off + r), cnt, cnt_up)
            zero_range(block_zero, (off + cnt_up) // zr, (off + padded) // zr)

        used = meta_ref[0, N_EXPERTS - 1] + meta_ref[3, N_EXPERTS - 1]
        zero_range(block_zero, used // zr, xs_hbm.shape[0] // zr)

    for _ in range(2):
        pltpu.make_async_copy(x_ref, xs_hbm.at[pl.ds(0, td)], sem).wait()


def _dispatch(x, pos, meta, n_rows, to_bf16):
    n = x.shape[0]
    td = TD_DISPATCH
    pos_spec = pl.BlockSpec((1, 1, td), lambda i: (i, 0, 0), memory_space=pltpu.SMEM)
    (cast_spec,), (cast_shape,) = _cast_specs([to_bf16], n // td)
    return pl.pallas_call(
        functools.partial(_dispatch_kernel, td=td),
        grid=(n // td,),
        in_specs=[pos_spec, pos_spec, pl.BlockSpec(memory_space=pltpu.SMEM),
                  pl.BlockSpec((td, D_MODEL), lambda i: (i, 0)), cast_spec],
        out_specs=[pl.BlockSpec(memory_space=pl.ANY), cast_spec],
        out_shape=[jax.ShapeDtypeStruct((n_rows, D_MODEL), F32), cast_shape],
        scratch_shapes=[pltpu.VMEM((ZERO_ROWS, D_MODEL), F32), pltpu.SemaphoreType.DMA, pltpu.SemaphoreType.DMA],
        compiler_params=_params("arbitrary"),
        name="moe_dispatch",
    )(pos[0].reshape(n // td, 1, td), pos[1].reshape(n // td, 1, td), meta, x, to_bf16)


def _moe_kernel(we_ref, wb_ref, nv_ref, cnt_ref, *refs, n_sub, tm):
    xs_refs = refs[:n_sub]
    wg_ref, wu_ref, wd_ref, ys_hbm, acc_ref, xb_ref, sem = refs[n_sub:]
    j = pl.program_id(0)
    f = pl.program_id(1)
    last_f = pl.num_programs(1) - 1
    n_windows, n_tiles = cnt_ref[0], cnt_ref[1]

    def tile_write(slot, block):
        return pltpu.make_async_copy(acc_ref.at[slot], ys_hbm.at[pl.ds(pl.multiple_of(block * tm, tm), tm)],
                                     sem.at[slot])

    @pl.when(j < n_windows)
    def _():
        prev_tiles = jnp.where(j > 0, nv_ref[jnp.maximum(j - 1, 0)], 0)

        for s in range(n_sub):
            @pl.when(s < nv_ref[j])
            def _(s=s):
                @pl.when(f == 0)
                def _():
                    xb_ref[s] = xs_refs[s][...].astype(BF16)

                xb = xb_ref[s]
                h = (_silu(_dot(xb, wg_ref[0])) * _dot(xb, wu_ref[0])).astype(BF16)
                part = _dot(h, wd_ref[0])

                @pl.when(f == 0)
                def _():
                    @pl.when(s < prev_tiles)
                    def _():
                        tile_write(s, 0).wait()

                    acc_ref[s] = part

                @pl.when(f != 0)
                def _():
                    acc_ref[s] += part

                @pl.when(f == last_f)
                def _():
                    tile_write(s, wb_ref[j] + s).start()

            @pl.when((f == 0) & (s >= nv_ref[j]) & (s < prev_tiles))
            def _(s=s):
                tile_write(s, 0).wait()

        @pl.when((f == last_f) & (j == n_windows - 1))
        def _():
            for s in range(n_sub):
                @pl.when(s < nv_ref[j])
                def _(s=s):
                    tile_write(s, 0).wait()

            acc_ref[0] = jnp.zeros(acc_ref.shape[1:], F32)
            total_tiles = ys_hbm.shape[0] // tm
            lax.fori_loop(n_tiles, total_tiles, lambda t, c: (tile_write(0, t).start(), c)[1], 0)
            lax.fori_loop(n_tiles, total_tiles, lambda t, c: (tile_write(0, 0).wait(), c)[1], 0)


def _moe_experts(xs, wg, wu, wd, win_expert, win_block, win_tiles, counts):
    n_rows = xs.shape[0]
    tm, tf, n_sub = TM_MOE, TF_MOE, MOE_TILES_PER_WINDOW
    max_windows = win_expert.shape[0]

    def xs_spec(s):
        return pl.BlockSpec((tm, D_MODEL), lambda j, f, we, wb, nv, cnt: (wb[j] + jnp.minimum(s, nv[j] - 1), 0))

    grid_spec = pltpu.PrefetchScalarGridSpec(
        num_scalar_prefetch=4,
        grid=(max_windows, EXPERT_DIM // tf),
        in_specs=[xs_spec(s) for s in range(n_sub)] + [
            pl.BlockSpec((1, D_MODEL, tf), lambda j, f, we, wb, nv, cnt: (we[j], 0, f)),
            pl.BlockSpec((1, D_MODEL, tf), lambda j, f, we, wb, nv, cnt: (we[j], 0, f)),
            pl.BlockSpec((1, tf, D_MODEL), lambda j, f, we, wb, nv, cnt: (we[j], f, 0))],
        out_specs=pl.BlockSpec(memory_space=pl.ANY),
        scratch_shapes=[pltpu.VMEM((n_sub, tm, D_MODEL), F32), pltpu.VMEM((n_sub, tm, D_MODEL), BF16),
                        pltpu.SemaphoreType.DMA((n_sub,))],
    )
    return pl.pallas_call(
        functools.partial(_moe_kernel, n_sub=n_sub, tm=tm),
        grid_spec=grid_spec,
        out_shape=jax.ShapeDtypeStruct((n_rows, D_MODEL), F32),
        compiler_params=_params("arbitrary", "arbitrary"),
        name="moe_experts",
    )(win_expert, win_block, win_tiles, counts, *([xs] * n_sub), wg, wu, wd)


def _combine_kernel(pos0_ref, pos1_ref, pos0n_ref, pos1n_ref, x_ref, route_ref, ys_hbm, g_ref, b_ref, o_ref,
                    y_ref, sem, *, tc):
    i = pl.program_id(0)
    slot = i % 2

    def issue(p_refs, s):
        def start(t, carry):
            for kk in range(2):
                pltpu.make_async_copy(ys_hbm.at[pl.ds(p_refs[kk][0, 0, t], 1)],
                                      y_ref.at[s, kk, pl.ds(t, 1)], sem.at[s]).start(priority=kk)
            return carry

        lax.fori_loop(0, tc, start, 0, unroll=DMA_UNROLL)

    @pl.when(i == 0)
    def _():
        issue((pos0_ref, pos1_ref), 0)

    @pl.when(i + 1 < pl.num_programs(0))
    def _():
        issue((pos0n_ref, pos1n_ref), 1 - slot)

    for kk in range(2):
        pltpu.make_async_copy(ys_hbm.at[pl.ds(0, tc)], y_ref.at[slot, kk], sem.at[slot]).wait()

    route = route_ref[...]
    f = route[:, 4:5] * y_ref[slot, 0] + route[:, 5:6] * y_ref[slot, 1]
    o_ref[...] = _layer_norm(ALPHA * x_ref[...] + f, g_ref[...], b_ref[...])


def _combine_ln(x, route, pos, ys, g, b):
    n = x.shape[0]
    tc = TC_COMBINE
    nt = n // tc
    pos0, pos1 = pos[0].reshape(nt, 1, tc), pos[1].reshape(nt, 1, tc)
    cur = pl.BlockSpec((1, 1, tc), lambda i: (i, 0, 0), memory_space=pltpu.SMEM)
    nxt = pl.BlockSpec((1, 1, tc), lambda i: (jnp.minimum(i + 1, nt - 1), 0, 0), memory_space=pltpu.SMEM)
    return pl.pallas_call(
        functools.partial(_combine_kernel, tc=tc),
        grid=(nt,),
        in_specs=[cur, cur, nxt, nxt,
                  pl.BlockSpec((tc, D_MODEL), lambda i: (i, 0)),
                  pl.BlockSpec((tc, LANES), lambda i: (i, 0)),
                  pl.BlockSpec(memory_space=pl.ANY),
                  pl.BlockSpec((1, D_MODEL), lambda i: (0, 0)), pl.BlockSpec((1, D_MODEL), lambda i: (0, 0))],
        out_specs=pl.BlockSpec((tc, D_MODEL), lambda i: (i, 0)),
        out_shape=jax.ShapeDtypeStruct((n, D_MODEL), F32),
        scratch_shapes=[pltpu.VMEM((2, 2, tc, D_MODEL), F32), pltpu.SemaphoreType.DMA((2,))],
        compiler_params=_params("arbitrary"),
        name="moe_combine_ln",
    )(pos0, pos1, pos0, pos1, x, route, ys, g, b)


def _moe_layer(x, route, route_t, counts_f, wg, wu, wd_f32, g, b):
    n = x.shape[0]
    tm = TM_MOE

    counts = counts_f[0, :N_EXPERTS].astype(jnp.int32)
    padded = ((counts + tm - 1) // tm) * tm
    offs = jnp.cumsum(padded) - padded
    eids = jnp.arange(N_EXPERTS, dtype=jnp.int32)
    e01 = route_t[0:2].astype(jnp.int32)
    rank01 = route_t[2:4].astype(jnp.int32)
    base = jnp.sum(jnp.where(e01[:, None, :] == eids[None, :, None], offs[None, :, None], 0), axis=1)
    pos = base + rank01
    counts_up = jnp.minimum(padded, ((counts + ZERO_ROWS - 1) // ZERO_ROWS) * ZERO_ROWS)
    meta = jnp.stack([offs, counts, counts_up, padded]).astype(jnp.int32)

    n_rows = 2 * n + N_EXPERTS * tm
    nsub = MOE_TILES_PER_WINDOW
    tiles = padded // tm
    wins = (tiles + nsub - 1) // nsub
    win_end = jnp.cumsum(wins)
    n_windows = win_end[-1]
    max_windows = (n_rows // tm + nsub - 1) // nsub + N_EXPERTS
    wj = jnp.minimum(jnp.arange(max_windows, dtype=jnp.int32), n_windows - 1)
    win_expert = jnp.minimum(jnp.sum((wj[:, None] >= win_end[None, :]).astype(jnp.int32), axis=-1), N_EXPERTS - 1)
    pick = lambda v: jnp.sum(jnp.where(win_expert[:, None] == eids[None, :], v[None, :], 0), axis=-1)
    local = wj - pick(win_end - wins)
    win_block = pick(offs // tm) + nsub * local
    win_tiles = jnp.minimum(nsub, pick(tiles) - nsub * local)
    counts_nt = jnp.stack([n_windows, jnp.sum(tiles)])

    xs, wd = _dispatch(x, pos, meta, n_rows, wd_f32.reshape(-1, D_MODEL))
    wd = wd.reshape(wd_f32.shape)
    i32 = lambda a: a.astype(jnp.int32)
    ys = _moe_experts(xs, wg, wu, wd, i32(win_expert), i32(win_block), i32(win_tiles), i32(counts_nt))
    return _combine_ln(x, route, pos, ys, g, b)


def kernel(x, a_w_in, a_conv_w, a_a_log, a_dt_bias, a_norm_w, a_w_out, b_w_in, b_b_in, b_sinks, b_w_out, rel_bias,
           ffn_w_gate, ffn_w_up, ffn_w_down, moe_router, moe_w_gate, moe_w_up, moe_w_down, ln_g, ln_b):
    batch, seq_len, _ = x.shape
    n = batch * seq_len
    x0 = x.reshape(n, D_MODEL)
    ln_g = ln_g.reshape(DEPTH, 2, 1, D_MODEL)
    ln_b = ln_b.reshape(DEPTH, 2, 1, D_MODEL)

    pad_gate = lambda p: jnp.pad(p.reshape(1, GDN_V_HEADS), ((0, 0), (GDN_V_HEADS, LANES - 2 * GDN_V_HEADS)))
    gate_params = jnp.concatenate([pad_gate(a_a_log[0]), pad_gate(a_dt_bias[0])], axis=0)
    q, k, v, zs, gb = _gdn_inproj(x0, jnp.swapaxes(a_w_in, 1, 2), a_conv_w[0], gate_params, seq_len)
    o, casted = _gdn_chunk(q, k, v, zs, gb, a_norm_w[0].reshape(1, GDN_HEAD), batch, seq_len,
                           [moe_w_gate[0].reshape(-1, EXPERT_DIM), moe_w_up[0].reshape(-1, EXPERT_DIM),
                            ffn_w_gate[0], ffn_w_up[0], ffn_w_down[0],
                            a_w_out[0], b_w_in[0], b_w_out[0]])
    moe_wg, moe_wu, ffn_wg, ffn_wu, ffn_wd, w_out_a, w_in_b, w_out_b = casted
    x2 = _outproj_ffn(o, w_out_a, x0, ln_g[0, 0], ln_b[0, 0],
                      ffn_wg, ffn_wu, ffn_wd, ln_g[0, 1], ln_b[0, 1])

    bias = _bias_table(rel_bias)
    sinks = jnp.pad(b_sinks[0].reshape(1, SWA_Q_HEADS), ((0, 0), (0, LANES - SWA_Q_HEADS)))
    x3 = _swa_layer(x2, w_in_b, b_b_in[0].reshape(1, -1), bias, sinks, w_out_b, ln_g[1, 0], ln_b[1, 0],
                    batch, seq_len)
    w_router = jnp.pad(moe_router[0], ((0, 0), (0, LANES - N_EXPERTS)))
    route, route_t, counts = _router(x3, w_router)
    expert_shape = (N_EXPERTS, D_MODEL, EXPERT_DIM)
    x4 = _moe_layer(x3, route, route_t, counts, moe_wg.reshape(expert_shape), moe_wu.reshape(expert_shape),
                    moe_w_down[0], ln_g[1, 1], ln_b[1, 1])
    return x4.reshape(batch, seq_len, D_MODEL)
```

```python
import functools
import math

import numpy as np
import jax
import jax.numpy as jnp
from jax import lax
from jax.experimental import pallas as pl
from jax.experimental.pallas import tpu as pltpu

F32 = jnp.float32
BF16 = jnp.bfloat16

D_MODEL = 1024
DEPTH = 2
ALPHA = (2.0 * DEPTH) ** 0.25
LN_EPS = 1e-5

GDN_K_HEADS = 4
GDN_V_HEADS = 8
GDN_HEAD = 128
GDN_KDIM = GDN_K_HEADS * GDN_HEAD
GDN_VDIM = GDN_V_HEADS * GDN_HEAD
GDN_CONV = 4
GDN_CHUNK = 64
GDN_QKV = 2 * GDN_KDIM + GDN_VDIM
GDN_EPS = 1e-6

SWA_Q_HEADS = 16
SWA_KV_HEADS = 2
SWA_GROUP = SWA_Q_HEADS // SWA_KV_HEADS
SWA_HEAD_DIM = 64
SWA_WINDOW = 128
SWA_BLOCK = 128
SWA_QDIM = SWA_Q_HEADS * SWA_HEAD_DIM
SWA_KVDIM = SWA_KV_HEADS * SWA_HEAD_DIM
REL_BUCKETS = 32
REL_MAX_DIST = 128

FFN_DIM = 2816
N_EXPERTS = 8
EXPERT_DIM = 3584

LANES = 128
SUBLANES = 8
NEG_BIG = -1e30
VMEM_LIMIT = 56 * 1024 * 1024

TM_GDN_IN = 512
TM_FFN = 1024
FFN_PARTS = 4
SWA_BLOCKS_PER_STEP = 8
TM_ROUTER = 2048
ROUTER_PARTS = 8
TM_MOE = 512
TF_MOE = 1792
MOE_TILES_PER_WINDOW = 3
TD_DISPATCH = 1024
TC_COMBINE = 512
ZERO_ROWS = 64
DMA_UNROLL = 8


def _params(*sem):
    return pltpu.CompilerParams(dimension_semantics=sem, vmem_limit_bytes=VMEM_LIMIT)


def _dot(a, b):
    return jnp.dot(a, b, preferred_element_type=F32)


def _dot_nt(a, b):
    return lax.dot_general(a, b, (((1,), (1,)), ((), ())), preferred_element_type=F32)


def _dot_tn(a, b):
    return lax.dot_general(a, b, (((0,), (0,)), ((), ())), preferred_element_type=F32)


def _split(x):
    hi = x.astype(BF16)
    lo = (x - hi.astype(F32)).astype(BF16)
    return hi, lo


def _dot3(a, b):
    ah, al = _split(a)
    bh, bl = _split(b)
    return _dot(ah, bh) + (_dot(ah, bl) + _dot(al, bh))


def _silu(x):
    return x * jax.nn.sigmoid(x)


def _layer_norm(y, g, b):
    mu = jnp.mean(y, axis=-1, keepdims=True)
    yc = y - mu
    var = jnp.mean(yc * yc, axis=-1, keepdims=True)
    return yc * lax.rsqrt(var + LN_EPS) * g + b


def _gdn_inproj_kernel(x_ref, wf_ref, convw_ref, gp_ref,
                       q_ref, k_ref, v_ref, z_ref, gb_ref, ext_ref, w_ref, *, tm, tiles_per_seq):
    i = pl.program_id(0)
    xb = x_ref[...].astype(BF16)

    @pl.when(i == 0)
    def _():
        n_in = wf_ref.shape[1]
        n_full = n_in // LANES
        for c in range(n_full):
            cs = slice(c * LANES, (c + 1) * LANES)
            w_ref[:, cs] = wf_ref[0, cs, :].T.astype(BF16)
        tail = jnp.concatenate([wf_ref[0, n_full * LANES:, :],
                                jnp.zeros(((n_full + 1) * LANES - n_in, wf_ref.shape[2]), F32)], axis=0)
        w_ref[:, n_full * LANES:] = tail.T.astype(BF16)

    @pl.when(i % tiles_per_seq == 0)
    def _():
        ext_ref[0:SUBLANES, :] = jnp.zeros((SUBLANES, GDN_QKV), F32)

    @pl.when(i % tiles_per_seq != 0)
    def _():
        ext_ref[0:SUBLANES, :] = ext_ref[tm:tm + SUBLANES, :]

    ext_ref[SUBLANES:, :] = _dot(xb, w_ref[:, :GDN_QKV])
    z_ref[...] = _dot(xb, w_ref[:, GDN_QKV:GDN_QKV + GDN_VDIM])

    n_chunks = GDN_QKV // LANES
    for c in range(n_chunks):
        cs = slice(c * LANES, (c + 1) * LANES)
        acc = convw_ref[GDN_CONV - 1:GDN_CONV, cs] * ext_ref[SUBLANES:SUBLANES + tm, cs]
        for j in range(GDN_CONV - 1):
            off = SUBLANES - (GDN_CONV - 1) + j
            acc = acc + convw_ref[j:j + 1, cs] * ext_ref[off:off + tm, cs]
        y = _silu(acc)
        if c < 2 * GDN_K_HEADS:
            y = y * lax.rsqrt(jnp.sum(y * y, axis=-1, keepdims=True) + GDN_EPS)
            if c < GDN_K_HEADS:
                q_ref[:, cs] = y * (GDN_HEAD ** -0.5)
            else:
                k_ref[:, (c - GDN_K_HEADS) * LANES:(c - GDN_K_HEADS + 1) * LANES] = y
        else:
            cv = c - 2 * GDN_K_HEADS
            v_ref[:, cv * LANES:(cv + 1) * LANES] = y

    ba = _dot(xb, w_ref[:, GDN_QKV + GDN_VDIM:])
    lane = lax.broadcasted_iota(jnp.int32, ba.shape, 1)
    sp = ba + gp_ref[1:2, :]
    softplus = jnp.maximum(sp, 0.0) + jnp.log(1.0 + jnp.exp(-jnp.abs(sp)))
    g = -jnp.exp(gp_ref[0:1, :]) * softplus
    gb_ref[...] = jnp.where(lane < GDN_V_HEADS, jax.nn.sigmoid(ba), g)


def _gdn_inproj(x2d, w_in, conv_w, gate_params, seq_len):
    n = x2d.shape[0]
    tm = TM_GDN_IN
    kern = functools.partial(_gdn_inproj_kernel, tm=tm, tiles_per_seq=seq_len // tm)
    full = lambda shape: pl.BlockSpec(shape, lambda i: (0,) * len(shape))
    rows = lambda width: pl.BlockSpec((tm, width), lambda i: (i, 0))
    return pl.pallas_call(
        kern,
        grid=(n // tm,),
        in_specs=[rows(D_MODEL), pl.BlockSpec(w_in.shape, lambda i: (0, 0, 0), pipeline_mode=pl.Buffered(1)),
                  full(conv_w.shape), full(gate_params.shape)],
        out_specs=[rows(GDN_KDIM), rows(GDN_KDIM), rows(GDN_VDIM), rows(GDN_VDIM), rows(LANES)],
        out_shape=[jax.ShapeDtypeStruct((n, GDN_KDIM), F32), jax.ShapeDtypeStruct((n, GDN_KDIM), F32),
                   jax.ShapeDtypeStruct((n, GDN_VDIM), F32), jax.ShapeDtypeStruct((n, GDN_VDIM), F32),
                   jax.ShapeDtypeStruct((n, LANES), F32)],
        scratch_shapes=[pltpu.VMEM((tm + SUBLANES, GDN_QKV), F32),
                        pltpu.VMEM((D_MODEL, GDN_QKV + GDN_VDIM + LANES), BF16)],
        compiler_params=_params("arbitrary"),
        name="gdn_inproj",
    )(x2d, w_in, conv_w, gate_params)


GDN_CHUNKS_PER_STEP = 2
NEUMANN_BLOCK = 4


def _bf16_all(xs):
    return [x.astype(BF16) for x in xs]


def _dot_all(a_list, b_list):
    return [_dot(a, b) for a, b in zip(a_list, b_list)]


def _unit_lower_inverse_all(a_list, row, col):
    shift = int(math.log2(NEUMANN_BLOCK))
    eye = jnp.where(row == col, 1.0, 0.0).astype(F32)
    on_diag_block = (row >> shift) == (col >> shift)
    d = [jnp.where(on_diag_block, a, 0.0) for a in a_list]
    d_b = _bf16_all(d)
    x = [eye - di for di in d]
    d2_b = _bf16_all(_dot_all(d_b, d_b))
    x = [xi + t for xi, t in zip(x, _dot_all(_bf16_all(x), d2_b))]
    size = NEUMANN_BLOCK
    while size < a_list[0].shape[0]:
        shift = int(math.log2(size))
        rbl, cbl = row >> shift, col >> shift
        below = ((rbl & 1) == 1) & (cbl == rbl - 1)
        l_b = _bf16_all([jnp.where(below, a, 0.0) for a in a_list])
        x_b = _bf16_all(x)
        xl_b = _bf16_all(_dot_all(x_b, l_b))
        x = [xi - t for xi, t in zip(x, _dot_all(xl_b, x_b))]
        size *= 2
    return x


def _gdn_chunk_kernel(q_ref, k_ref, v_ref, z_ref, gb_ref, nw_ref, *rest, nb, nck, n_cast):
    cast_src, (o_ref, *cast_dst), s_ref = rest[:n_cast], rest[n_cast:2 * n_cast + 1], rest[-1]
    for src, dst in zip(cast_src, cast_dst):
        dst[...] = src[...].astype(BF16)
    c = GDN_CHUNK
    nh = GDN_V_HEADS

    @pl.when(pl.program_id(0) == 0)
    def _():
        s_ref[...] = jnp.zeros(s_ref.shape, F32)

    row = lax.broadcasted_iota(jnp.int32, (c, c), 0)
    col = lax.broadcasted_iota(jnp.int32, (c, c), 1)
    causal = row >= col
    strict = row > col
    tril = jnp.where(causal, 1.0, 0.0).astype(BF16)
    rs = lambda ci: slice(ci * c, (ci + 1) * c)
    ks = lambda h: slice((h // (nh // GDN_K_HEADS)) * GDN_HEAD, (h // (nh // GDN_K_HEADS) + 1) * GDN_HEAD)
    vs = lambda h: slice(h * GDN_HEAD, (h + 1) * GDN_HEAD)

    gates = {}
    for ci in range(nck):
        for b in range(nb):
            gb = gb_ref[b, rs(ci), :]
            g_hi, g_lo = _split(gb)
            gc = _dot(tril, g_hi) + _dot(tril, g_lo)
            g_last = gc[c - 1:c, :]
            gates[ci, b] = dict(gb=gb, gc=gc, gc_t=gc.T, eg=jnp.exp(gc), e_last=jnp.exp(g_last),
                                e_rest=jnp.exp(g_last - gc))

    items = [(ci, b, h) for ci in range(nck) for b in range(nb) for h in range(nh)]

    a_kk, a_qk, k_beta = [], {}, {}
    for it in items:
        ci, b, h = it
        gt = gates[ci, b]
        k = k_ref[b, rs(ci), ks(h)]
        kb = k * gt["gb"][:, h:h + 1]
        kq = jnp.concatenate([kb, q_ref[b, rs(ci), ks(h)]], axis=0).astype(BF16)
        gram = _dot_nt(kq, k.astype(BF16))
        gl = nh + h
        decay = jnp.exp(jnp.where(causal, gt["gc"][:, gl:gl + 1] - gt["gc_t"][gl:gl + 1, :], NEG_BIG))
        a_kk.append(jnp.where(strict, gram[:c] * decay, 0.0))
        a_qk[it] = (gram[c:] * decay).astype(BF16)
        k_beta[it] = kb

    t_inv = dict(zip(items, _unit_lower_inverse_all(a_kk, row, col)))

    uw = {}
    for it in items:
        ci, b, h = it
        gt = gates[ci, b]
        gl = nh + h
        rhs = jnp.concatenate([v_ref[b, rs(ci), vs(h)] * gt["gb"][:, h:h + 1],
                               k_beta[it] * gt["eg"][:, gl:gl + 1]], axis=1)
        uw[it] = _dot(t_inv[it].astype(BF16), rhs.astype(BF16))

    for ci in range(nck):
        chunk_items = [(ci, b, h) for b in range(nb) for h in range(nh)]

        ws_qs, states = {}, {}
        for it in chunk_items:
            _, b, h = it
            gl = nh + h
            s = s_ref[b * nh + h]
            wq = jnp.concatenate([uw[it][:, GDN_HEAD:],
                                  q_ref[b, rs(ci), ks(h)] * gates[ci, b]["eg"][:, gl:gl + 1]], axis=0)
            ws_qs[it] = _dot(wq.astype(BF16), s.astype(BF16))
            states[it] = s

        for it in chunk_items:
            _, b, h = it
            gt = gates[ci, b]
            gl = nh + h
            v_new = (uw[it][:, :GDN_HEAD] - ws_qs[it][:c]).astype(BF16)
            o = ws_qs[it][c:] + _dot(a_qk[it], v_new)
            k_dec = (k_ref[b, rs(ci), ks(h)] * gt["e_rest"][:, gl:gl + 1]).astype(BF16)
            s_ref[b * nh + h] = states[it] * gt["e_last"][:, gl:gl + 1] + _dot_tn(k_dec, v_new)
            o = o * lax.rsqrt(jnp.mean(o * o, axis=-1, keepdims=True) + GDN_EPS) * nw_ref[...]
            o_ref[b, rs(ci), vs(h)] = (o * _silu(z_ref[b, rs(ci), vs(h)])).astype(o_ref.dtype)


def _cast_specs(arrays, n_steps):
    specs, shapes = [], []
    for a in arrays:
        span = next(d for d in (1, 2, 4, 8)
                    if (a.shape[0] * d) % n_steps == 0 and (a.shape[0] * d // n_steps) % (2 * SUBLANES) == 0)
        specs.append(pl.BlockSpec((a.shape[0] * span // n_steps, a.shape[1]), lambda j, span=span: (j // span, 0)))
        shapes.append(jax.ShapeDtypeStruct(a.shape, BF16))
    return specs, shapes


def _gdn_chunk(q, k, v, zs, gb, norm_w, batch, seq_len, to_bf16):
    rows_per_step = GDN_CHUNK * GDN_CHUNKS_PER_STEP
    n_steps = seq_len // rows_per_step
    seq = lambda a: a.reshape(batch, seq_len, a.shape[-1])
    rows = lambda width: pl.BlockSpec((batch, rows_per_step, width), lambda j: (0, j, 0))
    cast_specs, cast_shapes = _cast_specs(to_bf16, n_steps)
    out, *casted = pl.pallas_call(
        functools.partial(_gdn_chunk_kernel, nb=batch, nck=GDN_CHUNKS_PER_STEP, n_cast=len(to_bf16)),
        grid=(n_steps,),
        in_specs=[rows(GDN_KDIM), rows(GDN_KDIM), rows(GDN_VDIM), rows(GDN_VDIM), rows(LANES),
                  pl.BlockSpec((1, GDN_HEAD), lambda j: (0, 0))] + cast_specs,
        out_specs=[rows(GDN_VDIM)] + cast_specs,
        out_shape=[jax.ShapeDtypeStruct((batch, seq_len, GDN_VDIM), BF16)] + cast_shapes,
        scratch_shapes=[pltpu.VMEM((batch * GDN_V_HEADS, GDN_HEAD, GDN_HEAD), F32)],
        compiler_params=_params("arbitrary"),
        name="gdn_chunk",
    )(seq(q), seq(k), seq(v), seq(zs), seq(gb), norm_w, *to_bf16)
    return out.reshape(batch * seq_len, GDN_VDIM), casted


def _proj_res_ln(a_ref, w_ref, r_ref, g_ref, b_ref):
    return _layer_norm(ALPHA * r_ref[...] + _dot(a_ref[...], w_ref[...]), g_ref[...], b_ref[...])


def _outproj_ffn_kernel(a_ref, wo_ref, r_ref, g0_ref, b0_ref, wg_ref, wu_ref, wd_ref, g1_ref, b1_ref, o_ref, *,
                        n_parts):
    pm = o_ref.shape[0] // n_parts
    parts = [slice(p * pm, (p + 1) * pm) for p in range(n_parts)]
    x1 = [_layer_norm(ALPHA * r_ref[ps, :] + _dot(a_ref[ps, :], wo_ref[...]), g0_ref[...], b0_ref[...])
          for ps in parts]
    xb = [x.astype(BF16) for x in x1]
    gate_up = lambda p: (_dot(xb[p], wg_ref[...]), _dot(xb[p], wu_ref[...]))
    ahead = gate_up(0)
    for p, ps in enumerate(parts):
        (gate, up), ahead = ahead, (gate_up(p + 1) if p + 1 < n_parts else None)
        h = (_silu(gate) * up).astype(BF16)
        o_ref[ps, :] = _layer_norm(ALPHA * x1[p] + _dot(h, wd_ref[...]), g1_ref[...], b1_ref[...])


def _outproj_ffn(a, w_out, res, g0, b0, wg, wu, wd, g1, b1):
    n, kdim = a.shape
    tm = TM_FFN
    resident = lambda w: pl.BlockSpec(w.shape, lambda i: (0, 0), pipeline_mode=pl.Buffered(1))
    rows = lambda width: pl.BlockSpec((tm, width), lambda i: (i, 0))
    vec = pl.BlockSpec((1, D_MODEL), lambda i: (0, 0))
    return pl.pallas_call(
        functools.partial(_outproj_ffn_kernel, n_parts=FFN_PARTS),
        grid=(n // tm,),
        in_specs=[rows(kdim), resident(w_out), rows(D_MODEL), vec, vec,
                  resident(wg), resident(wu), resident(wd), vec, vec],
        out_specs=rows(D_MODEL),
        out_shape=jax.ShapeDtypeStruct((n, D_MODEL), F32),
        compiler_params=_params("parallel"),
        name="gdn_outproj_ffn",
    )(a, w_out, res, g0, b0, wg, wu, wd, g1, b1)


def _band_tables():
    assert SWA_WINDOW == SWA_BLOCK
    r = np.arange(SWA_BLOCK)[:, None]
    q = np.arange(SWA_BLOCK)[None, :]
    d = (q - r) % SWA_BLOCK
    max_exact = REL_BUCKETS // 2
    df = np.maximum(d, 1).astype(np.float32)
    large = max_exact + (np.log(df / np.float32(max_exact)) / np.float32(math.log(REL_MAX_DIST / max_exact))
                         * np.float32(REL_BUCKETS - max_exact)).astype(np.int32)
    large = np.minimum(large, REL_BUCKETS - 1)
    bucket = np.where(d < max_exact, d, large).astype(np.int32)
    valid = np.stack([q >= r, np.ones_like(q >= r)]).astype(np.int32)
    return bucket, valid


def _bias_kernel(relb_ref, bucket_ref, valid_ref, o_ref):
    bucket = bucket_ref[...]
    in_bucket = [bucket == b for b in range(REL_BUCKETS)]
    for h in range(SWA_Q_HEADS):
        acc = jnp.zeros(bucket.shape, F32)
        for b in range(REL_BUCKETS):
            acc = jnp.where(in_bucket[b], relb_ref[b, h], acc)
        for t in range(2):
            o_ref[t, h] = jnp.where(valid_ref[t] != 0, acc, NEG_BIG)


def _bias_table(rel_bias):
    bucket, valid = _band_tables()
    shape = (SWA_BLOCK, SWA_BLOCK)
    return pl.pallas_call(
        _bias_kernel,
        grid=(1,),
        in_specs=[pl.BlockSpec(memory_space=pltpu.SMEM), pl.BlockSpec(shape, lambda i: (0, 0)),
                  pl.BlockSpec((2,) + shape, lambda i: (0, 0, 0))],
        out_specs=pl.BlockSpec((2, SWA_Q_HEADS) + shape, lambda i: (0, 0, 0, 0)),
        out_shape=jax.ShapeDtypeStruct((2, SWA_Q_HEADS) + shape, F32),
        compiler_params=_params("arbitrary"),
        name="swa_bias_table",
    )(rel_bias, jnp.asarray(bucket), jnp.asarray(valid))


def _band_attention(q_pair, kv, bias_of, sink_ref, put_pair):
    dh = SWA_HEAD_DIM
    blk = SWA_BLOCK
    nk = 2 * blk
    from_prev = lax.broadcasted_iota(jnp.int32, (blk, blk), 1) < lax.broadcasted_iota(jnp.int32, (blk, blk), 0)
    k_all = kv[:, :SWA_KVDIM]
    lane = lax.broadcasted_iota(jnp.int32, k_all.shape, 1)
    zero = jnp.zeros_like(k_all)
    for kh in range(SWA_KV_HEADS):
        k_own = jnp.where((lane >= kh * dh) & (lane < (kh + 1) * dh), k_all, zero)
        k_other = pltpu.roll(k_own.astype(F32), dh, axis=1).astype(BF16)
        k_even, k_odd = (k_own, k_other) if kh == 0 else (k_other, k_own)
        k2 = jnp.concatenate([k_even, k_odd], axis=0)
        v_t = kv[:, SWA_KVDIM + kh * dh:SWA_KVDIM + (kh + 1) * dh].T
        pairs = [kh * (SWA_GROUP // 2) + p for p in range(SWA_GROUP // 2)]
        heads = [2 * p + r for p in pairs for r in range(2)]
        st = [_dot_nt(k2, q_pair(p)) for p in pairs]
        scores = [jnp.where(from_prev, st[i // 2][(i % 2) * nk:(i % 2) * nk + blk],
                            st[i // 2][(i % 2) * nk + blk:(i % 2 + 1) * nk]) + bias_of(h)
                  for i, h in enumerate(heads)]
        sinks = [sink_ref[0:1, h:h + 1] for h in heads]
        maxes = [jnp.maximum(jnp.max(s, axis=0, keepdims=True), sk) for s, sk in zip(scores, sinks)]
        expo = [jnp.exp(s - m) for s, m in zip(scores, maxes)]
        sums = [jnp.sum(p, axis=0, keepdims=True) for p in expo]
        inv_den = [1.0 / (t + jnp.exp(sk - m)) for t, sk, m in zip(sums, sinks, maxes)]
        expo = [jnp.concatenate([jnp.where(from_prev, p, 0.0), jnp.where(from_prev, 0.0, p)], axis=0) for p in expo]
        for i, p in enumerate(pairs):
            o_t = _dot(v_t, jnp.concatenate(expo[2 * i:2 * i + 2], axis=1).astype(BF16))
            z = jnp.concatenate([o_t[:, :SWA_BLOCK] * inv_den[2 * i], o_t[:, SWA_BLOCK:] * inv_den[2 * i + 1]],
                                axis=0)
            put_pair(p, z.T)


def _swa_layer_kernel(x_ref, win_ref, bin_ref, bias_ref, sink_ref, wout_ref, g_ref, b_ref, o_ref,
                      q_ref, kv_ref, att_ref, *, nblk):
    j = pl.program_id(1)
    blk = SWA_BLOCK
    tm = nblk * blk
    dh2 = 2 * SWA_HEAD_DIM

    p = _dot(x_ref[...].astype(BF16), win_ref[...]) + bin_ref[...]
    q_ref[...] = (p[:, :SWA_QDIM] * (SWA_HEAD_DIM ** -0.5)).astype(BF16)

    @pl.when(j == 0)
    def _():
        kv_ref[0:blk, :] = jnp.zeros((blk, 2 * SWA_KVDIM), BF16)

    @pl.when(j != 0)
    def _():
        kv_ref[0:blk, :] = kv_ref[tm:tm + blk, :]

    kv_ref[blk:, :] = p[:, SWA_QDIM:].astype(BF16)

    for i in range(nblk):
        rows = slice(i * blk, (i + 1) * blk)
        table = jnp.where(j == 0, 0, 1) if i == 0 else 1

        def put_pair(pair, o, rows=rows):
            att_ref[rows, pair * dh2:(pair + 1) * dh2] = o.astype(att_ref.dtype)

        _band_attention(lambda pair, rows=rows: q_ref[rows, pair * dh2:(pair + 1) * dh2],
                        kv_ref[i * blk:(i + 2) * blk, :],
                        lambda h, table=table: bias_ref[table, h], sink_ref, put_pair)

    o_ref[...] = _proj_res_ln(att_ref, wout_ref, x_ref, g_ref, b_ref)


def _swa_layer(x, w_in, b_in, bias, sinks, w_out, g, b, batch, seq_len):
    n = x.shape[0]
    nblk = SWA_BLOCKS_PER_STEP
    tm = nblk * SWA_BLOCK
    steps = seq_len // tm
    resident = lambda a: pl.BlockSpec(a.shape, lambda s, j: (0,) * a.ndim, pipeline_mode=pl.Buffered(1))
    rows = pl.BlockSpec((tm, D_MODEL), lambda s, j: (s * steps + j, 0))
    return pl.pallas_call(
        functools.partial(_swa_layer_kernel, nblk=nblk),
        grid=(batch, steps),
        in_specs=[rows, resident(w_in), resident(b_in), resident(bias), resident(sinks), resident(w_out),
                  resident(g), resident(b)],
        out_specs=rows,
        out_shape=jax.ShapeDtypeStruct((n, D_MODEL), F32),
        scratch_shapes=[pltpu.VMEM((tm, SWA_QDIM), BF16), pltpu.VMEM((tm + SWA_BLOCK, 2 * SWA_KVDIM), BF16),
                        pltpu.VMEM((tm, SWA_QDIM), BF16)],
        compiler_params=_params("parallel", "arbitrary"),
        name="swa_layer",
    )(x, w_in, b_in, bias, sinks, w_out, g, b)


def _router_kernel(x_ref, wr_ref, route_ref, route_t_ref, cnt_ref, run_ref, *, tm, n_parts):
    @pl.when(pl.program_id(0) == 0)
    def _():
        run_ref[...] = jnp.zeros(run_ref.shape, F32)

    pm = tm // n_parts
    parts = [slice(p * pm, (p + 1) * pm) for p in range(n_parts)]
    lane = lax.broadcasted_iota(jnp.int32, (pm, LANES), 1)
    lane_f = lane.astype(F32)

    wh, wl = _split(wr_ref[...])
    w_cat = jnp.concatenate([wh, wl], axis=1)
    logits = []
    for ps in parts:
        xh, xl = _split(x_ref[ps, :])
        hi = _dot(xh, w_cat)
        logits.append(hi[:, :LANES] + (hi[:, LANES:] + _dot(xl, wh)))

    picks = []
    for lgt in logits:
        lg = jnp.where(lane < N_EXPERTS, lgt, NEG_BIG)
        m1 = jnp.max(lg, axis=-1, keepdims=True)
        i1 = jnp.min(jnp.where(lg == m1, lane_f, float(LANES)), axis=-1, keepdims=True)
        oh1 = lane_f == i1
        lg2 = jnp.where(oh1, NEG_BIG, lg)
        m2 = jnp.max(lg2, axis=-1, keepdims=True)
        i2 = jnp.min(jnp.where(lg2 == m2, lane_f, float(LANES)), axis=-1, keepdims=True)
        oh2 = lane_f == i2
        e = jnp.exp(m2 - m1)
        w0 = 1.0 / (1.0 + e)
        cnt = jnp.where(oh1, 1.0, 0.0) + jnp.where(oh2, 1.0, 0.0)
        picks.append((i1, i2, oh1, oh2, w0, e * w0, cnt))

    r = lax.broadcasted_iota(jnp.int32, (pm, pm), 0)
    c = lax.broadcasted_iota(jnp.int32, (pm, pm), 1)
    before = jnp.where(r > c, 1.0, 0.0).astype(BF16)
    within = [_dot(before, pk[-1].astype(BF16)) for pk in picks]

    run = run_ref[...]
    for ps, (i1, i2, oh1, oh2, w0, w1, cnt), earlier in zip(parts, picks, within):
        excl = earlier + run
        rank0 = jnp.sum(jnp.where(oh1, excl, 0.0), axis=-1, keepdims=True)
        rank1 = jnp.sum(jnp.where(oh2, excl, 0.0), axis=-1, keepdims=True)
        run = run + jnp.sum(cnt, axis=0, keepdims=True)
        out = jnp.zeros((pm, LANES), F32)
        for idx, val in enumerate((i1, i2, rank0, rank1, w0, w1)):
            out = jnp.where(lane == idx, val, out)
        route_ref[ps, :] = out
        route_t_ref[:, ps] = out.T[:SUBLANES, :]
    run_ref[...] = run
    cnt_ref[...] = run


def _router(x, w_router_padded):
    n = x.shape[0]
    tm = TM_ROUTER
    return pl.pallas_call(
        functools.partial(_router_kernel, tm=tm, n_parts=ROUTER_PARTS),
        grid=(n // tm,),
        in_specs=[pl.BlockSpec((tm, D_MODEL), lambda i: (i, 0)), pl.BlockSpec((D_MODEL, LANES), lambda i: (0, 0))],
        out_specs=[pl.BlockSpec((tm, LANES), lambda i: (i, 0)), pl.BlockSpec((SUBLANES, tm), lambda i: (0, i)),
                   pl.BlockSpec((1, LANES), lambda i: (0, 0))],
        out_shape=[jax.ShapeDtypeStruct((n, LANES), F32), jax.ShapeDtypeStruct((SUBLANES, n), F32),
                   jax.ShapeDtypeStruct((1, LANES), F32)],
        scratch_shapes=[pltpu.VMEM((1, LANES), F32)],
        compiler_params=_params("arbitrary"),
        name="moe_router",
    )(x, w_router_padded)


def _dispatch_kernel(pos0_ref, pos1_ref, meta_ref, x_ref, wsrc_ref, xs_hbm, wdst_ref, zero_ref, sem, pad_sem, *, td):
    i = pl.program_id(0)
    wdst_ref[...] = wsrc_ref[...].astype(BF16)

    def row_copy(t, dst):
        return pltpu.make_async_copy(x_ref.at[pl.ds(t, 1)], xs_hbm.at[pl.ds(dst, 1)], sem)

    def start(t, carry):
        row_copy(t, pos0_ref[0, 0, t]).start()
        row_copy(t, pos1_ref[0, 0, t]).start(priority=1)
        return carry

    lax.fori_loop(0, td, start, 0, unroll=DMA_UNROLL)

    @pl.when(i == pl.num_programs(0) - 1)
    def _():
        zero_ref[...] = jnp.zeros(zero_ref.shape, F32)

        zr = zero_ref.shape[0]

        def row_zero(dst):
            return pltpu.make_async_copy(zero_ref.at[pl.ds(0, 1)], xs_hbm.at[pl.ds(dst, 1)], pad_sem)

        def block_zero(blk):
            return pltpu.make_async_copy(zero_ref, xs_hbm.at[pl.ds(pl.multiple_of(blk * zr, zr), zr)], pad_sem)

        def zero_range(copy, lo, hi):
            lax.fori_loop(lo, hi, lambda r, carry: (copy(r).start(), carry)[1], 0)
            lax.fori_loop(lo, hi, lambda r, carry: (copy(0).wait(), carry)[1], 0)

        for e in range(N_EXPERTS):
            off, cnt, cnt_up, padded = (meta_ref[r, e] for r in range(4))
            zero_range(lambda r, off=off: row_zero(off + r), cnt, cnt_up)
            zero_range(block_zero, (off + cnt_up) // zr, (off + padded) // zr)

        used = meta_ref[0, N_EXPERTS - 1] + meta_ref[3, N_EXPERTS - 1]
        zero_range(block_zero, used // zr, xs_hbm.shape[0] // zr)

    for _ in range(2):
        pltpu.make_async_copy(x_ref, xs_hbm.at[pl.ds(0, td)], sem).wait()


def _dispatch(x, pos, meta, n_rows, to_bf16):
    n = x.shape[0]
    td = TD_DISPATCH
    pos_spec = pl.BlockSpec((1, 1, td), lambda i: (i, 0, 0), memory_space=pltpu.SMEM)
    (cast_spec,), (cast_shape,) = _cast_specs([to_bf16], n // td)
    return pl.pallas_call(
        functools.partial(_dispatch_kernel, td=td),
        grid=(n // td,),
        in_specs=[pos_spec, pos_spec, pl.BlockSpec(memory_space=pltpu.SMEM),
                  pl.BlockSpec((td, D_MODEL), lambda i: (i, 0)), cast_spec],
        out_specs=[pl.BlockSpec(memory_space=pl.ANY), cast_spec],
        out_shape=[jax.ShapeDtypeStruct((n_rows, D_MODEL), F32), cast_shape],
        scratch_shapes=[pltpu.VMEM((ZERO_ROWS, D_MODEL), F32), pltpu.SemaphoreType.DMA, pltpu.SemaphoreType.DMA],
        compiler_params=_params("arbitrary"),
        name="moe_dispatch",
    )(pos[0].reshape(n // td, 1, td), pos[1].reshape(n // td, 1, td), meta, x, to_bf16)


def _moe_kernel(we_ref, wb_ref, nv_ref, cnt_ref, *refs, n_sub, tm):
    xs_refs = refs[:n_sub]
    wg_ref, wu_ref, wd_ref, ys_hbm, acc_ref, xb_ref, sem = refs[n_sub:]
    j = pl.program_id(0)
    f = pl.program_id(1)
    last_f = pl.num_programs(1) - 1
    n_windows, n_tiles = cnt_ref[0], cnt_ref[1]

    def tile_write(slot, block):
        return pltpu.make_async_copy(acc_ref.at[slot], ys_hbm.at[pl.ds(pl.multiple_of(block * tm, tm), tm)],
                                     sem.at[slot])

    @pl.when(j < n_windows)
    def _():
        prev_tiles = jnp.where(j > 0, nv_ref[jnp.maximum(j - 1, 0)], 0)

        for s in range(n_sub):
            @pl.when(s < nv_ref[j])
            def _(s=s):
                @pl.when(f == 0)
                def _():
                    xb_ref[s] = xs_refs[s][...].astype(BF16)

                xb = xb_ref[s]
                h = (_silu(_dot(xb, wg_ref[0])) * _dot(xb, wu_ref[0])).astype(BF16)
                part = _dot(h, wd_ref[0])

                @pl.when(f == 0)
                def _():
                    @pl.when(s < prev_tiles)
                    def _():
                        tile_write(s, 0).wait()

                    acc_ref[s] = part

                @pl.when(f != 0)
                def _():
                    acc_ref[s] += part

                @pl.when(f == last_f)
                def _():
                    tile_write(s, wb_ref[j] + s).start()

            @pl.when((f == 0) & (s >= nv_ref[j]) & (s < prev_tiles))
            def _(s=s):
                tile_write(s, 0).wait()

        @pl.when((f == last_f) & (j == n_windows - 1))
        def _():
            for s in range(n_sub):
                @pl.when(s < nv_ref[j])
                def _(s=s):
                    tile_write(s, 0).wait()

            acc_ref[0] = jnp.zeros(acc_ref.shape[1:], F32)
            total_tiles = ys_hbm.shape[0] // tm
            lax.fori_loop(n_tiles, total_tiles, lambda t, c: (tile_write(0, t).start(), c)[1], 0)
            lax.fori_loop(n_tiles, total_tiles, lambda t, c: (tile_write(0, 0).wait(), c)[1], 0)


def _moe_experts(xs, wg, wu, wd, win_expert, win_block, win_tiles, counts):
    n_rows = xs.shape[0]
    tm, tf, n_sub = TM_MOE, TF_MOE, MOE_TILES_PER_WINDOW
    max_windows = win_expert.shape[0]

    def xs_spec(s):
        return pl.BlockSpec((tm, D_MODEL), lambda j, f, we, wb, nv, cnt: (wb[j] + jnp.minimum(s, nv[j] - 1), 0))

    grid_spec = pltpu.PrefetchScalarGridSpec(
        num_scalar_prefetch=4,
        grid=(max_windows, EXPERT_DIM // tf),
        in_specs=[xs_spec(s) for s in range(n_sub)] + [
            pl.BlockSpec((1, D_MODEL, tf), lambda j, f, we, wb, nv, cnt: (we[j], 0, f)),
            pl.BlockSpec((1, D_MODEL, tf), lambda j, f, we, wb, nv, cnt: (we[j], 0, f)),
            pl.BlockSpec((1, tf, D_MODEL), lambda j, f, we, wb, nv, cnt: (we[j], f, 0))],
        out_specs=pl.BlockSpec(memory_space=pl.ANY),
        scratch_shapes=[pltpu.VMEM((n_sub, tm, D_MODEL), F32), pltpu.VMEM((n_sub, tm, D_MODEL), BF16),
                        pltpu.SemaphoreType.DMA((n_sub,))],
    )
    return pl.pallas_call(
        functools.partial(_moe_kernel, n_sub=n_sub, tm=tm),
        grid_spec=grid_spec,
        out_shape=jax.ShapeDtypeStruct((n_rows, D_MODEL), F32),
        compiler_params=_params("arbitrary", "arbitrary"),
        name="moe_experts",
    )(win_expert, win_block, win_tiles, counts, *([xs] * n_sub), wg, wu, wd)


def _combine_kernel(pos0_ref, pos1_ref, pos0n_ref, pos1n_ref, x_ref, route_ref, ys_hbm, g_ref, b_ref, o_ref,
                    y_ref, sem, *, tc):
    i = pl.program_id(0)
    slot = i % 2

    def issue(p_refs, s):
        def start(t, carry):
            for kk in range(2):
                pltpu.make_async_copy(ys_hbm.at[pl.ds(p_refs[kk][0, 0, t], 1)],
                                      y_ref.at[s, kk, pl.ds(t, 1)], sem.at[s]).start(priority=kk)
            return carry

        lax.fori_loop(0, tc, start, 0, unroll=DMA_UNROLL)

    @pl.when(i == 0)
    def _():
        issue((pos0_ref, pos1_ref), 0)

    @pl.when(i + 1 < pl.num_programs(0))
    def _():
        issue((pos0n_ref, pos1n_ref), 1 - slot)

    for kk in range(2):
        pltpu.make_async_copy(ys_hbm.at[pl.ds(0, tc)], y_ref.at[slot, kk], sem.at[slot]).wait()

    route = route_ref[...]
    f = route[:, 4:5] * y_ref[slot, 0] + route[:, 5:6] * y_ref[slot, 1]
    o_ref[...] = _layer_norm(ALPHA * x_ref[...] + f, g_ref[...], b_ref[...])


def _combine_ln(x, route, pos, ys, g, b):
    n = x.shape[0]
    tc = TC_COMBINE
    nt = n // tc
    pos0, pos1 = pos[0].reshape(nt, 1, tc), pos[1].reshape(nt, 1, tc)
    cur = pl.BlockSpec((1, 1, tc), lambda i: (i, 0, 0), memory_space=pltpu.SMEM)
    nxt = pl.BlockSpec((1, 1, tc), lambda i: (jnp.minimum(i + 1, nt - 1), 0, 0), memory_space=pltpu.SMEM)
    return pl.pallas_call(
        functools.partial(_combine_kernel, tc=tc),
        grid=(nt,),
        in_specs=[cur, cur, nxt, nxt,
                  pl.BlockSpec((tc, D_MODEL), lambda i: (i, 0)),
                  pl.BlockSpec((tc, LANES), lambda i: (i, 0)),
                  pl.BlockSpec(memory_space=pl.ANY),
                  pl.BlockSpec((1, D_MODEL), lambda i: (0, 0)), pl.BlockSpec((1, D_MODEL), lambda i: (0, 0))],
        out_specs=pl.BlockSpec((tc, D_MODEL), lambda i: (i, 0)),
        out_shape=jax.ShapeDtypeStruct((n, D_MODEL), F32),
        scratch_shapes=[pltpu.VMEM((2, 2, tc, D_MODEL), F32), pltpu.SemaphoreType.DMA((2,))],
        compiler_params=_params("arbitrary"),
        name="moe_combine_ln",
    )(pos0, pos1, pos0, pos1, x, route, ys, g, b)


def _moe_layer(x, route, route_t, counts_f, wg, wu, wd_f32, g, b):
    n = x.shape[0]
    tm = TM_MOE

    counts = counts_f[0, :N_EXPERTS].astype(jnp.int32)
    padded = ((counts + tm - 1) // tm) * tm
    offs = jnp.cumsum(padded) - padded
    eids = jnp.arange(N_EXPERTS, dtype=jnp.int32)
    e01 = route_t[0:2].astype(jnp.int32)
    rank01 = route_t[2:4].astype(jnp.int32)
    base = jnp.sum(jnp.where(e01[:, None, :] == eids[None, :, None], offs[None, :, None], 0), axis=1)
    pos = base + rank01
    counts_up = jnp.minimum(padded, ((counts + ZERO_ROWS - 1) // ZERO_ROWS) * ZERO_ROWS)
    meta = jnp.stack([offs, counts, counts_up, padded]).astype(jnp.int32)

    n_rows = 2 * n + N_EXPERTS * tm
    nsub = MOE_TILES_PER_WINDOW
    tiles = padded // tm
    wins = (tiles + nsub - 1) // nsub
    win_end = jnp.cumsum(wins)
    n_windows = win_end[-1]
    max_windows = (n_rows // tm + nsub - 1) // nsub + N_EXPERTS
    wj = jnp.minimum(jnp.arange(max_windows, dtype=jnp.int32), n_windows - 1)
    win_expert = jnp.minimum(jnp.sum((wj[:, None] >= win_end[None, :]).astype(jnp.int32), axis=-1), N_EXPERTS - 1)
    pick = lambda v: jnp.sum(jnp.where(win_expert[:, None] == eids[None, :], v[None, :], 0), axis=-1)
    local = wj - pick(win_end - wins)
    win_block = pick(offs // tm) + nsub * local
    win_tiles = jnp.minimum(nsub, pick(tiles) - nsub * local)
    counts_nt = jnp.stack([n_windows, jnp.sum(tiles)])

    xs, wd = _dispatch(x, pos, meta, n_rows, wd_f32.reshape(-1, D_MODEL))
    wd = wd.reshape(wd_f32.shape)
    i32 = lambda a: a.astype(jnp.int32)
    ys = _moe_experts(xs, wg, wu, wd, i32(win_expert), i32(win_block), i32(win_tiles), i32(counts_nt))
    return _combine_ln(x, route, pos, ys, g, b)


def kernel(x, a_w_in, a_conv_w, a_a_log, a_dt_bias, a_norm_w, a_w_out, b_w_in, b_b_in, b_sinks, b_w_out, rel_bias,
           ffn_w_gate, ffn_w_up, ffn_w_down, moe_router, moe_w_gate, moe_w_up, moe_w_down, ln_g, ln_b):
    batch, seq_len, _ = x.shape
    n = batch * seq_len
    x0 = x.reshape(n, D_MODEL)
    ln_g = ln_g.reshape(DEPTH, 2, 1, D_MODEL)
    ln_b = ln_b.reshape(DEPTH, 2, 1, D_MODEL)

    pad_gate = lambda p: jnp.pad(p.reshape(1, GDN_V_HEADS), ((0, 0), (GDN_V_HEADS, LANES - 2 * GDN_V_HEADS)))
    gate_params = jnp.concatenate([pad_gate(a_a_log[0]), pad_gate(a_dt_bias[0])], axis=0)
    q, k, v, zs, gb = _gdn_inproj(x0, jnp.swapaxes(a_w_in, 1, 2), a_conv_w[0], gate_params, seq_len)
    o, casted = _gdn_chunk(q, k, v, zs, gb, a_norm_w[0].reshape(1, GDN_HEAD), batch, seq_len,
                           [moe_w_gate[0].reshape(-1, EXPERT_DIM), moe_w_up[0].reshape(-1, EXPERT_DIM),
                            ffn_w_gate[0], ffn_w_up[0], ffn_w_down[0],
                            a_w_out[0], b_w_in[0], b_w_out[0]])
    moe_wg, moe_wu, ffn_wg, ffn_wu, ffn_wd, w_out_a, w_in_b, w_out_b = casted
    x2 = _outproj_ffn(o, w_out_a, x0, ln_g[0, 0], ln_b[0, 0],
                      ffn_wg, ffn_wu, ffn_wd, ln_g[0, 1], ln_b[0, 1])

    bias = _bias_table(rel_bias)
    sinks = jnp.pad(b_sinks[0].reshape(1, SWA_Q_HEADS), ((0, 0), (0, LANES - SWA_Q_HEADS)))
    x3 = _swa_layer(x2, w_in_b, b_b_in[0].reshape(1, -1), bias, sinks, w_out_b, ln_g[1, 0], ln_b[1, 0],
                    batch, seq_len)
    w_router = jnp.pad(moe_router[0], ((0, 0), (0, LANES - N_EXPERTS)))
    route, route_t, counts = _router(x3, w_router)
    expert_shape = (N_EXPERTS, D_MODEL, EXPERT_DIM)
    x4 = _moe_layer(x3, route, route_t, counts, moe_wg.reshape(expert_shape), moe_wu.reshape(expert_shape),
                    moe_w_down[0], ln_g[1, 1], ln_b[1, 1])
    return x4.reshape(batch, seq_len, D_MODEL)
```

```python
import functools
import math

import numpy as np
import jax
import jax.numpy as jnp
from jax import lax
from jax.experimental import pallas as pl
from jax.experimental.pallas import tpu as pltpu

F32 = jnp.float32
BF16 = jnp.bfloat16

D_MODEL = 1024
DEPTH = 2
ALPHA = (2.0 * DEPTH) ** 0.25
LN_EPS = 1e-5

GDN_K_HEADS = 4
GDN_V_HEADS = 8
GDN_HEAD = 128
GDN_KDIM = GDN_K_HEADS * GDN_HEAD
GDN_VDIM = GDN_V_HEADS * GDN_HEAD
GDN_CONV = 4
GDN_CHUNK = 64
GDN_QKV = 2 * GDN_KDIM + GDN_VDIM
GDN_EPS = 1e-6

SWA_Q_HEADS = 16
SWA_KV_HEADS = 2
SWA_GROUP = SWA_Q_HEADS // SWA_KV_HEADS
SWA_HEAD_DIM = 64
SWA_WINDOW = 128
SWA_BLOCK = 128
SWA_QDIM = SWA_Q_HEADS * SWA_HEAD_DIM
SWA_KVDIM = SWA_KV_HEADS * SWA_HEAD_DIM
REL_BUCKETS = 32
REL_MAX_DIST = 128

FFN_DIM = 2816
N_EXPERTS = 8
EXPERT_DIM = 3584

LANES = 128
SUBLANES = 8
NEG_BIG = -1e30
VMEM_LIMIT = 56 * 1024 * 1024

TM_GDN_IN = 512
TM_FFN = 1024
FFN_PARTS = 4
SWA_BLOCKS_PER_STEP = 8
TM_ROUTER = 2048
ROUTER_PARTS = 8
TM_MOE = 512
TF_MOE = 1792
MOE_TILES_PER_WINDOW = 3
MOE_PARTS = 2
TD_DISPATCH = 1024
TC_COMBINE = 512
ZERO_ROWS = 64
DMA_UNROLL = 8


def _params(*sem):
    return pltpu.CompilerParams(dimension_semantics=sem, vmem_limit_bytes=VMEM_LIMIT)


def _dot(a, b):
    return jnp.dot(a, b, preferred_element_type=F32)


def _dot_nt(a, b):
    return lax.dot_general(a, b, (((1,), (1,)), ((), ())), preferred_element_type=F32)


def _dot_tn(a, b):
    return lax.dot_general(a, b, (((0,), (0,)), ((), ())), preferred_element_type=F32)


def _split(x):
    hi = x.astype(BF16)
    lo = (x - hi.astype(F32)).astype(BF16)
    return hi, lo


def _dot3(a, b):
    ah, al = _split(a)
    bh, bl = _split(b)
    return _dot(ah, bh) + (_dot(ah, bl) + _dot(al, bh))


def _silu(x):
    return x * jax.nn.sigmoid(x)


def _layer_norm(y, g, b):
    mu = jnp.mean(y, axis=-1, keepdims=True)
    yc = y - mu
    var = jnp.mean(yc * yc, axis=-1, keepdims=True)
    return yc * lax.rsqrt(var + LN_EPS) * g + b


def _gdn_inproj_kernel(x_ref, wf_ref, convw_ref, gp_ref,
                       q_ref, k_ref, v_ref, z_ref, gb_ref, ext_ref, w_ref, *, tm, tiles_per_seq):
    i = pl.program_id(0)
    xb = x_ref[...].astype(BF16)

    @pl.when(i == 0)
    def _():
        n_in = wf_ref.shape[1]
        n_full = n_in // LANES
        for c in range(n_full):
            cs = slice(c * LANES, (c + 1) * LANES)
            w_ref[:, cs] = wf_ref[0, cs, :].T.astype(BF16)
        tail = jnp.concatenate([wf_ref[0, n_full * LANES:, :],
                                jnp.zeros(((n_full + 1) * LANES - n_in, wf_ref.shape[2]), F32)], axis=0)
        w_ref[:, n_full * LANES:] = tail.T.astype(BF16)

    @pl.when(i % tiles_per_seq == 0)
    def _():
        ext_ref[0:SUBLANES, :] = jnp.zeros((SUBLANES, GDN_QKV), F32)

    @pl.when(i % tiles_per_seq != 0)
    def _():
        ext_ref[0:SUBLANES, :] = ext_ref[tm:tm + SUBLANES, :]

    ext_ref[SUBLANES:, :] = _dot(xb, w_ref[:, :GDN_QKV])
    z_ref[...] = _dot(xb, w_ref[:, GDN_QKV:GDN_QKV + GDN_VDIM])

    n_chunks = GDN_QKV // LANES
    for c in range(n_chunks):
        cs = slice(c * LANES, (c + 1) * LANES)
        acc = convw_ref[GDN_CONV - 1:GDN_CONV, cs] * ext_ref[SUBLANES:SUBLANES + tm, cs]
        for j in range(GDN_CONV - 1):
            off = SUBLANES - (GDN_CONV - 1) + j
            acc = acc + convw_ref[j:j + 1, cs] * ext_ref[off:off + tm, cs]
        y = _silu(acc)
        if c < 2 * GDN_K_HEADS:
            y = y * lax.rsqrt(jnp.sum(y * y, axis=-1, keepdims=True) + GDN_EPS)
            if c < GDN_K_HEADS:
                q_ref[:, cs] = y * (GDN_HEAD ** -0.5)
            else:
                k_ref[:, (c - GDN_K_HEADS) * LANES:(c - GDN_K_HEADS + 1) * LANES] = y
        else:
            cv = c - 2 * GDN_K_HEADS
            v_ref[:, cv * LANES:(cv + 1) * LANES] = y

    ba = _dot(xb, w_ref[:, GDN_QKV + GDN_VDIM:])
    lane = lax.broadcasted_iota(jnp.int32, ba.shape, 1)
    sp = ba + gp_ref[1:2, :]
    softplus = jnp.maximum(sp, 0.0) + jnp.log(1.0 + jnp.exp(-jnp.abs(sp)))
    g = -jnp.exp(gp_ref[0:1, :]) * softplus
    gb_ref[...] = jnp.where(lane < GDN_V_HEADS, jax.nn.sigmoid(ba), g)


def _gdn_inproj(x2d, w_in, conv_w, gate_params, seq_len):
    n = x2d.shape[0]
    tm = TM_GDN_IN
    kern = functools.partial(_gdn_inproj_kernel, tm=tm, tiles_per_seq=seq_len // tm)
    full = lambda shape: pl.BlockSpec(shape, lambda i: (0,) * len(shape))
    rows = lambda width: pl.BlockSpec((tm, width), lambda i: (i, 0))
    return pl.pallas_call(
        kern,
        grid=(n // tm,),
        in_specs=[rows(D_MODEL), pl.BlockSpec(w_in.shape, lambda i: (0, 0, 0), pipeline_mode=pl.Buffered(1)),
                  full(conv_w.shape), full(gate_params.shape)],
        out_specs=[rows(GDN_KDIM), rows(GDN_KDIM), rows(GDN_VDIM), rows(GDN_VDIM), rows(LANES)],
        out_shape=[jax.ShapeDtypeStruct((n, GDN_KDIM), F32), jax.ShapeDtypeStruct((n, GDN_KDIM), F32),
                   jax.ShapeDtypeStruct((n, GDN_VDIM), F32), jax.ShapeDtypeStruct((n, GDN_VDIM), F32),
                   jax.ShapeDtypeStruct((n, LANES), F32)],
        scratch_shapes=[pltpu.VMEM((tm + SUBLANES, GDN_QKV), F32),
                        pltpu.VMEM((D_MODEL, GDN_QKV + GDN_VDIM + LANES), BF16)],
        compiler_params=_params("arbitrary"),
        name="gdn_inproj",
    )(x2d, w_in, conv_w, gate_params)


GDN_CHUNKS_PER_STEP = 2
NEUMANN_BLOCK = 4


def _bf16_all(xs):
    return [x.astype(BF16) for x in xs]


def _dot_all(a_list, b_list):
    return [_dot(a, b) for a, b in zip(a_list, b_list)]


def _unit_lower_inverse_all(a_list, row, col):
    shift = int(math.log2(NEUMANN_BLOCK))
    eye = jnp.where(row == col, 1.0, 0.0).astype(F32)
    on_diag_block = (row >> shift) == (col >> shift)
    d = [jnp.where(on_diag_block, a, 0.0) for a in a_list]
    d_b = _bf16_all(d)
    x = [eye - di for di in d]
    d2_b = _bf16_all(_dot_all(d_b, d_b))
    x = [xi + t for xi, t in zip(x, _dot_all(_bf16_all(x), d2_b))]
    size = NEUMANN_BLOCK
    while size < a_list[0].shape[0]:
        shift = int(math.log2(size))
        rbl, cbl = row >> shift, col >> shift
        below = ((rbl & 1) == 1) & (cbl == rbl - 1)
        l_b = _bf16_all([jnp.where(below, a, 0.0) for a in a_list])
        x_b = _bf16_all(x)
        xl_b = _bf16_all(_dot_all(x_b, l_b))
        x = [xi - t for xi, t in zip(x, _dot_all(xl_b, x_b))]
        size *= 2
    return x


def _gdn_chunk_kernel(q_ref, k_ref, v_ref, z_ref, gb_ref, nw_ref, *rest, nb, nck, n_cast):
    cast_src, (o_ref, *cast_dst), s_ref = rest[:n_cast], rest[n_cast:2 * n_cast + 1], rest[-1]
    for src, dst in zip(cast_src, cast_dst):
        dst[...] = src[...].astype(BF16)
    c = GDN_CHUNK
    nh = GDN_V_HEADS

    @pl.when(pl.program_id(0) == 0)
    def _():
        s_ref[...] = jnp.zeros(s_ref.shape, F32)

    row = lax.broadcasted_iota(jnp.int32, (c, c), 0)
    col = lax.broadcasted_iota(jnp.int32, (c, c), 1)
    causal = row >= col
    strict = row > col
    tril = jnp.where(causal, 1.0, 0.0).astype(BF16)
    rs = lambda ci: slice(ci * c, (ci + 1) * c)
    ks = lambda h: slice((h // (nh // GDN_K_HEADS)) * GDN_HEAD, (h // (nh // GDN_K_HEADS) + 1) * GDN_HEAD)
    vs = lambda h: slice(h * GDN_HEAD, (h + 1) * GDN_HEAD)

    gates = {}
    for ci in range(nck):
        for b in range(nb):
            gb = gb_ref[b, rs(ci), :]
            g_hi, g_lo = _split(gb)
            gc = _dot(tril, g_hi) + _dot(tril, g_lo)
            g_last = gc[c - 1:c, :]
            gates[ci, b] = dict(gb=gb, gc=gc, gc_t=gc.T, eg=jnp.exp(gc), e_last=jnp.exp(g_last),
                                e_rest=jnp.exp(g_last - gc))

    items = [(ci, b, h) for ci in range(nck) for b in range(nb) for h in range(nh)]

    a_kk, a_qk, k_beta = [], {}, {}
    for it in items:
        ci, b, h = it
        gt = gates[ci, b]
        k = k_ref[b, rs(ci), ks(h)]
        kb = k * gt["gb"][:, h:h + 1]
        kq = jnp.concatenate([kb, q_ref[b, rs(ci), ks(h)]], axis=0).astype(BF16)
        gram = _dot_nt(kq, k.astype(BF16))
        gl = nh + h
        decay = jnp.exp(jnp.where(causal, gt["gc"][:, gl:gl + 1] - gt["gc_t"][gl:gl + 1, :], NEG_BIG))
        a_kk.append(jnp.where(strict, gram[:c] * decay, 0.0))
        a_qk[it] = (gram[c:] * decay).astype(BF16)
        k_beta[it] = kb

    t_inv = dict(zip(items, _unit_lower_inverse_all(a_kk, row, col)))

    uw = {}
    for it in items:
        ci, b, h = it
        gt = gates[ci, b]
        gl = nh + h
        rhs = jnp.concatenate([v_ref[b, rs(ci), vs(h)] * gt["gb"][:, h:h + 1],
                               k_beta[it] * gt["eg"][:, gl:gl + 1]], axis=1)
        uw[it] = _dot(t_inv[it].astype(BF16), rhs.astype(BF16))

    for ci in range(nck):
        chunk_items = [(ci, b, h) for b in range(nb) for h in range(nh)]

        ws_qs, states = {}, {}
        for it in chunk_items:
            _, b, h = it
            gl = nh + h
            s = s_ref[b * nh + h]
            wq = jnp.concatenate([uw[it][:, GDN_HEAD:],
                                  q_ref[b, rs(ci), ks(h)] * gates[ci, b]["eg"][:, gl:gl + 1]], axis=0)
            ws_qs[it] = _dot(wq.astype(BF16), s.astype(BF16))
            states[it] = s

        for it in chunk_items:
            _, b, h = it
            gt = gates[ci, b]
            gl = nh + h
            v_new = (uw[it][:, :GDN_HEAD] - ws_qs[it][:c]).astype(BF16)
            o = ws_qs[it][c:] + _dot(a_qk[it], v_new)
            k_dec = (k_ref[b, rs(ci), ks(h)] * gt["e_rest"][:, gl:gl + 1]).astype(BF16)
            s_ref[b * nh + h] = states[it] * gt["e_last"][:, gl:gl + 1] + _dot_tn(k_dec, v_new)
            o = o * lax.rsqrt(jnp.mean(o * o, axis=-1, keepdims=True) + GDN_EPS) * nw_ref[...]
            o_ref[b, rs(ci), vs(h)] = (o * _silu(z_ref[b, rs(ci), vs(h)])).astype(o_ref.dtype)


def _cast_specs(arrays, n_steps):
    specs, shapes = [], []
    for a in arrays:
        span = next(d for d in (1, 2, 4, 8)
                    if (a.shape[0] * d) % n_steps == 0 and (a.shape[0] * d // n_steps) % (2 * SUBLANES) == 0)
        specs.append(pl.BlockSpec((a.shape[0] * span // n_steps, a.shape[1]), lambda j, span=span: (j // span, 0)))
        shapes.append(jax.ShapeDtypeStruct(a.shape, BF16))
    return specs, shapes


def _gdn_chunk(q, k, v, zs, gb, norm_w, batch, seq_len, to_bf16):
    rows_per_step = GDN_CHUNK * GDN_CHUNKS_PER_STEP
    n_steps = seq_len // rows_per_step
    seq = lambda a: a.reshape(batch, seq_len, a.shape[-1])
    rows = lambda width: pl.BlockSpec((batch, rows_per_step, width), lambda j: (0, j, 0))
    cast_specs, cast_shapes = _cast_specs(to_bf16, n_steps)
    out, *casted = pl.pallas_call(
        functools.partial(_gdn_chunk_kernel, nb=batch, nck=GDN_CHUNKS_PER_STEP, n_cast=len(to_bf16)),
        grid=(n_steps,),
        in_specs=[rows(GDN_KDIM), rows(GDN_KDIM), rows(GDN_VDIM), rows(GDN_VDIM), rows(LANES),
                  pl.BlockSpec((1, GDN_HEAD), lambda j: (0, 0))] + cast_specs,
        out_specs=[rows(GDN_VDIM)] + cast_specs,
        out_shape=[jax.ShapeDtypeStruct((batch, seq_len, GDN_VDIM), BF16)] + cast_shapes,
        scratch_shapes=[pltpu.VMEM((batch * GDN_V_HEADS, GDN_HEAD, GDN_HEAD), F32)],
        compiler_params=_params("arbitrary"),
        name="gdn_chunk",
    )(seq(q), seq(k), seq(v), seq(zs), seq(gb), norm_w, *to_bf16)
    return out.reshape(batch * seq_len, GDN_VDIM), casted


def _proj_res_ln(a_ref, w_ref, r_ref, g_ref, b_ref):
    return _layer_norm(ALPHA * r_ref[...] + _dot(a_ref[...], w_ref[...]), g_ref[...], b_ref[...])


def _outproj_ffn_kernel(a_ref, wo_ref, r_ref, g0_ref, b0_ref, wg_ref, wu_ref, wd_ref, g1_ref, b1_ref, o_ref, *,
                        n_parts):
    pm = o_ref.shape[0] // n_parts
    parts = [slice(p * pm, (p + 1) * pm) for p in range(n_parts)]
    x1 = [_layer_norm(ALPHA * r_ref[ps, :] + _dot(a_ref[ps, :], wo_ref[...]), g0_ref[...], b0_ref[...])
          for ps in parts]
    xb = [x.astype(BF16) for x in x1]
    gate_up = lambda p: (_dot(xb[p], wg_ref[...]), _dot(xb[p], wu_ref[...]))
    ahead = gate_up(0)
    for p, ps in enumerate(parts):
        (gate, up), ahead = ahead, (gate_up(p + 1) if p + 1 < n_parts else None)
        h = (_silu(gate) * up).astype(BF16)
        o_ref[ps, :] = _layer_norm(ALPHA * x1[p] + _dot(h, wd_ref[...]), g1_ref[...], b1_ref[...])


def _outproj_ffn(a, w_out, res, g0, b0, wg, wu, wd, g1, b1):
    n, kdim = a.shape
    tm = TM_FFN
    resident = lambda w: pl.BlockSpec(w.shape, lambda i: (0, 0), pipeline_mode=pl.Buffered(1))
    rows = lambda width: pl.BlockSpec((tm, width), lambda i: (i, 0))
    vec = pl.BlockSpec((1, D_MODEL), lambda i: (0, 0))
    return pl.pallas_call(
        functools.partial(_outproj_ffn_kernel, n_parts=FFN_PARTS),
        grid=(n // tm,),
        in_specs=[rows(kdim), resident(w_out), rows(D_MODEL), vec, vec,
                  resident(wg), resident(wu), resident(wd), vec, vec],
        out_specs=rows(D_MODEL),
        out_shape=jax.ShapeDtypeStruct((n, D_MODEL), F32),
        compiler_params=_params("parallel"),
        name="gdn_outproj_ffn",
    )(a, w_out, res, g0, b0, wg, wu, wd, g1, b1)


def _band_tables():
    assert SWA_WINDOW == SWA_BLOCK
    r = np.arange(SWA_BLOCK)[:, None]
    q = np.arange(SWA_BLOCK)[None, :]
    d = (q - r) % SWA_BLOCK
    max_exact = REL_BUCKETS // 2
    df = np.maximum(d, 1).astype(np.float32)
    large = max_exact + (np.log(df / np.float32(max_exact)) / np.float32(math.log(REL_MAX_DIST / max_exact))
                         * np.float32(REL_BUCKETS - max_exact)).astype(np.int32)
    large = np.minimum(large, REL_BUCKETS - 1)
    bucket = np.where(d < max_exact, d, large).astype(np.int32)
    valid = np.stack([q >= r, np.ones_like(q >= r)]).astype(np.int32)
    return bucket, valid


def _bias_kernel(relb_ref, bucket_ref, valid_ref, o_ref):
    bucket = bucket_ref[...]
    in_bucket = [bucket == b for b in range(REL_BUCKETS)]
    for h in range(SWA_Q_HEADS):
        acc = jnp.zeros(bucket.shape, F32)
        for b in range(REL_BUCKETS):
            acc = jnp.where(in_bucket[b], relb_ref[b, h], acc)
        for t in range(2):
            o_ref[t, h] = jnp.where(valid_ref[t] != 0, acc, NEG_BIG)


def _bias_table(rel_bias):
    bucket, valid = _band_tables()
    shape = (SWA_BLOCK, SWA_BLOCK)
    return pl.pallas_call(
        _bias_kernel,
        grid=(1,),
        in_specs=[pl.BlockSpec(memory_space=pltpu.SMEM), pl.BlockSpec(shape, lambda i: (0, 0)),
                  pl.BlockSpec((2,) + shape, lambda i: (0, 0, 0))],
        out_specs=pl.BlockSpec((2, SWA_Q_HEADS) + shape, lambda i: (0, 0, 0, 0)),
        out_shape=jax.ShapeDtypeStruct((2, SWA_Q_HEADS) + shape, F32),
        compiler_params=_params("arbitrary"),
        name="swa_bias_table",
    )(rel_bias, jnp.asarray(bucket), jnp.asarray(valid))


def _band_attention(q_pair, kv, bias_of, sink_ref, put_pair):
    dh = SWA_HEAD_DIM
    blk = SWA_BLOCK
    nk = 2 * blk
    from_prev = lax.broadcasted_iota(jnp.int32, (blk, blk), 1) < lax.broadcasted_iota(jnp.int32, (blk, blk), 0)
    k_all = kv[:, :SWA_KVDIM]
    lane = lax.broadcasted_iota(jnp.int32, k_all.shape, 1)
    zero = jnp.zeros_like(k_all)
    for kh in range(SWA_KV_HEADS):
        k_own = jnp.where((lane >= kh * dh) & (lane < (kh + 1) * dh), k_all, zero)
        k_other = pltpu.roll(k_own.astype(F32), dh, axis=1).astype(BF16)
        k_even, k_odd = (k_own, k_other) if kh == 0 else (k_other, k_own)
        k2 = jnp.concatenate([k_even, k_odd], axis=0)
        v_t = kv[:, SWA_KVDIM + kh * dh:SWA_KVDIM + (kh + 1) * dh].T
        pairs = [kh * (SWA_GROUP // 2) + p for p in range(SWA_GROUP // 2)]
        heads = [2 * p + r for p in pairs for r in range(2)]
        st = [_dot_nt(k2, q_pair(p)) for p in pairs]
        scores = [jnp.where(from_prev, st[i // 2][(i % 2) * nk:(i % 2) * nk + blk],
                            st[i // 2][(i % 2) * nk + blk:(i % 2 + 1) * nk]) + bias_of(h)
                  for i, h in enumerate(heads)]
        sinks = [sink_ref[0:1, h:h + 1] for h in heads]
        maxes = [jnp.maximum(jnp.max(s, axis=0, keepdims=True), sk) for s, sk in zip(scores, sinks)]
        expo = [jnp.exp(s - m) for s, m in zip(scores, maxes)]
        sums = [jnp.sum(p, axis=0, keepdims=True) for p in expo]
        inv_den = [1.0 / (t + jnp.exp(sk - m)) for t, sk, m in zip(sums, sinks, maxes)]
        expo = [jnp.concatenate([jnp.where(from_prev, p, 0.0), jnp.where(from_prev, 0.0, p)], axis=0) for p in expo]
        for i, p in enumerate(pairs):
            o_t = _dot(v_t, jnp.concatenate(expo[2 * i:2 * i + 2], axis=1).astype(BF16))
            z = jnp.concatenate([o_t[:, :SWA_BLOCK] * inv_den[2 * i], o_t[:, SWA_BLOCK:] * inv_den[2 * i + 1]],
                                axis=0)
            put_pair(p, z.T)


def _swa_layer_kernel(x_ref, win_ref, bin_ref, bias_ref, sink_ref, wout_ref, g_ref, b_ref, o_ref,
                      q_ref, kv_ref, att_ref, *, nblk):
    j = pl.program_id(1)
    blk = SWA_BLOCK
    tm = nblk * blk
    dh2 = 2 * SWA_HEAD_DIM

    p = _dot(x_ref[...].astype(BF16), win_ref[...]) + bin_ref[...]
    q_ref[...] = (p[:, :SWA_QDIM] * (SWA_HEAD_DIM ** -0.5)).astype(BF16)

    @pl.when(j == 0)
    def _():
        kv_ref[0:blk, :] = jnp.zeros((blk, 2 * SWA_KVDIM), BF16)

    @pl.when(j != 0)
    def _():
        kv_ref[0:blk, :] = kv_ref[tm:tm + blk, :]

    kv_ref[blk:, :] = p[:, SWA_QDIM:].astype(BF16)

    for i in range(nblk):
        rows = slice(i * blk, (i + 1) * blk)
        table = jnp.where(j == 0, 0, 1) if i == 0 else 1

        def put_pair(pair, o, rows=rows):
            att_ref[rows, pair * dh2:(pair + 1) * dh2] = o.astype(att_ref.dtype)

        _band_attention(lambda pair, rows=rows: q_ref[rows, pair * dh2:(pair + 1) * dh2],
                        kv_ref[i * blk:(i + 2) * blk, :],
                        lambda h, table=table: bias_ref[table, h], sink_ref, put_pair)

    o_ref[...] = _proj_res_ln(att_ref, wout_ref, x_ref, g_ref, b_ref)


def _swa_layer(x, w_in, b_in, bias, sinks, w_out, g, b, batch, seq_len):
    n = x.shape[0]
    nblk = SWA_BLOCKS_PER_STEP
    tm = nblk * SWA_BLOCK
    steps = seq_len // tm
    resident = lambda a: pl.BlockSpec(a.shape, lambda s, j: (0,) * a.ndim, pipeline_mode=pl.Buffered(1))
    rows = pl.BlockSpec((tm, D_MODEL), lambda s, j: (s * steps + j, 0))
    return pl.pallas_call(
        functools.partial(_swa_layer_kernel, nblk=nblk),
        grid=(batch, steps),
        in_specs=[rows, resident(w_in), resident(b_in), resident(bias), resident(sinks), resident(w_out),
                  resident(g), resident(b)],
        out_specs=rows,
        out_shape=jax.ShapeDtypeStruct((n, D_MODEL), F32),
        scratch_shapes=[pltpu.VMEM((tm, SWA_QDIM), BF16), pltpu.VMEM((tm + SWA_BLOCK, 2 * SWA_KVDIM), BF16),
                        pltpu.VMEM((tm, SWA_QDIM), BF16)],
        compiler_params=_params("parallel", "arbitrary"),
        name="swa_layer",
    )(x, w_in, b_in, bias, sinks, w_out, g, b)


def _router_kernel(x_ref, wr_ref, route_ref, route_t_ref, cnt_ref, run_ref, *, tm, n_parts):
    @pl.when(pl.program_id(0) == 0)
    def _():
        run_ref[...] = jnp.zeros(run_ref.shape, F32)

    pm = tm // n_parts
    parts = [slice(p * pm, (p + 1) * pm) for p in range(n_parts)]
    lane = lax.broadcasted_iota(jnp.int32, (pm, LANES), 1)
    lane_f = lane.astype(F32)

    wh, wl = _split(wr_ref[...])
    w_cat = jnp.concatenate([wh, wl], axis=1)
    logits = []
    for ps in parts:
        xh, xl = _split(x_ref[ps, :])
        hi = _dot(xh, w_cat)
        logits.append(hi[:, :LANES] + (hi[:, LANES:] + _dot(xl, wh)))

    picks = []
    for lgt in logits:
        lg = jnp.where(lane < N_EXPERTS, lgt, NEG_BIG)
        m1 = jnp.max(lg, axis=-1, keepdims=True)
        i1 = jnp.min(jnp.where(lg == m1, lane_f, float(LANES)), axis=-1, keepdims=True)
        oh1 = lane_f == i1
        lg2 = jnp.where(oh1, NEG_BIG, lg)
        m2 = jnp.max(lg2, axis=-1, keepdims=True)
        i2 = jnp.min(jnp.where(lg2 == m2, lane_f, float(LANES)), axis=-1, keepdims=True)
        oh2 = lane_f == i2
        e = jnp.exp(m2 - m1)
        w0 = 1.0 / (1.0 + e)
        cnt = jnp.where(oh1, 1.0, 0.0) + jnp.where(oh2, 1.0, 0.0)
        picks.append((i1, i2, oh1, oh2, w0, e * w0, cnt))

    r = lax.broadcasted_iota(jnp.int32, (pm, pm), 0)
    c = lax.broadcasted_iota(jnp.int32, (pm, pm), 1)
    before = jnp.where(r > c, 1.0, 0.0).astype(BF16)
    within = [_dot(before, pk[-1].astype(BF16)) for pk in picks]

    run = run_ref[...]
    for ps, (i1, i2, oh1, oh2, w0, w1, cnt), earlier in zip(parts, picks, within):
        excl = earlier + run
        rank0 = jnp.sum(jnp.where(oh1, excl, 0.0), axis=-1, keepdims=True)
        rank1 = jnp.sum(jnp.where(oh2, excl, 0.0), axis=-1, keepdims=True)
        run = run + jnp.sum(cnt, axis=0, keepdims=True)
        out = jnp.zeros((pm, LANES), F32)
        for idx, val in enumerate((i1, i2, rank0, rank1, w0, w1)):
            out = jnp.where(lane == idx, val, out)
        route_ref[ps, :] = out
        route_t_ref[:, ps] = out.T[:SUBLANES, :]
    run_ref[...] = run
    cnt_ref[...] = run


def _router(x, w_router_padded):
    n = x.shape[0]
    tm = TM_ROUTER
    return pl.pallas_call(
        functools.partial(_router_kernel, tm=tm, n_parts=ROUTER_PARTS),
        grid=(n // tm,),
        in_specs=[pl.BlockSpec((tm, D_MODEL), lambda i: (i, 0)), pl.BlockSpec((D_MODEL, LANES), lambda i: (0, 0))],
        out_specs=[pl.BlockSpec((tm, LANES), lambda i: (i, 0)), pl.BlockSpec((SUBLANES, tm), lambda i: (0, i)),
                   pl.BlockSpec((1, LANES), lambda i: (0, 0))],
        out_shape=[jax.ShapeDtypeStruct((n, LANES), F32), jax.ShapeDtypeStruct((SUBLANES, n), F32),
                   jax.ShapeDtypeStruct((1, LANES), F32)],
        scratch_shapes=[pltpu.VMEM((1, LANES), F32)],
        compiler_params=_params("arbitrary"),
        name="moe_router",
    )(x, w_router_padded)


def _dispatch_kernel(pos0_ref, pos1_ref, meta_ref, x_ref, wsrc_ref, xs_hbm, wdst_ref, zero_ref, sem, pad_sem, *, td):
    i = pl.program_id(0)
    wdst_ref[...] = wsrc_ref[...].astype(BF16)

    def row_copy(t, dst):
        return pltpu.make_async_copy(x_ref.at[pl.ds(t, 1)], xs_hbm.at[pl.ds(dst, 1)], sem)

    def start(t, carry):
        row_copy(t, pos0_ref[0, 0, t]).start()
        row_copy(t, pos1_ref[0, 0, t]).start(priority=1)
        return carry

    lax.fori_loop(0, td, start, 0, unroll=DMA_UNROLL)

    @pl.when(i == pl.num_programs(0) - 1)
    def _():
        zero_ref[...] = jnp.zeros(zero_ref.shape, F32)

        zr = zero_ref.shape[0]

        def row_zero(dst):
            return pltpu.make_async_copy(zero_ref.at[pl.ds(0, 1)], xs_hbm.at[pl.ds(dst, 1)], pad_sem)

        def block_zero(blk):
            return pltpu.make_async_copy(zero_ref, xs_hbm.at[pl.ds(pl.multiple_of(blk * zr, zr), zr)], pad_sem)

        def zero_range(copy, lo, hi):
            lax.fori_loop(lo, hi, lambda r, carry: (copy(r).start(), carry)[1], 0)
            lax.fori_loop(lo, hi, lambda r, carry: (copy(0).wait(), carry)[1], 0)

        for e in range(N_EXPERTS):
            off, cnt, cnt_up, padded = (meta_ref[r, e] for r in range(4))
            zero_range(lambda r, off=off: row_zero(off + r), cnt, cnt_up)
            zero_range(block_zero, (off + cnt_up) // zr, (off + padded) // zr)

        used = meta_ref[0, N_EXPERTS - 1] + meta_ref[3, N_EXPERTS - 1]
        zero_range(block_zero, used // zr, xs_hbm.shape[0] // zr)

    for _ in range(2):
        pltpu.make_async_copy(x_ref, xs_hbm.at[pl.ds(0, td)], sem).wait()


def _dispatch(x, pos, meta, n_rows, to_bf16):
    n = x.shape[0]
    td = TD_DISPATCH
    pos_spec = pl.BlockSpec((1, 1, td), lambda i: (i, 0, 0), memory_space=pltpu.SMEM)
    (cast_spec,), (cast_shape,) = _cast_specs([to_bf16], n // td)
    return pl.pallas_call(
        functools.partial(_dispatch_kernel, td=td),
        grid=(n // td,),
        in_specs=[pos_spec, pos_spec, pl.BlockSpec(memory_space=pltpu.SMEM),
                  pl.BlockSpec((td, D_MODEL), lambda i: (i, 0)), cast_spec],
        out_specs=[pl.BlockSpec(memory_space=pl.ANY), cast_spec],
        out_shape=[jax.ShapeDtypeStruct((n_rows, D_MODEL), F32), cast_shape],
        scratch_shapes=[pltpu.VMEM((ZERO_ROWS, D_MODEL), F32), pltpu.SemaphoreType.DMA, pltpu.SemaphoreType.DMA],
        compiler_params=_params("arbitrary"),
        name="moe_dispatch",
    )(pos[0].reshape(n // td, 1, td), pos[1].reshape(n // td, 1, td), meta, x, to_bf16)


def _moe_kernel(we_ref, wb_ref, nv_ref, cnt_ref, *refs, n_sub, tm, n_parts):
    xs_refs = refs[:n_sub]
    wg_ref, wu_ref, wd_ref, ys_hbm, acc_ref, sem = refs[n_sub:]
    j = pl.program_id(0)
    f = pl.program_id(1)
    last_f = pl.num_programs(1) - 1
    n_windows, n_tiles = cnt_ref[0], cnt_ref[1]

    def tile_write(slot, block):
        return pltpu.make_async_copy(acc_ref.at[slot], ys_hbm.at[pl.ds(pl.multiple_of(block * tm, tm), tm)],
                                     sem.at[slot])

    @pl.when(j < n_windows)
    def _():
        prev_tiles = jnp.where(j > 0, nv_ref[jnp.maximum(j - 1, 0)], 0)

        @pl.when((j == 0) & (f == 0))
        def _():
            acc_ref[...] = jnp.zeros(acc_ref.shape, F32)

        pm = tm // n_parts
        parts = [slice(p * pm, (p + 1) * pm) for p in range(n_parts)]
        for s in range(n_sub):
            @pl.when((f == 0) & (s < prev_tiles))
            def _(s=s):
                tile_write(s, 0).wait()

            @pl.when(s < nv_ref[j])
            def _(s=s):
                def gate_up(p):
                    xb = xs_refs[s][parts[p], :].astype(BF16)
                    return _dot(xb, wg_ref[0]), _dot(xb, wu_ref[0])

                ahead = gate_up(0)
                for p, ps in enumerate(parts):
                    (gate, up), ahead = ahead, (gate_up(p + 1) if p + 1 < n_parts else None)
                    h = (_silu(gate) * up).astype(BF16)
                    part = _dot(h, wd_ref[0])
                    acc_ref[s, ps, :] = jnp.where(f == 0, part, acc_ref[s, ps, :] + part)

                @pl.when(f == last_f)
                def _():
                    tile_write(s, wb_ref[j] + s).start()

        @pl.when((f == last_f) & (j == n_windows - 1))
        def _():
            for s in range(n_sub):
                @pl.when(s < nv_ref[j])
                def _(s=s):
                    tile_write(s, 0).wait()

            acc_ref[0] = jnp.zeros(acc_ref.shape[1:], F32)
            total_tiles = ys_hbm.shape[0] // tm
            lax.fori_loop(n_tiles, total_tiles, lambda t, c: (tile_write(0, t).start(), c)[1], 0)
            lax.fori_loop(n_tiles, total_tiles, lambda t, c: (tile_write(0, 0).wait(), c)[1], 0)


def _moe_experts(xs, wg, wu, wd, win_expert, win_block, win_tiles, counts):
    n_rows = xs.shape[0]
    tm, tf, n_sub = TM_MOE, TF_MOE, MOE_TILES_PER_WINDOW
    max_windows = win_expert.shape[0]

    def xs_spec(s):
        return pl.BlockSpec((tm, D_MODEL), lambda j, f, we, wb, nv, cnt: (wb[j] + jnp.minimum(s, nv[j] - 1), 0))

    grid_spec = pltpu.PrefetchScalarGridSpec(
        num_scalar_prefetch=4,
        grid=(max_windows, EXPERT_DIM // tf),
        in_specs=[xs_spec(s) for s in range(n_sub)] + [
            pl.BlockSpec((1, D_MODEL, tf), lambda j, f, we, wb, nv, cnt: (we[j], 0, f)),
            pl.BlockSpec((1, D_MODEL, tf), lambda j, f, we, wb, nv, cnt: (we[j], 0, f)),
            pl.BlockSpec((1, tf, D_MODEL), lambda j, f, we, wb, nv, cnt: (we[j], f, 0))],
        out_specs=pl.BlockSpec(memory_space=pl.ANY),
        scratch_shapes=[pltpu.VMEM((n_sub, tm, D_MODEL), F32), pltpu.SemaphoreType.DMA((n_sub,))],
    )
    return pl.pallas_call(
        functools.partial(_moe_kernel, n_sub=n_sub, tm=tm, n_parts=MOE_PARTS),
        grid_spec=grid_spec,
        out_shape=jax.ShapeDtypeStruct((n_rows, D_MODEL), F32),
        compiler_params=_params("arbitrary", "arbitrary"),
        name="moe_experts",
    )(win_expert, win_block, win_tiles, counts, *([xs] * n_sub), wg, wu, wd)


def _combine_kernel(pos0_ref, pos1_ref, pos0n_ref, pos1n_ref, x_ref, route_ref, ys_hbm, g_ref, b_ref, o_ref,
                    y_ref, sem, *, tc):
    i = pl.program_id(0)
    slot = i % 2

    def issue(p_refs, s):
        def start(t, carry):
            for kk in range(2):
                pltpu.make_async_copy(ys_hbm.at[pl.ds(p_refs[kk][0, 0, t], 1)],
                                      y_ref.at[s, kk, pl.ds(t, 1)], sem.at[s]).start(priority=kk)
            return carry

        lax.fori_loop(0, tc, start, 0, unroll=DMA_UNROLL)

    @pl.when(i == 0)
    def _():
        issue((pos0_ref, pos1_ref), 0)

    @pl.when(i + 1 < pl.num_programs(0))
    def _():
        issue((pos0n_ref, pos1n_ref), 1 - slot)

    for kk in range(2):
        pltpu.make_async_copy(ys_hbm.at[pl.ds(0, tc)], y_ref.at[slot, kk], sem.at[slot]).wait()

    route = route_ref[...]
    f = route[:, 4:5] * y_ref[slot, 0] + route[:, 5:6] * y_ref[slot, 1]
    o_ref[...] = _layer_norm(ALPHA * x_ref[...] + f, g_ref[...], b_ref[...])


def _combine_ln(x, route, pos, ys, g, b):
    n = x.shape[0]
    tc = TC_COMBINE
    nt = n // tc
    pos0, pos1 = pos[0].reshape(nt, 1, tc), pos[1].reshape(nt, 1, tc)
    cur = pl.BlockSpec((1, 1, tc), lambda i: (i, 0, 0), memory_space=pltpu.SMEM)
    nxt = pl.BlockSpec((1, 1, tc), lambda i: (jnp.minimum(i + 1, nt - 1), 0, 0), memory_space=pltpu.SMEM)
    return pl.pallas_call(
        functools.partial(_combine_kernel, tc=tc),
        grid=(nt,),
        in_specs=[cur, cur, nxt, nxt,
                  pl.BlockSpec((tc, D_MODEL), lambda i: (i, 0)),
                  pl.BlockSpec((tc, LANES), lambda i: (i, 0)),
                  pl.BlockSpec(memory_space=pl.ANY),
                  pl.BlockSpec((1, D_MODEL), lambda i: (0, 0)), pl.BlockSpec((1, D_MODEL), lambda i: (0, 0))],
        out_specs=pl.BlockSpec((tc, D_MODEL), lambda i: (i, 0)),
        out_shape=jax.ShapeDtypeStruct((n, D_MODEL), F32),
        scratch_shapes=[pltpu.VMEM((2, 2, tc, D_MODEL), F32), pltpu.SemaphoreType.DMA((2,))],
        compiler_params=_params("arbitrary"),
        name="moe_combine_ln",
    )(pos0, pos1, pos0, pos1, x, route, ys, g, b)


def _moe_layer(x, route, route_t, counts_f, wg, wu, wd_f32, g, b):
    n = x.shape[0]
    tm = TM_MOE

    counts = counts_f[0, :N_EXPERTS].astype(jnp.int32)
    padded = ((counts + tm - 1) // tm) * tm
    offs = jnp.cumsum(padded) - padded
    eids = jnp.arange(N_EXPERTS, dtype=jnp.int32)
    e01 = route_t[0:2].astype(jnp.int32)
    rank01 = route_t[2:4].astype(jnp.int32)
    base = jnp.sum(jnp.where(e01[:, None, :] == eids[None, :, None], offs[None, :, None], 0), axis=1)
    pos = base + rank01
    counts_up = jnp.minimum(padded, ((counts + ZERO_ROWS - 1) // ZERO_ROWS) * ZERO_ROWS)
    meta = jnp.stack([offs, counts, counts_up, padded]).astype(jnp.int32)

    n_rows = 2 * n + N_EXPERTS * tm
    nsub = MOE_TILES_PER_WINDOW
    tiles = padded // tm
    wins = (tiles + nsub - 1) // nsub
    win_end = jnp.cumsum(wins)
    n_windows = win_end[-1]
    max_windows = (n_rows // tm + nsub - 1) // nsub + N_EXPERTS
    wj = jnp.minimum(jnp.arange(max_windows, dtype=jnp.int32), n_windows - 1)
    win_expert = jnp.minimum(jnp.sum((wj[:, None] >= win_end[None, :]).astype(jnp.int32), axis=-1), N_EXPERTS - 1)
    pick = lambda v: jnp.sum(jnp.where(win_expert[:, None] == eids[None, :], v[None, :], 0), axis=-1)
    local = wj - pick(win_end - wins)
    win_block = pick(offs // tm) + nsub * local
    win_tiles = jnp.minimum(nsub, pick(tiles) - nsub * local)
    counts_nt = jnp.stack([n_windows, jnp.sum(tiles)])

    xs, wd = _dispatch(x, pos, meta, n_rows, wd_f32.reshape(-1, D_MODEL))
    wd = wd.reshape(wd_f32.shape)
    i32 = lambda a: a.astype(jnp.int32)
    ys = _moe_experts(xs, wg, wu, wd, i32(win_expert), i32(win_block), i32(win_tiles), i32(counts_nt))
    return _combine_ln(x, route, pos, ys, g, b)


def kernel(x, a_w_in, a_conv_w, a_a_log, a_dt_bias, a_norm_w, a_w_out, b_w_in, b_b_in, b_sinks, b_w_out, rel_bias,
           ffn_w_gate, ffn_w_up, ffn_w_down, moe_router, moe_w_gate, moe_w_up, moe_w_down, ln_g, ln_b):
    batch, seq_len, _ = x.shape
    n = batch * seq_len
    x0 = x.reshape(n, D_MODEL)
    ln_g = ln_g.reshape(DEPTH, 2, 1, D_MODEL)
    ln_b = ln_b.reshape(DEPTH, 2, 1, D_MODEL)

    pad_gate = lambda p: jnp.pad(p.reshape(1, GDN_V_HEADS), ((0, 0), (GDN_V_HEADS, LANES - 2 * GDN_V_HEADS)))
    gate_params = jnp.concatenate([pad_gate(a_a_log[0]), pad_gate(a_dt_bias[0])], axis=0)
    q, k, v, zs, gb = _gdn_inproj(x0, jnp.swapaxes(a_w_in, 1, 2), a_conv_w[0], gate_params, seq_len)
    o, casted = _gdn_chunk(q, k, v, zs, gb, a_norm_w[0].reshape(1, GDN_HEAD), batch, seq_len,
                           [moe_w_gate[0].reshape(-1, EXPERT_DIM), moe_w_up[0].reshape(-1, EXPERT_DIM),
                            ffn_w_gate[0], ffn_w_up[0], ffn_w_down[0],
                            a_w_out[0], b_w_in[0], b_w_out[0]])
    moe_wg, moe_wu, ffn_wg, ffn_wu, ffn_wd, w_out_a, w_in_b, w_out_b = casted
    x2 = _outproj_ffn(o, w_out_a, x0, ln_g[0, 0], ln_b[0, 0],
                      ffn_wg, ffn_wu, ffn_wd, ln_g[0, 1], ln_b[0, 1])

    bias = _bias_table(rel_bias)
    sinks = jnp.pad(b_sinks[0].reshape(1, SWA_Q_HEADS), ((0, 0), (0, LANES - SWA_Q_HEADS)))
    x3 = _swa_layer(x2, w_in_b, b_b_in[0].reshape(1, -1), bias, sinks, w_out_b, ln_g[1, 0], ln_b[1, 0],
                    batch, seq_len)
    w_router = jnp.pad(moe_router[0], ((0, 0), (0, LANES - N_EXPERTS)))
    route, route_t, counts = _router(x3, w_router)
    expert_shape = (N_EXPERTS, D_MODEL, EXPERT_DIM)
    x4 = _moe_layer(x3, route, route_t, counts, moe_wg.reshape(expert_shape), moe_wu.reshape(expert_shape),
                    moe_w_down[0], ln_g[1, 1], ln_b[1, 1])
    return x4.reshape(batch, seq_len, D_MODEL)
```

```python
import functools
import math

import numpy as np
import jax
import jax.numpy as jnp
from jax import lax
from jax.experimental import pallas as pl
from jax.experimental.pallas import tpu as pltpu

F32 = jnp.float32
BF16 = jnp.bfloat16

D_MODEL = 1024
DEPTH = 2
ALPHA = (2.0 * DEPTH) ** 0.25
LN_EPS = 1e-5

GDN_K_HEADS = 4
GDN_V_HEADS = 8
GDN_HEAD = 128
GDN_KDIM = GDN_K_HEADS * GDN_HEAD
GDN_VDIM = GDN_V_HEADS * GDN_HEAD
GDN_CONV = 4
GDN_CHUNK = 64
GDN_QKV = 2 * GDN_KDIM + GDN_VDIM
GDN_EPS = 1e-6

SWA_Q_HEADS = 16
SWA_KV_HEADS = 2
SWA_GROUP = SWA_Q_HEADS // SWA_KV_HEADS
SWA_HEAD_DIM = 64
SWA_WINDOW = 128
SWA_BLOCK = 128
SWA_QDIM = SWA_Q_HEADS * SWA_HEAD_DIM
SWA_KVDIM = SWA_KV_HEADS * SWA_HEAD_DIM
REL_BUCKETS = 32
REL_MAX_DIST = 128

FFN_DIM = 2816
N_EXPERTS = 8
EXPERT_DIM = 3584

LANES = 128
SUBLANES = 8
NEG_BIG = -1e30
VMEM_LIMIT = 56 * 1024 * 1024

TM_GDN_IN = 512
TM_FFN = 1024
FFN_PARTS = 4
SWA_BLOCKS_PER_STEP = 8
SWA_PARTS = 2
TM_ROUTER = 2048
ROUTER_PARTS = 8
TM_MOE = 512
TF_MOE = 1792
MOE_TILES_PER_WINDOW = 3
MOE_PARTS = 2
TD_DISPATCH = 1024
TC_COMBINE = 512
ZERO_ROWS = 64
DMA_UNROLL = 8


def _params(*sem):
    return pltpu.CompilerParams(dimension_semantics=sem, vmem_limit_bytes=VMEM_LIMIT)


def _dot(a, b):
    return jnp.dot(a, b, preferred_element_type=F32)


def _dot_nt(a, b):
    return lax.dot_general(a, b, (((1,), (1,)), ((), ())), preferred_element_type=F32)


def _dot_tn(a, b):
    return lax.dot_general(a, b, (((0,), (0,)), ((), ())), preferred_element_type=F32)


def _split(x):
    hi = x.astype(BF16)
    lo = (x - hi.astype(F32)).astype(BF16)
    return hi, lo


def _dot3(a, b):
    ah, al = _split(a)
    bh, bl = _split(b)
    return _dot(ah, bh) + (_dot(ah, bl) + _dot(al, bh))


def _silu(x):
    return x * jax.nn.sigmoid(x)


def _layer_norm(y, g, b):
    mu = jnp.mean(y, axis=-1, keepdims=True)
    yc = y - mu
    var = jnp.mean(yc * yc, axis=-1, keepdims=True)
    return yc * lax.rsqrt(var + LN_EPS) * g + b


def _gdn_inproj_kernel(x_ref, wf_ref, convw_ref, gp_ref,
                       q_ref, k_ref, v_ref, z_ref, gb_ref, ext_ref, w_ref, *, tm, tiles_per_seq):
    i = pl.program_id(0)
    xb = x_ref[...].astype(BF16)

    @pl.when(i == 0)
    def _():
        n_in = wf_ref.shape[1]
        n_full = n_in // LANES
        for c in range(n_full):
            cs = slice(c * LANES, (c + 1) * LANES)
            w_ref[:, cs] = wf_ref[0, cs, :].T.astype(BF16)
        tail = jnp.concatenate([wf_ref[0, n_full * LANES:, :],
                                jnp.zeros(((n_full + 1) * LANES - n_in, wf_ref.shape[2]), F32)], axis=0)
        w_ref[:, n_full * LANES:] = tail.T.astype(BF16)

    @pl.when(i % tiles_per_seq == 0)
    def _():
        ext_ref[0:SUBLANES, :] = jnp.zeros((SUBLANES, GDN_QKV), F32)

    @pl.when(i % tiles_per_seq != 0)
    def _():
        ext_ref[0:SUBLANES, :] = ext_ref[tm:tm + SUBLANES, :]

    ext_ref[SUBLANES:, :] = _dot(xb, w_ref[:, :GDN_QKV])
    z_ref[...] = _dot(xb, w_ref[:, GDN_QKV:GDN_QKV + GDN_VDIM])

    n_chunks = GDN_QKV // LANES
    for c in range(n_chunks):
        cs = slice(c * LANES, (c + 1) * LANES)
        acc = convw_ref[GDN_CONV - 1:GDN_CONV, cs] * ext_ref[SUBLANES:SUBLANES + tm, cs]
        for j in range(GDN_CONV - 1):
            off = SUBLANES - (GDN_CONV - 1) + j
            acc = acc + convw_ref[j:j + 1, cs] * ext_ref[off:off + tm, cs]
        y = _silu(acc)
        if c < 2 * GDN_K_HEADS:
            y = y * lax.rsqrt(jnp.sum(y * y, axis=-1, keepdims=True) + GDN_EPS)
            if c < GDN_K_HEADS:
                q_ref[:, cs] = y * (GDN_HEAD ** -0.5)
            else:
                k_ref[:, (c - GDN_K_HEADS) * LANES:(c - GDN_K_HEADS + 1) * LANES] = y
        else:
            cv = c - 2 * GDN_K_HEADS
            v_ref[:, cv * LANES:(cv + 1) * LANES] = y

    ba = _dot(xb, w_ref[:, GDN_QKV + GDN_VDIM:])
    lane = lax.broadcasted_iota(jnp.int32, ba.shape, 1)
    sp = ba + gp_ref[1:2, :]
    softplus = jnp.maximum(sp, 0.0) + jnp.log(1.0 + jnp.exp(-jnp.abs(sp)))
    g = -jnp.exp(gp_ref[0:1, :]) * softplus
    gb_ref[...] = jnp.where(lane < GDN_V_HEADS, jax.nn.sigmoid(ba), g)


def _gdn_inproj(x2d, w_in, conv_w, gate_params, seq_len):
    n = x2d.shape[0]
    tm = TM_GDN_IN
    kern = functools.partial(_gdn_inproj_kernel, tm=tm, tiles_per_seq=seq_len // tm)
    full = lambda shape: pl.BlockSpec(shape, lambda i: (0,) * len(shape))
    rows = lambda width: pl.BlockSpec((tm, width), lambda i: (i, 0))
    return pl.pallas_call(
        kern,
        grid=(n // tm,),
        in_specs=[rows(D_MODEL), pl.BlockSpec(w_in.shape, lambda i: (0, 0, 0), pipeline_mode=pl.Buffered(1)),
                  full(conv_w.shape), full(gate_params.shape)],
        out_specs=[rows(GDN_KDIM), rows(GDN_KDIM), rows(GDN_VDIM), rows(GDN_VDIM), rows(LANES)],
        out_shape=[jax.ShapeDtypeStruct((n, GDN_KDIM), F32), jax.ShapeDtypeStruct((n, GDN_KDIM), F32),
                   jax.ShapeDtypeStruct((n, GDN_VDIM), F32), jax.ShapeDtypeStruct((n, GDN_VDIM), F32),
                   jax.ShapeDtypeStruct((n, LANES), F32)],
        scratch_shapes=[pltpu.VMEM((tm + SUBLANES, GDN_QKV), F32),
                        pltpu.VMEM((D_MODEL, GDN_QKV + GDN_VDIM + LANES), BF16)],
        compiler_params=_params("arbitrary"),
        name="gdn_inproj",
    )(x2d, w_in, conv_w, gate_params)


GDN_CHUNKS_PER_STEP = 2
NEUMANN_BLOCK = 4


def _bf16_all(xs):
    return [x.astype(BF16) for x in xs]


def _dot_all(a_list, b_list):
    return [_dot(a, b) for a, b in zip(a_list, b_list)]


def _unit_lower_inverse_all(a_list, row, col):
    shift = int(math.log2(NEUMANN_BLOCK))
    eye = jnp.where(row == col, 1.0, 0.0).astype(F32)
    on_diag_block = (row >> shift) == (col >> shift)
    d = [jnp.where(on_diag_block, a, 0.0) for a in a_list]
    d_b = _bf16_all(d)
    x = [eye - di for di in d]
    d2_b = _bf16_all(_dot_all(d_b, d_b))
    x = [xi + t for xi, t in zip(x, _dot_all(_bf16_all(x), d2_b))]
    size = NEUMANN_BLOCK
    while size < a_list[0].shape[0]:
        shift = int(math.log2(size))
        rbl, cbl = row >> shift, col >> shift
        below = ((rbl & 1) == 1) & (cbl == rbl - 1)
        l_b = _bf16_all([jnp.where(below, a, 0.0) for a in a_list])
        x_b = _bf16_all(x)
        xl_b = _bf16_all(_dot_all(x_b, l_b))
        x = [xi - t for xi, t in zip(x, _dot_all(xl_b, x_b))]
        size *= 2
    return x


def _gdn_chunk_kernel(q_ref, k_ref, v_ref, z_ref, gb_ref, nw_ref, *rest, nb, nck, n_cast):
    cast_src, (o_ref, *cast_dst), s_ref = rest[:n_cast], rest[n_cast:2 * n_cast + 1], rest[-1]
    for src, dst in zip(cast_src, cast_dst):
        dst[...] = src[...].astype(BF16)
    c = GDN_CHUNK
    nh = GDN_V_HEADS

    @pl.when(pl.program_id(0) == 0)
    def _():
        s_ref[...] = jnp.zeros(s_ref.shape, F32)

    row = lax.broadcasted_iota(jnp.int32, (c, c), 0)
    col = lax.broadcasted_iota(jnp.int32, (c, c), 1)
    causal = row >= col
    strict = row > col
    tril = jnp.where(causal, 1.0, 0.0).astype(BF16)
    rs = lambda ci: slice(ci * c, (ci + 1) * c)
    ks = lambda h: slice((h // (nh // GDN_K_HEADS)) * GDN_HEAD, (h // (nh // GDN_K_HEADS) + 1) * GDN_HEAD)
    vs = lambda h: slice(h * GDN_HEAD, (h + 1) * GDN_HEAD)

    gates = {}
    for ci in range(nck):
        for b in range(nb):
            gb = gb_ref[b, rs(ci), :]
            g_hi, g_lo = _split(gb)
            gc = _dot(tril, g_hi) + _dot(tril, g_lo)
            g_last = gc[c - 1:c, :]
            gates[ci, b] = dict(gb=gb, gc=gc, gc_t=gc.T, eg=jnp.exp(gc), e_last=jnp.exp(g_last),
                                e_rest=jnp.exp(g_last - gc))

    items = [(ci, b, h) for ci in range(nck) for b in range(nb) for h in range(nh)]

    a_kk, a_qk, k_beta = [], {}, {}
    for it in items:
        ci, b, h = it
        gt = gates[ci, b]
        k = k_ref[b, rs(ci), ks(h)]
        kb = k * gt["gb"][:, h:h + 1]
        kq = jnp.concatenate([kb, q_ref[b, rs(ci), ks(h)]], axis=0).astype(BF16)
        gram = _dot_nt(kq, k.astype(BF16))
        gl = nh + h
        decay = jnp.exp(jnp.where(causal, gt["gc"][:, gl:gl + 1] - gt["gc_t"][gl:gl + 1, :], NEG_BIG))
        a_kk.append(jnp.where(strict, gram[:c] * decay, 0.0))
        a_qk[it] = (gram[c:] * decay).astype(BF16)
        k_beta[it] = kb

    t_inv = dict(zip(items, _unit_lower_inverse_all(a_kk, row, col)))

    uw = {}
    for it in items:
        ci, b, h = it
        gt = gates[ci, b]
        gl = nh + h
        rhs = jnp.concatenate([v_ref[b, rs(ci), vs(h)] * gt["gb"][:, h:h + 1],
                               k_beta[it] * gt["eg"][:, gl:gl + 1]], axis=1)
        uw[it] = _dot(t_inv[it].astype(BF16), rhs.astype(BF16))

    for ci in range(nck):
        chunk_items = [(ci, b, h) for b in range(nb) for h in range(nh)]

        ws_qs, states = {}, {}
        for it in chunk_items:
            _, b, h = it
            gl = nh + h
            s = s_ref[b * nh + h]
            wq = jnp.concatenate([uw[it][:, GDN_HEAD:],
                                  q_ref[b, rs(ci), ks(h)] * gates[ci, b]["eg"][:, gl:gl + 1]], axis=0)
            ws_qs[it] = _dot(wq.astype(BF16), s.astype(BF16))
            states[it] = s

        for it in chunk_items:
            _, b, h = it
            gt = gates[ci, b]
            gl = nh + h
            v_new = (uw[it][:, :GDN_HEAD] - ws_qs[it][:c]).astype(BF16)
            o = ws_qs[it][c:] + _dot(a_qk[it], v_new)
            k_dec = (k_ref[b, rs(ci), ks(h)] * gt["e_rest"][:, gl:gl + 1]).astype(BF16)
            s_ref[b * nh + h] = states[it] * gt["e_last"][:, gl:gl + 1] + _dot_tn(k_dec, v_new)
            o = o * lax.rsqrt(jnp.mean(o * o, axis=-1, keepdims=True) + GDN_EPS) * nw_ref[...]
            o_ref[b, rs(ci), vs(h)] = (o * _silu(z_ref[b, rs(ci), vs(h)])).astype(o_ref.dtype)


def _cast_specs(arrays, n_steps):
    specs, shapes = [], []
    for a in arrays:
        span = next(d for d in (1, 2, 4, 8)
                    if (a.shape[0] * d) % n_steps == 0 and (a.shape[0] * d // n_steps) % (2 * SUBLANES) == 0)
        specs.append(pl.BlockSpec((a.shape[0] * span // n_steps, a.shape[1]), lambda j, span=span: (j // span, 0)))
        shapes.append(jax.ShapeDtypeStruct(a.shape, BF16))
    return specs, shapes


def _gdn_chunk(q, k, v, zs, gb, norm_w, batch, seq_len, to_bf16):
    rows_per_step = GDN_CHUNK * GDN_CHUNKS_PER_STEP
    n_steps = seq_len // rows_per_step
    seq = lambda a: a.reshape(batch, seq_len, a.shape[-1])
    rows = lambda width: pl.BlockSpec((batch, rows_per_step, width), lambda j: (0, j, 0))
    cast_specs, cast_shapes = _cast_specs(to_bf16, n_steps)
    out, *casted = pl.pallas_call(
        functools.partial(_gdn_chunk_kernel, nb=batch, nck=GDN_CHUNKS_PER_STEP, n_cast=len(to_bf16)),
        grid=(n_steps,),
        in_specs=[rows(GDN_KDIM), rows(GDN_KDIM), rows(GDN_VDIM), rows(GDN_VDIM), rows(LANES),
                  pl.BlockSpec((1, GDN_HEAD), lambda j: (0, 0))] + cast_specs,
        out_specs=[rows(GDN_VDIM)] + cast_specs,
        out_shape=[jax.ShapeDtypeStruct((batch, seq_len, GDN_VDIM), BF16)] + cast_shapes,
        scratch_shapes=[pltpu.VMEM((batch * GDN_V_HEADS, GDN_HEAD, GDN_HEAD), F32)],
        compiler_params=_params("arbitrary"),
        name="gdn_chunk",
    )(seq(q), seq(k), seq(v), seq(zs), seq(gb), norm_w, *to_bf16)
    return out.reshape(batch * seq_len, GDN_VDIM), casted


def _outproj_ffn_kernel(a_ref, wo_ref, r_ref, g0_ref, b0_ref, wg_ref, wu_ref, wd_ref, g1_ref, b1_ref, o_ref, *,
                        n_parts):
    pm = o_ref.shape[0] // n_parts
    parts = [slice(p * pm, (p + 1) * pm) for p in range(n_parts)]
    x1 = [_layer_norm(ALPHA * r_ref[ps, :] + _dot(a_ref[ps, :], wo_ref[...]), g0_ref[...], b0_ref[...])
          for ps in parts]
    xb = [x.astype(BF16) for x in x1]
    gate_up = lambda p: (_dot(xb[p], wg_ref[...]), _dot(xb[p], wu_ref[...]))
    ahead = gate_up(0)
    for p, ps in enumerate(parts):
        (gate, up), ahead = ahead, (gate_up(p + 1) if p + 1 < n_parts else None)
        h = (_silu(gate) * up).astype(BF16)
        o_ref[ps, :] = _layer_norm(ALPHA * x1[p] + _dot(h, wd_ref[...]), g1_ref[...], b1_ref[...])


def _outproj_ffn(a, w_out, res, g0, b0, wg, wu, wd, g1, b1):
    n, kdim = a.shape
    tm = TM_FFN
    resident = lambda w: pl.BlockSpec(w.shape, lambda i: (0, 0), pipeline_mode=pl.Buffered(1))
    rows = lambda width: pl.BlockSpec((tm, width), lambda i: (i, 0))
    vec = pl.BlockSpec((1, D_MODEL), lambda i: (0, 0))
    return pl.pallas_call(
        functools.partial(_outproj_ffn_kernel, n_parts=FFN_PARTS),
        grid=(n // tm,),
        in_specs=[rows(kdim), resident(w_out), rows(D_MODEL), vec, vec,
                  resident(wg), resident(wu), resident(wd), vec, vec],
        out_specs=rows(D_MODEL),
        out_shape=jax.ShapeDtypeStruct((n, D_MODEL), F32),
        compiler_params=_params("parallel"),
        name="gdn_outproj_ffn",
    )(a, w_out, res, g0, b0, wg, wu, wd, g1, b1)


def _band_tables():
    assert SWA_WINDOW == SWA_BLOCK
    r = np.arange(SWA_BLOCK)[:, None]
    q = np.arange(SWA_BLOCK)[None, :]
    d = (q - r) % SWA_BLOCK
    max_exact = REL_BUCKETS // 2
    df = np.maximum(d, 1).astype(np.float32)
    large = max_exact + (np.log(df / np.float32(max_exact)) / np.float32(math.log(REL_MAX_DIST / max_exact))
                         * np.float32(REL_BUCKETS - max_exact)).astype(np.int32)
    large = np.minimum(large, REL_BUCKETS - 1)
    bucket = np.where(d < max_exact, d, large).astype(np.int32)
    valid = np.stack([q >= r, np.ones_like(q >= r)]).astype(np.int32)
    return bucket, valid


def _bias_kernel(relb_ref, bucket_ref, valid_ref, o_ref):
    bucket = bucket_ref[...]
    in_bucket = [bucket == b for b in range(REL_BUCKETS)]
    for h in range(SWA_Q_HEADS):
        acc = jnp.zeros(bucket.shape, F32)
        for b in range(REL_BUCKETS):
            acc = jnp.where(in_bucket[b], relb_ref[b, h], acc)
        for t in range(2):
            o_ref[t, h] = jnp.where(valid_ref[t] != 0, acc, NEG_BIG)


def _bias_table(rel_bias):
    bucket, valid = _band_tables()
    shape = (SWA_BLOCK, SWA_BLOCK)
    return pl.pallas_call(
        _bias_kernel,
        grid=(1,),
        in_specs=[pl.BlockSpec(memory_space=pltpu.SMEM), pl.BlockSpec(shape, lambda i: (0, 0)),
                  pl.BlockSpec((2,) + shape, lambda i: (0, 0, 0))],
        out_specs=pl.BlockSpec((2, SWA_Q_HEADS) + shape, lambda i: (0, 0, 0, 0)),
        out_shape=jax.ShapeDtypeStruct((2, SWA_Q_HEADS) + shape, F32),
        compiler_params=_params("arbitrary"),
        name="swa_bias_table",
    )(rel_bias, jnp.asarray(bucket), jnp.asarray(valid))


def _band_attention(q_pair, kv, bias_of, sink_ref, put_pair):
    dh = SWA_HEAD_DIM
    blk = SWA_BLOCK
    nk = 2 * blk
    from_prev = lax.broadcasted_iota(jnp.int32, (blk, blk), 1) < lax.broadcasted_iota(jnp.int32, (blk, blk), 0)
    k_all = kv[:, :SWA_KVDIM]
    lane = lax.broadcasted_iota(jnp.int32, k_all.shape, 1)
    zero = jnp.zeros_like(k_all)
    for kh in range(SWA_KV_HEADS):
        k_own = jnp.where((lane >= kh * dh) & (lane < (kh + 1) * dh), k_all, zero)
        k_other = pltpu.roll(k_own.astype(F32), dh, axis=1).astype(BF16)
        k_even, k_odd = (k_own, k_other) if kh == 0 else (k_other, k_own)
        k2 = jnp.concatenate([k_even, k_odd], axis=0)
        v_t = kv[:, SWA_KVDIM + kh * dh:SWA_KVDIM + (kh + 1) * dh].T
        pairs = [kh * (SWA_GROUP // 2) + p for p in range(SWA_GROUP // 2)]
        heads = [2 * p + r for p in pairs for r in range(2)]
        st = [_dot_nt(k2, q_pair(p)) for p in pairs]
        scores = [jnp.where(from_prev, st[i // 2][(i % 2) * nk:(i % 2) * nk + blk],
                            st[i // 2][(i % 2) * nk + blk:(i % 2 + 1) * nk]) + bias_of(h)
                  for i, h in enumerate(heads)]
        sinks = [sink_ref[0:1, h:h + 1] for h in heads]
        maxes = [jnp.maximum(jnp.max(s, axis=0, keepdims=True), sk) for s, sk in zip(scores, sinks)]
        expo = [jnp.exp(s - m) for s, m in zip(scores, maxes)]
        sums = [jnp.sum(p, axis=0, keepdims=True) for p in expo]
        inv_den = [1.0 / (t + jnp.exp(sk - m)) for t, sk, m in zip(sums, sinks, maxes)]
        expo = [jnp.concatenate([jnp.where(from_prev, p, 0.0), jnp.where(from_prev, 0.0, p)], axis=0) for p in expo]
        for i, p in enumerate(pairs):
            o_t = _dot(v_t, jnp.concatenate(expo[2 * i:2 * i + 2], axis=1).astype(BF16))
            z = jnp.concatenate([o_t[:, :SWA_BLOCK] * inv_den[2 * i], o_t[:, SWA_BLOCK:] * inv_den[2 * i + 1]],
                                axis=0)
            put_pair(p, z.T)


def _swa_layer_kernel(x_ref, win_ref, bin_ref, bias_ref, sink_ref, wout_ref, g_ref, b_ref, o_ref,
                      q_ref, kv_ref, att_ref, *, nblk):
    j = pl.program_id(1)
    blk = SWA_BLOCK
    tm = nblk * blk
    dh2 = 2 * SWA_HEAD_DIM

    @pl.when(j == 0)
    def _():
        kv_ref[0:blk, :] = jnp.zeros((blk, 2 * SWA_KVDIM), BF16)

    @pl.when(j != 0)
    def _():
        kv_ref[0:blk, :] = kv_ref[tm:tm + blk, :]

    part_blocks = nblk // SWA_PARTS
    pm = part_blocks * blk
    for part in range(SWA_PARTS):
        ps = slice(part * pm, (part + 1) * pm)
        p = _dot(x_ref[ps, :].astype(BF16), win_ref[...]) + bin_ref[...]
        q_ref[ps, :] = (p[:, :SWA_QDIM] * (SWA_HEAD_DIM ** -0.5)).astype(BF16)
        kv_ref[blk + part * pm:blk + (part + 1) * pm, :] = p[:, SWA_QDIM:].astype(BF16)

    for i in range(nblk):
        rows = slice(i * blk, (i + 1) * blk)
        table = jnp.where(j == 0, 0, 1) if i == 0 else 1

        def put_pair(pair, o, rows=rows):
            att_ref[rows, pair * dh2:(pair + 1) * dh2] = o.astype(att_ref.dtype)

        _band_attention(lambda pair, rows=rows: q_ref[rows, pair * dh2:(pair + 1) * dh2],
                        kv_ref[i * blk:(i + 2) * blk, :],
                        lambda h, table=table: bias_ref[table, h], sink_ref, put_pair)

        if (i + 1) % part_blocks == 0:
            ps = slice((i + 1) * blk - pm, (i + 1) * blk)
            o_ref[ps, :] = _layer_norm(ALPHA * x_ref[ps, :] + _dot(att_ref[ps, :], wout_ref[...]),
                                       g_ref[...], b_ref[...])


def _swa_layer(x, w_in, b_in, bias, sinks, w_out, g, b, batch, seq_len):
    n = x.shape[0]
    nblk = SWA_BLOCKS_PER_STEP
    tm = nblk * SWA_BLOCK
    steps = seq_len // tm
    resident = lambda a: pl.BlockSpec(a.shape, lambda s, j: (0,) * a.ndim, pipeline_mode=pl.Buffered(1))
    rows = pl.BlockSpec((tm, D_MODEL), lambda s, j: (s * steps + j, 0))
    return pl.pallas_call(
        functools.partial(_swa_layer_kernel, nblk=nblk),
        grid=(batch, steps),
        in_specs=[rows, resident(w_in), resident(b_in), resident(bias), resident(sinks), resident(w_out),
                  resident(g), resident(b)],
        out_specs=rows,
        out_shape=jax.ShapeDtypeStruct((n, D_MODEL), F32),
        scratch_shapes=[pltpu.VMEM((tm, SWA_QDIM), BF16), pltpu.VMEM((tm + SWA_BLOCK, 2 * SWA_KVDIM), BF16),
                        pltpu.VMEM((tm, SWA_QDIM), BF16)],
        compiler_params=_params("parallel", "arbitrary"),
        name="swa_layer",
    )(x, w_in, b_in, bias, sinks, w_out, g, b)


def _router_kernel(x_ref, wr_ref, route_ref, route_t_ref, cnt_ref, run_ref, *, tm, n_parts):
    @pl.when(pl.program_id(0) == 0)
    def _():
        run_ref[...] = jnp.zeros(run_ref.shape, F32)

    pm = tm // n_parts
    parts = [slice(p * pm, (p + 1) * pm) for p in range(n_parts)]
    lane = lax.broadcasted_iota(jnp.int32, (pm, LANES), 1)
    lane_f = lane.astype(F32)

    wh, wl = _split(wr_ref[...])
    w_cat = jnp.concatenate([wh, wl], axis=1)
    logits = []
    for ps in parts:
        xh, xl = _split(x_ref[ps, :])
        hi = _dot(xh, w_cat)
        logits.append(hi[:, :LANES] + (hi[:, LANES:] + _dot(xl, wh)))

    picks = []
    for lgt in logits:
        lg = jnp.where(lane < N_EXPERTS, lgt, NEG_BIG)
        m1 = jnp.max(lg, axis=-1, keepdims=True)
        i1 = jnp.min(jnp.where(lg == m1, lane_f, float(LANES)), axis=-1, keepdims=True)
        oh1 = lane_f == i1
        lg2 = jnp.where(oh1, NEG_BIG, lg)
        m2 = jnp.max(lg2, axis=-1, keepdims=True)
        i2 = jnp.min(jnp.where(lg2 == m2, lane_f, float(LANES)), axis=-1, keepdims=True)
        oh2 = lane_f == i2
        e = jnp.exp(m2 - m1)
        w0 = 1.0 / (1.0 + e)
        cnt = jnp.where(oh1, 1.0, 0.0) + jnp.where(oh2, 1.0, 0.0)
        picks.append((i1, i2, oh1, oh2, w0, e * w0, cnt))

    r = lax.broadcasted_iota(jnp.int32, (pm, pm), 0)
    c = lax.broadcasted_iota(jnp.int32, (pm, pm), 1)
    before = jnp.where(r > c, 1.0, 0.0).astype(BF16)
    within = [_dot(before, pk[-1].astype(BF16)) for pk in picks]

    run = run_ref[...]
    for ps, (i1, i2, oh1, oh2, w0, w1, cnt), earlier in zip(parts, picks, within):
        excl = earlier + run
        rank0 = jnp.sum(jnp.where(oh1, excl, 0.0), axis=-1, keepdims=True)
        rank1 = jnp.sum(jnp.where(oh2, excl, 0.0), axis=-1, keepdims=True)
        run = run + jnp.sum(cnt, axis=0, keepdims=True)
        out = jnp.zeros((pm, LANES), F32)
        for idx, val in enumerate((i1, i2, rank0, rank1, w0, w1)):
            out = jnp.where(lane == idx, val, out)
        route_ref[ps, :] = out
        route_t_ref[:, ps] = out.T[:SUBLANES, :]
    run_ref[...] = run
    cnt_ref[...] = run


def _router(x, w_router_padded):
    n = x.shape[0]
    tm = TM_ROUTER
    return pl.pallas_call(
        functools.partial(_router_kernel, tm=tm, n_parts=ROUTER_PARTS),
        grid=(n // tm,),
        in_specs=[pl.BlockSpec((tm, D_MODEL), lambda i: (i, 0)), pl.BlockSpec((D_MODEL, LANES), lambda i: (0, 0))],
        out_specs=[pl.BlockSpec((tm, LANES), lambda i: (i, 0)), pl.BlockSpec((SUBLANES, tm), lambda i: (0, i)),
                   pl.BlockSpec((1, LANES), lambda i: (0, 0))],
        out_shape=[jax.ShapeDtypeStruct((n, LANES), F32), jax.ShapeDtypeStruct((SUBLANES, n), F32),
                   jax.ShapeDtypeStruct((1, LANES), F32)],
        scratch_shapes=[pltpu.VMEM((1, LANES), F32)],
        compiler_params=_params("arbitrary"),
        name="moe_router",
    )(x, w_router_padded)


def _dispatch_kernel(pos0_ref, pos1_ref, meta_ref, x_ref, wsrc_ref, xs_hbm, wdst_ref, zero_ref, sem, pad_sem, *, td):
    i = pl.program_id(0)
    wdst_ref[...] = wsrc_ref[...].astype(BF16)

    def row_copy(t, dst):
        return pltpu.make_async_copy(x_ref.at[pl.ds(t, 1)], xs_hbm.at[pl.ds(dst, 1)], sem)

    def start(t, carry):
        row_copy(t, pos0_ref[0, 0, t]).start()
        row_copy(t, pos1_ref[0, 0, t]).start(priority=1)
        return carry

    lax.fori_loop(0, td, start, 0, unroll=DMA_UNROLL)

    @pl.when(i == pl.num_programs(0) - 1)
    def _():
        zero_ref[...] = jnp.zeros(zero_ref.shape, F32)

        zr = zero_ref.shape[0]

        def row_zero(dst):
            return pltpu.make_async_copy(zero_ref.at[pl.ds(0, 1)], xs_hbm.at[pl.ds(dst, 1)], pad_sem)

        def block_zero(blk):
            return pltpu.make_async_copy(zero_ref, xs_hbm.at[pl.ds(pl.multiple_of(blk * zr, zr), zr)], pad_sem)

        def zero_range(copy, lo, hi):
            lax.fori_loop(lo, hi, lambda r, carry: (copy(r).start(), carry)[1], 0)
            lax.fori_loop(lo, hi, lambda r, carry: (copy(0).wait(), carry)[1], 0)

        for e in range(N_EXPERTS):
            off, cnt, cnt_up, padded = (meta_ref[r, e] for r in range(4))
            zero_range(lambda r, off=off: row_zero(off + r), cnt, cnt_up)
            zero_range(block_zero, (off + cnt_up) // zr, (off + padded) // zr)

        used = meta_ref[0, N_EXPERTS - 1] + meta_ref[3, N_EXPERTS - 1]
        zero_range(block_zero, used // zr, xs_hbm.shape[0] // zr)

    for _ in range(2):
        pltpu.make_async_copy(x_ref, xs_hbm.at[pl.ds(0, td)], sem).wait()


def _dispatch(x, pos, meta, n_rows, to_bf16):
    n = x.shape[0]
    td = TD_DISPATCH
    pos_spec = pl.BlockSpec((1, 1, td), lambda i: (i, 0, 0), memory_space=pltpu.SMEM)
    (cast_spec,), (cast_shape,) = _cast_specs([to_bf16], n // td)
    return pl.pallas_call(
        functools.partial(_dispatch_kernel, td=td),
        grid=(n // td,),
        in_specs=[pos_spec, pos_spec, pl.BlockSpec(memory_space=pltpu.SMEM),
                  pl.BlockSpec((td, D_MODEL), lambda i: (i, 0)), cast_spec],
        out_specs=[pl.BlockSpec(memory_space=pl.ANY), cast_spec],
        out_shape=[jax.ShapeDtypeStruct((n_rows, D_MODEL), F32), cast_shape],
        scratch_shapes=[pltpu.VMEM((ZERO_ROWS, D_MODEL), F32), pltpu.SemaphoreType.DMA, pltpu.SemaphoreType.DMA],
        compiler_params=_params("arbitrary"),
        name="moe_dispatch",
    )(pos[0].reshape(n // td, 1, td), pos[1].reshape(n // td, 1, td), meta, x, to_bf16)


def _moe_kernel(we_ref, wb_ref, nv_ref, cnt_ref, *refs, n_sub, tm, n_parts):
    xs_refs = refs[:n_sub]
    wg_ref, wu_ref, wd_ref, ys_hbm, acc_ref, sem = refs[n_sub:]
    j = pl.program_id(0)
    f = pl.program_id(1)
    last_f = pl.num_programs(1) - 1
    n_windows, n_tiles = cnt_ref[0], cnt_ref[1]

    def tile_write(slot, block):
        return pltpu.make_async_copy(acc_ref.at[slot], ys_hbm.at[pl.ds(pl.multiple_of(block * tm, tm), tm)],
                                     sem.at[slot])

    @pl.when(j < n_windows)
    def _():
        prev_tiles = jnp.where(j > 0, nv_ref[jnp.maximum(j - 1, 0)], 0)

        @pl.when((j == 0) & (f == 0))
        def _():
            acc_ref[...] = jnp.zeros(acc_ref.shape, F32)

        pm = tm // n_parts
        parts = [slice(p * pm, (p + 1) * pm) for p in range(n_parts)]
        for s in range(n_sub):
            @pl.when((f == 0) & (s < prev_tiles))
            def _(s=s):
                tile_write(s, 0).wait()

            @pl.when(s < nv_ref[j])
            def _(s=s):
                def gate_up(p):
                    xb = xs_refs[s][parts[p], :].astype(BF16)
                    return _dot(xb, wg_ref[0]), _dot(xb, wu_ref[0])

                ahead = gate_up(0)
                for p, ps in enumerate(parts):
                    (gate, up), ahead = ahead, (gate_up(p + 1) if p + 1 < n_parts else None)
                    h = (_silu(gate) * up).astype(BF16)
                    part = _dot(h, wd_ref[0])
                    acc_ref[s, ps, :] = jnp.where(f == 0, part, acc_ref[s, ps, :] + part)

                @pl.when(f == last_f)
                def _():
                    tile_write(s, wb_ref[j] + s).start()

        @pl.when((f == last_f) & (j == n_windows - 1))
        def _():
            for s in range(n_sub):
                @pl.when(s < nv_ref[j])
                def _(s=s):
                    tile_write(s, 0).wait()

            acc_ref[0] = jnp.zeros(acc_ref.shape[1:], F32)
            total_tiles = ys_hbm.shape[0] // tm
            lax.fori_loop(n_tiles, total_tiles, lambda t, c: (tile_write(0, t).start(), c)[1], 0)
            lax.fori_loop(n_tiles, total_tiles, lambda t, c: (tile_write(0, 0).wait(), c)[1], 0)


def _moe_experts(xs, wg, wu, wd, win_expert, win_block, win_tiles, counts):
    n_rows = xs.shape[0]
    tm, tf, n_sub = TM_MOE, TF_MOE, MOE_TILES_PER_WINDOW
    max_windows = win_expert.shape[0]

    def xs_spec(s):
        return pl.BlockSpec((tm, D_MODEL), lambda j, f, we, wb, nv, cnt: (wb[j] + jnp.minimum(s, nv[j] - 1), 0))

    grid_spec = pltpu.PrefetchScalarGridSpec(
        num_scalar_prefetch=4,
        grid=(max_windows, EXPERT_DIM // tf),
        in_specs=[xs_spec(s) for s in range(n_sub)] + [
            pl.BlockSpec((1, D_MODEL, tf), lambda j, f, we, wb, nv, cnt: (we[j], 0, f)),
            pl.BlockSpec((1, D_MODEL, tf), lambda j, f, we, wb, nv, cnt: (we[j], 0, f)),
            pl.BlockSpec((1, tf, D_MODEL), lambda j, f, we, wb, nv, cnt: (we[j], f, 0))],
        out_specs=pl.BlockSpec(memory_space=pl.ANY),
        scratch_shapes=[pltpu.VMEM((n_sub, tm, D_MODEL), F32), pltpu.SemaphoreType.DMA((n_sub,))],
    )
    return pl.pallas_call(
        functools.partial(_moe_kernel, n_sub=n_sub, tm=tm, n_parts=MOE_PARTS),
        grid_spec=grid_spec,
        out_shape=jax.ShapeDtypeStruct((n_rows, D_MODEL), F32),
        compiler_params=_params("arbitrary", "arbitrary"),
        name="moe_experts",
    )(win_expert, win_block, win_tiles, counts, *([xs] * n_sub), wg, wu, wd)


def _combine_kernel(pos0_ref, pos1_ref, pos0n_ref, pos1n_ref, x_ref, route_ref, ys_hbm, g_ref, b_ref, o_ref,
                    y_ref, sem, *, tc):
    i = pl.program_id(0)
    slot = i % 2

    def issue(p_refs, s):
        def start(t, carry):
            for kk in range(2):
                pltpu.make_async_copy(ys_hbm.at[pl.ds(p_refs[kk][0, 0, t], 1)],
                                      y_ref.at[s, kk, pl.ds(t, 1)], sem.at[s]).start(priority=kk)
            return carry

        lax.fori_loop(0, tc, start, 0, unroll=DMA_UNROLL)

    @pl.when(i == 0)
    def _():
        issue((pos0_ref, pos1_ref), 0)

    @pl.when(i + 1 < pl.num_programs(0))
    def _():
        issue((pos0n_ref, pos1n_ref), 1 - slot)

    for kk in range(2):
        pltpu.make_async_copy(ys_hbm.at[pl.ds(0, tc)], y_ref.at[slot, kk], sem.at[slot]).wait()

    route = route_ref[...]
    f = route[:, 4:5] * y_ref[slot, 0] + route[:, 5:6] * y_ref[slot, 1]
    o_ref[...] = _layer_norm(ALPHA * x_ref[...] + f, g_ref[...], b_ref[...])


def _combine_ln(x, route, pos, ys, g, b):
    n = x.shape[0]
    tc = TC_COMBINE
    nt = n // tc
    pos0, pos1 = pos[0].reshape(nt, 1, tc), pos[1].reshape(nt, 1, tc)
    cur = pl.BlockSpec((1, 1, tc), lambda i: (i, 0, 0), memory_space=pltpu.SMEM)
    nxt = pl.BlockSpec((1, 1, tc), lambda i: (jnp.minimum(i + 1, nt - 1), 0, 0), memory_space=pltpu.SMEM)
    return pl.pallas_call(
        functools.partial(_combine_kernel, tc=tc),
        grid=(nt,),
        in_specs=[cur, cur, nxt, nxt,
                  pl.BlockSpec((tc, D_MODEL), lambda i: (i, 0)),
                  pl.BlockSpec((tc, LANES), lambda i: (i, 0)),
                  pl.BlockSpec(memory_space=pl.ANY),
                  pl.BlockSpec((1, D_MODEL), lambda i: (0, 0)), pl.BlockSpec((1, D_MODEL), lambda i: (0, 0))],
        out_specs=pl.BlockSpec((tc, D_MODEL), lambda i: (i, 0)),
        out_shape=jax.ShapeDtypeStruct((n, D_MODEL), F32),
        scratch_shapes=[pltpu.VMEM((2, 2, tc, D_MODEL), F32), pltpu.SemaphoreType.DMA((2,))],
        compiler_params=_params("arbitrary"),
        name="moe_combine_ln",
    )(pos0, pos1, pos0, pos1, x, route, ys, g, b)


def _moe_layer(x, route, route_t, counts_f, wg, wu, wd_f32, g, b):
    n = x.shape[0]
    tm = TM_MOE

    counts = counts_f[0, :N_EXPERTS].astype(jnp.int32)
    padded = ((counts + tm - 1) // tm) * tm
    offs = jnp.cumsum(padded) - padded
    eids = jnp.arange(N_EXPERTS, dtype=jnp.int32)
    e01 = route_t[0:2].astype(jnp.int32)
    rank01 = route_t[2:4].astype(jnp.int32)
    base = jnp.sum(jnp.where(e01[:, None, :] == eids[None, :, None], offs[None, :, None], 0), axis=1)
    pos = base + rank01
    counts_up = jnp.minimum(padded, ((counts + ZERO_ROWS - 1) // ZERO_ROWS) * ZERO_ROWS)
    meta = jnp.stack([offs, counts, counts_up, padded]).astype(jnp.int32)

    n_rows = 2 * n + N_EXPERTS * tm
    nsub = MOE_TILES_PER_WINDOW
    tiles = padded // tm
    wins = (tiles + nsub - 1) // nsub
    win_end = jnp.cumsum(wins)
    n_windows = win_end[-1]
    max_windows = (n_rows // tm + nsub - 1) // nsub + N_EXPERTS
    wj = jnp.minimum(jnp.arange(max_windows, dtype=jnp.int32), n_windows - 1)
    win_expert = jnp.minimum(jnp.sum((wj[:, None] >= win_end[None, :]).astype(jnp.int32), axis=-1), N_EXPERTS - 1)
    pick = lambda v: jnp.sum(jnp.where(win_expert[:, None] == eids[None, :], v[None, :], 0), axis=-1)
    local = wj - pick(win_end - wins)
    win_block = pick(offs // tm) + nsub * local
    win_tiles = jnp.minimum(nsub, pick(tiles) - nsub * local)
    counts_nt = jnp.stack([n_windows, jnp.sum(tiles)])

    xs, wd = _dispatch(x, pos, meta, n_rows, wd_f32.reshape(-1, D_MODEL))
    wd = wd.reshape(wd_f32.shape)
    i32 = lambda a: a.astype(jnp.int32)
    ys = _moe_experts(xs, wg, wu, wd, i32(win_expert), i32(win_block), i32(win_tiles), i32(counts_nt))
    return _combine_ln(x, route, pos, ys, g, b)


def kernel(x, a_w_in, a_conv_w, a_a_log, a_dt_bias, a_norm_w, a_w_out, b_w_in, b_b_in, b_sinks, b_w_out, rel_bias,
           ffn_w_gate, ffn_w_up, ffn_w_down, moe_router, moe_w_gate, moe_w_up, moe_w_down, ln_g, ln_b):
    batch, seq_len, _ = x.shape
    n = batch * seq_len
    x0 = x.reshape(n, D_MODEL)
    ln_g = ln_g.reshape(DEPTH, 2, 1, D_MODEL)
    ln_b = ln_b.reshape(DEPTH, 2, 1, D_MODEL)

    pad_gate = lambda p: jnp.pad(p.reshape(1, GDN_V_HEADS), ((0, 0), (GDN_V_HEADS, LANES - 2 * GDN_V_HEADS)))
    gate_params = jnp.concatenate([pad_gate(a_a_log[0]), pad_gate(a_dt_bias[0])], axis=0)
    q, k, v, zs, gb = _gdn_inproj(x0, jnp.swapaxes(a_w_in, 1, 2), a_conv_w[0], gate_params, seq_len)
    o, casted = _gdn_chunk(q, k, v, zs, gb, a_norm_w[0].reshape(1, GDN_HEAD), batch, seq_len,
                           [moe_w_gate[0].reshape(-1, EXPERT_DIM), moe_w_up[0].reshape(-1, EXPERT_DIM),
                            ffn_w_gate[0], ffn_w_up[0], ffn_w_down[0],
                            a_w_out[0], b_w_in[0], b_w_out[0]])
    moe_wg, moe_wu, ffn_wg, ffn_wu, ffn_wd, w_out_a, w_in_b, w_out_b = casted
    x2 = _outproj_ffn(o, w_out_a, x0, ln_g[0, 0], ln_b[0, 0],
                      ffn_wg, ffn_wu, ffn_wd, ln_g[0, 1], ln_b[0, 1])

    bias = _bias_table(rel_bias)
    sinks = jnp.pad(b_sinks[0].reshape(1, SWA_Q_HEADS), ((0, 0), (0, LANES - SWA_Q_HEADS)))
    x3 = _swa_layer(x2, w_in_b, b_b_in[0].reshape(1, -1), bias, sinks, w_out_b, ln_g[1, 0], ln_b[1, 0],
                    batch, seq_len)
    w_router = jnp.pad(moe_router[0], ((0, 0), (0, LANES - N_EXPERTS)))
    route, route_t, counts = _router(x3, w_router)
    expert_shape = (N_EXPERTS, D_MODEL, EXPERT_DIM)
    x4 = _moe_layer(x3, route, route_t, counts, moe_wg.reshape(expert_shape), moe_wu.reshape(expert_shape),
                    moe_w_down[0], ln_g[1, 1], ln_b[1, 1])
    return x4.reshape(batch, seq_len, D_MODEL)
```
